```python
import math
import jax, jax.numpy as jnp
from jax import lax
import numpy as np

D_MODEL = 2048
BATCH = 8
SEQ = 8192
DEPTH = 2

HEAD_DIM = 128
DIL_GROUPS = ((128, 1), (512, 4), (2048, 16))
N_GROUPS = 3
HEADS_PER_GROUP = 4
DIL_WIDTH = N_GROUPS * HEADS_PER_GROUP * HEAD_DIM
DIL_OUT = HEADS_PER_GROUP * HEAD_DIM
SB_HEADS = 8
SB_WIDTH = SB_HEADS * HEAD_DIM
N_BRANCHES = 2
N_IN = 3 * DIL_WIDTH + 3 * SB_WIDTH + N_BRANCHES * D_MODEL
D_FF = 4 * D_MODEL
ROPE_THETA = 10000.0
BLOCK = 128
EPS = 1e-6

kernel_name = "hybrid_dilated_stickbreak_block"


def rms_norm(x, g):
    xf = x.astype(jnp.float32)
    y = xf * lax.rsqrt(jnp.mean(xf * xf, axis=-1, keepdims=True) + EPS)
    return (y * g.astype(jnp.float32)).astype(x.dtype)


def rotary(x):
    s = x.shape[1]
    half = HEAD_DIM // 2
    inv_freq = ROPE_THETA ** (-jnp.arange(half, dtype=jnp.float32) / half)
    ang = jnp.arange(s, dtype=jnp.float32)[:, None] * inv_freq[None, :]
    cos = jnp.cos(ang)[None, :, None, :]
    sin = jnp.sin(ang)[None, :, None, :]
    xf = x.astype(jnp.float32)
    x1, x2 = xf[..., :half], xf[..., half:]
    out = jnp.concatenate([x1 * cos - x2 * sin, x2 * cos + x1 * sin], axis=-1)
    return out.astype(x.dtype)


def dilated_window_attention(q, k, v, window, dilation):
    b, s, h, d = q.shape
    span = window // dilation
    length = s // dilation
    nb = -(-length // BLOCK)
    lp = nb * BLOCK

    def to_sub(t):
        t = t.reshape(b, length, dilation, h, d).transpose(0, 2, 3, 1, 4)
        t = jnp.pad(t, ((0, 0), (0, 0), (0, 0), (0, lp - length), (0, 0)))
        return t.reshape(b, dilation, h, nb, BLOCK, d)

    def with_prev(t):
        prev = jnp.pad(t[:, :, :, :-1], ((0, 0), (0, 0), (0, 0), (1, 0), (0, 0), (0, 0)))
        return jnp.concatenate([prev, t], axis=4)

    qb = to_sub(q)
    kw = with_prev(to_sub(k))
    vw = with_prev(to_sub(v))
    scores = jnp.einsum('brhnqd,brhnkd->brhnqk', qb, kw).astype(jnp.float32) / math.sqrt(d)
    blk = jnp.arange(nb)
    qi = blk[:, None, None] * BLOCK + jnp.arange(BLOCK)[None, :, None]
    ki = (blk[:, None, None] - 1) * BLOCK + jnp.arange(2 * BLOCK)[None, None, :]
    off = qi - ki
    valid = (off >= 0) & (off <= span) & (ki >= 0)
    scores = jnp.where(valid, scores, -jnp.inf)
    m = jnp.max(scores, axis=-1, keepdims=True)
    p = jnp.exp(scores - m)
    l = jnp.sum(p, axis=-1, keepdims=True)
    o = jnp.einsum('brhnqk,brhnkd->brhnqd', (p / l).astype(v.dtype), vw)
    log_den = (m + jnp.log(l))[..., 0]
    o = o.reshape(b, dilation, h, lp, d)[:, :, :, :length]
    o = o.transpose(0, 3, 1, 2, 4).reshape(b, s, h, d)
    log_den = log_den.reshape(b, dilation, h, lp)[..., :length]
    log_den = log_den.transpose(0, 3, 1, 2).reshape(b, s, h)
    return o, log_den


def stick_breaking_attention(q, k, v):
    b, s, h, d = q.shape
    nb = s // BLOCK
    qb = q.reshape(b, nb, BLOCK, h, d).transpose(1, 0, 3, 2, 4)
    kt = k.transpose(0, 2, 1, 3)
    vt = v.transpose(0, 2, 1, 3)
    key_pos = jnp.arange(s)

    def block(args):
        q_blk, start = args
        z = jnp.einsum('bhqd,bhkd->bhqk', q_blk, kt).astype(jnp.float32) / math.sqrt(d)
        qpos = start + jnp.arange(BLOCK)
        mask = key_pos[None, :] < qpos[:, None]
        log_keep = jnp.where(mask, -jax.nn.softplus(z), 0.0)
        between = lax.cumsum(log_keep, axis=3, reverse=True) - log_keep
        a = jnp.where(mask, jnp.exp(jax.nn.log_sigmoid(z) + between), 0.0)
        return jnp.einsum('bhqk,bhkd->bhqd', a.astype(vt.dtype), vt)

    starts = jnp.arange(nb) * BLOCK
    o = lax.map(block, (qb, starts))
    return o.transpose(1, 0, 3, 2, 4).reshape(b, s, h, d)


def _fwd_setup_inputs(seed: int = 0) -> dict:
    key = jax.random.key(seed)
    ks = jax.random.split(key, 13)
    f32 = jnp.float32

    def nrm(k, shape, fan_in):
        return jax.random.normal(k, shape, f32) * (fan_in ** -0.5)

    def gain(k, shape):
        return 1.0 + 0.02 * jax.random.normal(k, shape, f32)

    return {
        "x": jax.random.normal(ks[0], (BATCH, SEQ, D_MODEL), f32),
        "norm1_g": gain(ks[1], (DEPTH, D_MODEL)),
        "w_in": nrm(ks[2], (DEPTH, D_MODEL, N_IN), D_MODEL),
        "q_norm_g": gain(ks[3], (DEPTH, N_GROUPS, HEAD_DIM)),
        "k_norm_g": gain(ks[4], (DEPTH, N_GROUPS, HEAD_DIM)),
        "w_up_dil": nrm(ks[5], (DEPTH, DIL_OUT, D_MODEL), DIL_OUT),
        "w_up_sb": nrm(ks[6], (DEPTH, SB_WIDTH, D_MODEL), SB_WIDTH),
        "gate_b": 0.01 * jax.random.normal(ks[7], (DEPTH, N_BRANCHES, D_MODEL), f32),
        "w_out": nrm(ks[8], (DEPTH, D_MODEL, D_MODEL), D_MODEL),
        "norm2_g": gain(ks[9], (DEPTH, D_MODEL)),
        "w_ff1": nrm(ks[10], (DEPTH, D_MODEL, D_FF), D_MODEL),
        "w_ff2": nrm(ks[11], (DEPTH, D_FF, D_MODEL), D_FF),
    }


def _fwd_reference(x, norm1_g, w_in, q_norm_g, k_norm_g, w_up_dil, w_up_sb, gate_b, w_out, norm2_g, w_ff1, w_ff2):
    b, s, _ = x.shape
    cuts = [DIL_WIDTH, 2 * DIL_WIDTH, 3 * DIL_WIDTH,
            3 * DIL_WIDTH + SB_WIDTH, 3 * DIL_WIDTH + 2 * SB_WIDTH, 3 * DIL_WIDTH + 3 * SB_WIDTH]
    for layer in range(DEPTH):
        h = rms_norm(x, norm1_g[layer])
        proj = h @ w_in[layer]
        q_d, k_d, v_d, q_s, k_s, v_s, g_pre = jnp.split(proj, cuts, axis=-1)
        q_d = q_d.reshape(b, s, N_GROUPS, HEADS_PER_GROUP, HEAD_DIM)
        k_d = k_d.reshape(b, s, N_GROUPS, HEADS_PER_GROUP, HEAD_DIM)
        v_d = v_d.reshape(b, s, N_GROUPS, HEADS_PER_GROUP, HEAD_DIM)

        outs, dens = [], []
        for g, (window, dilation) in enumerate(DIL_GROUPS):
            qg = rotary(rms_norm(q_d[:, :, g], q_norm_g[layer, g]))
            kg = rotary(rms_norm(k_d[:, :, g], k_norm_g[layer, g]))
            o_g, den_g = dilated_window_attention(qg, kg, v_d[:, :, g], window, dilation)
            outs.append(o_g)
            dens.append(den_g)
        wts = jax.nn.softmax(jnp.stack(dens, axis=0), axis=0)
        y_dil = jnp.sum(wts[..., None] * jnp.stack(outs, axis=0).astype(jnp.float32), axis=0)
        y_dil = y_dil.astype(x.dtype).reshape(b, s, DIL_OUT)

        y_sb = stick_breaking_attention(
            q_s.reshape(b, s, SB_HEADS, HEAD_DIM),
            k_s.reshape(b, s, SB_HEADS, HEAD_DIM),
            v_s.reshape(b, s, SB_HEADS, HEAD_DIM),
        ).reshape(b, s, SB_WIDTH)

        gates = jax.nn.sigmoid(g_pre.reshape(b, s, N_BRANCHES, D_MODEL) + gate_b[layer])
        mixed = gates[:, :, 0] * (y_dil @ w_up_dil[layer]) + gates[:, :, 1] * (y_sb @ w_up_sb[layer])
        x = x + mixed @ w_out[layer]

        h2 = rms_norm(x, norm2_g[layer])
        x = x + jnp.square(jax.nn.relu(h2 @ w_ff1[layer])) @ w_ff2[layer]
    return x


import jax as _jax
import jax.numpy as _jnp

TWIN_FORMAT = 'train_step'
FWD_PARAMS = ['x', 'norm1_g', 'w_in', 'q_norm_g', 'k_norm_g', 'w_up_dil', 'w_up_sb', 'gate_b', 'w_out', 'norm2_g', 'w_ff1', 'w_ff2']
TWIN_WEIGHTS = ['norm1_g', 'w_in', 'q_norm_g', 'k_norm_g', 'w_up_dil', 'w_up_sb', 'gate_b', 'w_out', 'norm2_g', 'w_ff1', 'w_ff2']
TWIN_DIFF_INPUT = 'x'
TWIN_INPUTS = ['x', 'norm1_g', 'w_in', 'q_norm_g', 'k_norm_g', 'w_up_dil', 'w_up_sb', 'gate_b', 'w_out', 'norm2_g', 'w_ff1', 'w_ff2', 'loss_target', 'm_norm1_g', 'm_w_in', 'm_q_norm_g', 'm_k_norm_g', 'm_w_up_dil', 'm_w_up_sb', 'm_gate_b', 'm_w_out', 'm_norm2_g', 'm_w_ff1', 'm_w_ff2', 'v_norm1_g', 'v_w_in', 'v_q_norm_g', 'v_k_norm_g', 'v_w_up_dil', 'v_w_up_sb', 'v_gate_b', 'v_w_out', 'v_norm2_g', 'v_w_ff1', 'v_w_ff2']
TWIN_OUTPUTS = ['loss', 'grad_x', 'grad_norm1_g', 'grad_w_in', 'grad_q_norm_g', 'grad_k_norm_g', 'grad_w_up_dil', 'grad_w_up_sb', 'grad_gate_b', 'grad_w_out', 'grad_norm2_g', 'grad_w_ff1', 'grad_w_ff2', 'delta_norm1_g', 'delta_w_in', 'delta_q_norm_g', 'delta_k_norm_g', 'delta_w_up_dil', 'delta_w_up_sb', 'delta_gate_b', 'delta_w_out', 'delta_norm2_g', 'delta_w_ff1', 'delta_w_ff2', 'new_m_norm1_g', 'new_m_w_in', 'new_m_q_norm_g', 'new_m_k_norm_g', 'new_m_w_up_dil', 'new_m_w_up_sb', 'new_m_gate_b', 'new_m_w_out', 'new_m_norm2_g', 'new_m_w_ff1', 'new_m_w_ff2', 'new_v_norm1_g', 'new_v_w_in', 'new_v_q_norm_g', 'new_v_k_norm_g', 'new_v_w_up_dil', 'new_v_w_up_sb', 'new_v_gate_b', 'new_v_w_out', 'new_v_norm2_g', 'new_v_w_ff1', 'new_v_w_ff2']
TWIN_LEAF_KINDS = {'loss': 'loss', 'grad_x': 'grad_x', 'grad_norm1_g': 'grad_w', 'grad_w_in': 'grad_w', 'grad_q_norm_g': 'grad_w', 'grad_k_norm_g': 'grad_w', 'grad_w_up_dil': 'grad_w', 'grad_w_up_sb': 'grad_w', 'grad_gate_b': 'grad_w', 'grad_w_out': 'grad_w', 'grad_norm2_g': 'grad_w', 'grad_w_ff1': 'grad_w', 'grad_w_ff2': 'grad_w', 'delta_norm1_g': 'delta_w', 'delta_w_in': 'delta_w', 'delta_q_norm_g': 'delta_w', 'delta_k_norm_g': 'delta_w', 'delta_w_up_dil': 'delta_w', 'delta_w_up_sb': 'delta_w', 'delta_gate_b': 'delta_w', 'delta_w_out': 'delta_w', 'delta_norm2_g': 'delta_w', 'delta_w_ff1': 'delta_w', 'delta_w_ff2': 'delta_w', 'new_m_norm1_g': 'new_m', 'new_m_w_in': 'new_m', 'new_m_q_norm_g': 'new_m', 'new_m_k_norm_g': 'new_m', 'new_m_w_up_dil': 'new_m', 'new_m_w_up_sb': 'new_m', 'new_m_gate_b': 'new_m', 'new_m_w_out': 'new_m', 'new_m_norm2_g': 'new_m', 'new_m_w_ff1': 'new_m', 'new_m_w_ff2': 'new_m', 'new_v_norm1_g': 'new_v', 'new_v_w_in': 'new_v', 'new_v_q_norm_g': 'new_v', 'new_v_k_norm_g': 'new_v', 'new_v_w_up_dil': 'new_v', 'new_v_w_up_sb': 'new_v', 'new_v_gate_b': 'new_v', 'new_v_w_out': 'new_v', 'new_v_norm2_g': 'new_v', 'new_v_w_ff1': 'new_v', 'new_v_w_ff2': 'new_v'}


def _forward(args):
    return _fwd_reference(*[args[k] for k in FWD_PARAMS])


def _output_shape():
    def fwd():
        inp = _fwd_setup_inputs(0)
        return _fwd_reference(*[inp[k] for k in FWD_PARAMS])
    out = _jax.eval_shape(fwd)
    return out.shape, out.dtype

N_MICROBATCH = 1
ADAM_LR = 0.001
ADAM_B1 = 0.9
ADAM_B2 = 0.999
ADAM_EPS = 1e-08
ADAM_WD = 0.01
ADAM_STEP = 10
PER_EXAMPLE_BATCH_AXIS = {'x': 0, 'loss_target': 0}
SHARED_INPUTS = []
_WEIGHT_DTYPES = {'norm1_g': _jnp.float32, 'w_in': _jnp.float32, 'q_norm_g': _jnp.float32, 'k_norm_g': _jnp.float32, 'w_up_dil': _jnp.float32, 'w_up_sb': _jnp.float32, 'gate_b': _jnp.float32, 'w_out': _jnp.float32, 'norm2_g': _jnp.float32, 'w_ff1': _jnp.float32, 'w_ff2': _jnp.float32}
MOMENT_SCALE = {'norm1_g': 7.273203e+00, 'w_in': 1.780652e+00, 'q_norm_g': 3.656104e-01, 'k_norm_g': 3.711147e-01, 'w_up_dil': 1.830699e+00, 'w_up_sb': 3.603300e+00, 'gate_b': 1.545453e+00, 'w_out': 3.844902e+00, 'norm2_g': 9.822844e+01, 'w_ff1': 4.188082e+00, 'w_ff2': 1.627106e+01}


def _to_microbatches(a, axis):
    t = _jnp.moveaxis(a, axis, 0)
    t = t.reshape((N_MICROBATCH, t.shape[0] // N_MICROBATCH) + t.shape[1:])
    return _jnp.moveaxis(t, 1, axis + 1)


def setup_inputs(seed: int = 0) -> dict:
    inp = _fwd_setup_inputs(seed)
    key = _jax.random.fold_in(_jax.random.key(seed), 7919)
    shape, _ = _output_shape()
    out = dict(inp)
    out["loss_target"] = _jax.random.normal(_jax.random.fold_in(key, 0), shape, _jnp.float32)
    for i, name in enumerate(TWIN_WEIGHTS):
        w = inp[name].astype(_jnp.float32)
        if MOMENT_SCALE is None:
            s = _jnp.sqrt(_jnp.mean(_jnp.square(w)) + 1e-30)
        else:
            s = MOMENT_SCALE[name]
        km, kv = _jax.random.split(_jax.random.fold_in(key, i + 1))
        out[name] = w
        out["m_" + name] = s * _jax.random.normal(km, w.shape, _jnp.float32)
        out["v_" + name] = (s * s) * _jax.random.uniform(kv, w.shape, _jnp.float32, 0.5, 1.5)
    if N_MICROBATCH > 1:
        for name, axis in PER_EXAMPLE_BATCH_AXIS.items():
            out[name] = _to_microbatches(out[name], axis)
    return {'x': out['x'], 'norm1_g': out['norm1_g'], 'w_in': out['w_in'], 'q_norm_g': out['q_norm_g'], 'k_norm_g': out['k_norm_g'], 'w_up_dil': out['w_up_dil'], 'w_up_sb': out['w_up_sb'], 'gate_b': out['gate_b'], 'w_out': out['w_out'], 'norm2_g': out['norm2_g'], 'w_ff1': out['w_ff1'], 'w_ff2': out['w_ff2'], 'loss_target': out['loss_target'], 'm_norm1_g': out['m_norm1_g'], 'm_w_in': out['m_w_in'], 'm_q_norm_g': out['m_q_norm_g'], 'm_k_norm_g': out['m_k_norm_g'], 'm_w_up_dil': out['m_w_up_dil'], 'm_w_up_sb': out['m_w_up_sb'], 'm_gate_b': out['m_gate_b'], 'm_w_out': out['m_w_out'], 'm_norm2_g': out['m_norm2_g'], 'm_w_ff1': out['m_w_ff1'], 'm_w_ff2': out['m_w_ff2'], 'v_norm1_g': out['v_norm1_g'], 'v_w_in': out['v_w_in'], 'v_q_norm_g': out['v_q_norm_g'], 'v_k_norm_g': out['v_k_norm_g'], 'v_w_up_dil': out['v_w_up_dil'], 'v_w_up_sb': out['v_w_up_sb'], 'v_gate_b': out['v_gate_b'], 'v_w_out': out['v_w_out'], 'v_norm2_g': out['v_norm2_g'], 'v_w_ff1': out['v_w_ff1'], 'v_w_ff2': out['v_w_ff2']}


def _loss(weights, diff, rest, loss_target):
    with _jax.named_scope("forward"):
        args = {**rest, TWIN_DIFF_INPUT: diff, **{k: w.astype(_WEIGHT_DTYPES[k]) for k, w in weights.items()}}
        y = _forward(args)
    with _jax.named_scope("loss_head"):
        err = _jnp.square(y.astype(_jnp.float32) - loss_target)
        return 0.5 * _jnp.sum(_jnp.mean(err, axis=-1)) if err.ndim else 0.5 * err


def _adamw(w, g, m, v):
    m = ADAM_B1 * m + (1.0 - ADAM_B1) * g
    v = ADAM_B2 * v + (1.0 - ADAM_B2) * _jnp.square(g)
    m_hat = m / (1.0 - ADAM_B1 ** ADAM_STEP)
    v_hat = v / (1.0 - ADAM_B2 ** ADAM_STEP)
    delta = -ADAM_LR * (m_hat / (_jnp.sqrt(v_hat) + ADAM_EPS) + ADAM_WD * w)
    return delta, m, v


def reference(x, norm1_g, w_in, q_norm_g, k_norm_g, w_up_dil, w_up_sb, gate_b, w_out, norm2_g, w_ff1, w_ff2, loss_target, m_norm1_g, m_w_in, m_q_norm_g, m_k_norm_g, m_w_up_dil, m_w_up_sb, m_gate_b, m_w_out, m_norm2_g, m_w_ff1, m_w_ff2, v_norm1_g, v_w_in, v_q_norm_g, v_k_norm_g, v_w_up_dil, v_w_up_sb, v_gate_b, v_w_out, v_norm2_g, v_w_ff1, v_w_ff2):
    given = dict(x=x, norm1_g=norm1_g, w_in=w_in, q_norm_g=q_norm_g, k_norm_g=k_norm_g, w_up_dil=w_up_dil, w_up_sb=w_up_sb, gate_b=gate_b, w_out=w_out, norm2_g=norm2_g, w_ff1=w_ff1, w_ff2=w_ff2, loss_target=loss_target, m_norm1_g=m_norm1_g, m_w_in=m_w_in, m_q_norm_g=m_q_norm_g, m_k_norm_g=m_k_norm_g, m_w_up_dil=m_w_up_dil, m_w_up_sb=m_w_up_sb, m_gate_b=m_gate_b, m_w_out=m_w_out, m_norm2_g=m_norm2_g, m_w_ff1=m_w_ff1, m_w_ff2=m_w_ff2, v_norm1_g=v_norm1_g, v_w_in=v_w_in, v_q_norm_g=v_q_norm_g, v_k_norm_g=v_k_norm_g, v_w_up_dil=v_w_up_dil, v_w_up_sb=v_w_up_sb, v_gate_b=v_gate_b, v_w_out=v_w_out, v_norm2_g=v_norm2_g, v_w_ff1=v_w_ff1, v_w_ff2=v_w_ff2)
    weights = {n: given[n] for n in TWIN_WEIGHTS}
    shared = {n: given[n] for n in SHARED_INPUTS}
    per_example = {n: given[n] for n in ['x']}
    grad_fn = _jax.value_and_grad(_loss, argnums=(0, 1))

    def one_microbatch(ex, loss_target):
        ex = dict(ex)
        diff = ex.pop(TWIN_DIFF_INPUT)
        return grad_fn(weights, diff, {**shared, **ex}, loss_target)

    if N_MICROBATCH == 1:
        loss, (grad_w, grad_x) = one_microbatch(per_example, given["loss_target"])
    else:
        def body(carry, xs):
            loss_sum, grad_sum = carry
            l_k, (gw_k, gx_k) = one_microbatch(xs[0], xs[1])
            with _jax.named_scope("update"):
                return (loss_sum + l_k, _jax.tree.map(_jnp.add, grad_sum, gw_k)), gx_k

        init = (_jnp.zeros((), _jnp.float32), _jax.tree.map(_jnp.zeros_like, weights))
        (loss, grad_w), grad_x = _jax.lax.scan(body, init, (per_example, given["loss_target"]))
    with _jax.named_scope("update"):
        delta_w, new_m, new_v = {}, {}, {}
        for n in TWIN_WEIGHTS:
            delta_w[n], new_m[n], new_v[n] = _adamw(weights[n], grad_w[n], given["m_" + n], given["v_" + n])
    return (loss, grad_x, *[grad_w[n] for n in TWIN_WEIGHTS], *[delta_w[n] for n in TWIN_WEIGHTS],
            *[new_m[n] for n in TWIN_WEIGHTS], *[new_v[n] for n in TWIN_WEIGHTS])
```

```python
import functools
import math

import jax
import jax.numpy as jnp
from jax import lax
from jax.experimental import pallas as pl
from jax.experimental.pallas import tpu as pltpu

F32 = jnp.float32
BF16 = jnp.bfloat16

HEAD_DIM = 128
DIL_GROUPS = ((128, 1), (512, 4), (2048, 16))
N_GROUPS = 3
HEADS_PER_GROUP = 4
GROUP_WIDTH = HEADS_PER_GROUP * HEAD_DIM
DIL_WIDTH = N_GROUPS * GROUP_WIDTH
SB_HEADS = 8
SB_WIDTH = SB_HEADS * HEAD_DIM
QKV_WIDTH = 3 * DIL_WIDTH + 3 * SB_WIDTH
BLOCK = 128
ROPE_THETA = 10000.0
EPS = 1e-6
SCALE = 1.0 / math.sqrt(HEAD_DIM)
NEG_INF = float("-inf")

ADAM_LR = 0.001
ADAM_B1 = 0.9
ADAM_B2 = 0.999
ADAM_EPS = 1e-08
ADAM_WD = 0.01
ADAM_STEP = 10

N_CHIPS = 4
MESH = pl.DeviceIdType.MESH
MIB = 1024 * 1024
VMEM_LIMIT = 56 * MIB


def _params(semantics=None, vmem=VMEM_LIMIT):
    return pltpu.CompilerParams(dimension_semantics=semantics, vmem_limit_bytes=vmem)


def _pick(n, pref):
    if n <= pref:
        return n
    t = 1 << (pref.bit_length() - 1)
    while n % t:
        t //= 2
    return t


def _tile(n, pref):
    best = 0
    for t in range(256, min(n, pref) + 1, 256):
        if n % t == 0:
            best = t
    return best if best else n


def _matmul(a, b, *, mode, name, out_dtypes=(F32,), extras=(), epilogue=None, b_sharded=False, out_sharded=False,
            tm=1024, tn=1024, tk=512):
    if mode == "nn":
        m, kdim = a.shape
        n = b.shape[-1] * (N_CHIPS if b_sharded else 1)
        assert (b.shape[-2] if b_sharded else b.shape[0]) == kdim
    elif mode == "nt":
        m, kdim = a.shape
        n = b.shape[-2]
        assert b.shape[-1] * (N_CHIPS if b_sharded else 1) == kdim
    else:
        kdim, m = a.shape
        n = b.shape[1]
        assert b.shape[0] == kdim and not b_sharded
    shard_n = n // N_CHIPS if (out_sharded or (b_sharded and mode == "nn")) else n
    shard_k = kdim // N_CHIPS if (b_sharded and mode == "nt") else kdim
    tm = _tile(m, tm)
    tn = _tile(shard_n, tn)
    tk = _tile(shard_k, tk)
    nk = kdim // tk
    grid = (m // tm, n // tn, nk)
    nj_n = shard_n // tn
    nj_k = shard_k // tk

    if mode == "nn":
        a_spec = pl.BlockSpec((tm, tk), lambda i, j, k: (i, k))
        if b_sharded:
            b_spec = pl.BlockSpec((None, tk, tn), lambda i, j, k: (j // nj_n, k, j % nj_n))
        else:
            b_spec = pl.BlockSpec((tk, tn), lambda i, j, k: (k, j))
        dims = (((1,), (0,)), ((), ()))
    elif mode == "nt":
        a_spec = pl.BlockSpec((tm, tk), lambda i, j, k: (i, k))
        if b_sharded:
            b_spec = pl.BlockSpec((None, tn, tk), lambda i, j, k: (k // nj_k, j, k % nj_k))
        else:
            b_spec = pl.BlockSpec((tn, tk), lambda i, j, k: (j, k))
        dims = (((1,), (1,)), ((), ()))
    else:
        a_spec = pl.BlockSpec((tk, tm), lambda i, j, k: (k, i))
        b_spec = pl.BlockSpec((tk, tn), lambda i, j, k: (k, j))
        dims = (((0,), (0,)), ((), ()))
    if out_sharded:
        o_spec = pl.BlockSpec((None, tm, tn), lambda i, j, k: (j // nj_n, i, j % nj_n))
        o_shape = (N_CHIPS, m, shard_n)
    else:
        o_spec = pl.BlockSpec((tm, tn), lambda i, j, k: (i, j))
        o_shape = (m, n)
    x_spec = pl.BlockSpec((tm, tn), lambda i, j, k: (i, j))
    n_extra = len(extras)
    n_out = len(out_dtypes)

    def body(a_ref, b_ref, *rest):
        extra_refs = rest[:n_extra]
        out_refs = rest[n_extra:n_extra + n_out]
        acc = rest[-1]
        k = pl.program_id(2)

        @pl.when(k == 0)
        def _():
            acc[...] = jnp.zeros_like(acc)

        acc[...] += lax.dot_general(a_ref[...], b_ref[...], dims, preferred_element_type=F32)

        @pl.when(k == nk - 1)
        def _():
            vals = [e[...] for e in extra_refs]
            res = (acc[...],) if epilogue is None else epilogue(acc[...], *vals)
            for o, r in zip(out_refs, res):
                o[...] = r.astype(o.dtype)

    outs = pl.pallas_call(
        body,
        name=name,
        grid=grid,
        in_specs=[a_spec, b_spec] + [x_spec] * n_extra,
        out_specs=[o_spec] * n_out,
        out_shape=[jax.ShapeDtypeStruct(o_shape, dt) for dt in out_dtypes],
        scratch_shapes=[pltpu.VMEM((tm, tn), F32)],
        compiler_params=_params(("parallel", "parallel", "arbitrary")),
    )(a, b, *extras)
    return outs[0] if n_out == 1 else outs


def _rmsnorm_fwd(x, g, *, name):
    s, d = x.shape
    tr = _pick(s, 256)

    def body(x_ref, g_ref, o_ref):
        xv = x_ref[...]
        r = lax.rsqrt(jnp.mean(xv * xv, axis=-1, keepdims=True) + EPS)
        o_ref[...] = (xv * r * g_ref[...]).astype(o_ref.dtype)

    return pl.pallas_call(
        body, name=name, grid=(s // tr,),
        in_specs=[pl.BlockSpec((tr, d), lambda i: (i, 0)), pl.BlockSpec((1, d), lambda i: (0, 0))],
        out_specs=pl.BlockSpec((tr, d), lambda i: (i, 0)),
        out_shape=jax.ShapeDtypeStruct((s, d), BF16),
        compiler_params=_params(("parallel",)),
    )(x, g.reshape(1, d))


def _rmsnorm_bwd(x, g, dh, dres, *, name):
    s, d = x.shape
    tr = _pick(s, 256)

    def body(x_ref, g_ref, dh_ref, dres_ref, dx_ref, dxb_ref, dg_ref):
        xv = x_ref[...]
        r = lax.rsqrt(jnp.mean(xv * xv, axis=-1, keepdims=True) + EPS)
        y = xv * r
        dhv = dh_ref[...]
        dy = dhv * g_ref[...]
        dx = dres_ref[...] + r * (dy - y * jnp.mean(dy * y, axis=-1, keepdims=True))
        dx_ref[...] = dx
        dxb_ref[...] = dx.astype(BF16)

        @pl.when(pl.program_id(0) == 0)
        def _():
            dg_ref[...] = jnp.zeros_like(dg_ref)

        dg_ref[...] += jnp.sum(dhv * y, axis=0, keepdims=True)

    row = pl.BlockSpec((tr, d), lambda i: (i, 0))
    vec = pl.BlockSpec((1, d), lambda i: (0, 0))
    dx, dxb, dg = pl.pallas_call(
        body, name=name, grid=(s // tr,),
        in_specs=[row, vec, row, row],
        out_specs=[row, row, vec],
        out_shape=[jax.ShapeDtypeStruct((s, d), F32), jax.ShapeDtypeStruct((s, d), BF16),
                   jax.ShapeDtypeStruct((1, d), F32)],
        compiler_params=_params(("arbitrary",)),
    )(x, g.reshape(1, d), dh, dres)
    return dx, dxb, dg.reshape(d)


def _loss_head(y, target):
    s, d = y.shape
    tr = _pick(s, 256)

    def body(y_ref, t_ref, dy_ref, dyb_ref, part_ref):
        err = y_ref[...] - t_ref[...]
        dy = err * (1.0 / d)
        dy_ref[...] = dy
        dyb_ref[...] = dy.astype(BF16)

        @pl.when(pl.program_id(0) == 0)
        def _():
            part_ref[...] = jnp.zeros_like(part_ref)

        part_ref[...] += jnp.sum(err * err, axis=0, keepdims=True)

    row = pl.BlockSpec((tr, d), lambda i: (i, 0))
    vec = pl.BlockSpec((1, d), lambda i: (0, 0))
    dy, dyb, part = pl.pallas_call(
        body, name="loss_head", grid=(s // tr,),
        in_specs=[row, row], out_specs=[row, row, vec],
        out_shape=[jax.ShapeDtypeStruct((s, d), F32), jax.ShapeDtypeStruct((s, d), BF16),
                   jax.ShapeDtypeStruct((1, d), F32)],
        compiler_params=_params(("arbitrary",)),
    )(y, target)
    return 0.5 * jnp.sum(part) / d, dy, dyb


def _rope_tables(s):
    half = HEAD_DIM // 2
    inv_freq = ROPE_THETA ** (-jnp.arange(half, dtype=F32) / half)
    ang = jnp.arange(s, dtype=F32)[:, None] * inv_freq[None, :]
    cos, sin = jnp.cos(ang), jnp.sin(ang)
    return jnp.concatenate([cos, cos], axis=-1), jnp.concatenate([-sin, sin], axis=-1)


def _qkv_prep(proj, qg, kg, cos, sin, *, name):
    s = proj.shape[0]
    tr = _pick(s, 256)

    def body(p_ref, qg_ref, kg_ref, cos_ref, sin_ref, o_ref):
        c = cos_ref[...]
        sn = sin_ref[...]
        for which, g_ref in ((0, qg_ref), (1, kg_ref)):
            for g in range(N_GROUPS):
                gain = g_ref[g:g + 1, :]
                for h in range(HEADS_PER_GROUP):
                    off = which * DIL_WIDTH + g * GROUP_WIDTH + h * HEAD_DIM
                    xv = p_ref[:, off:off + HEAD_DIM]
                    r = lax.rsqrt(jnp.mean(xv * xv, axis=-1, keepdims=True) + EPS)
                    y = xv * r * gain
                    o_ref[:, off:off + HEAD_DIM] = (y * c + pltpu.roll(y, HEAD_DIM // 2, 1) * sn).astype(BF16)
        o_ref[:, 2 * DIL_WIDTH:] = p_ref[:, 2 * DIL_WIDTH:].astype(BF16)

    return pl.pallas_call(
        body, name=name, grid=(s // tr,),
        in_specs=[pl.BlockSpec((tr, QKV_WIDTH), lambda i: (i, 0)),
                  pl.BlockSpec((N_GROUPS, HEAD_DIM), lambda i: (0, 0)),
                  pl.BlockSpec((N_GROUPS, HEAD_DIM), lambda i: (0, 0)),
                  pl.BlockSpec((tr, HEAD_DIM), lambda i: (i, 0)),
                  pl.BlockSpec((tr, HEAD_DIM), lambda i: (i, 0))],
        out_specs=pl.BlockSpec((tr, QKV_WIDTH), lambda i: (i, 0)),
        out_shape=jax.ShapeDtypeStruct((s, QKV_WIDTH), BF16),
        compiler_params=_params(("parallel",)),
    )(proj, qg, kg, cos, sin)


def _qkv_prep_bwd(proj, qg, kg, cos, sin, dqkv_dil, dqkv_sb, dgate, *, name):
    s, n_in = proj.shape
    n_gate = dgate.shape[1]
    tr = _pick(s, 256)

    def body(p_ref, qg_ref, kg_ref, cos_ref, sin_ref, *rest):
        dil_refs = rest[:9]
        sb_refs = rest[9:12]
        dgate_ref = rest[12]
        o_ref, dqg_ref, dkg_ref = rest[13:16]
        c = cos_ref[...]
        sn = sin_ref[...]

        @pl.when(pl.program_id(0) == 0)
        def _():
            dqg_ref[...] = jnp.zeros_like(dqg_ref)
            dkg_ref[...] = jnp.zeros_like(dkg_ref)

        for which, g_ref, dg_ref in ((0, qg_ref, dqg_ref), (1, kg_ref, dkg_ref)):
            for g in range(N_GROUPS):
                gain = g_ref[g:g + 1, :]
                d_ref = dil_refs[3 * g + which]
                dgain = jnp.zeros((1, HEAD_DIM), F32)
                for h in range(HEADS_PER_GROUP):
                    off = which * DIL_WIDTH + g * GROUP_WIDTH + h * HEAD_DIM
                    xv = p_ref[:, off:off + HEAD_DIM]
                    r = lax.rsqrt(jnp.mean(xv * xv, axis=-1, keepdims=True) + EPS)
                    nx = xv * r
                    dout = d_ref[:, h * HEAD_DIM:(h + 1) * HEAD_DIM]
                    dy = dout * c + pltpu.roll(dout * sn, HEAD_DIM // 2, 1)
                    dgain = dgain + jnp.sum(dy * nx, axis=0, keepdims=True)
                    dn = dy * gain
                    dxv = r * (dn - nx * jnp.mean(dn * nx, axis=-1, keepdims=True))
                    o_ref[:, off:off + HEAD_DIM] = dxv.astype(BF16)
                dg_ref[g:g + 1, :] += dgain
        for g in range(N_GROUPS):
            off = 2 * DIL_WIDTH + g * GROUP_WIDTH
            o_ref[:, off:off + GROUP_WIDTH] = dil_refs[3 * g + 2][...].astype(BF16)
        for t in range(3):
            off = 3 * DIL_WIDTH + t * SB_WIDTH
            o_ref[:, off:off + SB_WIDTH] = sb_refs[t][...].astype(BF16)
        o_ref[:, QKV_WIDTH:] = dgate_ref[...]

    grp = pl.BlockSpec((tr, GROUP_WIDTH), lambda i: (i, 0))
    sbs = pl.BlockSpec((tr, SB_WIDTH), lambda i: (i, 0))
    gain_spec = pl.BlockSpec((N_GROUPS, HEAD_DIM), lambda i: (0, 0))
    tab = pl.BlockSpec((tr, HEAD_DIM), lambda i: (i, 0))
    flat_dil = [t for grp3 in dqkv_dil for t in grp3]
    return pl.pallas_call(
        body, name=name, grid=(s // tr,),
        in_specs=[pl.BlockSpec((tr, 2 * DIL_WIDTH), lambda i: (i, 0)), gain_spec, gain_spec, tab, tab]
                 + [grp] * 9 + [sbs] * 3 + [pl.BlockSpec((tr, n_gate), lambda i: (i, 0))],
        out_specs=[pl.BlockSpec((tr, n_in), lambda i: (i, 0)), gain_spec, gain_spec],
        out_shape=[jax.ShapeDtypeStruct((s, n_in), BF16),
                   jax.ShapeDtypeStruct((N_GROUPS, HEAD_DIM), F32),
                   jax.ShapeDtypeStruct((N_GROUPS, HEAD_DIM), F32)],
        compiler_params=_params(("arbitrary",)),
    )(proj, qg, kg, cos, sin, *flat_dil, *dqkv_sb, dgate)


def _dil_masks(i):
    row = lax.broadcasted_iota(jnp.int32, (BLOCK, BLOCK), 0)
    col = lax.broadcasted_iota(jnp.int32, (BLOCK, BLOCK), 1)
    return col <= row, (col >= row) & (i > 0)


def _dot_nt(a, b):
    return lax.dot_general(a, b, (((1,), (1,)), ((), ())), preferred_element_type=F32)


def _dot_tn(a, b):
    return lax.dot_general(a, b, (((0,), (0,)), ((), ())), preferred_element_type=F32)


def _dot(a, b):
    return jnp.dot(a, b, preferred_element_type=F32)


def _dil_attn_fwd(qkv, g, *, name):
    s = qkv.shape[0]
    r = DIL_GROUPS[g][1]
    length = s // r
    nb = length // BLOCK
    wb = QKV_WIDTH // GROUP_WIDTH
    view = qkv.reshape(length, r * QKV_WIDTH)

    def body(q_ref, kp_ref, kc_ref, vp_ref, vc_ref, o_ref, ld_ref):
        i = pl.program_id(1)
        mask_c, mask_p = _dil_masks(i)
        for h in range(HEADS_PER_GROUP):
            sl = slice(h * HEAD_DIM, (h + 1) * HEAD_DIM)
            q = q_ref[:, sl]
            s_c = jnp.where(mask_c, _dot_nt(q, kc_ref[:, sl]) * SCALE, NEG_INF)
            s_p = jnp.where(mask_p, _dot_nt(q, kp_ref[:, sl]) * SCALE, NEG_INF)
            m = jnp.maximum(jnp.max(s_c, axis=-1, keepdims=True), jnp.max(s_p, axis=-1, keepdims=True))
            p_c = jnp.exp(s_c - m)
            p_p = jnp.exp(s_p - m)
            l = jnp.sum(p_c, axis=-1, keepdims=True) + jnp.sum(p_p, axis=-1, keepdims=True)
            inv = 1.0 / l
            o = _dot((p_c * inv).astype(BF16), vc_ref[:, sl]) + _dot((p_p * inv).astype(BF16), vp_ref[:, sl])
            o_ref[:, sl] = o
            ld_ref[:, sl] = jnp.broadcast_to(m + jnp.log(l), (BLOCK, HEAD_DIM))

    def col(which):
        return lambda c, i: (i, c * wb + 3 * which + g)

    def col_prev(which):
        return lambda c, i: (jnp.maximum(i - 1, 0), c * wb + 3 * which + g)

    blk = (BLOCK, GROUP_WIDTH)
    o, ld = pl.pallas_call(
        body, name=name, grid=(r, nb),
        in_specs=[pl.BlockSpec(blk, col(0)), pl.BlockSpec(blk, col_prev(1)), pl.BlockSpec(blk, col(1)),
                  pl.BlockSpec(blk, col_prev(2)), pl.BlockSpec(blk, col(2))],
        out_specs=[pl.BlockSpec(blk, lambda c, i: (i, c))] * 2,
        out_shape=[jax.ShapeDtypeStruct((length, r * GROUP_WIDTH), F32)] * 2,
        compiler_params=_params(("parallel", "parallel")),
    )(view, view, view, view, view)
    return o.reshape(s, GROUP_WIDTH), ld.reshape(s, GROUP_WIDTH)


def _dil_merge(outs, lds, *, name):
    s = outs[0].shape[0]
    tr = _pick(s, 512)

    def body(o0, o1, o2, l0, l1, l2, y_ref, yb_ref, lse_ref):
        a0, a1, a2 = l0[...], l1[...], l2[...]
        m = jnp.maximum(jnp.maximum(a0, a1), a2)
        e0, e1, e2 = jnp.exp(a0 - m), jnp.exp(a1 - m), jnp.exp(a2 - m)
        tot = e0 + e1 + e2
        inv = 1.0 / tot
        y = (e0 * inv) * o0[...] + (e1 * inv) * o1[...] + (e2 * inv) * o2[...]
        y_ref[...] = y
        yb_ref[...] = y.astype(BF16)
        lse_ref[...] = m + jnp.log(tot)

    blk = pl.BlockSpec((tr, GROUP_WIDTH), lambda i: (i, 0))
    return pl.pallas_call(
        body, name=name, grid=(s // tr,),
        in_specs=[blk] * 6, out_specs=[blk] * 3,
        out_shape=[jax.ShapeDtypeStruct((s, GROUP_WIDTH), F32), jax.ShapeDtypeStruct((s, GROUP_WIDTH), BF16),
                   jax.ShapeDtypeStruct((s, GROUP_WIDTH), F32)],
        compiler_params=_params(("parallel",)),
    )(*outs, *lds)


def _dil_attn_bwd(qkv, y, lse, dy, g, *, name):
    s = qkv.shape[0]
    r = DIL_GROUPS[g][1]
    length = s // r
    nb = length // BLOCK
    wb = QKV_WIDTH // GROUP_WIDTH
    view = qkv.reshape(length, r * QKV_WIDTH)
    yv, lv, dv_ = (t.reshape(length, r * GROUP_WIDTH) for t in (y, lse, dy))

    def body(q_ref, qn_ref, kp_ref, kc_ref, vp_ref, vc_ref, y_ref, yn_ref, l_ref, ln_ref, d_ref, dn_ref,
             dq_ref, dk_ref, dv_ref):
        i = pl.program_id(1)
        mask_c, mask_p = _dil_masks(i)
        _, mask_n = _dil_masks(jnp.where(i + 1 < nb, 1, 0))
        for h in range(HEADS_PER_GROUP):
            sl = slice(h * HEAD_DIM, (h + 1) * HEAD_DIM)
            q, qn = q_ref[:, sl], qn_ref[:, sl]
            kc, kp, vc, vp = kc_ref[:, sl], kp_ref[:, sl], vc_ref[:, sl], vp_ref[:, sl]
            dyf, dynf = d_ref[:, sl], dn_ref[:, sl]
            dyb, dynb = dyf.astype(BF16), dynf.astype(BF16)
            delta = jnp.sum(dyf * y_ref[:, sl], axis=-1, keepdims=True)
            delta_n = jnp.sum(dynf * yn_ref[:, sl], axis=-1, keepdims=True)
            lse_q, lse_n = l_ref[:, sl], ln_ref[:, sl]
            p_c = jnp.where(mask_c, jnp.exp(_dot_nt(q, kc) * SCALE - lse_q), 0.0)
            p_p = jnp.where(mask_p, jnp.exp(_dot_nt(q, kp) * SCALE - lse_q), 0.0)
            p_n = jnp.where(mask_n, jnp.exp(_dot_nt(qn, kc) * SCALE - lse_n), 0.0)
            ds_c = (p_c * (_dot_nt(dyb, vc) - delta) * SCALE).astype(BF16)
            ds_p = (p_p * (_dot_nt(dyb, vp) - delta) * SCALE).astype(BF16)
            ds_n = (p_n * (_dot_nt(dynb, vc) - delta_n) * SCALE).astype(BF16)
            dq_ref[:, sl] = _dot(ds_c, kc) + _dot(ds_p, kp)
            dk_ref[:, sl] = _dot_tn(ds_c, q) + _dot_tn(ds_n, qn)
            dv_ref[:, sl] = _dot_tn(p_c.astype(BF16), dyb) + _dot_tn(p_n.astype(BF16), dynb)

    def col(which):
        return lambda c, i: (i, c * wb + 3 * which + g)

    def col_prev(which):
        return lambda c, i: (jnp.maximum(i - 1, 0), c * wb + 3 * which + g)

    def col_next(which):
        return lambda c, i: (jnp.minimum(i + 1, nb - 1), c * wb + 3 * which + g)

    blk = (BLOCK, GROUP_WIDTH)
    own = pl.BlockSpec(blk, lambda c, i: (i, c))
    nxt = pl.BlockSpec(blk, lambda c, i: (jnp.minimum(i + 1, nb - 1), c))
    dq, dk, dv = pl.pallas_call(
        body, name=name, grid=(r, nb),
        in_specs=[pl.BlockSpec(blk, col(0)), pl.BlockSpec(blk, col_next(0)),
                  pl.BlockSpec(blk, col_prev(1)), pl.BlockSpec(blk, col(1)),
                  pl.BlockSpec(blk, col_prev(2)), pl.BlockSpec(blk, col(2)),
                  own, nxt, own, nxt, own, nxt],
        out_specs=[own] * 3,
        out_shape=[jax.ShapeDtypeStruct((length, r * GROUP_WIDTH), F32)] * 3,
        compiler_params=_params(("parallel", "parallel")),
    )(view, view, view, view, view, view, yv, yv, lv, lv, dv_, dv_)
    return tuple(t.reshape(s, GROUP_WIDTH) for t in (dq, dk, dv))


SB_TQ = 256


def _split_dot(x, w):
    hi = x.astype(BF16)
    lo = (x - hi.astype(F32)).astype(BF16)
    return _dot(hi, w) + _dot(lo, w)


def _sb_consts():
    j = lax.broadcasted_iota(jnp.int32, (BLOCK, BLOCK), 0)
    k = lax.broadcasted_iota(jnp.int32, (BLOCK, BLOCK), 1)
    ones = jnp.ones((BLOCK, BLOCK), F32)
    after = jnp.concatenate([(j > k).astype(F32), ones], axis=1).astype(BF16)
    before = jnp.concatenate([(j < k).astype(F32), ones], axis=1).astype(BF16)
    return after, before


def _sb_scores(q, kc, row0, col0, tq):
    z = _dot_nt(q, kc) * SCALE
    row = row0 + lax.broadcasted_iota(jnp.int32, (tq, BLOCK), 0)
    col = col0 + lax.broadcasted_iota(jnp.int32, (tq, BLOCK), 1)
    mask = col < row
    sp = jnp.maximum(z, 0.0) + jnp.log1p(jnp.exp(-jnp.abs(z)))
    return z, mask, sp


def _sb_attn_fwd(qkv, *, name):
    s = qkv.shape[0]
    tq = _pick(s, SB_TQ)
    nq = s // tq
    per = tq // BLOCK
    base = 3 * DIL_WIDTH // HEAD_DIM
    after, _ = _sb_consts()

    def body(q_ref, k_ref, v_ref, w_ref, o_ref, rc_ref, acc_ref, run_ref):
        qi = pl.program_id(1)
        q = q_ref[...]
        w = w_ref[...]
        acc_ref[...] = jnp.zeros_like(acc_ref)
        run_ref[...] = jnp.zeros_like(run_ref)
        rc_ref[...] = jnp.zeros_like(rc_ref)
        lane = lax.broadcasted_iota(jnp.int32, (tq, BLOCK), 1)
        n_chunks = (qi + 1) * per

        def step(t, carry):
            j = n_chunks - 1 - t
            start = pl.multiple_of(j * BLOCK, BLOCK)
            kc = k_ref[pl.ds(start, BLOCK), :]
            vc = v_ref[pl.ds(start, BLOCK), :]
            z, mask, sp = _sb_scores(q, kc, qi * tq, j * BLOCK, tq)
            lk = jnp.where(mask, -sp, 0.0)
            sums = _split_dot(lk, w)
            run = run_ref[...]
            a = jnp.where(mask, jnp.exp(z - sp + run + sums[:, :BLOCK]), 0.0)
            acc_ref[...] += _dot(a.astype(BF16), vc)
            rc_ref[...] = jnp.where(lane == j, run, rc_ref[...])
            run_ref[...] = run + sums[:, BLOCK:]
            return carry

        lax.fori_loop(0, n_chunks, step, 0)
        o_ref[...] = acc_ref[...].astype(o_ref.dtype)

    head = lambda off: (lambda h, i: (0, base + off + h))
    o, rc = pl.pallas_call(
        body, name=name, grid=(SB_HEADS, nq),
        in_specs=[pl.BlockSpec((tq, HEAD_DIM), lambda h, i: (i, base + h)),
                  pl.BlockSpec((s, HEAD_DIM), head(SB_HEADS)),
                  pl.BlockSpec((s, HEAD_DIM), head(2 * SB_HEADS)),
                  pl.BlockSpec((BLOCK, 2 * BLOCK), lambda h, i: (0, 0))],
        out_specs=[pl.BlockSpec((tq, HEAD_DIM), lambda h, i: (i, h))] * 2,
        out_shape=[jax.ShapeDtypeStruct((s, SB_WIDTH), BF16), jax.ShapeDtypeStruct((s, SB_WIDTH), F32)],
        scratch_shapes=[pltpu.VMEM((tq, HEAD_DIM), F32), pltpu.VMEM((tq, BLOCK), F32)],
        compiler_params=_params(("parallel", "arbitrary")),
    )(qkv, qkv, qkv, after)
    return o, rc


def _sb_attn_bwd(qkv, rc, do, *, name):
    s = qkv.shape[0]
    tq = _pick(s, SB_TQ)
    nq = s // tq
    per = tq // BLOCK
    base = 3 * DIL_WIDTH // HEAD_DIM
    _, before = _sb_consts()

    def body(q_ref, k_ref, v_ref, w_ref, rc_ref, do_ref, dq_ref, dk_ref, dv_ref, acc_ref, pre_ref):
        qi = pl.program_id(1)

        @pl.when(qi == 0)
        def _():
            dk_ref[...] = jnp.zeros_like(dk_ref)
            dv_ref[...] = jnp.zeros_like(dv_ref)

        q = q_ref[...]
        w = w_ref[...]
        wa = (lax.broadcasted_iota(jnp.int32, (BLOCK, BLOCK), 0)
              > lax.broadcasted_iota(jnp.int32, (BLOCK, BLOCK), 1)).astype(BF16)
        dob = do_ref[...].astype(BF16)
        rcv = rc_ref[...]
        acc_ref[...] = jnp.zeros_like(acc_ref)
        pre_ref[...] = jnp.zeros_like(pre_ref)
        lane = lax.broadcasted_iota(jnp.int32, (tq, BLOCK), 1)
        n_chunks = (qi + 1) * per

        def step(j, carry):
            start = pl.multiple_of(j * BLOCK, BLOCK)
            kc = k_ref[pl.ds(start, BLOCK), :]
            vc = v_ref[pl.ds(start, BLOCK), :]
            z, mask, sp = _sb_scores(q, kc, qi * tq, j * BLOCK, tq)
            lk = jnp.where(mask, -sp, 0.0)
            run = jnp.sum(jnp.where(lane == j, rcv, 0.0), axis=-1, keepdims=True)
            sig = jnp.exp(z - sp)
            a = jnp.where(mask, sig * jnp.exp(run + _split_dot(lk, wa)), 0.0)
            e = a * _dot_nt(dob, vc)
            sums = _split_dot(e, w)
            pre = pre_ref[...]
            dz = jnp.where(mask, (e * (1.0 - sig) - sig * (pre + sums[:, :BLOCK])) * SCALE, 0.0).astype(BF16)
            pre_ref[...] = pre + sums[:, BLOCK:]
            acc_ref[...] += _dot(dz, kc)
            dk_ref[pl.ds(start, BLOCK), :] += _dot_tn(dz, q)
            dv_ref[pl.ds(start, BLOCK), :] += _dot_tn(a.astype(BF16), dob)
            return carry

        lax.fori_loop(0, n_chunks, step, 0)
        dq_ref[...] = acc_ref[...]

    head = lambda off: (lambda h, i: (0, base + off + h))
    blk = pl.BlockSpec((tq, HEAD_DIM), lambda h, i: (i, h))
    whole = pl.BlockSpec((s, HEAD_DIM), lambda h, i: (0, h))
    return pl.pallas_call(
        body, name=name, grid=(SB_HEADS, nq),
        in_specs=[pl.BlockSpec((tq, HEAD_DIM), lambda h, i: (i, base + h)),
                  pl.BlockSpec((s, HEAD_DIM), head(SB_HEADS)),
                  pl.BlockSpec((s, HEAD_DIM), head(2 * SB_HEADS)),
                  pl.BlockSpec((BLOCK, 2 * BLOCK), lambda h, i: (0, 0)),
                  blk, blk],
        out_specs=[blk, whole, whole],
        out_shape=[jax.ShapeDtypeStruct((s, SB_WIDTH), F32)] * 3,
        scratch_shapes=[pltpu.VMEM((tq, HEAD_DIM), F32), pltpu.VMEM((tq, BLOCK), F32)],
        compiler_params=_params(("parallel", "arbitrary")),
    )(qkv, qkv, qkv, before, rc, do)


GATE_TC = 512


def _gate_fwd(proj, gate_b, up_dil, up_sb, *, name):
    s, d = up_dil.shape
    tr = _pick(s, 512)
    g0 = QKV_WIDTH // GATE_TC
    nc = d // GATE_TC

    def body(ga_ref, gb_ref, b_ref, ud_ref, us_ref, o_ref):
        ga = jax.nn.sigmoid(ga_ref[...] + b_ref[0:1, :])
        gb = jax.nn.sigmoid(gb_ref[...] + b_ref[1:2, :])
        o_ref[...] = (ga * ud_ref[...] + gb * us_ref[...]).astype(o_ref.dtype)

    blk = pl.BlockSpec((tr, GATE_TC), lambda i, j: (i, j))
    return pl.pallas_call(
        body, name=name, grid=(s // tr, nc),
        in_specs=[pl.BlockSpec((tr, GATE_TC), lambda i, j: (i, g0 + j)),
                  pl.BlockSpec((tr, GATE_TC), lambda i, j: (i, g0 + nc + j)),
                  pl.BlockSpec((2, GATE_TC), lambda i, j: (0, j)), blk, blk],
        out_specs=blk,
        out_shape=jax.ShapeDtypeStruct((s, d), BF16),
        compiler_params=_params(("parallel", "parallel")),
    )(proj, proj, gate_b, up_dil, up_sb)


def _gate_bwd(proj, gate_b, up_dil, up_sb, dmixed, *, name):
    s, d = up_dil.shape
    tr = _pick(s, 512)
    g0 = QKV_WIDTH // GATE_TC
    nc = d // GATE_TC

    def body(ga_ref, gb_ref, b_ref, ud_ref, us_ref, dm_ref, dud_ref, dus_ref, dga_ref, dgb_ref, db_ref):
        ga = jax.nn.sigmoid(ga_ref[...] + b_ref[0:1, :])
        gb = jax.nn.sigmoid(gb_ref[...] + b_ref[1:2, :])
        dm = dm_ref[...]
        dud_ref[...] = (dm * ga).astype(BF16)
        dus_ref[...] = (dm * gb).astype(BF16)
        dla = dm * ud_ref[...] * ga * (1.0 - ga)
        dlb = dm * us_ref[...] * gb * (1.0 - gb)
        dga_ref[...] = dla.astype(BF16)
        dgb_ref[...] = dlb.astype(BF16)

        @pl.when(pl.program_id(1) == 0)
        def _():
            db_ref[...] = jnp.zeros_like(db_ref)

        db_ref[0:1, :] += jnp.sum(dla, axis=0, keepdims=True)
        db_ref[1:2, :] += jnp.sum(dlb, axis=0, keepdims=True)

    blk = pl.BlockSpec((tr, GATE_TC), lambda j, i: (i, j))
    dud, dus, dga, dgb, db = pl.pallas_call(
        body, name=name, grid=(nc, s // tr),
        in_specs=[pl.BlockSpec((tr, GATE_TC), lambda j, i: (i, g0 + j)),
                  pl.BlockSpec((tr, GATE_TC), lambda j, i: (i, g0 + nc + j)),
                  pl.BlockSpec((2, GATE_TC), lambda j, i: (0, j)), blk, blk, blk],
        out_specs=[blk, blk, blk, blk, pl.BlockSpec((2, GATE_TC), lambda j, i: (0, j))],
        out_shape=[jax.ShapeDtypeStruct((s, d), BF16)] * 4 + [jax.ShapeDtypeStruct((2, d), F32)],
        compiler_params=_params(("parallel", "arbitrary")),
    )(proj, proj, gate_b, up_dil, up_sb, dmixed)
    return dud, dus, jnp.concatenate([dga, dgb], axis=1), db


def _adamw(w, g, m, v, *, name):
    shape = w.shape
    cols = shape[-1]
    rows = w.size // cols
    tr = _pick(rows, max(8, (512 * 1024) // cols))
    if rows % tr or (tr % 8 and tr != rows):
        tr = rows
    c1 = 1.0 - ADAM_B1 ** ADAM_STEP
    c2 = 1.0 - ADAM_B2 ** ADAM_STEP

    def body(w_ref, g_ref, m_ref, v_ref, d_ref, nm_ref, nv_ref):
        gv = g_ref[...]
        nm = ADAM_B1 * m_ref[...] + (1.0 - ADAM_B1) * gv
        nv = ADAM_B2 * v_ref[...] + (1.0 - ADAM_B2) * (gv * gv)
        d_ref[...] = -ADAM_LR * ((nm / c1) / (jnp.sqrt(nv / c2) + ADAM_EPS) + ADAM_WD * w_ref[...])
        nm_ref[...] = nm
        nv_ref[...] = nv

    blk = pl.BlockSpec((tr, cols), lambda i: (i, 0))
    outs = pl.pallas_call(
        body, name=name, grid=(rows // tr,),
        in_specs=[blk] * 4, out_specs=[blk] * 3,
        out_shape=[jax.ShapeDtypeStruct((rows, cols), F32)] * 3,
        compiler_params=_params(("parallel",)),
    )(*(t.reshape(rows, cols) for t in (w, g, m, v)))
    return tuple(t.reshape(shape) for t in outs)


def _add_half(full, other, which, *, name):
    n, rh, c = other.shape
    tr = _pick(rh, max(8, (1024 * 1024) // c))
    nb = rh // tr

    def body(which_ref, a_ref, b_ref, o_ref):
        o_ref[...] = a_ref[...] + b_ref[...]

    grid_spec = pltpu.PrefetchScalarGridSpec(
        num_scalar_prefetch=1, grid=(n, nb),
        in_specs=[pl.BlockSpec((None, tr, c), lambda s, i, w: (s, w[0] * nb + i, 0)),
                  pl.BlockSpec((None, tr, c), lambda s, i, w: (s, i, 0))],
        out_specs=pl.BlockSpec((None, tr, c), lambda s, i, w: (s, i, 0)),
    )
    return pl.pallas_call(
        body, name=name, grid_spec=grid_spec,
        out_shape=jax.ShapeDtypeStruct((n, rh, c), F32),
        compiler_params=_params(("parallel", "parallel")),
    )(which, full, other)


def _sum_chips(parts, *, name):
    n, r, c = parts.shape
    tr = _pick(r, max(8, (1024 * 1024) // c))

    def body(p_ref, o_ref):
        acc = p_ref[0]
        for t in range(1, n):
            acc = acc + p_ref[t]
        o_ref[...] = acc

    return pl.pallas_call(
        body, name=name, grid=(r // tr,),
        in_specs=[pl.BlockSpec((n, tr, c), lambda i: (0, i, 0))],
        out_specs=pl.BlockSpec((tr, c), lambda i: (i, 0)),
        out_shape=jax.ShapeDtypeStruct((r, c), F32),
        compiler_params=_params(("parallel",)),
    )(parts)


ANY = pl.BlockSpec(memory_space=pl.ANY)


def _place():
    x, y, c = lax.axis_index("x"), lax.axis_index("y"), lax.axis_index("c")
    chips = [(1 - x, y), (x, 1 - y), (1 - x, 1 - y)]
    return x, y, c, chips


def _remote(src, dst, send_sem, recv_sem, to):
    return pltpu.make_async_remote_copy(src_ref=src, dst_ref=dst, send_sem=send_sem, recv_sem=recv_sem,
                                        device_id=to, device_id_type=MESH)


def _gather_weights(shards):
    n = len(shards)

    def body(*refs):
        ins, outs = refs[:n], refs[n:2 * n]
        send_sems, recv_sems, fwd_send, fwd_recv, local_sems = refs[2 * n:]
        x, y, c, chips = _place()
        me = 2 * x + y
        sibling = (x, y, 1 - c)

        def half(ref, chip, which):
            rh = ref.shape[2] // 2
            return ref.at[:, chip, pl.ds(which * rh, rh), :]

        def src_half(ref, which):
            rh = ref.shape[1] // 2
            return ref.at[:, pl.ds(which * rh, rh), :]

        local = []
        for t in range(n):
            cp = pltpu.make_async_copy(ins[t], outs[t].at[:, me], local_sems.at[t])
            cp.start()
            local.append(cp)
        first = []
        for t in range(n):
            for k, (px, py) in enumerate(chips):
                cp = _remote(src_half(ins[t], c), half(outs[t], me, c), send_sems.at[t, k], recv_sems.at[t, k],
                             (px, py, c))
                cp.start()
                first.append(cp)
        passed = []
        for t in range(n):
            for k, (px, py) in enumerate(chips):
                theirs = 2 * px + py
                landed = half(outs[t], theirs, c)
                _remote(landed, landed, send_sems.at[t, k], recv_sems.at[t, k], (px, py, c)).wait_recv()
                cp = _remote(landed, landed, fwd_send.at[t, k], fwd_recv.at[t, k], sibling)
                cp.start()
                passed.append(cp)
        for t in range(n):
            for k, (px, py) in enumerate(chips):
                theirs = 2 * px + py
                other = half(outs[t], theirs, 1 - c)
                _remote(other, other, fwd_send.at[t, k], fwd_recv.at[t, k], sibling).wait_recv()
        for cp in first + passed:
            cp.wait_send()
        for cp in local:
            cp.wait()

    return pl.pallas_call(
        body, name="gather_weights",
        in_specs=[ANY] * n, out_specs=[ANY] * n,
        out_shape=[jax.ShapeDtypeStruct((w.shape[0], N_CHIPS) + w.shape[1:], w.dtype) for w in shards],
        scratch_shapes=[pltpu.SemaphoreType.DMA((n, 3)), pltpu.SemaphoreType.DMA((n, 3)),
                        pltpu.SemaphoreType.DMA((n, 3)), pltpu.SemaphoreType.DMA((n, 3)),
                        pltpu.SemaphoreType.DMA((n,))],
        compiler_params=pltpu.CompilerParams(has_side_effects=True),
    )(*shards)


def _pair_swap_halves(grads, *, name):
    n = len(grads)

    def body(*refs):
        ins, outs = refs[:n], refs[n:2 * n]
        send_sems, recv_sems = refs[2 * n:]
        x, y, c, _ = _place()
        sibling = (x, y, 1 - c)
        cps = []
        for t in range(n):
            rh = ins[t].shape[1] // 2
            cp = _remote(ins[t].at[:, pl.ds((1 - c) * rh, rh), :], outs[t], send_sems.at[t], recv_sems.at[t], sibling)
            cp.start()
            cps.append(cp)
        for cp in cps:
            cp.wait()

    return pl.pallas_call(
        body, name=name,
        in_specs=[ANY] * n, out_specs=[ANY] * n,
        out_shape=[jax.ShapeDtypeStruct((g.shape[0], g.shape[1] // 2, g.shape[2]), g.dtype) for g in grads],
        scratch_shapes=[pltpu.SemaphoreType.DMA((n,)), pltpu.SemaphoreType.DMA((n,))],
        compiler_params=pltpu.CompilerParams(has_side_effects=True),
    )(*grads)


def _chip_exchange(parts, *, name):
    n = len(parts)

    def body(*refs):
        ins, outs = refs[:n], refs[n:2 * n]
        send_sems, recv_sems, local_sems = refs[2 * n:]
        x, y, c, chips = _place()
        me = 2 * x + y
        cps, local = [], []
        for t in range(n):
            cp = pltpu.make_async_copy(ins[t].at[me], outs[t].at[me], local_sems.at[t])
            cp.start()
            local.append(cp)
            for k, (px, py) in enumerate(chips):
                cp = _remote(ins[t].at[2 * px + py], outs[t].at[me], send_sems.at[t, k], recv_sems.at[t, k], (px, py, c))
                cp.start()
                cps.append(cp)
        for t in range(n):
            for k, (px, py) in enumerate(chips):
                theirs = outs[t].at[2 * px + py]
                _remote(theirs, theirs, send_sems.at[t, k], recv_sems.at[t, k], (px, py, c)).wait_recv()
        for cp in cps:
            cp.wait_send()
        for cp in local:
            cp.wait()

    return pl.pallas_call(
        body, name=name,
        in_specs=[ANY] * n, out_specs=[ANY] * n,
        out_shape=[jax.ShapeDtypeStruct(p.shape, p.dtype) for p in parts],
        scratch_shapes=[pltpu.SemaphoreType.DMA((n, 3)), pltpu.SemaphoreType.DMA((n, 3)), pltpu.SemaphoreType.DMA((n,))],
        compiler_params=pltpu.CompilerParams(has_side_effects=True),
    )(*parts)


def _pair_join_halves(halves, *, name):
    n = len(halves)

    def body(*refs):
        ins, outs = refs[:n], refs[n:2 * n]
        send_sems, recv_sems, local_sems = refs[2 * n:]
        x, y, c, _ = _place()
        sibling = (x, y, 1 - c)
        cps = []
        for t in range(n):
            rh = ins[t].shape[0]
            mine = outs[t].at[pl.ds(c * rh, rh), :]
            lc = pltpu.make_async_copy(ins[t], mine, local_sems.at[t])
            lc.start()
            cp = _remote(ins[t], mine, send_sems.at[t], recv_sems.at[t], sibling)
            cp.start()
            cps.append((lc, cp))
        for t, (lc, cp) in enumerate(cps):
            rh = ins[t].shape[0]
            theirs = outs[t].at[pl.ds((1 - c) * rh, rh), :]
            _remote(theirs, theirs, send_sems.at[t], recv_sems.at[t], sibling).wait_recv()
            cp.wait_send()
            lc.wait()

    return pl.pallas_call(
        body, name=name,
        in_specs=[ANY] * n, out_specs=[ANY] * n,
        out_shape=[jax.ShapeDtypeStruct((2 * h.shape[0], h.shape[1]), h.dtype) for h in halves],
        scratch_shapes=[pltpu.SemaphoreType.DMA((n,)), pltpu.SemaphoreType.DMA((n,)), pltpu.SemaphoreType.DMA((n,))],
        compiler_params=pltpu.CompilerParams(has_side_effects=True),
    )(*halves)


def _exchange_small(buf, *, name):
    rows, lanes = buf.shape

    def body(in_ref, out_ref, send_sems, recv_sems, local_sem):
        x, y, c = lax.axis_index("x"), lax.axis_index("y"), lax.axis_index("c")
        me = 4 * x + 2 * y + c
        lc = pltpu.make_async_copy(in_ref, out_ref.at[me], local_sem)
        lc.start()
        cps = []
        for k in range(1, 8):
            fx, fy, fc = (k >> 2) & 1, (k >> 1) & 1, k & 1
            to = (x ^ fx, y ^ fy, c ^ fc)
            cp = _remote(in_ref, out_ref.at[me], send_sems.at[k - 1], recv_sems.at[k - 1], to)
            cp.start()
            cps.append(cp)
        for k in range(1, 8):
            fx, fy, fc = (k >> 2) & 1, (k >> 1) & 1, k & 1
            theirs = out_ref.at[4 * (x ^ fx) + 2 * (y ^ fy) + (c ^ fc)]
            _remote(theirs, theirs, send_sems.at[k - 1], recv_sems.at[k - 1], (x ^ fx, y ^ fy, c ^ fc)).wait_recv()
        for cp in cps:
            cp.wait_send()
        lc.wait()

    return pl.pallas_call(
        body, name=name,
        in_specs=[ANY], out_specs=ANY,
        out_shape=jax.ShapeDtypeStruct((8, rows, lanes), buf.dtype),
        scratch_shapes=[pltpu.SemaphoreType.DMA((7,)), pltpu.SemaphoreType.DMA((7,)), pltpu.SemaphoreType.DMA],
        compiler_params=pltpu.CompilerParams(has_side_effects=True),
    )(buf)


def _sum_slots(slots, *, name):
    n, rows, lanes = slots.shape

    def body(s_ref, o_ref):
        acc = s_ref[0]
        for t in range(1, n):
            acc = acc + s_ref[t]
        o_ref[...] = acc

    return pl.pallas_call(
        body, name=name,
        out_shape=jax.ShapeDtypeStruct((rows, lanes), slots.dtype),
    )(slots)


def _relu2(acc):
    r = jnp.maximum(acc, 0.0)
    return acc, r * r


def _layer_fwd(x, w, cos, sin, tag):
    h = _rmsnorm_fwd(x, w["norm1_g"], name=f"norm1_{tag}")
    proj = _matmul(h, w["w_in"], mode="nn", b_sharded=True, tm=512, name=f"proj_{tag}")
    qkv = _qkv_prep(proj, w["q_norm_g"], w["k_norm_g"], cos, sin, name=f"qkv_prep_{tag}")
    outs, lds = [], []
    for g in range(N_GROUPS):
        o_g, ld_g = _dil_attn_fwd(qkv, g, name=f"dil{g}_fwd_{tag}")
        outs.append(o_g)
        lds.append(ld_g)
    y, yb, lse = _dil_merge(outs, lds, name=f"dil_merge_{tag}")
    o_sb, rc = _sb_attn_fwd(qkv, name=f"sb_fwd_{tag}")
    up_dil = _matmul(yb, w["w_up_dil"], mode="nn", b_sharded=True, name=f"up_dil_{tag}")
    up_sb = _matmul(o_sb, w["w_up_sb"], mode="nn", b_sharded=True, name=f"up_sb_{tag}")
    mixed = _gate_fwd(proj, w["gate_b"], up_dil, up_sb, name=f"gate_{tag}")
    x1 = _matmul(mixed, w["w_out"], mode="nn", extras=(x,), epilogue=lambda acc, res: (acc + res,), name=f"out_{tag}")
    h2 = _rmsnorm_fwd(x1, w["norm2_g"], name=f"norm2_{tag}")
    u, a = _matmul(h2, w["w_ff1"], mode="nn", b_sharded=True, out_dtypes=(F32, BF16), epilogue=_relu2,
                   name=f"ff1_{tag}")
    x2 = _matmul(a, w["w_ff2"], mode="nn", extras=(x1,), epilogue=lambda acc, res: (acc + res,), name=f"ff2_{tag}")
    saved = dict(x=x, h=h, proj=proj, qkv=qkv, y=y, yb=yb, lse=lse, o_sb=o_sb, rc=rc, up_dil=up_dil, up_sb=up_sb,
                 mixed=mixed, x1=x1, h2=h2, u=u, a=a)
    return x2, saved


def _layer_bwd(dx, dxb, w, sv, cos, sin, tag):
    grads = {}
    du = _matmul(dxb, w["w_ff2"], mode="nt", extras=(sv["u"],), out_dtypes=(BF16,),
                 epilogue=lambda acc, u: (acc * (2.0 * jnp.maximum(u, 0.0)),), name=f"d_a_{tag}")
    grads["w_ff2"] = _matmul(sv["a"], dxb, mode="tn", name=f"dw_ff2_{tag}")
    dh2 = _matmul(du, w["w_ff1"], mode="nt", b_sharded=True, name=f"d_h2_{tag}")
    grads["w_ff1"] = _matmul(sv["h2"], du, mode="tn", out_sharded=True, name=f"dw_ff1_{tag}")
    dx1, dx1b, grads["norm2_g"] = _rmsnorm_bwd(sv["x1"], w["norm2_g"], dh2, dx, name=f"norm2_bwd_{tag}")
    dmixed = _matmul(dx1b, w["w_out"], mode="nt", name=f"d_mixed_{tag}")
    grads["w_out"] = _matmul(sv["mixed"], dx1b, mode="tn", name=f"dw_out_{tag}")
    dud, dus, dgate, grads["gate_b"] = _gate_bwd(sv["proj"], w["gate_b"], sv["up_dil"], sv["up_sb"], dmixed,
                                                 name=f"gate_bwd_{tag}")
    dy_dil = _matmul(dud, w["w_up_dil"], mode="nt", b_sharded=True, name=f"d_ydil_{tag}")
    grads["w_up_dil"] = _matmul(sv["yb"], dud, mode="tn", out_sharded=True, name=f"dw_up_dil_{tag}")
    dy_sb = _matmul(dus, w["w_up_sb"], mode="nt", b_sharded=True, name=f"d_ysb_{tag}")
    grads["w_up_sb"] = _matmul(sv["o_sb"], dus, mode="tn", out_sharded=True, name=f"dw_up_sb_{tag}")
    d_dil = [_dil_attn_bwd(sv["qkv"], sv["y"], sv["lse"], dy_dil, g, name=f"dil{g}_bwd_{tag}") for g in range(N_GROUPS)]
    d_sb = _sb_attn_bwd(sv["qkv"], sv["rc"], dy_sb, name=f"sb_bwd_{tag}")
    dproj, grads["q_norm_g"], grads["k_norm_g"] = _qkv_prep_bwd(
        sv["proj"], w["q_norm_g"], w["k_norm_g"], cos, sin, d_dil, d_sb, dgate, name=f"qkv_prep_bwd_{tag}")
    dh = _matmul(dproj, w["w_in"], mode="nt", b_sharded=True, tm=512, name=f"d_h_{tag}")
    grads["w_in"] = _matmul(sv["h"], dproj, mode="tn", out_sharded=True, tm=512, name=f"dw_in_{tag}")
    dx0, dx0b, grads["norm1_g"] = _rmsnorm_bwd(sv["x"], w["norm1_g"], dh, dx1, name=f"norm1_bwd_{tag}")
    return dx0, dx0b, grads


def _local_step(x, target, layers, cos, sin):
    saved = []
    for l, w in enumerate(layers):
        x, sv = _layer_fwd(x, w, cos, sin, f"l{l}")
        saved.append(sv)
    loss, dx, dxb = _loss_head(x, target)
    grads = [None] * len(layers)
    for l in reversed(range(len(layers))):
        dx, dxb, grads[l] = _layer_bwd(dx, dxb, layers[l], saved[l], cos, sin, f"l{l}")
    return loss, dx, grads


BIG = ("w_in", "w_up_dil", "w_up_sb", "w_out", "w_ff1", "w_ff2")
ROW_SHARDED = ("w_out", "w_ff2")
WEIGHTS = ("norm1_g", "w_in", "q_norm_g", "k_norm_g", "w_up_dil", "w_up_sb", "gate_b", "w_out", "norm2_g", "w_ff1", "w_ff2")


def _reduce_to_shards(grads, which, tag):
    names = list(grads)
    full = [grads[n] for n in names]
    theirs = _pair_swap_halves(full, name=f"rs_pair_{tag}")
    pair = [_add_half(f, t, which, name=f"rs_pair_sum_{n}_{tag}") for n, f, t in zip(names, full, theirs)]
    parts = _chip_exchange(pair, name=f"rs_chips_{tag}")
    halves = [_sum_chips(p, name=f"rs_chip_sum_{n}_{tag}") for n, p in zip(names, parts)]
    joined = _pair_join_halves(halves, name=f"rs_join_{tag}")
    return dict(zip(names, joined))


def _pack_rows(vecs):
    rows, spans, at = [], [], 0
    for v in vecs:
        r = v.size // 128
        padded = -(-r // 8) * 8
        rows.append(jnp.pad(v.reshape(r, 128), ((0, padded - r), (0, 0))))
        spans.append((at, r))
        at += padded
    return jnp.concatenate(rows, axis=0), spans


def kernel(x, norm1_g, w_in, q_norm_g, k_norm_g, w_up_dil, w_up_sb, gate_b, w_out, norm2_g, w_ff1, w_ff2, loss_target, m_norm1_g, m_w_in, m_q_norm_g, m_k_norm_g, m_w_up_dil, m_w_up_sb, m_gate_b, m_w_out, m_norm2_g, m_w_ff1, m_w_ff2, v_norm1_g, v_w_in, v_q_norm_g, v_k_norm_g, v_w_up_dil, v_w_up_sb, v_gate_b, v_w_out, v_norm2_g, v_w_ff1, v_w_ff2):
    weights = dict(norm1_g=norm1_g, w_in=w_in, q_norm_g=q_norm_g, k_norm_g=k_norm_g, w_up_dil=w_up_dil,
                   w_up_sb=w_up_sb, gate_b=gate_b, w_out=w_out, norm2_g=norm2_g, w_ff1=w_ff1, w_ff2=w_ff2)
    moments_m = dict(norm1_g=m_norm1_g, w_in=m_w_in, q_norm_g=m_q_norm_g, k_norm_g=m_k_norm_g, w_up_dil=m_w_up_dil,
                     w_up_sb=m_w_up_sb, gate_b=m_gate_b, w_out=m_w_out, norm2_g=m_norm2_g, w_ff1=m_w_ff1, w_ff2=m_w_ff2)
    moments_v = dict(norm1_g=v_norm1_g, w_in=v_w_in, q_norm_g=v_q_norm_g, k_norm_g=v_k_norm_g, w_up_dil=v_w_up_dil,
                     w_up_sb=v_w_up_sb, gate_b=v_gate_b, w_out=v_w_out, norm2_g=v_norm2_g, w_ff1=v_w_ff1, w_ff2=v_w_ff2)
    depth = w_in.shape[0]
    seq, d_model = x.shape[1], x.shape[2]
    chip = 2 * lax.axis_index("x") + lax.axis_index("y")
    core = lax.axis_index("c")
    which = jnp.reshape(core, (1,)).astype(jnp.int32)

    gathered = dict(zip(BIG, _gather_weights([weights[n].astype(BF16) for n in BIG])))
    bias_buf, ((_, bias_rows),) = _pack_rows([gate_b])
    bias_slots = _exchange_small(bias_buf, name="gather_gate_b")
    bias = bias_slots[0::2, :bias_rows].reshape(N_CHIPS, depth, 2, d_model // N_CHIPS)
    bias = jnp.transpose(bias, (1, 2, 0, 3)).reshape(depth, 2, d_model)
    layers = []
    for l in range(depth):
        w = {n: gathered[n][l] for n in BIG}
        for n in ROW_SHARDED:
            w[n] = w[n].reshape(-1, w[n].shape[-1])
        w.update(norm1_g=norm1_g[l], norm2_g=norm2_g[l], q_norm_g=q_norm_g[l], k_norm_g=k_norm_g[l], gate_b=bias[l])
        layers.append(w)

    cos, sin = _rope_tables(seq)
    loss, dx, grads = _local_step(x[0], loss_target[0], layers, cos, sin)
    loss = lax.psum(loss, ("x", "y", "c"))

    reduced = []
    for l in range(depth):
        g = {n: grads[l][n] for n in BIG}
        for n in ROW_SHARDED:
            g[n] = g[n].reshape(N_CHIPS, -1, g[n].shape[-1])
        reduced.append(_reduce_to_shards(g, which, f"l{l}"))
    final = {n: jnp.stack([reduced[l][n] for l in range(depth)]) for n in BIG}

    small_names = ("norm1_g", "norm2_g", "q_norm_g", "k_norm_g", "gate_b")
    packed, spans = _pack_rows([grads[l][n] for l in range(depth) for n in small_names])
    total = _sum_slots(_exchange_small(packed, name="exchange_small_grads"), name="sum_small_grads")
    pieces = iter(total[at:at + r] for at, r in spans)
    small = {n: [] for n in small_names}
    for l in range(depth):
        for n in small_names:
            small[n].append(next(pieces))
    for n in ("norm1_g", "norm2_g"):
        final[n] = jnp.stack([p.reshape(d_model) for p in small[n]])
    for n in ("q_norm_g", "k_norm_g"):
        final[n] = jnp.stack([p.reshape(N_GROUPS, HEAD_DIM) for p in small[n]])
    shard_cols = d_model // N_CHIPS
    final["gate_b"] = jnp.stack([lax.dynamic_slice_in_dim(p.reshape(2, d_model), chip * shard_cols, shard_cols, axis=1)
                                 for p in small["gate_b"]])

    deltas, new_m, new_v = {}, {}, {}
    for n in WEIGHTS:
        deltas[n], new_m[n], new_v[n] = _adamw(weights[n], final[n], moments_m[n], moments_v[n], name=f"adamw_{n}")
    return (loss, dx[None], *[final[n] for n in WEIGHTS], *[deltas[n] for n in WEIGHTS],
            *[new_m[n] for n in WEIGHTS], *[new_v[n] for n in WEIGHTS])
```

```python
import functools
import math

import jax
import jax.numpy as jnp
from jax import lax
from jax.experimental import pallas as pl
from jax.experimental.pallas import tpu as pltpu

F32 = jnp.float32
BF16 = jnp.bfloat16

HEAD_DIM = 128
DIL_GROUPS = ((128, 1), (512, 4), (2048, 16))
N_GROUPS = 3
HEADS_PER_GROUP = 4
GROUP_WIDTH = HEADS_PER_GROUP * HEAD_DIM
DIL_WIDTH = N_GROUPS * GROUP_WIDTH
SB_HEADS = 8
SB_WIDTH = SB_HEADS * HEAD_DIM
QKV_WIDTH = 3 * DIL_WIDTH + 3 * SB_WIDTH
BLOCK = 128
ROPE_THETA = 10000.0
EPS = 1e-6
SCALE = 1.0 / math.sqrt(HEAD_DIM)
NEG_INF = float("-inf")

ADAM_LR = 0.001
ADAM_B1 = 0.9
ADAM_B2 = 0.999
ADAM_EPS = 1e-08
ADAM_WD = 0.01
ADAM_STEP = 10

N_CHIPS = 4
MESH = pl.DeviceIdType.MESH
MIB = 1024 * 1024
VMEM_LIMIT = 56 * MIB


def _params(semantics=None, vmem=VMEM_LIMIT):
    return pltpu.CompilerParams(dimension_semantics=semantics, vmem_limit_bytes=vmem)


def _pick(n, pref):
    if n <= pref:
        return n
    t = 1 << (pref.bit_length() - 1)
    while n % t:
        t //= 2
    return t


def _tile(n, pref):
    best = 0
    for t in range(256, min(n, pref) + 1, 256):
        if n % t == 0:
            best = t
    return best if best else n


def _matmul(a, b, *, mode, name, out_dtypes=(F32,), extras=(), epilogue=None, b_sharded=False, out_sharded=False,
            tm=1024, tn=1024, tk=512):
    if mode == "nn":
        m, kdim = a.shape
        n = b.shape[-1] * (N_CHIPS if b_sharded else 1)
        assert (b.shape[-2] if b_sharded else b.shape[0]) == kdim
    elif mode == "nt":
        m, kdim = a.shape
        n = b.shape[-2]
        assert b.shape[-1] * (N_CHIPS if b_sharded else 1) == kdim
    else:
        kdim, m = a.shape
        n = b.shape[1]
        assert b.shape[0] == kdim and not b_sharded
    shard_n = n // N_CHIPS if (out_sharded or (b_sharded and mode == "nn")) else n
    shard_k = kdim // N_CHIPS if (b_sharded and mode == "nt") else kdim
    tm = _tile(m, tm)
    tn = _tile(shard_n, tn)
    tk = _tile(shard_k, tk)
    nk = kdim // tk
    grid = (m // tm, n // tn, nk)
    nj_n = shard_n // tn
    nj_k = shard_k // tk

    if mode == "nn":
        a_spec = pl.BlockSpec((tm, tk), lambda i, j, k: (i, k))
        if b_sharded:
            b_spec = pl.BlockSpec((None, tk, tn), lambda i, j, k: (j // nj_n, k, j % nj_n))
        else:
            b_spec = pl.BlockSpec((tk, tn), lambda i, j, k: (k, j))
        dims = (((1,), (0,)), ((), ()))
    elif mode == "nt":
        a_spec = pl.BlockSpec((tm, tk), lambda i, j, k: (i, k))
        if b_sharded:
            b_spec = pl.BlockSpec((None, tn, tk), lambda i, j, k: (k // nj_k, j, k % nj_k))
        else:
            b_spec = pl.BlockSpec((tn, tk), lambda i, j, k: (j, k))
        dims = (((1,), (1,)), ((), ()))
    else:
        a_spec = pl.BlockSpec((tk, tm), lambda i, j, k: (k, i))
        b_spec = pl.BlockSpec((tk, tn), lambda i, j, k: (k, j))
        dims = (((0,), (0,)), ((), ()))
    if out_sharded:
        o_spec = pl.BlockSpec((None, tm, tn), lambda i, j, k: (j // nj_n, i, j % nj_n))
        o_shape = (N_CHIPS, m, shard_n)
    else:
        o_spec = pl.BlockSpec((tm, tn), lambda i, j, k: (i, j))
        o_shape = (m, n)
    x_spec = pl.BlockSpec((tm, tn), lambda i, j, k: (i, j))
    n_extra = len(extras)
    n_out = len(out_dtypes)

    def body(a_ref, b_ref, *rest):
        extra_refs = rest[:n_extra]
        out_refs = rest[n_extra:n_extra + n_out]
        acc = rest[-1]
        k = pl.program_id(2)

        @pl.when(k == 0)
        def _():
            acc[...] = jnp.zeros_like(acc)

        acc[...] += lax.dot_general(a_ref[...], b_ref[...], dims, preferred_element_type=F32)

        @pl.when(k == nk - 1)
        def _():
            vals = [e[...] for e in extra_refs]
            res = (acc[...],) if epilogue is None else epilogue(acc[...], *vals)
            for o, r in zip(out_refs, res):
                o[...] = r.astype(o.dtype)

    outs = pl.pallas_call(
        body,
        name=name,
        grid=grid,
        in_specs=[a_spec, b_spec] + [x_spec] * n_extra,
        out_specs=[o_spec] * n_out,
        out_shape=[jax.ShapeDtypeStruct(o_shape, dt) for dt in out_dtypes],
        scratch_shapes=[pltpu.VMEM((tm, tn), F32)],
        compiler_params=_params(("parallel", "parallel", "arbitrary")),
    )(a, b, *extras)
    return outs[0] if n_out == 1 else outs


def _rmsnorm_fwd(x, g, *, name):
    s, d = x.shape
    tr = _pick(s, 256)

    def body(x_ref, g_ref, o_ref):
        xv = x_ref[...]
        r = lax.rsqrt(jnp.mean(xv * xv, axis=-1, keepdims=True) + EPS)
        o_ref[...] = (xv * r * g_ref[...]).astype(o_ref.dtype)

    return pl.pallas_call(
        body, name=name, grid=(s // tr,),
        in_specs=[pl.BlockSpec((tr, d), lambda i: (i, 0)), pl.BlockSpec((1, d), lambda i: (0, 0))],
        out_specs=pl.BlockSpec((tr, d), lambda i: (i, 0)),
        out_shape=jax.ShapeDtypeStruct((s, d), BF16),
        compiler_params=_params(("parallel",)),
    )(x, g.reshape(1, d))


def _rmsnorm_bwd(x, g, dh, dres, *, name):
    s, d = x.shape
    tr = _pick(s, 256)

    def body(x_ref, g_ref, dh_ref, dres_ref, dx_ref, dxb_ref, dg_ref):
        xv = x_ref[...]
        r = lax.rsqrt(jnp.mean(xv * xv, axis=-1, keepdims=True) + EPS)
        y = xv * r
        dhv = dh_ref[...]
        dy = dhv * g_ref[...]
        dx = dres_ref[...] + r * (dy - y * jnp.mean(dy * y, axis=-1, keepdims=True))
        dx_ref[...] = dx
        dxb_ref[...] = dx.astype(BF16)

        @pl.when(pl.program_id(0) == 0)
        def _():
            dg_ref[...] = jnp.zeros_like(dg_ref)

        dg_ref[...] += jnp.sum(dhv * y, axis=0, keepdims=True)

    row = pl.BlockSpec((tr, d), lambda i: (i, 0))
    vec = pl.BlockSpec((1, d), lambda i: (0, 0))
    dx, dxb, dg = pl.pallas_call(
        body, name=name, grid=(s // tr,),
        in_specs=[row, vec, row, row],
        out_specs=[row, row, vec],
        out_shape=[jax.ShapeDtypeStruct((s, d), F32), jax.ShapeDtypeStruct((s, d), BF16),
                   jax.ShapeDtypeStruct((1, d), F32)],
        compiler_params=_params(("arbitrary",)),
    )(x, g.reshape(1, d), dh, dres)
    return dx, dxb, dg.reshape(d)


def _loss_head(y, target):
    s, d = y.shape
    tr = _pick(s, 256)

    def body(y_ref, t_ref, dy_ref, dyb_ref, part_ref):
        err = y_ref[...] - t_ref[...]
        dy = err * (1.0 / d)
        dy_ref[...] = dy
        dyb_ref[...] = dy.astype(BF16)

        @pl.when(pl.program_id(0) == 0)
        def _():
            part_ref[...] = jnp.zeros_like(part_ref)

        part_ref[...] += jnp.sum(err * err, axis=0, keepdims=True)

    row = pl.BlockSpec((tr, d), lambda i: (i, 0))
    vec = pl.BlockSpec((1, d), lambda i: (0, 0))
    dy, dyb, part = pl.pallas_call(
        body, name="loss_head", grid=(s // tr,),
        in_specs=[row, row], out_specs=[row, row, vec],
        out_shape=[jax.ShapeDtypeStruct((s, d), F32), jax.ShapeDtypeStruct((s, d), BF16),
                   jax.ShapeDtypeStruct((1, d), F32)],
        compiler_params=_params(("arbitrary",)),
    )(y, target)
    return 0.5 * jnp.sum(part) / d, dy, dyb


def _rope_tables(s):
    half = HEAD_DIM // 2
    inv_freq = ROPE_THETA ** (-jnp.arange(half, dtype=F32) / half)
    ang = jnp.arange(s, dtype=F32)[:, None] * inv_freq[None, :]
    cos, sin = jnp.cos(ang), jnp.sin(ang)
    return jnp.concatenate([cos, cos], axis=-1), jnp.concatenate([-sin, sin], axis=-1)


def _qkv_prep(proj, qg, kg, cos, sin, *, name):
    s = proj.shape[0]
    tr = _pick(s, 256)

    def body(p_ref, qg_ref, kg_ref, cos_ref, sin_ref, o_ref):
        c = cos_ref[...]
        sn = sin_ref[...]
        for which, g_ref in ((0, qg_ref), (1, kg_ref)):
            for g in range(N_GROUPS):
                gain = g_ref[g:g + 1, :]
                for h in range(HEADS_PER_GROUP):
                    off = which * DIL_WIDTH + g * GROUP_WIDTH + h * HEAD_DIM
                    xv = p_ref[:, off:off + HEAD_DIM]
                    r = lax.rsqrt(jnp.mean(xv * xv, axis=-1, keepdims=True) + EPS)
                    y = xv * r * gain
                    o_ref[:, off:off + HEAD_DIM] = (y * c + pltpu.roll(y, HEAD_DIM // 2, 1) * sn).astype(BF16)
        o_ref[:, 2 * DIL_WIDTH:] = p_ref[:, 2 * DIL_WIDTH:].astype(BF16)

    return pl.pallas_call(
        body, name=name, grid=(s // tr,),
        in_specs=[pl.BlockSpec((tr, QKV_WIDTH), lambda i: (i, 0)),
                  pl.BlockSpec((N_GROUPS, HEAD_DIM), lambda i: (0, 0)),
                  pl.BlockSpec((N_GROUPS, HEAD_DIM), lambda i: (0, 0)),
                  pl.BlockSpec((tr, HEAD_DIM), lambda i: (i, 0)),
                  pl.BlockSpec((tr, HEAD_DIM), lambda i: (i, 0))],
        out_specs=pl.BlockSpec((tr, QKV_WIDTH), lambda i: (i, 0)),
        out_shape=jax.ShapeDtypeStruct((s, QKV_WIDTH), BF16),
        compiler_params=_params(("parallel",)),
    )(proj, qg, kg, cos, sin)


def _qkv_prep_bwd(proj, qg, kg, cos, sin, dqkv_dil, dqkv_sb, dgate, *, name):
    s, n_in = proj.shape
    n_gate = dgate.shape[1]
    tr = _pick(s, 256)

    def body(p_ref, qg_ref, kg_ref, cos_ref, sin_ref, *rest):
        dil_refs = rest[:9]
        sb_refs = rest[9:12]
        dgate_ref = rest[12]
        o_ref, dqg_ref, dkg_ref = rest[13:16]
        c = cos_ref[...]
        sn = sin_ref[...]

        @pl.when(pl.program_id(0) == 0)
        def _():
            dqg_ref[...] = jnp.zeros_like(dqg_ref)
            dkg_ref[...] = jnp.zeros_like(dkg_ref)

        for which, g_ref, dg_ref in ((0, qg_ref, dqg_ref), (1, kg_ref, dkg_ref)):
            for g in range(N_GROUPS):
                gain = g_ref[g:g + 1, :]
                d_ref = dil_refs[3 * g + which]
                dgain = jnp.zeros((1, HEAD_DIM), F32)
                for h in range(HEADS_PER_GROUP):
                    off = which * DIL_WIDTH + g * GROUP_WIDTH + h * HEAD_DIM
                    xv = p_ref[:, off:off + HEAD_DIM]
                    r = lax.rsqrt(jnp.mean(xv * xv, axis=-1, keepdims=True) + EPS)
                    nx = xv * r
                    dout = d_ref[:, h * HEAD_DIM:(h + 1) * HEAD_DIM]
                    dy = dout * c + pltpu.roll(dout * sn, HEAD_DIM // 2, 1)
                    dgain = dgain + jnp.sum(dy * nx, axis=0, keepdims=True)
                    dn = dy * gain
                    dxv = r * (dn - nx * jnp.mean(dn * nx, axis=-1, keepdims=True))
                    o_ref[:, off:off + HEAD_DIM] = dxv.astype(BF16)
                dg_ref[g:g + 1, :] += dgain
        for g in range(N_GROUPS):
            off = 2 * DIL_WIDTH + g * GROUP_WIDTH
            o_ref[:, off:off + GROUP_WIDTH] = dil_refs[3 * g + 2][...].astype(BF16)
        for t in range(3):
            off = 3 * DIL_WIDTH + t * SB_WIDTH
            o_ref[:, off:off + SB_WIDTH] = sb_refs[t][...].astype(BF16)
        o_ref[:, QKV_WIDTH:] = dgate_ref[...]

    grp = pl.BlockSpec((tr, GROUP_WIDTH), lambda i: (i, 0))
    sbs = pl.BlockSpec((tr, SB_WIDTH), lambda i: (i, 0))
    gain_spec = pl.BlockSpec((N_GROUPS, HEAD_DIM), lambda i: (0, 0))
    tab = pl.BlockSpec((tr, HEAD_DIM), lambda i: (i, 0))
    flat_dil = [t for grp3 in dqkv_dil for t in grp3]
    return pl.pallas_call(
        body, name=name, grid=(s // tr,),
        in_specs=[pl.BlockSpec((tr, 2 * DIL_WIDTH), lambda i: (i, 0)), gain_spec, gain_spec, tab, tab]
                 + [grp] * 9 + [sbs] * 3 + [pl.BlockSpec((tr, n_gate), lambda i: (i, 0))],
        out_specs=[pl.BlockSpec((tr, n_in), lambda i: (i, 0)), gain_spec, gain_spec],
        out_shape=[jax.ShapeDtypeStruct((s, n_in), BF16),
                   jax.ShapeDtypeStruct((N_GROUPS, HEAD_DIM), F32),
                   jax.ShapeDtypeStruct((N_GROUPS, HEAD_DIM), F32)],
        compiler_params=_params(("arbitrary",)),
    )(proj, qg, kg, cos, sin, *flat_dil, *dqkv_sb, dgate)


def _dil_masks(i):
    row = lax.broadcasted_iota(jnp.int32, (BLOCK, BLOCK), 0)
    col = lax.broadcasted_iota(jnp.int32, (BLOCK, BLOCK), 1)
    return col <= row, (col >= row) & (i > 0)


def _dot_nt(a, b):
    return lax.dot_general(a, b, (((1,), (1,)), ((), ())), preferred_element_type=F32)


def _dot_tn(a, b):
    return lax.dot_general(a, b, (((0,), (0,)), ((), ())), preferred_element_type=F32)


def _dot(a, b):
    return jnp.dot(a, b, preferred_element_type=F32)


def _dil_attn_fwd(qkv, g, *, name):
    s = qkv.shape[0]
    r = DIL_GROUPS[g][1]
    length = s // r
    nb = length // BLOCK
    wb = QKV_WIDTH // GROUP_WIDTH
    view = qkv.reshape(length, r * QKV_WIDTH)

    def body(q_ref, kp_ref, kc_ref, vp_ref, vc_ref, o_ref, ld_ref):
        i = pl.program_id(1)
        mask_c, mask_p = _dil_masks(i)
        for h in range(HEADS_PER_GROUP):
            sl = slice(h * HEAD_DIM, (h + 1) * HEAD_DIM)
            q = q_ref[:, sl]
            s_c = jnp.where(mask_c, _dot_nt(q, kc_ref[:, sl]) * SCALE, NEG_INF)
            s_p = jnp.where(mask_p, _dot_nt(q, kp_ref[:, sl]) * SCALE, NEG_INF)
            m = jnp.maximum(jnp.max(s_c, axis=-1, keepdims=True), jnp.max(s_p, axis=-1, keepdims=True))
            p_c = jnp.exp(s_c - m)
            p_p = jnp.exp(s_p - m)
            l = jnp.sum(p_c, axis=-1, keepdims=True) + jnp.sum(p_p, axis=-1, keepdims=True)
            inv = 1.0 / l
            o = _dot((p_c * inv).astype(BF16), vc_ref[:, sl]) + _dot((p_p * inv).astype(BF16), vp_ref[:, sl])
            o_ref[:, sl] = o
            ld_ref[:, sl] = jnp.broadcast_to(m + jnp.log(l), (BLOCK, HEAD_DIM))

    def col(which):
        return lambda c, i: (i, c * wb + 3 * which + g)

    def col_prev(which):
        return lambda c, i: (jnp.maximum(i - 1, 0), c * wb + 3 * which + g)

    blk = (BLOCK, GROUP_WIDTH)
    o, ld = pl.pallas_call(
        body, name=name, grid=(r, nb),
        in_specs=[pl.BlockSpec(blk, col(0)), pl.BlockSpec(blk, col_prev(1)), pl.BlockSpec(blk, col(1)),
                  pl.BlockSpec(blk, col_prev(2)), pl.BlockSpec(blk, col(2))],
        out_specs=[pl.BlockSpec(blk, lambda c, i: (i, c))] * 2,
        out_shape=[jax.ShapeDtypeStruct((length, r * GROUP_WIDTH), F32)] * 2,
        compiler_params=_params(("parallel", "parallel")),
    )(view, view, view, view, view)
    return o.reshape(s, GROUP_WIDTH), ld.reshape(s, GROUP_WIDTH)


def _dil_merge(outs, lds, *, name):
    s = outs[0].shape[0]
    tr = _pick(s, 512)

    def body(o0, o1, o2, l0, l1, l2, y_ref, yb_ref, lse_ref):
        a0, a1, a2 = l0[...], l1[...], l2[...]
        m = jnp.maximum(jnp.maximum(a0, a1), a2)
        e0, e1, e2 = jnp.exp(a0 - m), jnp.exp(a1 - m), jnp.exp(a2 - m)
        tot = e0 + e1 + e2
        inv = 1.0 / tot
        y = (e0 * inv) * o0[...] + (e1 * inv) * o1[...] + (e2 * inv) * o2[...]
        y_ref[...] = y
        yb_ref[...] = y.astype(BF16)
        lse_ref[...] = m + jnp.log(tot)

    blk = pl.BlockSpec((tr, GROUP_WIDTH), lambda i: (i, 0))
    return pl.pallas_call(
        body, name=name, grid=(s // tr,),
        in_specs=[blk] * 6, out_specs=[blk] * 3,
        out_shape=[jax.ShapeDtypeStruct((s, GROUP_WIDTH), F32), jax.ShapeDtypeStruct((s, GROUP_WIDTH), BF16),
                   jax.ShapeDtypeStruct((s, GROUP_WIDTH), F32)],
        compiler_params=_params(("parallel",)),
    )(*outs, *lds)


def _dil_attn_bwd(qkv, y, lse, dy, g, *, name):
    s = qkv.shape[0]
    r = DIL_GROUPS[g][1]
    length = s // r
    nb = length // BLOCK
    wb = QKV_WIDTH // GROUP_WIDTH
    view = qkv.reshape(length, r * QKV_WIDTH)
    yv, lv, dv_ = (t.reshape(length, r * GROUP_WIDTH) for t in (y, lse, dy))

    def body(q_ref, qn_ref, kp_ref, kc_ref, vp_ref, vc_ref, y_ref, yn_ref, l_ref, ln_ref, d_ref, dn_ref,
             dq_ref, dk_ref, dv_ref):
        i = pl.program_id(1)
        mask_c, mask_p = _dil_masks(i)
        _, mask_n = _dil_masks(jnp.where(i + 1 < nb, 1, 0))
        for h in range(HEADS_PER_GROUP):
            sl = slice(h * HEAD_DIM, (h + 1) * HEAD_DIM)
            q, qn = q_ref[:, sl], qn_ref[:, sl]
            kc, kp, vc, vp = kc_ref[:, sl], kp_ref[:, sl], vc_ref[:, sl], vp_ref[:, sl]
            dyf, dynf = d_ref[:, sl], dn_ref[:, sl]
            dyb, dynb = dyf.astype(BF16), dynf.astype(BF16)
            delta = jnp.sum(dyf * y_ref[:, sl], axis=-1, keepdims=True)
            delta_n = jnp.sum(dynf * yn_ref[:, sl], axis=-1, keepdims=True)
            lse_q, lse_n = l_ref[:, sl], ln_ref[:, sl]
            p_c = jnp.where(mask_c, jnp.exp(_dot_nt(q, kc) * SCALE - lse_q), 0.0)
            p_p = jnp.where(mask_p, jnp.exp(_dot_nt(q, kp) * SCALE - lse_q), 0.0)
            p_n = jnp.where(mask_n, jnp.exp(_dot_nt(qn, kc) * SCALE - lse_n), 0.0)
            ds_c = (p_c * (_dot_nt(dyb, vc) - delta) * SCALE).astype(BF16)
            ds_p = (p_p * (_dot_nt(dyb, vp) - delta) * SCALE).astype(BF16)
            ds_n = (p_n * (_dot_nt(dynb, vc) - delta_n) * SCALE).astype(BF16)
            dq_ref[:, sl] = _dot(ds_c, kc) + _dot(ds_p, kp)
            dk_ref[:, sl] = _dot_tn(ds_c, q) + _dot_tn(ds_n, qn)
            dv_ref[:, sl] = _dot_tn(p_c.astype(BF16), dyb) + _dot_tn(p_n.astype(BF16), dynb)

    def col(which):
        return lambda c, i: (i, c * wb + 3 * which + g)

    def col_prev(which):
        return lambda c, i: (jnp.maximum(i - 1, 0), c * wb + 3 * which + g)

    def col_next(which):
        return lambda c, i: (jnp.minimum(i + 1, nb - 1), c * wb + 3 * which + g)

    blk = (BLOCK, GROUP_WIDTH)
    own = pl.BlockSpec(blk, lambda c, i: (i, c))
    nxt = pl.BlockSpec(blk, lambda c, i: (jnp.minimum(i + 1, nb - 1), c))
    dq, dk, dv = pl.pallas_call(
        body, name=name, grid=(r, nb),
        in_specs=[pl.BlockSpec(blk, col(0)), pl.BlockSpec(blk, col_next(0)),
                  pl.BlockSpec(blk, col_prev(1)), pl.BlockSpec(blk, col(1)),
                  pl.BlockSpec(blk, col_prev(2)), pl.BlockSpec(blk, col(2)),
                  own, nxt, own, nxt, own, nxt],
        out_specs=[own] * 3,
        out_shape=[jax.ShapeDtypeStruct((length, r * GROUP_WIDTH), F32)] * 3,
        compiler_params=_params(("parallel", "parallel")),
    )(view, view, view, view, view, view, yv, yv, lv, lv, dv_, dv_)
    return tuple(t.reshape(s, GROUP_WIDTH) for t in (dq, dk, dv))


SB_TQ = 256
SB_GROUP = 2
SB_NH_FWD = 4
SB_NH_BWD = 2


def _split_dot(x, w):
    hi = x.astype(BF16)
    lo = (x - hi.astype(F32)).astype(BF16)
    return _dot(hi, w) + _dot(lo, w)


def _sb_consts():
    j = lax.broadcasted_iota(jnp.int32, (BLOCK, BLOCK), 0)
    k = lax.broadcasted_iota(jnp.int32, (BLOCK, BLOCK), 1)
    ones = jnp.ones((BLOCK, BLOCK), F32)
    after = jnp.concatenate([(j > k).astype(F32), ones], axis=1).astype(BF16)
    before = jnp.concatenate([(j < k).astype(F32), ones], axis=1).astype(BF16)
    return after, before


def _sb_mask(row0, col0, tq):
    row = row0 + lax.broadcasted_iota(jnp.int32, (tq, BLOCK), 0)
    col = col0 + lax.broadcasted_iota(jnp.int32, (tq, BLOCK), 1)
    return col < row


def _softplus(z):
    return jnp.maximum(z, 0.0) + jnp.log1p(jnp.exp(-jnp.abs(z)))


def _sb_attn_fwd(qkv, *, name):
    s = qkv.shape[0]
    tq = _pick(s, SB_TQ)
    nq = s // tq
    per = tq // BLOCK
    base = 3 * DIL_WIDTH // HEAD_DIM
    after, _ = _sb_consts()

    nh = SB_NH_FWD
    wide = nh * HEAD_DIM

    def body(q_ref, k_ref, v_ref, w_ref, o_ref, rc_ref, acc_ref, run_ref):
        qi = pl.program_id(1)
        w = w_ref[...]
        acc_ref[...] = jnp.zeros_like(acc_ref)
        run_ref[...] = jnp.zeros_like(run_ref)
        rc_ref[...] = jnp.zeros_like(rc_ref)
        lane = lax.broadcasted_iota(jnp.int32, (tq, BLOCK), 1)
        n_chunks = (qi + 1) * per

        def chunks(js, masked):
            sls = [slice(h * HEAD_DIM, (h + 1) * HEAD_DIM) for h in range(nh)]
            work = [(sl, j, pl.multiple_of(j * BLOCK, BLOCK), _sb_mask(qi * tq, j * BLOCK, tq) if masked else None)
                    for j in js for sl in sls]
            zs = [_dot_nt(q_ref[:, sl], k_ref[pl.ds(start, BLOCK), sl]) * SCALE for sl, _, start, _ in work]
            logits, sums = [], []
            for z, (_, _, _, mask) in zip(zs, work):
                sp = _softplus(z)
                logits.append(z - sp)
                sums.append(_split_dot(jnp.where(mask, -sp, 0.0) if masked else -sp, w))
            for (sl, j, start, mask), lg, sm in zip(work, logits, sums):
                run = run_ref[:, sl]
                a = jnp.exp(lg + run + sm[:, :BLOCK])
                if masked:
                    a = jnp.where(mask, a, 0.0)
                acc_ref[:, sl] += _dot(a.astype(BF16), v_ref[pl.ds(start, BLOCK), sl])
                rc_ref[:, sl] = jnp.where(lane == j, run, rc_ref[:, sl])
                run_ref[:, sl] = run + sm[:, BLOCK:]

        def diagonal(t, carry):
            last = n_chunks - 1 - SB_GROUP * t
            chunks([last - g for g in range(SB_GROUP)], True)
            return carry

        def left(t, carry):
            last = n_chunks - 1 - per - SB_GROUP * t
            chunks([last - g for g in range(SB_GROUP)], False)
            return carry

        lax.fori_loop(0, per // SB_GROUP, diagonal, 0)
        lax.fori_loop(0, (n_chunks - per) // SB_GROUP, left, 0)
        o_ref[...] = acc_ref[...].astype(o_ref.dtype)

    head = lambda off: (lambda h, i: (0, (base + off) // nh + h))
    once = pl.Buffered(1)
    o, rc = pl.pallas_call(
        body, name=name, grid=(SB_HEADS // nh, nq),
        in_specs=[pl.BlockSpec((tq, wide), lambda h, i: (i, base // nh + h)),
                  pl.BlockSpec((s, wide), head(SB_HEADS), pipeline_mode=once),
                  pl.BlockSpec((s, wide), head(2 * SB_HEADS), pipeline_mode=once),
                  pl.BlockSpec((BLOCK, 2 * BLOCK), lambda h, i: (0, 0))],
        out_specs=[pl.BlockSpec((tq, wide), lambda h, i: (i, h))] * 2,
        out_shape=[jax.ShapeDtypeStruct((s, SB_WIDTH), BF16), jax.ShapeDtypeStruct((s, SB_WIDTH), F32)],
        scratch_shapes=[pltpu.VMEM((tq, wide), F32), pltpu.VMEM((tq, wide), F32)],
        compiler_params=_params(("parallel", "arbitrary")),
    )(qkv, qkv, qkv, after)
    return o, rc


def _sb_attn_bwd(qkv, rc, do, *, name):
    s = qkv.shape[0]
    tq = _pick(s, SB_TQ)
    nq = s // tq
    per = tq // BLOCK
    base = 3 * DIL_WIDTH // HEAD_DIM
    _, before = _sb_consts()

    nh = SB_NH_BWD
    wide = nh * HEAD_DIM

    def body(q_ref, k_ref, v_ref, w_ref, rc_ref, do_ref, dq_ref, dk_ref, dv_ref, acc_ref, pre_ref, dob_ref):
        qi = pl.program_id(1)

        @pl.when(qi == 0)
        def _():
            dk_ref[...] = jnp.zeros_like(dk_ref)
            dv_ref[...] = jnp.zeros_like(dv_ref)

        w = w_ref[...]
        wa = (lax.broadcasted_iota(jnp.int32, (BLOCK, BLOCK), 0)
              > lax.broadcasted_iota(jnp.int32, (BLOCK, BLOCK), 1)).astype(BF16)
        dob_ref[...] = do_ref[...].astype(BF16)
        acc_ref[...] = jnp.zeros_like(acc_ref)
        pre_ref[...] = jnp.zeros_like(pre_ref)
        lane = lax.broadcasted_iota(jnp.int32, (tq, BLOCK), 1)
        n_chunks = (qi + 1) * per

        def chunks(js, masked):
            sls = [slice(h * HEAD_DIM, (h + 1) * HEAD_DIM) for h in range(nh)]
            work = [(sl, j, pl.multiple_of(j * BLOCK, BLOCK), _sb_mask(qi * tq, j * BLOCK, tq) if masked else None)
                    for j in js for sl in sls]
            zs = [_dot_nt(q_ref[:, sl], k_ref[pl.ds(start, BLOCK), sl]) * SCALE for sl, _, start, _ in work]
            das = [_dot_nt(dob_ref[:, sl], v_ref[pl.ds(start, BLOCK), sl]) for sl, _, start, _ in work]
            sigs, betweens = [], []
            for z, (_, _, _, mask) in zip(zs, work):
                sp = _softplus(z)
                sigs.append(jnp.exp(z - sp))
                betweens.append(_split_dot(jnp.where(mask, -sp, 0.0) if masked else -sp, wa))
            es, avs, sums = [], [], []
            for (sl, j, _, mask), sig, bt, da in zip(work, sigs, betweens, das):
                run = jnp.sum(jnp.where(lane == j, rc_ref[:, sl], 0.0), axis=-1, keepdims=True)
                a = sig * jnp.exp(run + bt)
                if masked:
                    a = jnp.where(mask, a, 0.0)
                e = a * da
                es.append(e)
                avs.append(a.astype(BF16))
                sums.append(_split_dot(e, w))
            for (sl, _, start, mask), sig, e, ab, sm in zip(work, sigs, es, avs, sums):
                pre = pre_ref[:, sl]
                dz = (e * (1.0 - sig) - sig * (pre + sm[:, :BLOCK])) * SCALE
                if masked:
                    dz = jnp.where(mask, dz, 0.0)
                dz = dz.astype(BF16)
                pre_ref[:, sl] = pre + sm[:, BLOCK:]
                acc_ref[:, sl] += _dot(dz, k_ref[pl.ds(start, BLOCK), sl])
                dk_ref[pl.ds(start, BLOCK), sl] += _dot_tn(dz, q_ref[:, sl])
                dv_ref[pl.ds(start, BLOCK), sl] += _dot_tn(ab, dob_ref[:, sl])

        def left(t, carry):
            chunks([SB_GROUP * t + g for g in range(SB_GROUP)], False)
            return carry

        def diagonal(t, carry):
            chunks([n_chunks - per + SB_GROUP * t + g for g in range(SB_GROUP)], True)
            return carry

        lax.fori_loop(0, (n_chunks - per) // SB_GROUP, left, 0)
        lax.fori_loop(0, per // SB_GROUP, diagonal, 0)
        dq_ref[...] = acc_ref[...]

    head = lambda off: (lambda h, i: (0, (base + off) // nh + h))
    blk = pl.BlockSpec((tq, wide), lambda h, i: (i, h))
    whole = pl.BlockSpec((s, wide), lambda h, i: (0, h))
    once = pl.Buffered(1)
    return pl.pallas_call(
        body, name=name, grid=(SB_HEADS // nh, nq),
        in_specs=[pl.BlockSpec((tq, wide), lambda h, i: (i, base // nh + h)),
                  pl.BlockSpec((s, wide), head(SB_HEADS), pipeline_mode=once),
                  pl.BlockSpec((s, wide), head(2 * SB_HEADS), pipeline_mode=once),
                  pl.BlockSpec((BLOCK, 2 * BLOCK), lambda h, i: (0, 0)),
                  blk, blk],
        out_specs=[blk, whole, whole],
        out_shape=[jax.ShapeDtypeStruct((s, SB_WIDTH), F32)] * 3,
        scratch_shapes=[pltpu.VMEM((tq, wide), F32), pltpu.VMEM((tq, wide), F32), pltpu.VMEM((tq, wide), BF16)],
        compiler_params=_params(("parallel", "arbitrary")),
    )(qkv, qkv, qkv, before, rc, do)


GATE_TC = 512


def _gate_fwd(proj, gate_b, up_dil, up_sb, *, name):
    s, d = up_dil.shape
    tr = _pick(s, 512)
    g0 = QKV_WIDTH // GATE_TC
    nc = d // GATE_TC

    def body(ga_ref, gb_ref, b_ref, ud_ref, us_ref, o_ref):
        ga = jax.nn.sigmoid(ga_ref[...] + b_ref[0:1, :])
        gb = jax.nn.sigmoid(gb_ref[...] + b_ref[1:2, :])
        o_ref[...] = (ga * ud_ref[...] + gb * us_ref[...]).astype(o_ref.dtype)

    blk = pl.BlockSpec((tr, GATE_TC), lambda i, j: (i, j))
    return pl.pallas_call(
        body, name=name, grid=(s // tr, nc),
        in_specs=[pl.BlockSpec((tr, GATE_TC), lambda i, j: (i, g0 + j)),
                  pl.BlockSpec((tr, GATE_TC), lambda i, j: (i, g0 + nc + j)),
                  pl.BlockSpec((2, GATE_TC), lambda i, j: (0, j)), blk, blk],
        out_specs=blk,
        out_shape=jax.ShapeDtypeStruct((s, d), BF16),
        compiler_params=_params(("parallel", "parallel")),
    )(proj, proj, gate_b, up_dil, up_sb)


def _gate_bwd(proj, gate_b, up_dil, up_sb, dmixed, *, name):
    s, d = up_dil.shape
    tr = _pick(s, 512)
    g0 = QKV_WIDTH // GATE_TC
    nc = d // GATE_TC

    def body(ga_ref, gb_ref, b_ref, ud_ref, us_ref, dm_ref, dud_ref, dus_ref, dga_ref, dgb_ref, db_ref):
        ga = jax.nn.sigmoid(ga_ref[...] + b_ref[0:1, :])
        gb = jax.nn.sigmoid(gb_ref[...] + b_ref[1:2, :])
        dm = dm_ref[...]
        dud_ref[...] = (dm * ga).astype(BF16)
        dus_ref[...] = (dm * gb).astype(BF16)
        dla = dm * ud_ref[...] * ga * (1.0 - ga)
        dlb = dm * us_ref[...] * gb * (1.0 - gb)
        dga_ref[...] = dla.astype(BF16)
        dgb_ref[...] = dlb.astype(BF16)

        @pl.when(pl.program_id(1) == 0)
        def _():
            db_ref[...] = jnp.zeros_like(db_ref)

        db_ref[0:1, :] += jnp.sum(dla, axis=0, keepdims=True)
        db_ref[1:2, :] += jnp.sum(dlb, axis=0, keepdims=True)

    blk = pl.BlockSpec((tr, GATE_TC), lambda j, i: (i, j))
    dud, dus, dga, dgb, db = pl.pallas_call(
        body, name=name, grid=(nc, s // tr),
        in_specs=[pl.BlockSpec((tr, GATE_TC), lambda j, i: (i, g0 + j)),
                  pl.BlockSpec((tr, GATE_TC), lambda j, i: (i, g0 + nc + j)),
                  pl.BlockSpec((2, GATE_TC), lambda j, i: (0, j)), blk, blk, blk],
        out_specs=[blk, blk, blk, blk, pl.BlockSpec((2, GATE_TC), lambda j, i: (0, j))],
        out_shape=[jax.ShapeDtypeStruct((s, d), BF16)] * 4 + [jax.ShapeDtypeStruct((2, d), F32)],
        compiler_params=_params(("parallel", "arbitrary")),
    )(proj, proj, gate_b, up_dil, up_sb, dmixed)
    return dud, dus, jnp.concatenate([dga, dgb], axis=1), db


def _adamw(w, g, m, v, *, name):
    shape = w.shape
    cols = shape[-1]
    rows = w.size // cols
    tr = _pick(rows, max(8, (512 * 1024) // cols))
    if rows % tr or (tr % 8 and tr != rows):
        tr = rows
    c1 = 1.0 - ADAM_B1 ** ADAM_STEP
    c2 = 1.0 - ADAM_B2 ** ADAM_STEP

    def body(w_ref, g_ref, m_ref, v_ref, d_ref, nm_ref, nv_ref):
        gv = g_ref[...]
        nm = ADAM_B1 * m_ref[...] + (1.0 - ADAM_B1) * gv
        nv = ADAM_B2 * v_ref[...] + (1.0 - ADAM_B2) * (gv * gv)
        d_ref[...] = -ADAM_LR * ((nm / c1) / (jnp.sqrt(nv / c2) + ADAM_EPS) + ADAM_WD * w_ref[...])
        nm_ref[...] = nm
        nv_ref[...] = nv

    blk = pl.BlockSpec((tr, cols), lambda i: (i, 0))
    outs = pl.pallas_call(
        body, name=name, grid=(rows // tr,),
        in_specs=[blk] * 4, out_specs=[blk] * 3,
        out_shape=[jax.ShapeDtypeStruct((rows, cols), F32)] * 3,
        compiler_params=_params(("parallel",)),
    )(*(t.reshape(rows, cols) for t in (w, g, m, v)))
    return tuple(t.reshape(shape) for t in outs)


def _add_half(full, other, which, *, name):
    n, rh, c = other.shape
    tr = _pick(rh, max(8, (1024 * 1024) // c))
    nb = rh // tr

    def body(which_ref, a_ref, b_ref, o_ref):
        o_ref[...] = (a_ref[...] + b_ref[...]).astype(o_ref.dtype)

    grid_spec = pltpu.PrefetchScalarGridSpec(
        num_scalar_prefetch=1, grid=(n, nb),
        in_specs=[pl.BlockSpec((None, tr, c), lambda s, i, w: (s, w[0] * nb + i, 0)),
                  pl.BlockSpec((None, tr, c), lambda s, i, w: (s, i, 0))],
        out_specs=pl.BlockSpec((None, tr, c), lambda s, i, w: (s, i, 0)),
    )
    return pl.pallas_call(
        body, name=name, grid_spec=grid_spec,
        out_shape=jax.ShapeDtypeStruct((n, rh, c), BF16),
        compiler_params=_params(("parallel", "parallel")),
    )(which, full, other)


def _sum_chips(own, parts, order, *, name):
    n, r, c = parts.shape
    tr = _pick(r, max(16, (1024 * 1024) // c))

    def body(order_ref, own_ref, p1_ref, p2_ref, p3_ref, o_ref):
        acc = own_ref[...].astype(F32)
        for p in (p1_ref, p2_ref, p3_ref):
            acc = acc + p[...].astype(F32)
        o_ref[...] = acc

    def slot(k):
        return pl.BlockSpec((None, tr, c), lambda i, o: (o[k], i, 0))

    grid_spec = pltpu.PrefetchScalarGridSpec(
        num_scalar_prefetch=1, grid=(r // tr,),
        in_specs=[slot(0), slot(1), slot(2), slot(3)],
        out_specs=pl.BlockSpec((tr, c), lambda i, o: (i, 0)),
    )
    return pl.pallas_call(
        body, name=name, grid_spec=grid_spec,
        out_shape=jax.ShapeDtypeStruct((r, c), F32),
        compiler_params=_params(("parallel",)),
    )(order, own, parts, parts, parts)


def _place_piece(full, piece, index, *, name):
    a, n, r, c = full.shape
    tr = _pick(r, max(16, (1024 * 1024) // c))

    def body(index_ref, piece_ref, full_ref, o_ref):
        o_ref[...] = piece_ref[...]

    grid_spec = pltpu.PrefetchScalarGridSpec(
        num_scalar_prefetch=1, grid=(a, r // tr),
        in_specs=[pl.BlockSpec((None, tr, c), lambda l, i, k: (l, i, 0)), ANY],
        out_specs=pl.BlockSpec((None, None, tr, c), lambda l, i, k: (l, k[0], i, 0)),
    )
    return pl.pallas_call(
        body, name=name, grid_spec=grid_spec,
        out_shape=jax.ShapeDtypeStruct(full.shape, full.dtype),
        input_output_aliases={2: 0},
        compiler_params=_params(("parallel", "parallel")),
    )(index, piece, full)


ANY = pl.BlockSpec(memory_space=pl.ANY)


def _place():
    x, y, c = lax.axis_index("x"), lax.axis_index("y"), lax.axis_index("c")
    chips = [(1 - x, y), (x, 1 - y), (1 - x, 1 - y)]
    return x, y, c, chips


def _remote(src, dst, send_sem, recv_sem, to):
    return pltpu.make_async_remote_copy(src_ref=src, dst_ref=dst, send_sem=send_sem, recv_sem=recv_sem,
                                        device_id=to, device_id_type=MESH)


def _gather_weights(shards):
    n = len(shards)

    def body(*refs):
        ins, outs = refs[:n], refs[n:2 * n]
        send_sems, recv_sems, fwd_send, fwd_recv = refs[2 * n:]
        x, y, c, chips = _place()
        me = 2 * x + y
        sibling = (x, y, 1 - c)

        def half(ref, chip, which):
            rh = ref.shape[2] // 2
            return ref.at[:, chip, pl.ds(which * rh, rh), :]

        def src_half(ref, which):
            rh = ref.shape[1] // 2
            return ref.at[:, pl.ds(which * rh, rh), :]

        first = []
        for t in range(n):
            for k, (px, py) in enumerate(chips):
                cp = _remote(src_half(ins[t], c), half(outs[t], me, c), send_sems.at[t, k], recv_sems.at[t, k],
                             (px, py, c))
                cp.start()
                first.append(cp)
        passed = []
        for t in range(n):
            for k, (px, py) in enumerate(chips):
                theirs = 2 * px + py
                landed = half(outs[t], theirs, c)
                _remote(landed, landed, send_sems.at[t, k], recv_sems.at[t, k], (px, py, c)).wait_recv()
                cp = _remote(landed, landed, fwd_send.at[t, k], fwd_recv.at[t, k], sibling)
                cp.start()
                passed.append(cp)
        for t in range(n):
            for k, (px, py) in enumerate(chips):
                theirs = 2 * px + py
                other = half(outs[t], theirs, 1 - c)
                _remote(other, other, fwd_send.at[t, k], fwd_recv.at[t, k], sibling).wait_recv()
        for cp in first + passed:
            cp.wait_send()

    return pl.pallas_call(
        body, name="gather_weights",
        in_specs=[ANY] * n, out_specs=[ANY] * n,
        out_shape=[jax.ShapeDtypeStruct((w.shape[0], N_CHIPS) + w.shape[1:], w.dtype) for w in shards],
        scratch_shapes=[pltpu.SemaphoreType.DMA((n, 3)), pltpu.SemaphoreType.DMA((n, 3)),
                        pltpu.SemaphoreType.DMA((n, 3)), pltpu.SemaphoreType.DMA((n, 3))],
        compiler_params=pltpu.CompilerParams(has_side_effects=True),
    )(*shards)


def _pair_swap_halves(grads, *, name):
    n = len(grads)

    def body(*refs):
        ins, outs = refs[:n], refs[n:2 * n]
        send_sems, recv_sems = refs[2 * n:]
        x, y, c, _ = _place()
        sibling = (x, y, 1 - c)
        cps = []
        for t in range(n):
            rh = ins[t].shape[1] // 2
            cp = _remote(ins[t].at[:, pl.ds((1 - c) * rh, rh), :], outs[t], send_sems.at[t], recv_sems.at[t], sibling)
            cp.start()
            cps.append(cp)
        for cp in cps:
            cp.wait()

    return pl.pallas_call(
        body, name=name,
        in_specs=[ANY] * n, out_specs=[ANY] * n,
        out_shape=[jax.ShapeDtypeStruct((g.shape[0], g.shape[1] // 2, g.shape[2]), g.dtype) for g in grads],
        scratch_shapes=[pltpu.SemaphoreType.DMA((n,)), pltpu.SemaphoreType.DMA((n,))],
        compiler_params=pltpu.CompilerParams(has_side_effects=True),
    )(*grads)


def _chip_exchange(parts, *, name):
    n = len(parts)

    def body(*refs):
        ins, outs = refs[:n], refs[n:2 * n]
        send_sems, recv_sems = refs[2 * n:]
        x, y, c, chips = _place()
        me = 2 * x + y
        cps = []
        for t in range(n):
            for k, (px, py) in enumerate(chips):
                cp = _remote(ins[t].at[2 * px + py], outs[t].at[me], send_sems.at[t, k], recv_sems.at[t, k], (px, py, c))
                cp.start()
                cps.append(cp)
        for t in range(n):
            for k, (px, py) in enumerate(chips):
                theirs = outs[t].at[2 * px + py]
                _remote(theirs, theirs, send_sems.at[t, k], recv_sems.at[t, k], (px, py, c)).wait_recv()
        for cp in cps:
            cp.wait_send()

    return pl.pallas_call(
        body, name=name,
        in_specs=[ANY] * n, out_specs=[ANY] * n,
        out_shape=[jax.ShapeDtypeStruct(p.shape, p.dtype) for p in parts],
        scratch_shapes=[pltpu.SemaphoreType.DMA((n, 3)), pltpu.SemaphoreType.DMA((n, 3))],
        compiler_params=pltpu.CompilerParams(has_side_effects=True),
    )(*parts)


def _pair_join_halves(halves, *, name):
    n = len(halves)

    def body(*refs):
        ins, outs = refs[:n], refs[n:2 * n]
        send_sems, recv_sems = refs[2 * n:]
        x, y, c, _ = _place()
        sibling = (x, y, 1 - c)
        cps = []
        for t in range(n):
            rh = ins[t].shape[0]
            cp = _remote(ins[t], outs[t].at[pl.ds(c * rh, rh), :], send_sems.at[t], recv_sems.at[t], sibling)
            cp.start()
            cps.append(cp)
        for t, cp in enumerate(cps):
            rh = ins[t].shape[0]
            theirs = outs[t].at[pl.ds((1 - c) * rh, rh), :]
            _remote(theirs, theirs, send_sems.at[t], recv_sems.at[t], sibling).wait_recv()
            cp.wait_send()

    return pl.pallas_call(
        body, name=name,
        in_specs=[ANY] * n, out_specs=[ANY] * n,
        out_shape=[jax.ShapeDtypeStruct((2 * h.shape[0], h.shape[1]), h.dtype) for h in halves],
        scratch_shapes=[pltpu.SemaphoreType.DMA((n,)), pltpu.SemaphoreType.DMA((n,))],
        compiler_params=pltpu.CompilerParams(has_side_effects=True),
    )(*halves)


def _exchange_small(buf, *, name):
    rows, lanes = buf.shape

    def body(in_ref, out_ref, send_sems, recv_sems, local_sem):
        x, y, c = lax.axis_index("x"), lax.axis_index("y"), lax.axis_index("c")
        me = 4 * x + 2 * y + c
        lc = pltpu.make_async_copy(in_ref, out_ref.at[me], local_sem)
        lc.start()
        cps = []
        for k in range(1, 8):
            fx, fy, fc = (k >> 2) & 1, (k >> 1) & 1, k & 1
            to = (x ^ fx, y ^ fy, c ^ fc)
            cp = _remote(in_ref, out_ref.at[me], send_sems.at[k - 1], recv_sems.at[k - 1], to)
            cp.start()
            cps.append(cp)
        for k in range(1, 8):
            fx, fy, fc = (k >> 2) & 1, (k >> 1) & 1, k & 1
            theirs = out_ref.at[4 * (x ^ fx) + 2 * (y ^ fy) + (c ^ fc)]
            _remote(theirs, theirs, send_sems.at[k - 1], recv_sems.at[k - 1], (x ^ fx, y ^ fy, c ^ fc)).wait_recv()
        for cp in cps:
            cp.wait_send()
        lc.wait()

    return pl.pallas_call(
        body, name=name,
        in_specs=[ANY], out_specs=ANY,
        out_shape=jax.ShapeDtypeStruct((8, rows, lanes), buf.dtype),
        scratch_shapes=[pltpu.SemaphoreType.DMA((7,)), pltpu.SemaphoreType.DMA((7,)), pltpu.SemaphoreType.DMA],
        compiler_params=pltpu.CompilerParams(has_side_effects=True),
    )(buf)


def _sum_slots(slots, *, name):
    n, rows, lanes = slots.shape

    def body(s_ref, o_ref):
        acc = s_ref[0]
        for t in range(1, n):
            acc = acc + s_ref[t]
        o_ref[...] = acc

    return pl.pallas_call(
        body, name=name,
        out_shape=jax.ShapeDtypeStruct((rows, lanes), slots.dtype),
    )(slots)


def _relu2(acc):
    r = jnp.maximum(acc, 0.0)
    return acc, r * r


def _layer_fwd(x, w, cos, sin, tag):
    h = _rmsnorm_fwd(x, w["norm1_g"], name=f"norm1_{tag}")
    proj = _matmul(h, w["w_in"], mode="nn", b_sharded=True, tm=512, name=f"proj_{tag}")
    qkv = _qkv_prep(proj, w["q_norm_g"], w["k_norm_g"], cos, sin, name=f"qkv_prep_{tag}")
    outs, lds = [], []
    for g in range(N_GROUPS):
        o_g, ld_g = _dil_attn_fwd(qkv, g, name=f"dil{g}_fwd_{tag}")
        outs.append(o_g)
        lds.append(ld_g)
    y, yb, lse = _dil_merge(outs, lds, name=f"dil_merge_{tag}")
    o_sb, rc = _sb_attn_fwd(qkv, name=f"sb_fwd_{tag}")
    up_dil = _matmul(yb, w["w_up_dil"], mode="nn", b_sharded=True, name=f"up_dil_{tag}")
    up_sb = _matmul(o_sb, w["w_up_sb"], mode="nn", b_sharded=True, name=f"up_sb_{tag}")
    mixed = _gate_fwd(proj, w["gate_b"], up_dil, up_sb, name=f"gate_{tag}")
    x1 = _matmul(mixed, w["w_out"], mode="nn", extras=(x,), epilogue=lambda acc, res: (acc + res,), name=f"out_{tag}")
    h2 = _rmsnorm_fwd(x1, w["norm2_g"], name=f"norm2_{tag}")
    u, a = _matmul(h2, w["w_ff1"], mode="nn", b_sharded=True, out_dtypes=(F32, BF16), epilogue=_relu2,
                   name=f"ff1_{tag}")
    x2 = _matmul(a, w["w_ff2"], mode="nn", extras=(x1,), epilogue=lambda acc, res: (acc + res,), name=f"ff2_{tag}")
    saved = dict(x=x, h=h, proj=proj, qkv=qkv, y=y, yb=yb, lse=lse, o_sb=o_sb, rc=rc, up_dil=up_dil, up_sb=up_sb,
                 mixed=mixed, x1=x1, h2=h2, u=u, a=a)
    return x2, saved


def _layer_bwd(dx, dxb, w, sv, cos, sin, tag):
    grads = {}
    du = _matmul(dxb, w["w_ff2"], mode="nt", extras=(sv["u"],), out_dtypes=(BF16,),
                 epilogue=lambda acc, u: (acc * (2.0 * jnp.maximum(u, 0.0)),), name=f"d_a_{tag}")
    grads["w_ff2"] = _matmul(sv["a"], dxb, mode="tn", name=f"dw_ff2_{tag}")
    dh2 = _matmul(du, w["w_ff1"], mode="nt", b_sharded=True, name=f"d_h2_{tag}")
    grads["w_ff1"] = _matmul(sv["h2"], du, mode="tn", out_sharded=True, name=f"dw_ff1_{tag}")
    dx1, dx1b, grads["norm2_g"] = _rmsnorm_bwd(sv["x1"], w["norm2_g"], dh2, dx, name=f"norm2_bwd_{tag}")
    dmixed = _matmul(dx1b, w["w_out"], mode="nt", name=f"d_mixed_{tag}")
    grads["w_out"] = _matmul(sv["mixed"], dx1b, mode="tn", name=f"dw_out_{tag}")
    dud, dus, dgate, grads["gate_b"] = _gate_bwd(sv["proj"], w["gate_b"], sv["up_dil"], sv["up_sb"], dmixed,
                                                 name=f"gate_bwd_{tag}")
    dy_dil = _matmul(dud, w["w_up_dil"], mode="nt", b_sharded=True, name=f"d_ydil_{tag}")
    grads["w_up_dil"] = _matmul(sv["yb"], dud, mode="tn", out_sharded=True, name=f"dw_up_dil_{tag}")
    dy_sb = _matmul(dus, w["w_up_sb"], mode="nt", b_sharded=True, name=f"d_ysb_{tag}")
    grads["w_up_sb"] = _matmul(sv["o_sb"], dus, mode="tn", out_sharded=True, name=f"dw_up_sb_{tag}")
    d_dil = [_dil_attn_bwd(sv["qkv"], sv["y"], sv["lse"], dy_dil, g, name=f"dil{g}_bwd_{tag}") for g in range(N_GROUPS)]
    d_sb = _sb_attn_bwd(sv["qkv"], sv["rc"], dy_sb, name=f"sb_bwd_{tag}")
    dproj, grads["q_norm_g"], grads["k_norm_g"] = _qkv_prep_bwd(
        sv["proj"], w["q_norm_g"], w["k_norm_g"], cos, sin, d_dil, d_sb, dgate, name=f"qkv_prep_bwd_{tag}")
    dh = _matmul(dproj, w["w_in"], mode="nt", b_sharded=True, tm=512, name=f"d_h_{tag}")
    grads["w_in"] = _matmul(sv["h"], dproj, mode="tn", out_sharded=True, tm=512, name=f"dw_in_{tag}")
    dx0, dx0b, grads["norm1_g"] = _rmsnorm_bwd(sv["x"], w["norm1_g"], dh, dx1, name=f"norm1_bwd_{tag}")
    return dx0, dx0b, grads


def _local_step(x, target, layers, cos, sin):
    saved = []
    for l, w in enumerate(layers):
        x, sv = _layer_fwd(x, w, cos, sin, f"l{l}")
        saved.append(sv)
    loss, dx, dxb = _loss_head(x, target)
    grads = [None] * len(layers)
    for l in reversed(range(len(layers))):
        dx, dxb, grads[l] = _layer_bwd(dx, dxb, layers[l], saved[l], cos, sin, f"l{l}")
    return loss, dx, grads


BIG = ("w_in", "w_up_dil", "w_up_sb", "w_out", "w_ff1", "w_ff2")
ROW_SHARDED = ("w_out", "w_ff2")
WEIGHTS = ("norm1_g", "w_in", "q_norm_g", "k_norm_g", "w_up_dil", "w_up_sb", "gate_b", "w_out", "norm2_g", "w_ff1", "w_ff2")


def _reduce_to_shards(grads, which, order, tag):
    names = list(grads)
    full = [grads[n] for n in names]
    theirs = _pair_swap_halves(full, name=f"rs_pair_{tag}")
    pair = [_add_half(f, t, which, name=f"rs_pair_sum_{n}_{tag}") for n, f, t in zip(names, full, theirs)]
    parts = _chip_exchange(pair, name=f"rs_chips_{tag}")
    halves = [_sum_chips(p, q, order, name=f"rs_chip_sum_{n}_{tag}") for n, p, q in zip(names, pair, parts)]
    joined = _pair_join_halves(halves, name=f"rs_join_{tag}")
    out = {}
    for n, j, h in zip(names, joined, halves):
        r, c = j.shape
        out[n] = _place_piece(j.reshape(1, 2, r // 2, c), h[None], which, name=f"rs_place_{n}_{tag}").reshape(r, c)
    return out


def _pack_rows(vecs):
    rows, spans, at = [], [], 0
    for v in vecs:
        r = v.size // 128
        padded = -(-r // 8) * 8
        rows.append(jnp.pad(v.reshape(r, 128), ((0, padded - r), (0, 0))))
        spans.append((at, r))
        at += padded
    return jnp.concatenate(rows, axis=0), spans


def kernel(x, norm1_g, w_in, q_norm_g, k_norm_g, w_up_dil, w_up_sb, gate_b, w_out, norm2_g, w_ff1, w_ff2, loss_target, m_norm1_g, m_w_in, m_q_norm_g, m_k_norm_g, m_w_up_dil, m_w_up_sb, m_gate_b, m_w_out, m_norm2_g, m_w_ff1, m_w_ff2, v_norm1_g, v_w_in, v_q_norm_g, v_k_norm_g, v_w_up_dil, v_w_up_sb, v_gate_b, v_w_out, v_norm2_g, v_w_ff1, v_w_ff2):
    weights = dict(norm1_g=norm1_g, w_in=w_in, q_norm_g=q_norm_g, k_norm_g=k_norm_g, w_up_dil=w_up_dil,
                   w_up_sb=w_up_sb, gate_b=gate_b, w_out=w_out, norm2_g=norm2_g, w_ff1=w_ff1, w_ff2=w_ff2)
    moments_m = dict(norm1_g=m_norm1_g, w_in=m_w_in, q_norm_g=m_q_norm_g, k_norm_g=m_k_norm_g, w_up_dil=m_w_up_dil,
                     w_up_sb=m_w_up_sb, gate_b=m_gate_b, w_out=m_w_out, norm2_g=m_norm2_g, w_ff1=m_w_ff1, w_ff2=m_w_ff2)
    moments_v = dict(norm1_g=v_norm1_g, w_in=v_w_in, q_norm_g=v_q_norm_g, k_norm_g=v_k_norm_g, w_up_dil=v_w_up_dil,
                     w_up_sb=v_w_up_sb, gate_b=v_gate_b, w_out=v_w_out, norm2_g=v_norm2_g, w_ff1=v_w_ff1, w_ff2=v_w_ff2)
    depth = w_in.shape[0]
    seq, d_model = x.shape[1], x.shape[2]
    chip = 2 * lax.axis_index("x") + lax.axis_index("y")
    core = lax.axis_index("c")
    which = jnp.reshape(core, (1,)).astype(jnp.int32)

    chip_index = jnp.reshape(chip, (1,)).astype(jnp.int32)
    order = jnp.stack([chip, chip ^ 2, chip ^ 1, chip ^ 3]).astype(jnp.int32)
    shards = [weights[n].astype(BF16) for n in BIG]
    gathered = {n: _place_piece(g, s, chip_index, name=f"gather_place_{n}")
                for n, g, s in zip(BIG, _gather_weights(shards), shards)}
    bias_buf, ((_, bias_rows),) = _pack_rows([gate_b])
    bias_slots = _exchange_small(bias_buf, name="gather_gate_b")
    bias = bias_slots[0::2, :bias_rows].reshape(N_CHIPS, depth, 2, d_model // N_CHIPS)
    bias = jnp.transpose(bias, (1, 2, 0, 3)).reshape(depth, 2, d_model)
    layers = []
    for l in range(depth):
        w = {n: gathered[n][l] for n in BIG}
        for n in ROW_SHARDED:
            w[n] = w[n].reshape(-1, w[n].shape[-1])
        w.update(norm1_g=norm1_g[l], norm2_g=norm2_g[l], q_norm_g=q_norm_g[l], k_norm_g=k_norm_g[l], gate_b=bias[l])
        layers.append(w)

    cos, sin = _rope_tables(seq)
    loss, dx, grads = _local_step(x[0], loss_target[0], layers, cos, sin)
    loss = lax.psum(loss, ("x", "y", "c"))

    reduced = []
    for l in range(depth):
        g = {n: grads[l][n] for n in BIG}
        for n in ROW_SHARDED:
            g[n] = g[n].reshape(N_CHIPS, -1, g[n].shape[-1])
        reduced.append(_reduce_to_shards(g, which, order, f"l{l}"))
    final = {n: jnp.stack([reduced[l][n] for l in range(depth)]) for n in BIG}

    small_names = ("norm1_g", "norm2_g", "q_norm_g", "k_norm_g", "gate_b")
    packed, spans = _pack_rows([grads[l][n] for l in range(depth) for n in small_names])
    total = _sum_slots(_exchange_small(packed, name="exchange_small_grads"), name="sum_small_grads")
    pieces = iter(total[at:at + r] for at, r in spans)
    small = {n: [] for n in small_names}
    for l in range(depth):
        for n in small_names:
            small[n].append(next(pieces))
    for n in ("norm1_g", "norm2_g"):
        final[n] = jnp.stack([p.reshape(d_model) for p in small[n]])
    for n in ("q_norm_g", "k_norm_g"):
        final[n] = jnp.stack([p.reshape(N_GROUPS, HEAD_DIM) for p in small[n]])
    shard_cols = d_model // N_CHIPS
    final["gate_b"] = jnp.stack([lax.dynamic_slice_in_dim(p.reshape(2, d_model), chip * shard_cols, shard_cols, axis=1)
                                 for p in small["gate_b"]])

    deltas, new_m, new_v = {}, {}, {}
    for n in WEIGHTS:
        deltas[n], new_m[n], new_v[n] = _adamw(weights[n], final[n], moments_m[n], moments_v[n], name=f"adamw_{n}")
    return (loss, dx[None], *[final[n] for n in WEIGHTS], *[deltas[n] for n in WEIGHTS],
            *[new_m[n] for n in WEIGHTS], *[new_v[n] for n in WEIGHTS])
```

```python
import functools
import math

import jax
import jax.numpy as jnp
from jax import lax
from jax.experimental import pallas as pl
from jax.experimental.pallas import tpu as pltpu

F32 = jnp.float32
BF16 = jnp.bfloat16

HEAD_DIM = 128
DIL_GROUPS = ((128, 1), (512, 4), (2048, 16))
N_GROUPS = 3
HEADS_PER_GROUP = 4
GROUP_WIDTH = HEADS_PER_GROUP * HEAD_DIM
DIL_WIDTH = N_GROUPS * GROUP_WIDTH
SB_HEADS = 8
SB_WIDTH = SB_HEADS * HEAD_DIM
QKV_WIDTH = 3 * DIL_WIDTH + 3 * SB_WIDTH
BLOCK = 128
ROPE_THETA = 10000.0
EPS = 1e-6
SCALE = 1.0 / math.sqrt(HEAD_DIM)
NEG_INF = float("-inf")

ADAM_LR = 0.001
ADAM_B1 = 0.9
ADAM_B2 = 0.999
ADAM_EPS = 1e-08
ADAM_WD = 0.01
ADAM_STEP = 10

N_CHIPS = 4
MESH = pl.DeviceIdType.MESH
MIB = 1024 * 1024
VMEM_LIMIT = 56 * MIB


def _params(semantics=None, vmem=VMEM_LIMIT):
    return pltpu.CompilerParams(dimension_semantics=semantics, vmem_limit_bytes=vmem)


def _pick(n, pref):
    if n <= pref:
        return n
    t = 1 << (pref.bit_length() - 1)
    while n % t:
        t //= 2
    return t


def _tile(n, pref):
    best = 0
    for t in range(256, min(n, pref) + 1, 256):
        if n % t == 0:
            best = t
    return best if best else n


def _matmul(a, b, *, mode, name, out_dtypes=(F32,), extras=(), epilogue=None, b_sharded=False, out_sharded=False,
            tm=1024, tn=1024, tk=512):
    if mode == "nn":
        m, kdim = a.shape
        n = b.shape[-1] * (N_CHIPS if b_sharded else 1)
        assert (b.shape[-2] if b_sharded else b.shape[0]) == kdim
    elif mode == "nt":
        m, kdim = a.shape
        n = b.shape[-2]
        assert b.shape[-1] * (N_CHIPS if b_sharded else 1) == kdim
    else:
        kdim, m = a.shape
        n = b.shape[1]
        assert b.shape[0] == kdim and not b_sharded
    shard_n = n // N_CHIPS if (out_sharded or (b_sharded and mode == "nn")) else n
    shard_k = kdim // N_CHIPS if (b_sharded and mode == "nt") else kdim
    tm = _tile(m, tm)
    tn = _tile(shard_n, tn)
    tk = _tile(shard_k, tk)
    nk = kdim // tk
    grid = (m // tm, n // tn, nk)
    nj_n = shard_n // tn
    nj_k = shard_k // tk

    if mode == "nn":
        a_spec = pl.BlockSpec((tm, tk), lambda i, j, k: (i, k))
        if b_sharded:
            b_spec = pl.BlockSpec((None, tk, tn), lambda i, j, k: (j // nj_n, k, j % nj_n))
        else:
            b_spec = pl.BlockSpec((tk, tn), lambda i, j, k: (k, j))
        dims = (((1,), (0,)), ((), ()))
    elif mode == "nt":
        a_spec = pl.BlockSpec((tm, tk), lambda i, j, k: (i, k))
        if b_sharded:
            b_spec = pl.BlockSpec((None, tn, tk), lambda i, j, k: (k // nj_k, j, k % nj_k))
        else:
            b_spec = pl.BlockSpec((tn, tk), lambda i, j, k: (j, k))
        dims = (((1,), (1,)), ((), ()))
    else:
        a_spec = pl.BlockSpec((tk, tm), lambda i, j, k: (k, i))
        b_spec = pl.BlockSpec((tk, tn), lambda i, j, k: (k, j))
        dims = (((0,), (0,)), ((), ()))
    if out_sharded:
        o_spec = pl.BlockSpec((None, tm, tn), lambda i, j, k: (j // nj_n, i, j % nj_n))
        o_shape = (N_CHIPS, m, shard_n)
    else:
        o_spec = pl.BlockSpec((tm, tn), lambda i, j, k: (i, j))
        o_shape = (m, n)
    x_spec = pl.BlockSpec((tm, tn), lambda i, j, k: (i, j))
    n_extra = len(extras)
    n_out = len(out_dtypes)

    def body(a_ref, b_ref, *rest):
        extra_refs = rest[:n_extra]
        out_refs = rest[n_extra:n_extra + n_out]
        acc = rest[-1]
        k = pl.program_id(2)

        @pl.when(k == 0)
        def _():
            acc[...] = jnp.zeros_like(acc)

        acc[...] += lax.dot_general(a_ref[...], b_ref[...], dims, preferred_element_type=F32)

        @pl.when(k == nk - 1)
        def _():
            vals = [e[...] for e in extra_refs]
            res = (acc[...],) if epilogue is None else epilogue(acc[...], *vals)
            for o, r in zip(out_refs, res):
                o[...] = r.astype(o.dtype)

    outs = pl.pallas_call(
        body,
        name=name,
        grid=grid,
        in_specs=[a_spec, b_spec] + [x_spec] * n_extra,
        out_specs=[o_spec] * n_out,
        out_shape=[jax.ShapeDtypeStruct(o_shape, dt) for dt in out_dtypes],
        scratch_shapes=[pltpu.VMEM((tm, tn), F32)],
        compiler_params=_params(("parallel", "parallel", "arbitrary")),
    )(a, b, *extras)
    return outs[0] if n_out == 1 else outs


def _rmsnorm_fwd(x, g, *, name):
    s, d = x.shape
    tr = _pick(s, 256)

    def body(x_ref, g_ref, o_ref):
        xv = x_ref[...]
        r = lax.rsqrt(jnp.mean(xv * xv, axis=-1, keepdims=True) + EPS)
        o_ref[...] = (xv * r * g_ref[...]).astype(o_ref.dtype)

    return pl.pallas_call(
        body, name=name, grid=(s // tr,),
        in_specs=[pl.BlockSpec((tr, d), lambda i: (i, 0)), pl.BlockSpec((1, d), lambda i: (0, 0))],
        out_specs=pl.BlockSpec((tr, d), lambda i: (i, 0)),
        out_shape=jax.ShapeDtypeStruct((s, d), BF16),
        compiler_params=_params(("parallel",)),
    )(x, g.reshape(1, d))


def _rmsnorm_bwd(x, g, dh, dres, *, name):
    s, d = x.shape
    tr = _pick(s, 256)

    def body(x_ref, g_ref, dh_ref, dres_ref, dx_ref, dxb_ref, dg_ref):
        xv = x_ref[...]
        r = lax.rsqrt(jnp.mean(xv * xv, axis=-1, keepdims=True) + EPS)
        y = xv * r
        dhv = dh_ref[...]
        dy = dhv * g_ref[...]
        dx = dres_ref[...] + r * (dy - y * jnp.mean(dy * y, axis=-1, keepdims=True))
        dx_ref[...] = dx
        dxb_ref[...] = dx.astype(BF16)

        @pl.when(pl.program_id(0) == 0)
        def _():
            dg_ref[...] = jnp.zeros_like(dg_ref)

        dg_ref[...] += jnp.sum(dhv * y, axis=0, keepdims=True)

    row = pl.BlockSpec((tr, d), lambda i: (i, 0))
    vec = pl.BlockSpec((1, d), lambda i: (0, 0))
    dx, dxb, dg = pl.pallas_call(
        body, name=name, grid=(s // tr,),
        in_specs=[row, vec, row, row],
        out_specs=[row, row, vec],
        out_shape=[jax.ShapeDtypeStruct((s, d), F32), jax.ShapeDtypeStruct((s, d), BF16),
                   jax.ShapeDtypeStruct((1, d), F32)],
        compiler_params=_params(("arbitrary",)),
    )(x, g.reshape(1, d), dh, dres)
    return dx, dxb, dg.reshape(d)


def _loss_head(y, target):
    s, d = y.shape
    tr = _pick(s, 256)

    def body(y_ref, t_ref, dy_ref, dyb_ref, part_ref):
        err = y_ref[...] - t_ref[...]
        dy = err * (1.0 / d)
        dy_ref[...] = dy
        dyb_ref[...] = dy.astype(BF16)

        @pl.when(pl.program_id(0) == 0)
        def _():
            part_ref[...] = jnp.zeros_like(part_ref)

        part_ref[...] += jnp.sum(err * err, axis=0, keepdims=True)

    row = pl.BlockSpec((tr, d), lambda i: (i, 0))
    vec = pl.BlockSpec((1, d), lambda i: (0, 0))
    dy, dyb, part = pl.pallas_call(
        body, name="loss_head", grid=(s // tr,),
        in_specs=[row, row], out_specs=[row, row, vec],
        out_shape=[jax.ShapeDtypeStruct((s, d), F32), jax.ShapeDtypeStruct((s, d), BF16),
                   jax.ShapeDtypeStruct((1, d), F32)],
        compiler_params=_params(("arbitrary",)),
    )(y, target)
    return 0.5 * jnp.sum(part) / d, dy, dyb


def _rope_tables(s):
    half = HEAD_DIM // 2
    inv_freq = ROPE_THETA ** (-jnp.arange(half, dtype=F32) / half)
    ang = jnp.arange(s, dtype=F32)[:, None] * inv_freq[None, :]
    cos, sin = jnp.cos(ang), jnp.sin(ang)
    return jnp.concatenate([cos, cos], axis=-1), jnp.concatenate([-sin, sin], axis=-1)


def _qkv_prep(proj, qg, kg, cos, sin, *, name):
    s = proj.shape[0]
    tr = _pick(s, 256)

    def body(p_ref, qg_ref, kg_ref, cos_ref, sin_ref, o_ref):
        c = cos_ref[...]
        sn = sin_ref[...]
        for which, g_ref in ((0, qg_ref), (1, kg_ref)):
            for g in range(N_GROUPS):
                gain = g_ref[g:g + 1, :]
                for h in range(HEADS_PER_GROUP):
                    off = which * DIL_WIDTH + g * GROUP_WIDTH + h * HEAD_DIM
                    xv = p_ref[:, off:off + HEAD_DIM]
                    r = lax.rsqrt(jnp.mean(xv * xv, axis=-1, keepdims=True) + EPS)
                    y = xv * r * gain
                    o_ref[:, off:off + HEAD_DIM] = (y * c + pltpu.roll(y, HEAD_DIM // 2, 1) * sn).astype(BF16)
        o_ref[:, 2 * DIL_WIDTH:] = p_ref[:, 2 * DIL_WIDTH:].astype(BF16)

    return pl.pallas_call(
        body, name=name, grid=(s // tr,),
        in_specs=[pl.BlockSpec((tr, QKV_WIDTH), lambda i: (i, 0)),
                  pl.BlockSpec((N_GROUPS, HEAD_DIM), lambda i: (0, 0)),
                  pl.BlockSpec((N_GROUPS, HEAD_DIM), lambda i: (0, 0)),
                  pl.BlockSpec((tr, HEAD_DIM), lambda i: (i, 0)),
                  pl.BlockSpec((tr, HEAD_DIM), lambda i: (i, 0))],
        out_specs=pl.BlockSpec((tr, QKV_WIDTH), lambda i: (i, 0)),
        out_shape=jax.ShapeDtypeStruct((s, QKV_WIDTH), BF16),
        compiler_params=_params(("parallel",)),
    )(proj, qg, kg, cos, sin)


def _qkv_prep_bwd(proj, qg, kg, cos, sin, dqkv_dil, dqkv_sb, dgate, *, name):
    s, n_in = proj.shape
    n_gate = dgate.shape[1]
    tr = _pick(s, 256)

    def body(p_ref, qg_ref, kg_ref, cos_ref, sin_ref, *rest):
        dil_refs = rest[:9]
        sb_refs = rest[9:12]
        dgate_ref = rest[12]
        o_ref, dqg_ref, dkg_ref = rest[13:16]
        c = cos_ref[...]
        sn = sin_ref[...]

        @pl.when(pl.program_id(0) == 0)
        def _():
            dqg_ref[...] = jnp.zeros_like(dqg_ref)
            dkg_ref[...] = jnp.zeros_like(dkg_ref)

        for which, g_ref, dg_ref in ((0, qg_ref, dqg_ref), (1, kg_ref, dkg_ref)):
            for g in range(N_GROUPS):
                gain = g_ref[g:g + 1, :]
                d_ref = dil_refs[3 * g + which]
                dgain = jnp.zeros((1, HEAD_DIM), F32)
                for h in range(HEADS_PER_GROUP):
                    off = which * DIL_WIDTH + g * GROUP_WIDTH + h * HEAD_DIM
                    xv = p_ref[:, off:off + HEAD_DIM]
                    r = lax.rsqrt(jnp.mean(xv * xv, axis=-1, keepdims=True) + EPS)
                    nx = xv * r
                    dout = d_ref[:, h * HEAD_DIM:(h + 1) * HEAD_DIM]
                    dy = dout * c + pltpu.roll(dout * sn, HEAD_DIM // 2, 1)
                    dgain = dgain + jnp.sum(dy * nx, axis=0, keepdims=True)
                    dn = dy * gain
                    dxv = r * (dn - nx * jnp.mean(dn * nx, axis=-1, keepdims=True))
                    o_ref[:, off:off + HEAD_DIM] = dxv.astype(BF16)
                dg_ref[g:g + 1, :] += dgain
        for g in range(N_GROUPS):
            off = 2 * DIL_WIDTH + g * GROUP_WIDTH
            o_ref[:, off:off + GROUP_WIDTH] = dil_refs[3 * g + 2][...].astype(BF16)
        for t in range(3):
            off = 3 * DIL_WIDTH + t * SB_WIDTH
            o_ref[:, off:off + SB_WIDTH] = sb_refs[t][...].astype(BF16)
        o_ref[:, QKV_WIDTH:] = dgate_ref[...]

    grp = pl.BlockSpec((tr, GROUP_WIDTH), lambda i: (i, 0))
    sbs = pl.BlockSpec((tr, SB_WIDTH), lambda i: (i, 0))
    gain_spec = pl.BlockSpec((N_GROUPS, HEAD_DIM), lambda i: (0, 0))
    tab = pl.BlockSpec((tr, HEAD_DIM), lambda i: (i, 0))
    flat_dil = [t for grp3 in dqkv_dil for t in grp3]
    return pl.pallas_call(
        body, name=name, grid=(s // tr,),
        in_specs=[pl.BlockSpec((tr, 2 * DIL_WIDTH), lambda i: (i, 0)), gain_spec, gain_spec, tab, tab]
                 + [grp] * 9 + [sbs] * 3 + [pl.BlockSpec((tr, n_gate), lambda i: (i, 0))],
        out_specs=[pl.BlockSpec((tr, n_in), lambda i: (i, 0)), gain_spec, gain_spec],
        out_shape=[jax.ShapeDtypeStruct((s, n_in), BF16),
                   jax.ShapeDtypeStruct((N_GROUPS, HEAD_DIM), F32),
                   jax.ShapeDtypeStruct((N_GROUPS, HEAD_DIM), F32)],
        compiler_params=_params(("arbitrary",)),
    )(proj, qg, kg, cos, sin, *flat_dil, *dqkv_sb, dgate)


def _dil_masks(i):
    row = lax.broadcasted_iota(jnp.int32, (BLOCK, BLOCK), 0)
    col = lax.broadcasted_iota(jnp.int32, (BLOCK, BLOCK), 1)
    return col <= row, (col >= row) & (i > 0)


def _dot_nt(a, b):
    return lax.dot_general(a, b, (((1,), (1,)), ((), ())), preferred_element_type=F32)


def _dot_tn(a, b):
    return lax.dot_general(a, b, (((0,), (0,)), ((), ())), preferred_element_type=F32)


def _dot(a, b):
    return jnp.dot(a, b, preferred_element_type=F32)


def _dil_attn_fwd(qkv, g, *, name):
    s = qkv.shape[0]
    r = DIL_GROUPS[g][1]
    length = s // r
    nb = length // BLOCK
    wb = QKV_WIDTH // GROUP_WIDTH
    view = qkv.reshape(length, r * QKV_WIDTH)

    def body(q_ref, kp_ref, kc_ref, vp_ref, vc_ref, o_ref, ld_ref):
        i = pl.program_id(1)
        mask_c, mask_p = _dil_masks(i)
        for h in range(HEADS_PER_GROUP):
            sl = slice(h * HEAD_DIM, (h + 1) * HEAD_DIM)
            q = q_ref[:, sl]
            s_c = jnp.where(mask_c, _dot_nt(q, kc_ref[:, sl]) * SCALE, NEG_INF)
            s_p = jnp.where(mask_p, _dot_nt(q, kp_ref[:, sl]) * SCALE, NEG_INF)
            m = jnp.maximum(jnp.max(s_c, axis=-1, keepdims=True), jnp.max(s_p, axis=-1, keepdims=True))
            p_c = jnp.exp(s_c - m)
            p_p = jnp.exp(s_p - m)
            l = jnp.sum(p_c, axis=-1, keepdims=True) + jnp.sum(p_p, axis=-1, keepdims=True)
            inv = 1.0 / l
            o = _dot((p_c * inv).astype(BF16), vc_ref[:, sl]) + _dot((p_p * inv).astype(BF16), vp_ref[:, sl])
            o_ref[:, sl] = o
            ld_ref[:, sl] = jnp.broadcast_to(m + jnp.log(l), (BLOCK, HEAD_DIM))

    def col(which):
        return lambda c, i: (i, c * wb + 3 * which + g)

    def col_prev(which):
        return lambda c, i: (jnp.maximum(i - 1, 0), c * wb + 3 * which + g)

    blk = (BLOCK, GROUP_WIDTH)
    o, ld = pl.pallas_call(
        body, name=name, grid=(r, nb),
        in_specs=[pl.BlockSpec(blk, col(0)), pl.BlockSpec(blk, col_prev(1)), pl.BlockSpec(blk, col(1)),
                  pl.BlockSpec(blk, col_prev(2)), pl.BlockSpec(blk, col(2))],
        out_specs=[pl.BlockSpec(blk, lambda c, i: (i, c))] * 2,
        out_shape=[jax.ShapeDtypeStruct((length, r * GROUP_WIDTH), F32)] * 2,
        compiler_params=_params(("parallel", "parallel")),
    )(view, view, view, view, view)
    return o.reshape(s, GROUP_WIDTH), ld.reshape(s, GROUP_WIDTH)


def _dil_merge(outs, lds, *, name):
    s = outs[0].shape[0]
    tr = _pick(s, 512)

    def body(o0, o1, o2, l0, l1, l2, y_ref, yb_ref, lse_ref):
        a0, a1, a2 = l0[...], l1[...], l2[...]
        m = jnp.maximum(jnp.maximum(a0, a1), a2)
        e0, e1, e2 = jnp.exp(a0 - m), jnp.exp(a1 - m), jnp.exp(a2 - m)
        tot = e0 + e1 + e2
        inv = 1.0 / tot
        y = (e0 * inv) * o0[...] + (e1 * inv) * o1[...] + (e2 * inv) * o2[...]
        y_ref[...] = y
        yb_ref[...] = y.astype(BF16)
        lse_ref[...] = m + jnp.log(tot)

    blk = pl.BlockSpec((tr, GROUP_WIDTH), lambda i: (i, 0))
    return pl.pallas_call(
        body, name=name, grid=(s // tr,),
        in_specs=[blk] * 6, out_specs=[blk] * 3,
        out_shape=[jax.ShapeDtypeStruct((s, GROUP_WIDTH), F32), jax.ShapeDtypeStruct((s, GROUP_WIDTH), BF16),
                   jax.ShapeDtypeStruct((s, GROUP_WIDTH), F32)],
        compiler_params=_params(("parallel",)),
    )(*outs, *lds)


def _dil_attn_bwd(qkv, y, lse, dy, g, *, name):
    s = qkv.shape[0]
    r = DIL_GROUPS[g][1]
    length = s // r
    nb = length // BLOCK
    wb = QKV_WIDTH // GROUP_WIDTH
    view = qkv.reshape(length, r * QKV_WIDTH)
    yv, lv, dv_ = (t.reshape(length, r * GROUP_WIDTH) for t in (y, lse, dy))

    def body(q_ref, qn_ref, kp_ref, kc_ref, vp_ref, vc_ref, y_ref, yn_ref, l_ref, ln_ref, d_ref, dn_ref,
             dq_ref, dk_ref, dv_ref):
        i = pl.program_id(1)
        mask_c, mask_p = _dil_masks(i)
        _, mask_n = _dil_masks(jnp.where(i + 1 < nb, 1, 0))
        for h in range(HEADS_PER_GROUP):
            sl = slice(h * HEAD_DIM, (h + 1) * HEAD_DIM)
            q, qn = q_ref[:, sl], qn_ref[:, sl]
            kc, kp, vc, vp = kc_ref[:, sl], kp_ref[:, sl], vc_ref[:, sl], vp_ref[:, sl]
            dyf, dynf = d_ref[:, sl], dn_ref[:, sl]
            dyb, dynb = dyf.astype(BF16), dynf.astype(BF16)
            delta = jnp.sum(dyf * y_ref[:, sl], axis=-1, keepdims=True)
            delta_n = jnp.sum(dynf * yn_ref[:, sl], axis=-1, keepdims=True)
            lse_q, lse_n = l_ref[:, sl], ln_ref[:, sl]
            p_c = jnp.where(mask_c, jnp.exp(_dot_nt(q, kc) * SCALE - lse_q), 0.0)
            p_p = jnp.where(mask_p, jnp.exp(_dot_nt(q, kp) * SCALE - lse_q), 0.0)
            p_n = jnp.where(mask_n, jnp.exp(_dot_nt(qn, kc) * SCALE - lse_n), 0.0)
            ds_c = (p_c * (_dot_nt(dyb, vc) - delta) * SCALE).astype(BF16)
            ds_p = (p_p * (_dot_nt(dyb, vp) - delta) * SCALE).astype(BF16)
            ds_n = (p_n * (_dot_nt(dynb, vc) - delta_n) * SCALE).astype(BF16)
            dq_ref[:, sl] = _dot(ds_c, kc) + _dot(ds_p, kp)
            dk_ref[:, sl] = _dot_tn(ds_c, q) + _dot_tn(ds_n, qn)
            dv_ref[:, sl] = _dot_tn(p_c.astype(BF16), dyb) + _dot_tn(p_n.astype(BF16), dynb)

    def col(which):
        return lambda c, i: (i, c * wb + 3 * which + g)

    def col_prev(which):
        return lambda c, i: (jnp.maximum(i - 1, 0), c * wb + 3 * which + g)

    def col_next(which):
        return lambda c, i: (jnp.minimum(i + 1, nb - 1), c * wb + 3 * which + g)

    blk = (BLOCK, GROUP_WIDTH)
    own = pl.BlockSpec(blk, lambda c, i: (i, c))
    nxt = pl.BlockSpec(blk, lambda c, i: (jnp.minimum(i + 1, nb - 1), c))
    dq, dk, dv = pl.pallas_call(
        body, name=name, grid=(r, nb),
        in_specs=[pl.BlockSpec(blk, col(0)), pl.BlockSpec(blk, col_next(0)),
                  pl.BlockSpec(blk, col_prev(1)), pl.BlockSpec(blk, col(1)),
                  pl.BlockSpec(blk, col_prev(2)), pl.BlockSpec(blk, col(2)),
                  own, nxt, own, nxt, own, nxt],
        out_specs=[own] * 3,
        out_shape=[jax.ShapeDtypeStruct((length, r * GROUP_WIDTH), F32)] * 3,
        compiler_params=_params(("parallel", "parallel")),
    )(view, view, view, view, view, view, yv, yv, lv, lv, dv_, dv_)
    return tuple(t.reshape(s, GROUP_WIDTH) for t in (dq, dk, dv))


SB_TQ = 256
SB_GROUP = 2
SB_DEAD = -104.0
SB_NH_FWD = 4
SB_NH_BWD = 2


def _split_dot(x, w):
    hi = x.astype(BF16)
    lo = (x - hi.astype(F32)).astype(BF16)
    return _dot(hi, w) + _dot(lo, w)


def _sb_consts():
    j = lax.broadcasted_iota(jnp.int32, (BLOCK, BLOCK), 0)
    k = lax.broadcasted_iota(jnp.int32, (BLOCK, BLOCK), 1)
    ones = jnp.ones((BLOCK, BLOCK), F32)
    after = jnp.concatenate([(j > k).astype(F32), ones], axis=1).astype(BF16)
    before = jnp.concatenate([(j < k).astype(F32), ones], axis=1).astype(BF16)
    return after, before


def _sb_mask(row0, col0, tq):
    row = row0 + lax.broadcasted_iota(jnp.int32, (tq, BLOCK), 0)
    col = col0 + lax.broadcasted_iota(jnp.int32, (tq, BLOCK), 1)
    return col < row


def _softplus(z):
    return jnp.maximum(z, 0.0) + jnp.log1p(jnp.exp(-jnp.abs(z)))


def _sb_attn_fwd(qkv, *, name):
    s = qkv.shape[0]
    tq = _pick(s, SB_TQ)
    nq = s // tq
    per = tq // BLOCK
    base = 3 * DIL_WIDTH // HEAD_DIM
    after, _ = _sb_consts()

    nh = SB_NH_FWD
    wide = nh * HEAD_DIM

    def body(q_ref, k_ref, v_ref, w_ref, o_ref, rc_ref, trips_ref, acc_ref, run_ref):
        qi = pl.program_id(1)
        w = w_ref[...]
        acc_ref[...] = jnp.zeros_like(acc_ref)
        run_ref[...] = jnp.zeros_like(run_ref)
        rc_ref[...] = jnp.zeros_like(rc_ref)
        lane = lax.broadcasted_iota(jnp.int32, (tq, BLOCK), 1)
        n_chunks = (qi + 1) * per

        def chunks(js, masked):
            sls = [slice(h * HEAD_DIM, (h + 1) * HEAD_DIM) for h in range(nh)]
            work = [(sl, j, pl.multiple_of(j * BLOCK, BLOCK), _sb_mask(qi * tq, j * BLOCK, tq) if masked else None)
                    for j in js for sl in sls]
            zs = [_dot_nt(q_ref[:, sl], k_ref[pl.ds(start, BLOCK), sl]) * SCALE for sl, _, start, _ in work]
            logits, sums = [], []
            for z, (_, _, _, mask) in zip(zs, work):
                sp = _softplus(z)
                logits.append(z - sp)
                sums.append(_split_dot(jnp.where(mask, -sp, 0.0) if masked else -sp, w))
            for (sl, j, start, mask), lg, sm in zip(work, logits, sums):
                run = run_ref[:, sl]
                a = jnp.exp(lg + run + sm[:, :BLOCK])
                if masked:
                    a = jnp.where(mask, a, 0.0)
                acc_ref[:, sl] += _dot(a.astype(BF16), v_ref[pl.ds(start, BLOCK), sl])
                rc_ref[:, sl] = jnp.where(lane == j, run, rc_ref[:, sl])
                run_ref[:, sl] = run + sm[:, BLOCK:]

        def diagonal(t, carry):
            last = n_chunks - 1 - SB_GROUP * t
            chunks([last - g for g in range(SB_GROUP)], True)
            return carry

        def alive():
            return (jnp.max(run_ref[...]) > SB_DEAD).astype(jnp.int32)

        def more(carry):
            t, live = carry
            return jnp.logical_and(t < n_left, live > 0)

        def left(carry):
            t, _ = carry
            last = n_chunks - 1 - per - SB_GROUP * t
            chunks([last - g for g in range(SB_GROUP)], False)
            return t + 1, alive()

        lax.fori_loop(0, per // SB_GROUP, diagonal, 0)
        n_left = (n_chunks - per) // SB_GROUP
        trips, _ = lax.while_loop(more, left, (jnp.int32(0), alive()))
        trips_ref[pl.program_id(0), qi] = trips
        o_ref[...] = acc_ref[...].astype(o_ref.dtype)

    head = lambda off: (lambda h, i: (0, (base + off) // nh + h))
    once = pl.Buffered(1)
    return pl.pallas_call(
        body, name=name, grid=(SB_HEADS // nh, nq),
        in_specs=[pl.BlockSpec((tq, wide), lambda h, i: (i, base // nh + h)),
                  pl.BlockSpec((s, wide), head(SB_HEADS), pipeline_mode=once),
                  pl.BlockSpec((s, wide), head(2 * SB_HEADS), pipeline_mode=once),
                  pl.BlockSpec((BLOCK, 2 * BLOCK), lambda h, i: (0, 0))],
        out_specs=[pl.BlockSpec((tq, wide), lambda h, i: (i, h))] * 2 + [pl.BlockSpec(memory_space=pltpu.SMEM)],
        out_shape=[jax.ShapeDtypeStruct((s, SB_WIDTH), BF16), jax.ShapeDtypeStruct((s, SB_WIDTH), F32),
                   jax.ShapeDtypeStruct((SB_HEADS // nh, nq), jnp.int32)],
        scratch_shapes=[pltpu.VMEM((tq, wide), F32), pltpu.VMEM((tq, wide), F32)],
        compiler_params=_params(("arbitrary", "arbitrary")),
    )(qkv, qkv, qkv, after)


def _sb_attn_bwd(qkv, rc, trips, do, *, name):
    s = qkv.shape[0]
    tq = _pick(s, SB_TQ)
    nq = s // tq
    per = tq // BLOCK
    base = 3 * DIL_WIDTH // HEAD_DIM
    _, before = _sb_consts()

    nh = SB_NH_BWD
    wide = nh * HEAD_DIM

    def body(trips_ref, q_ref, k_ref, v_ref, w_ref, rc_ref, do_ref, dq_ref, dk_ref, dv_ref, acc_ref, pre_ref, dob_ref):
        qi = pl.program_id(1)

        @pl.when(qi == 0)
        def _():
            dk_ref[...] = jnp.zeros_like(dk_ref)
            dv_ref[...] = jnp.zeros_like(dv_ref)

        w = w_ref[...]
        wa = (lax.broadcasted_iota(jnp.int32, (BLOCK, BLOCK), 0)
              > lax.broadcasted_iota(jnp.int32, (BLOCK, BLOCK), 1)).astype(BF16)
        dob_ref[...] = do_ref[...].astype(BF16)
        acc_ref[...] = jnp.zeros_like(acc_ref)
        pre_ref[...] = jnp.zeros_like(pre_ref)
        lane = lax.broadcasted_iota(jnp.int32, (tq, BLOCK), 1)
        n_chunks = (qi + 1) * per

        def chunks(js, masked):
            sls = [slice(h * HEAD_DIM, (h + 1) * HEAD_DIM) for h in range(nh)]
            work = [(sl, j, pl.multiple_of(j * BLOCK, BLOCK), _sb_mask(qi * tq, j * BLOCK, tq) if masked else None)
                    for j in js for sl in sls]
            zs = [_dot_nt(q_ref[:, sl], k_ref[pl.ds(start, BLOCK), sl]) * SCALE for sl, _, start, _ in work]
            das = [_dot_nt(dob_ref[:, sl], v_ref[pl.ds(start, BLOCK), sl]) for sl, _, start, _ in work]
            sigs, betweens = [], []
            for z, (_, _, _, mask) in zip(zs, work):
                sp = _softplus(z)
                sigs.append(jnp.exp(z - sp))
                betweens.append(_split_dot(jnp.where(mask, -sp, 0.0) if masked else -sp, wa))
            es, avs, sums = [], [], []
            for (sl, j, _, mask), sig, bt, da in zip(work, sigs, betweens, das):
                run = jnp.sum(jnp.where(lane == j, rc_ref[:, sl], 0.0), axis=-1, keepdims=True)
                a = sig * jnp.exp(run + bt)
                if masked:
                    a = jnp.where(mask, a, 0.0)
                e = a * da
                es.append(e)
                avs.append(a.astype(BF16))
                sums.append(_split_dot(e, w))
            for (sl, _, start, mask), sig, e, ab, sm in zip(work, sigs, es, avs, sums):
                pre = pre_ref[:, sl]
                dz = (e * (1.0 - sig) - sig * (pre + sm[:, :BLOCK])) * SCALE
                if masked:
                    dz = jnp.where(mask, dz, 0.0)
                dz = dz.astype(BF16)
                pre_ref[:, sl] = pre + sm[:, BLOCK:]
                acc_ref[:, sl] += _dot(dz, k_ref[pl.ds(start, BLOCK), sl])
                dk_ref[pl.ds(start, BLOCK), sl] += _dot_tn(dz, q_ref[:, sl])
                dv_ref[pl.ds(start, BLOCK), sl] += _dot_tn(ab, dob_ref[:, sl])

        def left(t, carry):
            chunks([SB_GROUP * t + g for g in range(SB_GROUP)], False)
            return carry

        def diagonal(t, carry):
            chunks([n_chunks - per + SB_GROUP * t + g for g in range(SB_GROUP)], True)
            return carry

        n_left = (n_chunks - per) // SB_GROUP
        ran = trips_ref[pl.program_id(0) // (SB_NH_FWD // nh), qi]
        lax.fori_loop(n_left - ran, n_left, left, 0)
        lax.fori_loop(0, per // SB_GROUP, diagonal, 0)
        dq_ref[...] = acc_ref[...]

    head = lambda off: (lambda h, i, t: (0, (base + off) // nh + h))
    blk = pl.BlockSpec((tq, wide), lambda h, i, t: (i, h))
    whole = pl.BlockSpec((s, wide), lambda h, i, t: (0, h))
    once = pl.Buffered(1)
    grid_spec = pltpu.PrefetchScalarGridSpec(
        num_scalar_prefetch=1, grid=(SB_HEADS // nh, nq),
        in_specs=[pl.BlockSpec((tq, wide), lambda h, i, t: (i, base // nh + h)),
                  pl.BlockSpec((s, wide), head(SB_HEADS), pipeline_mode=once),
                  pl.BlockSpec((s, wide), head(2 * SB_HEADS), pipeline_mode=once),
                  pl.BlockSpec((BLOCK, 2 * BLOCK), lambda h, i, t: (0, 0)),
                  blk, blk],
        out_specs=[blk, whole, whole],
        scratch_shapes=[pltpu.VMEM((tq, wide), F32), pltpu.VMEM((tq, wide), F32), pltpu.VMEM((tq, wide), BF16)],
    )
    return pl.pallas_call(
        body, name=name, grid_spec=grid_spec,
        out_shape=[jax.ShapeDtypeStruct((s, SB_WIDTH), F32)] * 3,
        compiler_params=_params(("parallel", "arbitrary")),
    )(trips, qkv, qkv, qkv, before, rc, do)


GATE_TC = 512


def _gate_fwd(proj, gate_b, up_dil, up_sb, *, name):
    s, d = up_dil.shape
    tr = _pick(s, 512)
    g0 = QKV_WIDTH // GATE_TC
    nc = d // GATE_TC

    def body(ga_ref, gb_ref, b_ref, ud_ref, us_ref, o_ref):
        ga = jax.nn.sigmoid(ga_ref[...] + b_ref[0:1, :])
        gb = jax.nn.sigmoid(gb_ref[...] + b_ref[1:2, :])
        o_ref[...] = (ga * ud_ref[...] + gb * us_ref[...]).astype(o_ref.dtype)

    blk = pl.BlockSpec((tr, GATE_TC), lambda i, j: (i, j))
    return pl.pallas_call(
        body, name=name, grid=(s // tr, nc),
        in_specs=[pl.BlockSpec((tr, GATE_TC), lambda i, j: (i, g0 + j)),
                  pl.BlockSpec((tr, GATE_TC), lambda i, j: (i, g0 + nc + j)),
                  pl.BlockSpec((2, GATE_TC), lambda i, j: (0, j)), blk, blk],
        out_specs=blk,
        out_shape=jax.ShapeDtypeStruct((s, d), BF16),
        compiler_params=_params(("parallel", "parallel")),
    )(proj, proj, gate_b, up_dil, up_sb)


def _gate_bwd(proj, gate_b, up_dil, up_sb, dmixed, *, name):
    s, d = up_dil.shape
    tr = _pick(s, 512)
    g0 = QKV_WIDTH // GATE_TC
    nc = d // GATE_TC

    def body(ga_ref, gb_ref, b_ref, ud_ref, us_ref, dm_ref, dud_ref, dus_ref, dga_ref, dgb_ref, db_ref):
        ga = jax.nn.sigmoid(ga_ref[...] + b_ref[0:1, :])
        gb = jax.nn.sigmoid(gb_ref[...] + b_ref[1:2, :])
        dm = dm_ref[...]
        dud_ref[...] = (dm * ga).astype(BF16)
        dus_ref[...] = (dm * gb).astype(BF16)
        dla = dm * ud_ref[...] * ga * (1.0 - ga)
        dlb = dm * us_ref[...] * gb * (1.0 - gb)
        dga_ref[...] = dla.astype(BF16)
        dgb_ref[...] = dlb.astype(BF16)

        @pl.when(pl.program_id(1) == 0)
        def _():
            db_ref[...] = jnp.zeros_like(db_ref)

        db_ref[0:1, :] += jnp.sum(dla, axis=0, keepdims=True)
        db_ref[1:2, :] += jnp.sum(dlb, axis=0, keepdims=True)

    blk = pl.BlockSpec((tr, GATE_TC), lambda j, i: (i, j))
    dud, dus, dga, dgb, db = pl.pallas_call(
        body, name=name, grid=(nc, s // tr),
        in_specs=[pl.BlockSpec((tr, GATE_TC), lambda j, i: (i, g0 + j)),
                  pl.BlockSpec((tr, GATE_TC), lambda j, i: (i, g0 + nc + j)),
                  pl.BlockSpec((2, GATE_TC), lambda j, i: (0, j)), blk, blk, blk],
        out_specs=[blk, blk, blk, blk, pl.BlockSpec((2, GATE_TC), lambda j, i: (0, j))],
        out_shape=[jax.ShapeDtypeStruct((s, d), BF16)] * 4 + [jax.ShapeDtypeStruct((2, d), F32)],
        compiler_params=_params(("parallel", "arbitrary")),
    )(proj, proj, gate_b, up_dil, up_sb, dmixed)
    return dud, dus, jnp.concatenate([dga, dgb], axis=1), db


def _adamw(w, g, m, v, *, name):
    shape = w.shape
    cols = shape[-1]
    rows = w.size // cols
    tr = _pick(rows, max(8, (512 * 1024) // cols))
    if rows % tr or (tr % 8 and tr != rows):
        tr = rows
    c1 = 1.0 - ADAM_B1 ** ADAM_STEP
    c2 = 1.0 - ADAM_B2 ** ADAM_STEP

    def body(w_ref, g_ref, m_ref, v_ref, d_ref, nm_ref, nv_ref):
        gv = g_ref[...]
        nm = ADAM_B1 * m_ref[...] + (1.0 - ADAM_B1) * gv
        nv = ADAM_B2 * v_ref[...] + (1.0 - ADAM_B2) * (gv * gv)
        d_ref[...] = -ADAM_LR * ((nm / c1) / (jnp.sqrt(nv / c2) + ADAM_EPS) + ADAM_WD * w_ref[...])
        nm_ref[...] = nm
        nv_ref[...] = nv

    blk = pl.BlockSpec((tr, cols), lambda i: (i, 0))
    outs = pl.pallas_call(
        body, name=name, grid=(rows // tr,),
        in_specs=[blk] * 4, out_specs=[blk] * 3,
        out_shape=[jax.ShapeDtypeStruct((rows, cols), F32)] * 3,
        compiler_params=_params(("parallel",)),
    )(*(t.reshape(rows, cols) for t in (w, g, m, v)))
    return tuple(t.reshape(shape) for t in outs)


def _add_half(full, other, which, *, name):
    n, rh, c = other.shape
    tr = _pick(rh, max(8, (1024 * 1024) // c))
    nb = rh // tr

    def body(which_ref, a_ref, b_ref, o_ref):
        o_ref[...] = (a_ref[...] + b_ref[...]).astype(o_ref.dtype)

    grid_spec = pltpu.PrefetchScalarGridSpec(
        num_scalar_prefetch=1, grid=(n, nb),
        in_specs=[pl.BlockSpec((None, tr, c), lambda s, i, w: (s, w[0] * nb + i, 0)),
                  pl.BlockSpec((None, tr, c), lambda s, i, w: (s, i, 0))],
        out_specs=pl.BlockSpec((None, tr, c), lambda s, i, w: (s, i, 0)),
    )
    return pl.pallas_call(
        body, name=name, grid_spec=grid_spec,
        out_shape=jax.ShapeDtypeStruct((n, rh, c), BF16),
        compiler_params=_params(("parallel", "parallel")),
    )(which, full, other)


def _sum_chips(own, parts, chip, *, name):
    n, r, c = parts.shape
    tr = _pick(r, max(16, (1024 * 1024) // c))

    def body(chip_ref, own_ref, p1_ref, p2_ref, p3_ref, o_ref):
        acc = own_ref[...].astype(F32)
        for p in (p1_ref, p2_ref, p3_ref):
            acc = acc + p[...].astype(F32)
        o_ref[...] = acc

    def slot(k):
        return pl.BlockSpec((None, tr, c), lambda i, o: (o[0] ^ k, i, 0))

    grid_spec = pltpu.PrefetchScalarGridSpec(
        num_scalar_prefetch=1, grid=(r // tr,),
        in_specs=[slot(0), slot(1), slot(2), slot(3)],
        out_specs=pl.BlockSpec((tr, c), lambda i, o: (i, 0)),
    )
    return pl.pallas_call(
        body, name=name, grid_spec=grid_spec,
        out_shape=jax.ShapeDtypeStruct((r, c), F32),
        compiler_params=_params(("parallel",)),
    )(chip, own, parts, parts, parts)


def _place_piece(full, piece, index, *, name):
    a, n, r, c = full.shape
    tr = _pick(r, max(16, (1024 * 1024) // c))

    def body(index_ref, piece_ref, full_ref, o_ref):
        o_ref[...] = piece_ref[...]

    grid_spec = pltpu.PrefetchScalarGridSpec(
        num_scalar_prefetch=1, grid=(a, r // tr),
        in_specs=[pl.BlockSpec((None, tr, c), lambda l, i, k: (l, i, 0)), ANY],
        out_specs=pl.BlockSpec((None, None, tr, c), lambda l, i, k: (l, k[0], i, 0)),
    )
    return pl.pallas_call(
        body, name=name, grid_spec=grid_spec,
        out_shape=jax.ShapeDtypeStruct(full.shape, full.dtype),
        input_output_aliases={2: 0},
        compiler_params=_params(("parallel", "parallel")),
    )(index, piece, full)


ANY = pl.BlockSpec(memory_space=pl.ANY)


def _place():
    x, y, c = lax.axis_index("x"), lax.axis_index("y"), lax.axis_index("c")
    chips = [(1 - x, y), (x, 1 - y), (1 - x, 1 - y)]
    return x, y, c, chips


def _remote(src, dst, send_sem, recv_sem, to):
    return pltpu.make_async_remote_copy(src_ref=src, dst_ref=dst, send_sem=send_sem, recv_sem=recv_sem,
                                        device_id=to, device_id_type=MESH)


def _gather_weights(shards):
    n = len(shards)

    def body(*refs):
        ins, outs = refs[:n], refs[n:2 * n]
        send_sems, recv_sems, fwd_send, fwd_recv = refs[2 * n:]
        x, y, c, chips = _place()
        me = 2 * x + y
        sibling = (x, y, 1 - c)

        def half(ref, chip, which):
            rh = ref.shape[2] // 2
            return ref.at[:, chip, pl.ds(which * rh, rh), :]

        def src_half(ref, which):
            rh = ref.shape[1] // 2
            return ref.at[:, pl.ds(which * rh, rh), :]

        first = []
        for t in range(n):
            for k, (px, py) in enumerate(chips):
                cp = _remote(src_half(ins[t], c), half(outs[t], me, c), send_sems.at[t, k], recv_sems.at[t, k],
                             (px, py, c))
                cp.start()
                first.append(cp)
        passed = []
        for t in range(n):
            for k, (px, py) in enumerate(chips):
                theirs = 2 * px + py
                landed = half(outs[t], theirs, c)
                _remote(landed, landed, send_sems.at[t, k], recv_sems.at[t, k], (px, py, c)).wait_recv()
                cp = _remote(landed, landed, fwd_send.at[t, k], fwd_recv.at[t, k], sibling)
                cp.start()
                passed.append(cp)
        for t in range(n):
            for k, (px, py) in enumerate(chips):
                theirs = 2 * px + py
                other = half(outs[t], theirs, 1 - c)
                _remote(other, other, fwd_send.at[t, k], fwd_recv.at[t, k], sibling).wait_recv()
        for cp in first + passed:
            cp.wait_send()

    return pl.pallas_call(
        body, name="gather_weights",
        in_specs=[ANY] * n, out_specs=[ANY] * n,
        out_shape=[jax.ShapeDtypeStruct((w.shape[0], N_CHIPS) + w.shape[1:], w.dtype) for w in shards],
        scratch_shapes=[pltpu.SemaphoreType.DMA((n, 3)), pltpu.SemaphoreType.DMA((n, 3)),
                        pltpu.SemaphoreType.DMA((n, 3)), pltpu.SemaphoreType.DMA((n, 3))],
        compiler_params=pltpu.CompilerParams(has_side_effects=True),
    )(*shards)


def _pair_swap_halves(grads, *, name):
    n = len(grads)

    def body(*refs):
        ins, outs = refs[:n], refs[n:2 * n]
        send_sems, recv_sems = refs[2 * n:]
        x, y, c, _ = _place()
        sibling = (x, y, 1 - c)
        cps = []
        for t in range(n):
            rh = ins[t].shape[1] // 2
            cp = _remote(ins[t].at[:, pl.ds((1 - c) * rh, rh), :], outs[t], send_sems.at[t], recv_sems.at[t], sibling)
            cp.start()
            cps.append(cp)
        for cp in cps:
            cp.wait()

    return pl.pallas_call(
        body, name=name,
        in_specs=[ANY] * n, out_specs=[ANY] * n,
        out_shape=[jax.ShapeDtypeStruct((g.shape[0], g.shape[1] // 2, g.shape[2]), g.dtype) for g in grads],
        scratch_shapes=[pltpu.SemaphoreType.DMA((n,)), pltpu.SemaphoreType.DMA((n,))],
        compiler_params=pltpu.CompilerParams(has_side_effects=True),
    )(*grads)


def _chip_exchange(parts, *, name):
    n = len(parts)

    def body(*refs):
        ins, outs = refs[:n], refs[n:2 * n]
        send_sems, recv_sems = refs[2 * n:]
        x, y, c, chips = _place()
        me = 2 * x + y
        cps = []
        for t in range(n):
            for k, (px, py) in enumerate(chips):
                cp = _remote(ins[t].at[2 * px + py], outs[t].at[me], send_sems.at[t, k], recv_sems.at[t, k], (px, py, c))
                cp.start()
                cps.append(cp)
        for t in range(n):
            for k, (px, py) in enumerate(chips):
                theirs = outs[t].at[2 * px + py]
                _remote(theirs, theirs, send_sems.at[t, k], recv_sems.at[t, k], (px, py, c)).wait_recv()
        for cp in cps:
            cp.wait_send()

    return pl.pallas_call(
        body, name=name,
        in_specs=[ANY] * n, out_specs=[ANY] * n,
        out_shape=[jax.ShapeDtypeStruct(p.shape, p.dtype) for p in parts],
        scratch_shapes=[pltpu.SemaphoreType.DMA((n, 3)), pltpu.SemaphoreType.DMA((n, 3))],
        compiler_params=pltpu.CompilerParams(has_side_effects=True),
    )(*parts)


def _pair_join_halves(halves, *, name):
    n = len(halves)

    def body(*refs):
        ins, outs = refs[:n], refs[n:2 * n]
        send_sems, recv_sems = refs[2 * n:]
        x, y, c, _ = _place()
        sibling = (x, y, 1 - c)
        cps = []
        for t in range(n):
            rh = ins[t].shape[0]
            cp = _remote(ins[t], outs[t].at[pl.ds(c * rh, rh), :], send_sems.at[t], recv_sems.at[t], sibling)
            cp.start()
            cps.append(cp)
        for t, cp in enumerate(cps):
            rh = ins[t].shape[0]
            theirs = outs[t].at[pl.ds((1 - c) * rh, rh), :]
            _remote(theirs, theirs, send_sems.at[t], recv_sems.at[t], sibling).wait_recv()
            cp.wait_send()

    return pl.pallas_call(
        body, name=name,
        in_specs=[ANY] * n, out_specs=[ANY] * n,
        out_shape=[jax.ShapeDtypeStruct((2 * h.shape[0], h.shape[1]), h.dtype) for h in halves],
        scratch_shapes=[pltpu.SemaphoreType.DMA((n,)), pltpu.SemaphoreType.DMA((n,))],
        compiler_params=pltpu.CompilerParams(has_side_effects=True),
    )(*halves)


def _exchange_small(buf, *, name):
    rows, lanes = buf.shape

    def body(in_ref, out_ref, send_sems, recv_sems, local_sem):
        x, y, c = lax.axis_index("x"), lax.axis_index("y"), lax.axis_index("c")
        me = 4 * x + 2 * y + c
        lc = pltpu.make_async_copy(in_ref, out_ref.at[me], local_sem)
        lc.start()
        cps = []
        for k in range(1, 8):
            fx, fy, fc = (k >> 2) & 1, (k >> 1) & 1, k & 1
            to = (x ^ fx, y ^ fy, c ^ fc)
            cp = _remote(in_ref, out_ref.at[me], send_sems.at[k - 1], recv_sems.at[k - 1], to)
            cp.start()
            cps.append(cp)
        for k in range(1, 8):
            fx, fy, fc = (k >> 2) & 1, (k >> 1) & 1, k & 1
            theirs = out_ref.at[4 * (x ^ fx) + 2 * (y ^ fy) + (c ^ fc)]
            _remote(theirs, theirs, send_sems.at[k - 1], recv_sems.at[k - 1], (x ^ fx, y ^ fy, c ^ fc)).wait_recv()
        for cp in cps:
            cp.wait_send()
        lc.wait()

    return pl.pallas_call(
        body, name=name,
        in_specs=[ANY], out_specs=ANY,
        out_shape=jax.ShapeDtypeStruct((8, rows, lanes), buf.dtype),
        scratch_shapes=[pltpu.SemaphoreType.DMA((7,)), pltpu.SemaphoreType.DMA((7,)), pltpu.SemaphoreType.DMA],
        compiler_params=pltpu.CompilerParams(has_side_effects=True),
    )(buf)


def _sum_slots(slots, *, name):
    n, rows, lanes = slots.shape

    def body(s_ref, o_ref):
        acc = s_ref[0]
        for t in range(1, n):
            acc = acc + s_ref[t]
        o_ref[...] = acc

    return pl.pallas_call(
        body, name=name,
        out_shape=jax.ShapeDtypeStruct((rows, lanes), slots.dtype),
    )(slots)


def _relu2(acc):
    r = jnp.maximum(acc, 0.0)
    return acc, r * r


def _layer_fwd(x, w, cos, sin, tag):
    h = _rmsnorm_fwd(x, w["norm1_g"], name=f"norm1_{tag}")
    proj = _matmul(h, w["w_in"], mode="nn", b_sharded=True, tm=512, name=f"proj_{tag}")
    qkv = _qkv_prep(proj, w["q_norm_g"], w["k_norm_g"], cos, sin, name=f"qkv_prep_{tag}")
    outs, lds = [], []
    for g in range(N_GROUPS):
        o_g, ld_g = _dil_attn_fwd(qkv, g, name=f"dil{g}_fwd_{tag}")
        outs.append(o_g)
        lds.append(ld_g)
    y, yb, lse = _dil_merge(outs, lds, name=f"dil_merge_{tag}")
    o_sb, rc, trips = _sb_attn_fwd(qkv, name=f"sb_fwd_{tag}")
    up_dil = _matmul(yb, w["w_up_dil"], mode="nn", b_sharded=True, name=f"up_dil_{tag}")
    up_sb = _matmul(o_sb, w["w_up_sb"], mode="nn", b_sharded=True, name=f"up_sb_{tag}")
    mixed = _gate_fwd(proj, w["gate_b"], up_dil, up_sb, name=f"gate_{tag}")
    x1 = _matmul(mixed, w["w_out"], mode="nn", extras=(x,), epilogue=lambda acc, res: (acc + res,), name=f"out_{tag}")
    h2 = _rmsnorm_fwd(x1, w["norm2_g"], name=f"norm2_{tag}")
    u, a = _matmul(h2, w["w_ff1"], mode="nn", b_sharded=True, out_dtypes=(F32, BF16), epilogue=_relu2,
                   name=f"ff1_{tag}")
    x2 = _matmul(a, w["w_ff2"], mode="nn", extras=(x1,), epilogue=lambda acc, res: (acc + res,), name=f"ff2_{tag}")
    saved = dict(x=x, h=h, proj=proj, qkv=qkv, y=y, yb=yb, lse=lse, o_sb=o_sb, rc=rc, trips=trips, up_dil=up_dil, up_sb=up_sb,
                 mixed=mixed, x1=x1, h2=h2, u=u, a=a)
    return x2, saved


def _layer_bwd(dx, dxb, w, sv, cos, sin, tag):
    grads = {}
    du = _matmul(dxb, w["w_ff2"], mode="nt", extras=(sv["u"],), out_dtypes=(BF16,),
                 epilogue=lambda acc, u: (acc * (2.0 * jnp.maximum(u, 0.0)),), name=f"d_a_{tag}")
    grads["w_ff2"] = _matmul(sv["a"], dxb, mode="tn", name=f"dw_ff2_{tag}")
    dh2 = _matmul(du, w["w_ff1"], mode="nt", b_sharded=True, name=f"d_h2_{tag}")
    grads["w_ff1"] = _matmul(sv["h2"], du, mode="tn", out_sharded=True, name=f"dw_ff1_{tag}")
    dx1, dx1b, grads["norm2_g"] = _rmsnorm_bwd(sv["x1"], w["norm2_g"], dh2, dx, name=f"norm2_bwd_{tag}")
    dmixed = _matmul(dx1b, w["w_out"], mode="nt", name=f"d_mixed_{tag}")
    grads["w_out"] = _matmul(sv["mixed"], dx1b, mode="tn", name=f"dw_out_{tag}")
    dud, dus, dgate, grads["gate_b"] = _gate_bwd(sv["proj"], w["gate_b"], sv["up_dil"], sv["up_sb"], dmixed,
                                                 name=f"gate_bwd_{tag}")
    dy_dil = _matmul(dud, w["w_up_dil"], mode="nt", b_sharded=True, name=f"d_ydil_{tag}")
    grads["w_up_dil"] = _matmul(sv["yb"], dud, mode="tn", out_sharded=True, name=f"dw_up_dil_{tag}")
    dy_sb = _matmul(dus, w["w_up_sb"], mode="nt", b_sharded=True, name=f"d_ysb_{tag}")
    grads["w_up_sb"] = _matmul(sv["o_sb"], dus, mode="tn", out_sharded=True, name=f"dw_up_sb_{tag}")
    d_dil = [_dil_attn_bwd(sv["qkv"], sv["y"], sv["lse"], dy_dil, g, name=f"dil{g}_bwd_{tag}") for g in range(N_GROUPS)]
    d_sb = _sb_attn_bwd(sv["qkv"], sv["rc"], sv["trips"], dy_sb, name=f"sb_bwd_{tag}")
    dproj, grads["q_norm_g"], grads["k_norm_g"] = _qkv_prep_bwd(
        sv["proj"], w["q_norm_g"], w["k_norm_g"], cos, sin, d_dil, d_sb, dgate, name=f"qkv_prep_bwd_{tag}")
    dh = _matmul(dproj, w["w_in"], mode="nt", b_sharded=True, tm=512, name=f"d_h_{tag}")
    grads["w_in"] = _matmul(sv["h"], dproj, mode="tn", out_sharded=True, tm=512, name=f"dw_in_{tag}")
    dx0, dx0b, grads["norm1_g"] = _rmsnorm_bwd(sv["x"], w["norm1_g"], dh, dx1, name=f"norm1_bwd_{tag}")
    return dx0, dx0b, grads


def _local_step(x, target, layers, cos, sin):
    saved = []
    for l, w in enumerate(layers):
        x, sv = _layer_fwd(x, w, cos, sin, f"l{l}")
        saved.append(sv)
    loss, dx, dxb = _loss_head(x, target)
    grads = [None] * len(layers)
    for l in reversed(range(len(layers))):
        dx, dxb, grads[l] = _layer_bwd(dx, dxb, layers[l], saved[l], cos, sin, f"l{l}")
    return loss, dx, grads


BIG = ("w_in", "w_up_dil", "w_up_sb", "w_out", "w_ff1", "w_ff2")
ROW_SHARDED = ("w_out", "w_ff2")
WEIGHTS = ("norm1_g", "w_in", "q_norm_g", "k_norm_g", "w_up_dil", "w_up_sb", "gate_b", "w_out", "norm2_g", "w_ff1", "w_ff2")


def _reduce_to_shards(grads, which, chip, tag):
    names = list(grads)
    full = [grads[n] for n in names]
    theirs = _pair_swap_halves(full, name=f"rs_pair_{tag}")
    pair = [_add_half(f, t, which, name=f"rs_pair_sum_{n}_{tag}") for n, f, t in zip(names, full, theirs)]
    parts = _chip_exchange(pair, name=f"rs_chips_{tag}")
    halves = [_sum_chips(p, q, chip, name=f"rs_chip_sum_{n}_{tag}") for n, p, q in zip(names, pair, parts)]
    joined = _pair_join_halves(halves, name=f"rs_join_{tag}")
    out = {}
    for n, j, h in zip(names, joined, halves):
        r, c = j.shape
        out[n] = _place_piece(j.reshape(1, 2, r // 2, c), h[None], which, name=f"rs_place_{n}_{tag}").reshape(r, c)
    return out


def _pack_rows(vecs):
    rows, spans, at = [], [], 0
    for v in vecs:
        r = v.size // 128
        padded = -(-r // 8) * 8
        rows.append(jnp.pad(v.reshape(r, 128), ((0, padded - r), (0, 0))))
        spans.append((at, r))
        at += padded
    return jnp.concatenate(rows, axis=0), spans


def kernel(x, norm1_g, w_in, q_norm_g, k_norm_g, w_up_dil, w_up_sb, gate_b, w_out, norm2_g, w_ff1, w_ff2, loss_target, m_norm1_g, m_w_in, m_q_norm_g, m_k_norm_g, m_w_up_dil, m_w_up_sb, m_gate_b, m_w_out, m_norm2_g, m_w_ff1, m_w_ff2, v_norm1_g, v_w_in, v_q_norm_g, v_k_norm_g, v_w_up_dil, v_w_up_sb, v_gate_b, v_w_out, v_norm2_g, v_w_ff1, v_w_ff2):
    weights = dict(norm1_g=norm1_g, w_in=w_in, q_norm_g=q_norm_g, k_norm_g=k_norm_g, w_up_dil=w_up_dil,
                   w_up_sb=w_up_sb, gate_b=gate_b, w_out=w_out, norm2_g=norm2_g, w_ff1=w_ff1, w_ff2=w_ff2)
    moments_m = dict(norm1_g=m_norm1_g, w_in=m_w_in, q_norm_g=m_q_norm_g, k_norm_g=m_k_norm_g, w_up_dil=m_w_up_dil,
                     w_up_sb=m_w_up_sb, gate_b=m_gate_b, w_out=m_w_out, norm2_g=m_norm2_g, w_ff1=m_w_ff1, w_ff2=m_w_ff2)
    moments_v = dict(norm1_g=v_norm1_g, w_in=v_w_in, q_norm_g=v_q_norm_g, k_norm_g=v_k_norm_g, w_up_dil=v_w_up_dil,
                     w_up_sb=v_w_up_sb, gate_b=v_gate_b, w_out=v_w_out, norm2_g=v_norm2_g, w_ff1=v_w_ff1, w_ff2=v_w_ff2)
    depth = w_in.shape[0]
    seq, d_model = x.shape[1], x.shape[2]
    chip = 2 * lax.axis_index("x") + lax.axis_index("y")
    core = lax.axis_index("c")
    which = jnp.reshape(core, (1,)).astype(jnp.int32)

    chip_index = jnp.reshape(chip, (1,)).astype(jnp.int32)
    shards = [weights[n].astype(BF16) for n in BIG]
    gathered = {n: _place_piece(g, s, chip_index, name=f"gather_place_{n}")
                for n, g, s in zip(BIG, _gather_weights(shards), shards)}
    bias_buf, ((_, bias_rows),) = _pack_rows([gate_b])
    bias_slots = _exchange_small(bias_buf, name="gather_gate_b")
    bias = bias_slots[0::2, :bias_rows].reshape(N_CHIPS, depth, 2, d_model // N_CHIPS)
    bias = jnp.transpose(bias, (1, 2, 0, 3)).reshape(depth, 2, d_model)
    layers = []
    for l in range(depth):
        w = {n: gathered[n][l] for n in BIG}
        for n in ROW_SHARDED:
            w[n] = w[n].reshape(-1, w[n].shape[-1])
        w.update(norm1_g=norm1_g[l], norm2_g=norm2_g[l], q_norm_g=q_norm_g[l], k_norm_g=k_norm_g[l], gate_b=bias[l])
        layers.append(w)

    cos, sin = _rope_tables(seq)
    loss, dx, grads = _local_step(x[0], loss_target[0], layers, cos, sin)
    loss = lax.psum(loss, ("x", "y", "c"))

    reduced = []
    for l in range(depth):
        g = {n: grads[l][n] for n in BIG}
        for n in ROW_SHARDED:
            g[n] = g[n].reshape(N_CHIPS, -1, g[n].shape[-1])
        reduced.append(_reduce_to_shards(g, which, chip_index, f"l{l}"))
    final = {n: jnp.stack([reduced[l][n] for l in range(depth)]) for n in BIG}

    small_names = ("norm1_g", "norm2_g", "q_norm_g", "k_norm_g", "gate_b")
    packed, spans = _pack_rows([grads[l][n] for l in range(depth) for n in small_names])
    total = _sum_slots(_exchange_small(packed, name="exchange_small_grads"), name="sum_small_grads")
    pieces = iter(total[at:at + r] for at, r in spans)
    small = {n: [] for n in small_names}
    for l in range(depth):
        for n in small_names:
            small[n].append(next(pieces))
    for n in ("norm1_g", "norm2_g"):
        final[n] = jnp.stack([p.reshape(d_model) for p in small[n]])
    for n in ("q_norm_g", "k_norm_g"):
        final[n] = jnp.stack([p.reshape(N_GROUPS, HEAD_DIM) for p in small[n]])
    shard_cols = d_model // N_CHIPS
    final["gate_b"] = jnp.stack([lax.dynamic_slice_in_dim(p.reshape(2, d_model), chip * shard_cols, shard_cols, axis=1)
                                 for p in small["gate_b"]])

    deltas, new_m, new_v = {}, {}, {}
    for n in WEIGHTS:
        deltas[n], new_m[n], new_v[n] = _adamw(weights[n], final[n], moments_m[n], moments_v[n], name=f"adamw_{n}")
    return (loss, dx[None], *[final[n] for n in WEIGHTS], *[deltas[n] for n in WEIGHTS],
            *[new_m[n] for n in WEIGHTS], *[new_v[n] for n in WEIGHTS])
```

```python
import functools
import math

import jax
import jax.numpy as jnp
from jax import lax
from jax.experimental import pallas as pl
from jax.experimental.pallas import tpu as pltpu

F32 = jnp.float32
BF16 = jnp.bfloat16

HEAD_DIM = 128
DIL_GROUPS = ((128, 1), (512, 4), (2048, 16))
N_GROUPS = 3
HEADS_PER_GROUP = 4
GROUP_WIDTH = HEADS_PER_GROUP * HEAD_DIM
DIL_WIDTH = N_GROUPS * GROUP_WIDTH
SB_HEADS = 8
SB_WIDTH = SB_HEADS * HEAD_DIM
QKV_WIDTH = 3 * DIL_WIDTH + 3 * SB_WIDTH
BLOCK = 128
ROPE_THETA = 10000.0
EPS = 1e-6
SCALE = 1.0 / math.sqrt(HEAD_DIM)
NEG_INF = float("-inf")

ADAM_LR = 0.001
ADAM_B1 = 0.9
ADAM_B2 = 0.999
ADAM_EPS = 1e-08
ADAM_WD = 0.01
ADAM_STEP = 10

N_CHIPS = 4
MESH = pl.DeviceIdType.MESH
MIB = 1024 * 1024
VMEM_LIMIT = 56 * MIB


def _params(semantics=None, vmem=VMEM_LIMIT):
    return pltpu.CompilerParams(dimension_semantics=semantics, vmem_limit_bytes=vmem)


def _pick(n, pref):
    if n <= pref:
        return n
    t = 1 << (pref.bit_length() - 1)
    while n % t:
        t //= 2
    return t


def _tile(n, pref):
    best = 0
    for t in range(256, min(n, pref) + 1, 256):
        if n % t == 0:
            best = t
    return best if best else n


def _matmul(a, b, *, mode, name, out_dtypes=(F32,), extras=(), epilogue=None, b_sharded=False, out_sharded=False,
            tm=1024, tn=1024, tk=2048):
    if mode == "nn":
        m, kdim = a.shape
        n = b.shape[-1] * (N_CHIPS if b_sharded else 1)
        assert (b.shape[-2] if b_sharded else b.shape[0]) == kdim
    elif mode == "nt":
        m, kdim = a.shape
        n = b.shape[-2]
        assert b.shape[-1] * (N_CHIPS if b_sharded else 1) == kdim
    else:
        kdim, m = a.shape
        n = b.shape[1]
        assert b.shape[0] == kdim and not b_sharded
    shard_n = n // N_CHIPS if (out_sharded or (b_sharded and mode == "nn")) else n
    shard_k = kdim // N_CHIPS if (b_sharded and mode == "nt") else kdim
    tm = _tile(m, tm)
    tn = _tile(shard_n, tn)
    tk = _tile(shard_k, tk)
    nk = kdim // tk
    nj_n = shard_n // tn
    nj_k = shard_k // tk
    j_outer = nk == 1 and (kdim * n + m * kdim * (n // tn)) < (m * kdim + kdim * n * (m // tm))
    grid = (n // tn, m // tm, nk) if j_outer else (m // tm, n // tn, nk)

    def at(index):
        return (lambda g0, g1, k: index(g1, g0, k)) if j_outer else index

    if mode == "nn":
        a_spec = pl.BlockSpec((tm, tk), at(lambda i, j, k: (i, k)))
        if b_sharded:
            b_spec = pl.BlockSpec((None, tk, tn), at(lambda i, j, k: (j // nj_n, k, j % nj_n)))
        else:
            b_spec = pl.BlockSpec((tk, tn), at(lambda i, j, k: (k, j)))
        dims = (((1,), (0,)), ((), ()))
    elif mode == "nt":
        a_spec = pl.BlockSpec((tm, tk), at(lambda i, j, k: (i, k)))
        if b_sharded:
            b_spec = pl.BlockSpec((None, tn, tk), at(lambda i, j, k: (k // nj_k, j, k % nj_k)))
        else:
            b_spec = pl.BlockSpec((tn, tk), at(lambda i, j, k: (j, k)))
        dims = (((1,), (1,)), ((), ()))
    else:
        a_spec = pl.BlockSpec((tk, tm), at(lambda i, j, k: (k, i)))
        b_spec = pl.BlockSpec((tk, tn), at(lambda i, j, k: (k, j)))
        dims = (((0,), (0,)), ((), ()))
    if out_sharded:
        o_spec = pl.BlockSpec((None, tm, tn), at(lambda i, j, k: (j // nj_n, i, j % nj_n)))
        o_shape = (N_CHIPS, m, shard_n)
    else:
        o_spec = pl.BlockSpec((tm, tn), at(lambda i, j, k: (i, j)))
        o_shape = (m, n)
    x_spec = pl.BlockSpec((tm, tn), at(lambda i, j, k: (i, j)))
    n_extra = len(extras)
    n_out = len(out_dtypes)

    def finish(acc, extra_refs, out_refs):
        res = (acc,) if epilogue is None else epilogue(acc, *[e[...] for e in extra_refs])
        for o, r in zip(out_refs, res):
            o[...] = r.astype(o.dtype)

    def body_single(a_ref, b_ref, *rest):
        finish(lax.dot_general(a_ref[...], b_ref[...], dims, preferred_element_type=F32),
               rest[:n_extra], rest[n_extra:n_extra + n_out])

    def body_steps(a_ref, b_ref, *rest):
        acc = rest[-1]
        k = pl.program_id(2)

        @pl.when(k == 0)
        def _():
            acc[...] = jnp.zeros_like(acc)

        acc[...] += lax.dot_general(a_ref[...], b_ref[...], dims, preferred_element_type=F32)

        @pl.when(k == nk - 1)
        def _():
            finish(acc[...], rest[:n_extra], rest[n_extra:n_extra + n_out])

    outs = pl.pallas_call(
        body_single if nk == 1 else body_steps,
        name=name,
        grid=grid,
        in_specs=[a_spec, b_spec] + [x_spec] * n_extra,
        out_specs=[o_spec] * n_out,
        out_shape=[jax.ShapeDtypeStruct(o_shape, dt) for dt in out_dtypes],
        scratch_shapes=[] if nk == 1 else [pltpu.VMEM((tm, tn), F32)],
        compiler_params=_params(("parallel", "parallel", "arbitrary")),
    )(a, b, *extras)
    return outs[0] if n_out == 1 else outs


def _rmsnorm_fwd(x, g, *, name):
    s, d = x.shape
    tr = _pick(s, 256)

    def body(x_ref, g_ref, o_ref):
        xv = x_ref[...]
        r = lax.rsqrt(jnp.mean(xv * xv, axis=-1, keepdims=True) + EPS)
        o_ref[...] = (xv * r * g_ref[...]).astype(o_ref.dtype)

    return pl.pallas_call(
        body, name=name, grid=(s // tr,),
        in_specs=[pl.BlockSpec((tr, d), lambda i: (i, 0)), pl.BlockSpec((1, d), lambda i: (0, 0))],
        out_specs=pl.BlockSpec((tr, d), lambda i: (i, 0)),
        out_shape=jax.ShapeDtypeStruct((s, d), BF16),
        compiler_params=_params(("parallel",)),
    )(x, g.reshape(1, d))


def _rmsnorm_bwd(x, g, dh, dres, *, name):
    s, d = x.shape
    tr = _pick(s, 256)

    def body(x_ref, g_ref, dh_ref, dres_ref, dx_ref, dxb_ref, dg_ref):
        xv = x_ref[...]
        r = lax.rsqrt(jnp.mean(xv * xv, axis=-1, keepdims=True) + EPS)
        y = xv * r
        dhv = dh_ref[...]
        dy = dhv * g_ref[...]
        dx = dres_ref[...] + r * (dy - y * jnp.mean(dy * y, axis=-1, keepdims=True))
        dx_ref[...] = dx
        dxb_ref[...] = dx.astype(BF16)

        @pl.when(pl.program_id(0) == 0)
        def _():
            dg_ref[...] = jnp.zeros_like(dg_ref)

        dg_ref[...] += jnp.sum(dhv * y, axis=0, keepdims=True)

    row = pl.BlockSpec((tr, d), lambda i: (i, 0))
    vec = pl.BlockSpec((1, d), lambda i: (0, 0))
    dx, dxb, dg = pl.pallas_call(
        body, name=name, grid=(s // tr,),
        in_specs=[row, vec, row, row],
        out_specs=[row, row, vec],
        out_shape=[jax.ShapeDtypeStruct((s, d), F32), jax.ShapeDtypeStruct((s, d), BF16),
                   jax.ShapeDtypeStruct((1, d), F32)],
        compiler_params=_params(("arbitrary",)),
    )(x, g.reshape(1, d), dh, dres)
    return dx, dxb, dg.reshape(d)


def _loss_head(y, target):
    s, d = y.shape
    tr = _pick(s, 256)

    def body(y_ref, t_ref, dy_ref, dyb_ref, part_ref):
        err = y_ref[...] - t_ref[...]
        dy = err * (1.0 / d)
        dy_ref[...] = dy
        dyb_ref[...] = dy.astype(BF16)

        @pl.when(pl.program_id(0) == 0)
        def _():
            part_ref[...] = jnp.zeros_like(part_ref)

        part_ref[...] += jnp.sum(err * err, axis=0, keepdims=True)

    row = pl.BlockSpec((tr, d), lambda i: (i, 0))
    vec = pl.BlockSpec((1, d), lambda i: (0, 0))
    dy, dyb, part = pl.pallas_call(
        body, name="loss_head", grid=(s // tr,),
        in_specs=[row, row], out_specs=[row, row, vec],
        out_shape=[jax.ShapeDtypeStruct((s, d), F32), jax.ShapeDtypeStruct((s, d), BF16),
                   jax.ShapeDtypeStruct((1, d), F32)],
        compiler_params=_params(("arbitrary",)),
    )(y, target)
    return 0.5 * jnp.sum(part) / d, dy, dyb


def _rope_tables(s):
    half = HEAD_DIM // 2
    inv_freq = ROPE_THETA ** (-jnp.arange(half, dtype=F32) / half)
    ang = jnp.arange(s, dtype=F32)[:, None] * inv_freq[None, :]
    cos, sin = jnp.cos(ang), jnp.sin(ang)
    return jnp.concatenate([cos, cos], axis=-1), jnp.concatenate([-sin, sin], axis=-1)


def _qkv_prep(proj, qg, kg, cos, sin, *, name):
    s = proj.shape[0]
    tr = _pick(s, 256)

    def body(p_ref, qg_ref, kg_ref, cos_ref, sin_ref, o_ref):
        c = cos_ref[...]
        sn = sin_ref[...]
        for which, g_ref in ((0, qg_ref), (1, kg_ref)):
            for g in range(N_GROUPS):
                gain = g_ref[g:g + 1, :]
                for h in range(HEADS_PER_GROUP):
                    off = which * DIL_WIDTH + g * GROUP_WIDTH + h * HEAD_DIM
                    xv = p_ref[:, off:off + HEAD_DIM]
                    r = lax.rsqrt(jnp.mean(xv * xv, axis=-1, keepdims=True) + EPS)
                    y = xv * r * gain
                    o_ref[:, off:off + HEAD_DIM] = (y * c + pltpu.roll(y, HEAD_DIM // 2, 1) * sn).astype(BF16)
        o_ref[:, 2 * DIL_WIDTH:] = p_ref[:, 2 * DIL_WIDTH:].astype(BF16)

    return pl.pallas_call(
        body, name=name, grid=(s // tr,),
        in_specs=[pl.BlockSpec((tr, QKV_WIDTH), lambda i: (i, 0)),
                  pl.BlockSpec((N_GROUPS, HEAD_DIM), lambda i: (0, 0)),
                  pl.BlockSpec((N_GROUPS, HEAD_DIM), lambda i: (0, 0)),
                  pl.BlockSpec((tr, HEAD_DIM), lambda i: (i, 0)),
                  pl.BlockSpec((tr, HEAD_DIM), lambda i: (i, 0))],
        out_specs=pl.BlockSpec((tr, QKV_WIDTH), lambda i: (i, 0)),
        out_shape=jax.ShapeDtypeStruct((s, QKV_WIDTH), BF16),
        compiler_params=_params(("parallel",)),
    )(proj, qg, kg, cos, sin)


def _qkv_prep_bwd(proj, qg, kg, cos, sin, dqkv_dil, dqkv_sb, dgate, *, name):
    s, n_in = proj.shape
    n_gate = dgate.shape[1]
    tr = _pick(s, 256)

    def body(p_ref, qg_ref, kg_ref, cos_ref, sin_ref, *rest):
        dil_refs = rest[:9]
        sb_refs = rest[9:12]
        dgate_ref = rest[12]
        o_ref, dqg_ref, dkg_ref = rest[13:16]
        c = cos_ref[...]
        sn = sin_ref[...]

        @pl.when(pl.program_id(0) == 0)
        def _():
            dqg_ref[...] = jnp.zeros_like(dqg_ref)
            dkg_ref[...] = jnp.zeros_like(dkg_ref)

        for which, g_ref, dg_ref in ((0, qg_ref, dqg_ref), (1, kg_ref, dkg_ref)):
            for g in range(N_GROUPS):
                gain = g_ref[g:g + 1, :]
                d_ref = dil_refs[3 * g + which]
                dgain = jnp.zeros((1, HEAD_DIM), F32)
                for h in range(HEADS_PER_GROUP):
                    off = which * DIL_WIDTH + g * GROUP_WIDTH + h * HEAD_DIM
                    xv = p_ref[:, off:off + HEAD_DIM]
                    r = lax.rsqrt(jnp.mean(xv * xv, axis=-1, keepdims=True) + EPS)
                    nx = xv * r
                    dout = d_ref[:, h * HEAD_DIM:(h + 1) * HEAD_DIM]
                    dy = dout * c + pltpu.roll(dout * sn, HEAD_DIM // 2, 1)
                    dgain = dgain + jnp.sum(dy * nx, axis=0, keepdims=True)
                    dn = dy * gain
                    dxv = r * (dn - nx * jnp.mean(dn * nx, axis=-1, keepdims=True))
                    o_ref[:, off:off + HEAD_DIM] = dxv.astype(BF16)
                dg_ref[g:g + 1, :] += dgain
        for g in range(N_GROUPS):
            off = 2 * DIL_WIDTH + g * GROUP_WIDTH
            o_ref[:, off:off + GROUP_WIDTH] = dil_refs[3 * g + 2][...].astype(BF16)
        for t in range(3):
            off = 3 * DIL_WIDTH + t * SB_WIDTH
            o_ref[:, off:off + SB_WIDTH] = sb_refs[t][...].astype(BF16)
        o_ref[:, QKV_WIDTH:] = dgate_ref[...]

    grp = pl.BlockSpec((tr, GROUP_WIDTH), lambda i: (i, 0))
    sbs = pl.BlockSpec((tr, SB_WIDTH), lambda i: (i, 0))
    gain_spec = pl.BlockSpec((N_GROUPS, HEAD_DIM), lambda i: (0, 0))
    tab = pl.BlockSpec((tr, HEAD_DIM), lambda i: (i, 0))
    flat_dil = [t for grp3 in dqkv_dil for t in grp3]
    return pl.pallas_call(
        body, name=name, grid=(s // tr,),
        in_specs=[pl.BlockSpec((tr, 2 * DIL_WIDTH), lambda i: (i, 0)), gain_spec, gain_spec, tab, tab]
                 + [grp] * 9 + [sbs] * 3 + [pl.BlockSpec((tr, n_gate), lambda i: (i, 0))],
        out_specs=[pl.BlockSpec((tr, n_in), lambda i: (i, 0)), gain_spec, gain_spec],
        out_shape=[jax.ShapeDtypeStruct((s, n_in), BF16),
                   jax.ShapeDtypeStruct((N_GROUPS, HEAD_DIM), F32),
                   jax.ShapeDtypeStruct((N_GROUPS, HEAD_DIM), F32)],
        compiler_params=_params(("arbitrary",)),
    )(proj, qg, kg, cos, sin, *flat_dil, *dqkv_sb, dgate)


def _dil_masks(i):
    row = lax.broadcasted_iota(jnp.int32, (BLOCK, BLOCK), 0)
    col = lax.broadcasted_iota(jnp.int32, (BLOCK, BLOCK), 1)
    return col <= row, (col >= row) & (i > 0)


def _dot_nt(a, b):
    return lax.dot_general(a, b, (((1,), (1,)), ((), ())), preferred_element_type=F32)


def _dot_tn(a, b):
    return lax.dot_general(a, b, (((0,), (0,)), ((), ())), preferred_element_type=F32)


def _dot(a, b):
    return jnp.dot(a, b, preferred_element_type=F32)


def _dil_attn_fwd(qkv, g, *, name):
    s = qkv.shape[0]
    r = DIL_GROUPS[g][1]
    length = s // r
    nb = length // BLOCK
    wb = QKV_WIDTH // GROUP_WIDTH
    view = qkv.reshape(length, r * QKV_WIDTH)

    def body(q_ref, kp_ref, kc_ref, vp_ref, vc_ref, o_ref, ld_ref):
        i = pl.program_id(1)
        mask_c, mask_p = _dil_masks(i)
        sls = [slice(h * HEAD_DIM, (h + 1) * HEAD_DIM) for h in range(HEADS_PER_GROUP)]
        scores = [(_dot_nt(q_ref[:, sl], kc_ref[:, sl]), _dot_nt(q_ref[:, sl], kp_ref[:, sl])) for sl in sls]
        probs = []
        for sl, (raw_c, raw_p) in zip(sls, scores):
            s_c = jnp.where(mask_c, raw_c * SCALE, NEG_INF)
            s_p = jnp.where(mask_p, raw_p * SCALE, NEG_INF)
            m = jnp.maximum(jnp.max(s_c, axis=-1, keepdims=True), jnp.max(s_p, axis=-1, keepdims=True))
            p_c = jnp.exp(s_c - m)
            p_p = jnp.exp(s_p - m)
            l = jnp.sum(p_c, axis=-1, keepdims=True) + jnp.sum(p_p, axis=-1, keepdims=True)
            inv = 1.0 / l
            probs.append(((p_c * inv).astype(BF16), (p_p * inv).astype(BF16)))
            ld_ref[:, sl] = jnp.broadcast_to(m + jnp.log(l), (BLOCK, HEAD_DIM))
        for sl, (pn_c, pn_p) in zip(sls, probs):
            o_ref[:, sl] = _dot(pn_c, vc_ref[:, sl]) + _dot(pn_p, vp_ref[:, sl])

    def col(which):
        return lambda c, i: (i, c * wb + 3 * which + g)

    def col_prev(which):
        return lambda c, i: (jnp.maximum(i - 1, 0), c * wb + 3 * which + g)

    blk = (BLOCK, GROUP_WIDTH)
    o, ld = pl.pallas_call(
        body, name=name, grid=(r, nb),
        in_specs=[pl.BlockSpec(blk, col(0)), pl.BlockSpec(blk, col_prev(1)), pl.BlockSpec(blk, col(1)),
                  pl.BlockSpec(blk, col_prev(2)), pl.BlockSpec(blk, col(2))],
        out_specs=[pl.BlockSpec(blk, lambda c, i: (i, c))] * 2,
        out_shape=[jax.ShapeDtypeStruct((length, r * GROUP_WIDTH), F32)] * 2,
        compiler_params=_params(("parallel", "parallel")),
    )(view, view, view, view, view)
    return o.reshape(s, GROUP_WIDTH), ld.reshape(s, GROUP_WIDTH)


def _dil_merge(outs, lds, *, name):
    s = outs[0].shape[0]
    tr = _pick(s, 512)

    def body(o0, o1, o2, l0, l1, l2, y_ref, yb_ref, lse_ref):
        a0, a1, a2 = l0[...], l1[...], l2[...]
        m = jnp.maximum(jnp.maximum(a0, a1), a2)
        e0, e1, e2 = jnp.exp(a0 - m), jnp.exp(a1 - m), jnp.exp(a2 - m)
        tot = e0 + e1 + e2
        inv = 1.0 / tot
        y = (e0 * inv) * o0[...] + (e1 * inv) * o1[...] + (e2 * inv) * o2[...]
        y_ref[...] = y
        yb_ref[...] = y.astype(BF16)
        lse_ref[...] = m + jnp.log(tot)

    blk = pl.BlockSpec((tr, GROUP_WIDTH), lambda i: (i, 0))
    return pl.pallas_call(
        body, name=name, grid=(s // tr,),
        in_specs=[blk] * 6, out_specs=[blk] * 3,
        out_shape=[jax.ShapeDtypeStruct((s, GROUP_WIDTH), F32), jax.ShapeDtypeStruct((s, GROUP_WIDTH), BF16),
                   jax.ShapeDtypeStruct((s, GROUP_WIDTH), F32)],
        compiler_params=_params(("parallel",)),
    )(*outs, *lds)


def _dil_attn_bwd(qkv, y, lse, dy, g, *, name):
    s = qkv.shape[0]
    r = DIL_GROUPS[g][1]
    length = s // r
    nb = length // BLOCK
    wb = QKV_WIDTH // GROUP_WIDTH
    view = qkv.reshape(length, r * QKV_WIDTH)
    yv, lv, dv_ = (t.reshape(length, r * GROUP_WIDTH) for t in (y, lse, dy))

    def body(q_ref, qn_ref, kp_ref, kc_ref, vp_ref, vc_ref, y_ref, yn_ref, l_ref, ln_ref, d_ref, dn_ref,
             dq_ref, dk_ref, dv_ref):
        i = pl.program_id(1)
        mask_c, mask_p = _dil_masks(i)
        _, mask_n = _dil_masks(jnp.where(i + 1 < nb, 1, 0))
        sls = [slice(h * HEAD_DIM, (h + 1) * HEAD_DIM) for h in range(HEADS_PER_GROUP)]
        first = []
        for sl in sls:
            q, qn = q_ref[:, sl], qn_ref[:, sl]
            kc, kp, vc, vp = kc_ref[:, sl], kp_ref[:, sl], vc_ref[:, sl], vp_ref[:, sl]
            dyb, dynb = d_ref[:, sl].astype(BF16), dn_ref[:, sl].astype(BF16)
            first.append((_dot_nt(q, kc), _dot_nt(q, kp), _dot_nt(qn, kc),
                          _dot_nt(dyb, vc), _dot_nt(dyb, vp), _dot_nt(dynb, vc)))
        second = []
        for sl, (s_c, s_p, s_n, dp_c, dp_p, dp_n) in zip(sls, first):
            delta = jnp.sum(d_ref[:, sl] * y_ref[:, sl], axis=-1, keepdims=True)
            delta_n = jnp.sum(dn_ref[:, sl] * yn_ref[:, sl], axis=-1, keepdims=True)
            lse_q, lse_n = l_ref[:, sl], ln_ref[:, sl]
            p_c = jnp.where(mask_c, jnp.exp(s_c * SCALE - lse_q), 0.0)
            p_p = jnp.where(mask_p, jnp.exp(s_p * SCALE - lse_q), 0.0)
            p_n = jnp.where(mask_n, jnp.exp(s_n * SCALE - lse_n), 0.0)
            second.append(((p_c * (dp_c - delta) * SCALE).astype(BF16), (p_p * (dp_p - delta) * SCALE).astype(BF16),
                           (p_n * (dp_n - delta_n) * SCALE).astype(BF16), p_c.astype(BF16), p_n.astype(BF16)))
        for sl, (ds_c, ds_p, ds_n, pb_c, pb_n) in zip(sls, second):
            q, qn = q_ref[:, sl], qn_ref[:, sl]
            dyb, dynb = d_ref[:, sl].astype(BF16), dn_ref[:, sl].astype(BF16)
            dq_ref[:, sl] = _dot(ds_c, kc_ref[:, sl]) + _dot(ds_p, kp_ref[:, sl])
            dk_ref[:, sl] = _dot_tn(ds_c, q) + _dot_tn(ds_n, qn)
            dv_ref[:, sl] = _dot_tn(pb_c, dyb) + _dot_tn(pb_n, dynb)

    def col(which):
        return lambda c, i: (i, c * wb + 3 * which + g)

    def col_prev(which):
        return lambda c, i: (jnp.maximum(i - 1, 0), c * wb + 3 * which + g)

    def col_next(which):
        return lambda c, i: (jnp.minimum(i + 1, nb - 1), c * wb + 3 * which + g)

    blk = (BLOCK, GROUP_WIDTH)
    own = pl.BlockSpec(blk, lambda c, i: (i, c))
    nxt = pl.BlockSpec(blk, lambda c, i: (jnp.minimum(i + 1, nb - 1), c))
    dq, dk, dv = pl.pallas_call(
        body, name=name, grid=(r, nb),
        in_specs=[pl.BlockSpec(blk, col(0)), pl.BlockSpec(blk, col_next(0)),
                  pl.BlockSpec(blk, col_prev(1)), pl.BlockSpec(blk, col(1)),
                  pl.BlockSpec(blk, col_prev(2)), pl.BlockSpec(blk, col(2)),
                  own, nxt, own, nxt, own, nxt],
        out_specs=[own] * 3,
        out_shape=[jax.ShapeDtypeStruct((length, r * GROUP_WIDTH), F32)] * 3,
        compiler_params=_params(("parallel", "parallel")),
    )(view, view, view, view, view, view, yv, yv, lv, lv, dv_, dv_)
    return tuple(t.reshape(s, GROUP_WIDTH) for t in (dq, dk, dv))


SB_TQ = 256
SB_GROUP = 2
SB_DEAD = -104.0
SB_NH_FWD = 4
SB_NH_BWD = 2


def _split_dot(x, w):
    hi = x.astype(BF16)
    lo = (x - hi.astype(F32)).astype(BF16)
    return _dot(hi, w) + _dot(lo, w)


def _sb_consts():
    j = lax.broadcasted_iota(jnp.int32, (BLOCK, BLOCK), 0)
    k = lax.broadcasted_iota(jnp.int32, (BLOCK, BLOCK), 1)
    ones = jnp.ones((BLOCK, BLOCK), F32)
    after = jnp.concatenate([(j > k).astype(F32), ones], axis=1).astype(BF16)
    before = jnp.concatenate([(j < k).astype(F32), ones], axis=1).astype(BF16)
    return after, before


def _sb_mask(row0, col0, tq):
    row = row0 + lax.broadcasted_iota(jnp.int32, (tq, BLOCK), 0)
    col = col0 + lax.broadcasted_iota(jnp.int32, (tq, BLOCK), 1)
    return col < row


def _softplus(z):
    return jnp.maximum(z, 0.0) + jnp.log1p(jnp.exp(-jnp.abs(z)))


def _sb_attn_fwd(qkv, *, name):
    s = qkv.shape[0]
    tq = _pick(s, SB_TQ)
    nq = s // tq
    per = tq // BLOCK
    base = 3 * DIL_WIDTH // HEAD_DIM
    after, _ = _sb_consts()

    nh = SB_NH_FWD
    wide = nh * HEAD_DIM

    def body(q_ref, k_ref, v_ref, w_ref, o_ref, rc_ref, trips_ref, acc_ref, run_ref):
        qi = pl.program_id(1)
        w = w_ref[...]
        acc_ref[...] = jnp.zeros_like(acc_ref)
        run_ref[...] = jnp.zeros_like(run_ref)
        rc_ref[...] = jnp.zeros_like(rc_ref)
        lane = lax.broadcasted_iota(jnp.int32, (tq, BLOCK), 1)
        n_chunks = (qi + 1) * per

        def chunks(js, masked):
            sls = [slice(h * HEAD_DIM, (h + 1) * HEAD_DIM) for h in range(nh)]
            work = [(sl, j, pl.multiple_of(j * BLOCK, BLOCK), _sb_mask(qi * tq, j * BLOCK, tq) if masked else None)
                    for j in js for sl in sls]
            zs = [_dot_nt(q_ref[:, sl], k_ref[pl.ds(start, BLOCK), sl]) * SCALE for sl, _, start, _ in work]
            logits, sums = [], []
            for z, (_, _, _, mask) in zip(zs, work):
                sp = _softplus(z)
                logits.append(z - sp)
                sums.append(_split_dot(jnp.where(mask, -sp, 0.0) if masked else -sp, w))
            for (sl, j, start, mask), lg, sm in zip(work, logits, sums):
                run = run_ref[:, sl]
                a = jnp.exp(lg + run + sm[:, :BLOCK])
                if masked:
                    a = jnp.where(mask, a, 0.0)
                acc_ref[:, sl] += _dot(a.astype(BF16), v_ref[pl.ds(start, BLOCK), sl])
                rc_ref[:, sl] = jnp.where(lane == j, run, rc_ref[:, sl])
                run_ref[:, sl] = run + sm[:, BLOCK:]

        def diagonal(t, carry):
            last = n_chunks - 1 - SB_GROUP * t
            chunks([last - g for g in range(SB_GROUP)], True)
            return carry

        def alive():
            return (jnp.max(run_ref[...]) > SB_DEAD).astype(jnp.int32)

        def more(carry):
            t, live = carry
            return jnp.logical_and(t < n_left, live > 0)

        def left(carry):
            t, _ = carry
            last = n_chunks - 1 - per - SB_GROUP * t
            chunks([last - g for g in range(SB_GROUP)], False)
            return t + 1, alive()

        lax.fori_loop(0, per // SB_GROUP, diagonal, 0)
        n_left = (n_chunks - per) // SB_GROUP
        trips, _ = lax.while_loop(more, left, (jnp.int32(0), alive()))
        trips_ref[pl.program_id(0), qi] = trips
        o_ref[...] = acc_ref[...].astype(o_ref.dtype)

    head = lambda off: (lambda h, i: (0, (base + off) // nh + h))
    once = pl.Buffered(1)
    return pl.pallas_call(
        body, name=name, grid=(SB_HEADS // nh, nq),
        in_specs=[pl.BlockSpec((tq, wide), lambda h, i: (i, base // nh + h)),
                  pl.BlockSpec((s, wide), head(SB_HEADS), pipeline_mode=once),
                  pl.BlockSpec((s, wide), head(2 * SB_HEADS), pipeline_mode=once),
                  pl.BlockSpec((BLOCK, 2 * BLOCK), lambda h, i: (0, 0))],
        out_specs=[pl.BlockSpec((tq, wide), lambda h, i: (i, h))] * 2 + [pl.BlockSpec(memory_space=pltpu.SMEM)],
        out_shape=[jax.ShapeDtypeStruct((s, SB_WIDTH), BF16), jax.ShapeDtypeStruct((s, SB_WIDTH), F32),
                   jax.ShapeDtypeStruct((SB_HEADS // nh, nq), jnp.int32)],
        scratch_shapes=[pltpu.VMEM((tq, wide), F32), pltpu.VMEM((tq, wide), F32)],
        compiler_params=_params(("arbitrary", "arbitrary")),
    )(qkv, qkv, qkv, after)


def _sb_attn_bwd(qkv, rc, trips, do, *, name):
    s = qkv.shape[0]
    tq = _pick(s, SB_TQ)
    nq = s // tq
    per = tq // BLOCK
    base = 3 * DIL_WIDTH // HEAD_DIM
    _, before = _sb_consts()

    nh = SB_NH_BWD
    wide = nh * HEAD_DIM

    def body(trips_ref, q_ref, k_ref, v_ref, w_ref, rc_ref, do_ref, dq_ref, dk_ref, dv_ref, acc_ref, pre_ref, dob_ref):
        qi = pl.program_id(1)

        @pl.when(qi == 0)
        def _():
            dk_ref[...] = jnp.zeros_like(dk_ref)
            dv_ref[...] = jnp.zeros_like(dv_ref)

        w = w_ref[...]
        wa = (lax.broadcasted_iota(jnp.int32, (BLOCK, BLOCK), 0)
              > lax.broadcasted_iota(jnp.int32, (BLOCK, BLOCK), 1)).astype(BF16)
        dob_ref[...] = do_ref[...].astype(BF16)
        acc_ref[...] = jnp.zeros_like(acc_ref)
        pre_ref[...] = jnp.zeros_like(pre_ref)
        lane = lax.broadcasted_iota(jnp.int32, (tq, BLOCK), 1)
        n_chunks = (qi + 1) * per

        def chunks(js, masked):
            sls = [slice(h * HEAD_DIM, (h + 1) * HEAD_DIM) for h in range(nh)]
            work = [(sl, j, pl.multiple_of(j * BLOCK, BLOCK), _sb_mask(qi * tq, j * BLOCK, tq) if masked else None)
                    for j in js for sl in sls]
            zs = [_dot_nt(q_ref[:, sl], k_ref[pl.ds(start, BLOCK), sl]) * SCALE for sl, _, start, _ in work]
            das = [_dot_nt(dob_ref[:, sl], v_ref[pl.ds(start, BLOCK), sl]) for sl, _, start, _ in work]
            sigs, betweens = [], []
            for z, (_, _, _, mask) in zip(zs, work):
                sp = _softplus(z)
                sigs.append(jnp.exp(z - sp))
                betweens.append(_split_dot(jnp.where(mask, -sp, 0.0) if masked else -sp, wa))
            es, avs, sums = [], [], []
            for (sl, j, _, mask), sig, bt, da in zip(work, sigs, betweens, das):
                run = jnp.sum(jnp.where(lane == j, rc_ref[:, sl], 0.0), axis=-1, keepdims=True)
                a = sig * jnp.exp(run + bt)
                if masked:
                    a = jnp.where(mask, a, 0.0)
                e = a * da
                es.append(e)
                avs.append(a.astype(BF16))
                sums.append(_split_dot(e, w))
            for (sl, _, start, mask), sig, e, ab, sm in zip(work, sigs, es, avs, sums):
                pre = pre_ref[:, sl]
                dz = (e * (1.0 - sig) - sig * (pre + sm[:, :BLOCK])) * SCALE
                if masked:
                    dz = jnp.where(mask, dz, 0.0)
                dz = dz.astype(BF16)
                pre_ref[:, sl] = pre + sm[:, BLOCK:]
                acc_ref[:, sl] += _dot(dz, k_ref[pl.ds(start, BLOCK), sl])
                dk_ref[pl.ds(start, BLOCK), sl] += _dot_tn(dz, q_ref[:, sl])
                dv_ref[pl.ds(start, BLOCK), sl] += _dot_tn(ab, dob_ref[:, sl])

        def left(t, carry):
            chunks([SB_GROUP * t + g for g in range(SB_GROUP)], False)
            return carry

        def diagonal(t, carry):
            chunks([n_chunks - per + SB_GROUP * t + g for g in range(SB_GROUP)], True)
            return carry

        n_left = (n_chunks - per) // SB_GROUP
        ran = trips_ref[pl.program_id(0) // (SB_NH_FWD // nh), qi]
        lax.fori_loop(n_left - ran, n_left, left, 0)
        lax.fori_loop(0, per // SB_GROUP, diagonal, 0)
        dq_ref[...] = acc_ref[...]

    head = lambda off: (lambda h, i, t: (0, (base + off) // nh + h))
    blk = pl.BlockSpec((tq, wide), lambda h, i, t: (i, h))
    whole = pl.BlockSpec((s, wide), lambda h, i, t: (0, h))
    once = pl.Buffered(1)
    grid_spec = pltpu.PrefetchScalarGridSpec(
        num_scalar_prefetch=1, grid=(SB_HEADS // nh, nq),
        in_specs=[pl.BlockSpec((tq, wide), lambda h, i, t: (i, base // nh + h)),
                  pl.BlockSpec((s, wide), head(SB_HEADS), pipeline_mode=once),
                  pl.BlockSpec((s, wide), head(2 * SB_HEADS), pipeline_mode=once),
                  pl.BlockSpec((BLOCK, 2 * BLOCK), lambda h, i, t: (0, 0)),
                  blk, blk],
        out_specs=[blk, whole, whole],
        scratch_shapes=[pltpu.VMEM((tq, wide), F32), pltpu.VMEM((tq, wide), F32), pltpu.VMEM((tq, wide), BF16)],
    )
    return pl.pallas_call(
        body, name=name, grid_spec=grid_spec,
        out_shape=[jax.ShapeDtypeStruct((s, SB_WIDTH), F32)] * 3,
        compiler_params=_params(("parallel", "arbitrary")),
    )(trips, qkv, qkv, qkv, before, rc, do)


GATE_TC = 512


def _gate_fwd(proj, gate_b, up_dil, up_sb, *, name):
    s, d = up_dil.shape
    tr = _pick(s, 512)
    g0 = QKV_WIDTH // GATE_TC
    nc = d // GATE_TC

    def body(ga_ref, gb_ref, b_ref, ud_ref, us_ref, o_ref):
        ga = jax.nn.sigmoid(ga_ref[...] + b_ref[0:1, :])
        gb = jax.nn.sigmoid(gb_ref[...] + b_ref[1:2, :])
        o_ref[...] = (ga * ud_ref[...] + gb * us_ref[...]).astype(o_ref.dtype)

    blk = pl.BlockSpec((tr, GATE_TC), lambda i, j: (i, j))
    return pl.pallas_call(
        body, name=name, grid=(s // tr, nc),
        in_specs=[pl.BlockSpec((tr, GATE_TC), lambda i, j: (i, g0 + j)),
                  pl.BlockSpec((tr, GATE_TC), lambda i, j: (i, g0 + nc + j)),
                  pl.BlockSpec((2, GATE_TC), lambda i, j: (0, j)), blk, blk],
        out_specs=blk,
        out_shape=jax.ShapeDtypeStruct((s, d), BF16),
        compiler_params=_params(("parallel", "parallel")),
    )(proj, proj, gate_b, up_dil, up_sb)


def _gate_bwd(proj, gate_b, up_dil, up_sb, dmixed, *, name):
    s, d = up_dil.shape
    tr = _pick(s, 512)
    g0 = QKV_WIDTH // GATE_TC
    nc = d // GATE_TC

    def body(ga_ref, gb_ref, b_ref, ud_ref, us_ref, dm_ref, dud_ref, dus_ref, dga_ref, dgb_ref, db_ref):
        ga = jax.nn.sigmoid(ga_ref[...] + b_ref[0:1, :])
        gb = jax.nn.sigmoid(gb_ref[...] + b_ref[1:2, :])
        dm = dm_ref[...]
        dud_ref[...] = (dm * ga).astype(BF16)
        dus_ref[...] = (dm * gb).astype(BF16)
        dla = dm * ud_ref[...] * ga * (1.0 - ga)
        dlb = dm * us_ref[...] * gb * (1.0 - gb)
        dga_ref[...] = dla.astype(BF16)
        dgb_ref[...] = dlb.astype(BF16)

        @pl.when(pl.program_id(1) == 0)
        def _():
            db_ref[...] = jnp.zeros_like(db_ref)

        db_ref[0:1, :] += jnp.sum(dla, axis=0, keepdims=True)
        db_ref[1:2, :] += jnp.sum(dlb, axis=0, keepdims=True)

    blk = pl.BlockSpec((tr, GATE_TC), lambda j, i: (i, j))
    dud, dus, dga, dgb, db = pl.pallas_call(
        body, name=name, grid=(nc, s // tr),
        in_specs=[pl.BlockSpec((tr, GATE_TC), lambda j, i: (i, g0 + j)),
                  pl.BlockSpec((tr, GATE_TC), lambda j, i: (i, g0 + nc + j)),
                  pl.BlockSpec((2, GATE_TC), lambda j, i: (0, j)), blk, blk, blk],
        out_specs=[blk, blk, blk, blk, pl.BlockSpec((2, GATE_TC), lambda j, i: (0, j))],
        out_shape=[jax.ShapeDtypeStruct((s, d), BF16)] * 4 + [jax.ShapeDtypeStruct((2, d), F32)],
        compiler_params=_params(("parallel", "arbitrary")),
    )(proj, proj, gate_b, up_dil, up_sb, dmixed)
    return dud, dus, jnp.concatenate([dga, dgb], axis=1), db


def _adamw(w, g, m, v, *, name):
    shape = w.shape
    cols = shape[-1]
    rows = w.size // cols
    tr = _pick(rows, max(8, (512 * 1024) // cols))
    if rows % tr or (tr % 8 and tr != rows):
        tr = rows
    c1 = 1.0 - ADAM_B1 ** ADAM_STEP
    c2 = 1.0 - ADAM_B2 ** ADAM_STEP

    def body(w_ref, g_ref, m_ref, v_ref, d_ref, nm_ref, nv_ref):
        gv = g_ref[...]
        nm = ADAM_B1 * m_ref[...] + (1.0 - ADAM_B1) * gv
        nv = ADAM_B2 * v_ref[...] + (1.0 - ADAM_B2) * (gv * gv)
        d_ref[...] = -ADAM_LR * ((nm / c1) / (jnp.sqrt(nv / c2) + ADAM_EPS) + ADAM_WD * w_ref[...])
        nm_ref[...] = nm
        nv_ref[...] = nv

    blk = pl.BlockSpec((tr, cols), lambda i: (i, 0))
    outs = pl.pallas_call(
        body, name=name, grid=(rows // tr,),
        in_specs=[blk] * 4, out_specs=[blk] * 3,
        out_shape=[jax.ShapeDtypeStruct((rows, cols), F32)] * 3,
        compiler_params=_params(("parallel",)),
    )(*(t.reshape(rows, cols) for t in (w, g, m, v)))
    return tuple(t.reshape(shape) for t in outs)


def _add_half(full, other, which, *, name):
    n, rh, c = other.shape
    tr = _pick(rh, max(8, (1024 * 1024) // c))
    nb = rh // tr

    def body(which_ref, a_ref, b_ref, o_ref):
        o_ref[...] = (a_ref[...] + b_ref[...]).astype(o_ref.dtype)

    grid_spec = pltpu.PrefetchScalarGridSpec(
        num_scalar_prefetch=1, grid=(n, nb),
        in_specs=[pl.BlockSpec((None, tr, c), lambda s, i, w: (s, w[0] * nb + i, 0)),
                  pl.BlockSpec((None, tr, c), lambda s, i, w: (s, i, 0))],
        out_specs=pl.BlockSpec((None, tr, c), lambda s, i, w: (s, i, 0)),
    )
    return pl.pallas_call(
        body, name=name, grid_spec=grid_spec,
        out_shape=jax.ShapeDtypeStruct((n, rh, c), BF16),
        compiler_params=_params(("parallel", "parallel")),
    )(which, full, other)


def _sum_chips(own, parts, chip, *, name):
    n, r, c = parts.shape
    tr = _pick(r, max(16, (1024 * 1024) // c))

    def body(chip_ref, own_ref, p1_ref, p2_ref, p3_ref, o_ref):
        acc = own_ref[...].astype(F32)
        for p in (p1_ref, p2_ref, p3_ref):
            acc = acc + p[...].astype(F32)
        o_ref[...] = acc

    def slot(k):
        return pl.BlockSpec((None, tr, c), lambda i, o: (o[0] ^ k, i, 0))

    grid_spec = pltpu.PrefetchScalarGridSpec(
        num_scalar_prefetch=1, grid=(r // tr,),
        in_specs=[slot(0), slot(1), slot(2), slot(3)],
        out_specs=pl.BlockSpec((tr, c), lambda i, o: (i, 0)),
    )
    return pl.pallas_call(
        body, name=name, grid_spec=grid_spec,
        out_shape=jax.ShapeDtypeStruct((r, c), F32),
        compiler_params=_params(("parallel",)),
    )(chip, own, parts, parts, parts)


def _place_piece(full, piece, index, *, name):
    a, n, r, c = full.shape
    tr = _pick(r, max(16, (1024 * 1024) // c))

    def body(index_ref, piece_ref, full_ref, o_ref):
        o_ref[...] = piece_ref[...]

    grid_spec = pltpu.PrefetchScalarGridSpec(
        num_scalar_prefetch=1, grid=(a, r // tr),
        in_specs=[pl.BlockSpec((None, tr, c), lambda l, i, k: (l, i, 0)), ANY],
        out_specs=pl.BlockSpec((None, None, tr, c), lambda l, i, k: (l, k[0], i, 0)),
    )
    return pl.pallas_call(
        body, name=name, grid_spec=grid_spec,
        out_shape=jax.ShapeDtypeStruct(full.shape, full.dtype),
        input_output_aliases={2: 0},
        compiler_params=_params(("parallel", "parallel")),
    )(index, piece, full)


ANY = pl.BlockSpec(memory_space=pl.ANY)


def _place():
    x, y, c = lax.axis_index("x"), lax.axis_index("y"), lax.axis_index("c")
    chips = [(1 - x, y), (x, 1 - y), (1 - x, 1 - y)]
    return x, y, c, chips


def _remote(src, dst, send_sem, recv_sem, to):
    return pltpu.make_async_remote_copy(src_ref=src, dst_ref=dst, send_sem=send_sem, recv_sem=recv_sem,
                                        device_id=to, device_id_type=MESH)


def _gather_weights(shards):
    n = len(shards)

    def body(*refs):
        ins, outs = refs[:n], refs[n:2 * n]
        send_sems, recv_sems, fwd_send, fwd_recv = refs[2 * n:]
        x, y, c, chips = _place()
        me = 2 * x + y
        sibling = (x, y, 1 - c)

        def half(ref, chip, which):
            rh = ref.shape[2] // 2
            return ref.at[:, chip, pl.ds(which * rh, rh), :]

        def src_half(ref, which):
            rh = ref.shape[1] // 2
            return ref.at[:, pl.ds(which * rh, rh), :]

        first = []
        for t in range(n):
            for k, (px, py) in enumerate(chips):
                cp = _remote(src_half(ins[t], c), half(outs[t], me, c), send_sems.at[t, k], recv_sems.at[t, k],
                             (px, py, c))
                cp.start()
                first.append(cp)
        passed = []
        for t in range(n):
            for k, (px, py) in enumerate(chips):
                theirs = 2 * px + py
                landed = half(outs[t], theirs, c)
                _remote(landed, landed, send_sems.at[t, k], recv_sems.at[t, k], (px, py, c)).wait_recv()
                cp = _remote(landed, landed, fwd_send.at[t, k], fwd_recv.at[t, k], sibling)
                cp.start()
                passed.append(cp)
        for t in range(n):
            for k, (px, py) in enumerate(chips):
                theirs = 2 * px + py
                other = half(outs[t], theirs, 1 - c)
                _remote(other, other, fwd_send.at[t, k], fwd_recv.at[t, k], sibling).wait_recv()
        for cp in first + passed:
            cp.wait_send()

    return pl.pallas_call(
        body, name="gather_weights",
        in_specs=[ANY] * n, out_specs=[ANY] * n,
        out_shape=[jax.ShapeDtypeStruct((w.shape[0], N_CHIPS) + w.shape[1:], w.dtype) for w in shards],
        scratch_shapes=[pltpu.SemaphoreType.DMA((n, 3)), pltpu.SemaphoreType.DMA((n, 3)),
                        pltpu.SemaphoreType.DMA((n, 3)), pltpu.SemaphoreType.DMA((n, 3))],
        compiler_params=pltpu.CompilerParams(has_side_effects=True),
    )(*shards)


def _pair_swap_halves(grads, *, name):
    n = len(grads)

    def body(*refs):
        ins, outs = refs[:n], refs[n:2 * n]
        send_sems, recv_sems = refs[2 * n:]
        x, y, c, _ = _place()
        sibling = (x, y, 1 - c)
        cps = []
        for t in range(n):
            rh = ins[t].shape[1] // 2
            cp = _remote(ins[t].at[:, pl.ds((1 - c) * rh, rh), :], outs[t], send_sems.at[t], recv_sems.at[t], sibling)
            cp.start()
            cps.append(cp)
        for cp in cps:
            cp.wait()

    return pl.pallas_call(
        body, name=name,
        in_specs=[ANY] * n, out_specs=[ANY] * n,
        out_shape=[jax.ShapeDtypeStruct((g.shape[0], g.shape[1] // 2, g.shape[2]), g.dtype) for g in grads],
        scratch_shapes=[pltpu.SemaphoreType.DMA((n,)), pltpu.SemaphoreType.DMA((n,))],
        compiler_params=pltpu.CompilerParams(has_side_effects=True),
    )(*grads)


def _chip_exchange(parts, *, name):
    n = len(parts)

    def body(*refs):
        ins, outs = refs[:n], refs[n:2 * n]
        send_sems, recv_sems = refs[2 * n:]
        x, y, c, chips = _place()
        me = 2 * x + y
        cps = []
        for t in range(n):
            for k, (px, py) in enumerate(chips):
                cp = _remote(ins[t].at[2 * px + py], outs[t].at[me], send_sems.at[t, k], recv_sems.at[t, k], (px, py, c))
                cp.start()
                cps.append(cp)
        for t in range(n):
            for k, (px, py) in enumerate(chips):
                theirs = outs[t].at[2 * px + py]
                _remote(theirs, theirs, send_sems.at[t, k], recv_sems.at[t, k], (px, py, c)).wait_recv()
        for cp in cps:
            cp.wait_send()

    return pl.pallas_call(
        body, name=name,
        in_specs=[ANY] * n, out_specs=[ANY] * n,
        out_shape=[jax.ShapeDtypeStruct(p.shape, p.dtype) for p in parts],
        scratch_shapes=[pltpu.SemaphoreType.DMA((n, 3)), pltpu.SemaphoreType.DMA((n, 3))],
        compiler_params=pltpu.CompilerParams(has_side_effects=True),
    )(*parts)


def _pair_join_halves(halves, *, name):
    n = len(halves)

    def body(*refs):
        ins, outs = refs[:n], refs[n:2 * n]
        send_sems, recv_sems = refs[2 * n:]
        x, y, c, _ = _place()
        sibling = (x, y, 1 - c)
        cps = []
        for t in range(n):
            rh = ins[t].shape[0]
            cp = _remote(ins[t], outs[t].at[pl.ds(c * rh, rh), :], send_sems.at[t], recv_sems.at[t], sibling)
            cp.start()
            cps.append(cp)
        for t, cp in enumerate(cps):
            rh = ins[t].shape[0]
            theirs = outs[t].at[pl.ds((1 - c) * rh, rh), :]
            _remote(theirs, theirs, send_sems.at[t], recv_sems.at[t], sibling).wait_recv()
            cp.wait_send()

    return pl.pallas_call(
        body, name=name,
        in_specs=[ANY] * n, out_specs=[ANY] * n,
        out_shape=[jax.ShapeDtypeStruct((2 * h.shape[0], h.shape[1]), h.dtype) for h in halves],
        scratch_shapes=[pltpu.SemaphoreType.DMA((n,)), pltpu.SemaphoreType.DMA((n,))],
        compiler_params=pltpu.CompilerParams(has_side_effects=True),
    )(*halves)


def _exchange_small(buf, *, name):
    rows, lanes = buf.shape

    def body(in_ref, out_ref, send_sems, recv_sems, local_sem):
        x, y, c = lax.axis_index("x"), lax.axis_index("y"), lax.axis_index("c")
        me = 4 * x + 2 * y + c
        lc = pltpu.make_async_copy(in_ref, out_ref.at[me], local_sem)
        lc.start()
        cps = []
        for k in range(1, 8):
            fx, fy, fc = (k >> 2) & 1, (k >> 1) & 1, k & 1
            to = (x ^ fx, y ^ fy, c ^ fc)
            cp = _remote(in_ref, out_ref.at[me], send_sems.at[k - 1], recv_sems.at[k - 1], to)
            cp.start()
            cps.append(cp)
        for k in range(1, 8):
            fx, fy, fc = (k >> 2) & 1, (k >> 1) & 1, k & 1
            theirs = out_ref.at[4 * (x ^ fx) + 2 * (y ^ fy) + (c ^ fc)]
            _remote(theirs, theirs, send_sems.at[k - 1], recv_sems.at[k - 1], (x ^ fx, y ^ fy, c ^ fc)).wait_recv()
        for cp in cps:
            cp.wait_send()
        lc.wait()

    return pl.pallas_call(
        body, name=name,
        in_specs=[ANY], out_specs=ANY,
        out_shape=jax.ShapeDtypeStruct((8, rows, lanes), buf.dtype),
        scratch_shapes=[pltpu.SemaphoreType.DMA((7,)), pltpu.SemaphoreType.DMA((7,)), pltpu.SemaphoreType.DMA],
        compiler_params=pltpu.CompilerParams(has_side_effects=True),
    )(buf)


def _sum_slots(slots, *, name):
    n, rows, lanes = slots.shape

    def body(s_ref, o_ref):
        acc = s_ref[0]
        for t in range(1, n):
            acc = acc + s_ref[t]
        o_ref[...] = acc

    return pl.pallas_call(
        body, name=name,
        out_shape=jax.ShapeDtypeStruct((rows, lanes), slots.dtype),
    )(slots)


def _relu2(acc):
    r = jnp.maximum(acc, 0.0)
    return acc, r * r


def _layer_fwd(x, w, cos, sin, tag):
    h = _rmsnorm_fwd(x, w["norm1_g"], name=f"norm1_{tag}")
    proj = _matmul(h, w["w_in"], mode="nn", b_sharded=True, tm=512, name=f"proj_{tag}")
    qkv = _qkv_prep(proj, w["q_norm_g"], w["k_norm_g"], cos, sin, name=f"qkv_prep_{tag}")
    outs, lds = [], []
    for g in range(N_GROUPS):
        o_g, ld_g = _dil_attn_fwd(qkv, g, name=f"dil{g}_fwd_{tag}")
        outs.append(o_g)
        lds.append(ld_g)
    y, yb, lse = _dil_merge(outs, lds, name=f"dil_merge_{tag}")
    o_sb, rc, trips = _sb_attn_fwd(qkv, name=f"sb_fwd_{tag}")
    up_dil = _matmul(yb, w["w_up_dil"], mode="nn", b_sharded=True, name=f"up_dil_{tag}")
    up_sb = _matmul(o_sb, w["w_up_sb"], mode="nn", b_sharded=True, name=f"up_sb_{tag}")
    mixed = _gate_fwd(proj, w["gate_b"], up_dil, up_sb, name=f"gate_{tag}")
    x1 = _matmul(mixed, w["w_out"], mode="nn", extras=(x,), epilogue=lambda acc, res: (acc + res,), name=f"out_{tag}")
    h2 = _rmsnorm_fwd(x1, w["norm2_g"], name=f"norm2_{tag}")
    u, a = _matmul(h2, w["w_ff1"], mode="nn", b_sharded=True, out_dtypes=(F32, BF16), epilogue=_relu2,
                   name=f"ff1_{tag}")
    x2 = _matmul(a, w["w_ff2"], mode="nn", extras=(x1,), epilogue=lambda acc, res: (acc + res,), name=f"ff2_{tag}")
    saved = dict(x=x, h=h, proj=proj, qkv=qkv, y=y, yb=yb, lse=lse, o_sb=o_sb, rc=rc, trips=trips, up_dil=up_dil, up_sb=up_sb,
                 mixed=mixed, x1=x1, h2=h2, u=u, a=a)
    return x2, saved


def _layer_bwd(dx, dxb, w, sv, cos, sin, tag):
    grads = {}
    du = _matmul(dxb, w["w_ff2"], mode="nt", extras=(sv["u"],), out_dtypes=(BF16,),
                 epilogue=lambda acc, u: (acc * (2.0 * jnp.maximum(u, 0.0)),), name=f"d_a_{tag}")
    grads["w_ff2"] = _matmul(sv["a"], dxb, mode="tn", name=f"dw_ff2_{tag}")
    dh2 = _matmul(du, w["w_ff1"], mode="nt", b_sharded=True, name=f"d_h2_{tag}")
    grads["w_ff1"] = _matmul(sv["h2"], du, mode="tn", out_sharded=True, name=f"dw_ff1_{tag}")
    dx1, dx1b, grads["norm2_g"] = _rmsnorm_bwd(sv["x1"], w["norm2_g"], dh2, dx, name=f"norm2_bwd_{tag}")
    dmixed = _matmul(dx1b, w["w_out"], mode="nt", name=f"d_mixed_{tag}")
    grads["w_out"] = _matmul(sv["mixed"], dx1b, mode="tn", name=f"dw_out_{tag}")
    dud, dus, dgate, grads["gate_b"] = _gate_bwd(sv["proj"], w["gate_b"], sv["up_dil"], sv["up_sb"], dmixed,
                                                 name=f"gate_bwd_{tag}")
    dy_dil = _matmul(dud, w["w_up_dil"], mode="nt", b_sharded=True, name=f"d_ydil_{tag}")
    grads["w_up_dil"] = _matmul(sv["yb"], dud, mode="tn", out_sharded=True, name=f"dw_up_dil_{tag}")
    dy_sb = _matmul(dus, w["w_up_sb"], mode="nt", b_sharded=True, name=f"d_ysb_{tag}")
    grads["w_up_sb"] = _matmul(sv["o_sb"], dus, mode="tn", out_sharded=True, name=f"dw_up_sb_{tag}")
    d_dil = [_dil_attn_bwd(sv["qkv"], sv["y"], sv["lse"], dy_dil, g, name=f"dil{g}_bwd_{tag}") for g in range(N_GROUPS)]
    d_sb = _sb_attn_bwd(sv["qkv"], sv["rc"], sv["trips"], dy_sb, name=f"sb_bwd_{tag}")
    dproj, grads["q_norm_g"], grads["k_norm_g"] = _qkv_prep_bwd(
        sv["proj"], w["q_norm_g"], w["k_norm_g"], cos, sin, d_dil, d_sb, dgate, name=f"qkv_prep_bwd_{tag}")
    dh = _matmul(dproj, w["w_in"], mode="nt", b_sharded=True, tm=512, tn=2048, name=f"d_h_{tag}")
    grads["w_in"] = _matmul(sv["h"], dproj, mode="tn", out_sharded=True, tm=512, tk=1024, name=f"dw_in_{tag}")
    dx0, dx0b, grads["norm1_g"] = _rmsnorm_bwd(sv["x"], w["norm1_g"], dh, dx1, name=f"norm1_bwd_{tag}")
    return dx0, dx0b, grads


def _local_step(x, target, layers, cos, sin):
    saved = []
    for l, w in enumerate(layers):
        x, sv = _layer_fwd(x, w, cos, sin, f"l{l}")
        saved.append(sv)
    loss, dx, dxb = _loss_head(x, target)
    grads = [None] * len(layers)
    for l in reversed(range(len(layers))):
        dx, dxb, grads[l] = _layer_bwd(dx, dxb, layers[l], saved[l], cos, sin, f"l{l}")
    return loss, dx, grads


BIG = ("w_in", "w_up_dil", "w_up_sb", "w_out", "w_ff1", "w_ff2")
ROW_SHARDED = ("w_out", "w_ff2")
WEIGHTS = ("norm1_g", "w_in", "q_norm_g", "k_norm_g", "w_up_dil", "w_up_sb", "gate_b", "w_out", "norm2_g", "w_ff1", "w_ff2")


def _reduce_to_shards(grads, which, chip, tag):
    names = list(grads)
    full = [grads[n] for n in names]
    theirs = _pair_swap_halves(full, name=f"rs_pair_{tag}")
    pair = [_add_half(f, t, which, name=f"rs_pair_sum_{n}_{tag}") for n, f, t in zip(names, full, theirs)]
    parts = _chip_exchange(pair, name=f"rs_chips_{tag}")
    halves = [_sum_chips(p, q, chip, name=f"rs_chip_sum_{n}_{tag}") for n, p, q in zip(names, pair, parts)]
    joined = _pair_join_halves(halves, name=f"rs_join_{tag}")
    out = {}
    for n, j, h in zip(names, joined, halves):
        r, c = j.shape
        out[n] = _place_piece(j.reshape(1, 2, r // 2, c), h[None], which, name=f"rs_place_{n}_{tag}").reshape(r, c)
    return out


def _pack_rows(vecs):
    rows, spans, at = [], [], 0
    for v in vecs:
        r = v.size // 128
        padded = -(-r // 8) * 8
        rows.append(jnp.pad(v.reshape(r, 128), ((0, padded - r), (0, 0))))
        spans.append((at, r))
        at += padded
    return jnp.concatenate(rows, axis=0), spans


def kernel(x, norm1_g, w_in, q_norm_g, k_norm_g, w_up_dil, w_up_sb, gate_b, w_out, norm2_g, w_ff1, w_ff2, loss_target, m_norm1_g, m_w_in, m_q_norm_g, m_k_norm_g, m_w_up_dil, m_w_up_sb, m_gate_b, m_w_out, m_norm2_g, m_w_ff1, m_w_ff2, v_norm1_g, v_w_in, v_q_norm_g, v_k_norm_g, v_w_up_dil, v_w_up_sb, v_gate_b, v_w_out, v_norm2_g, v_w_ff1, v_w_ff2):
    weights = dict(norm1_g=norm1_g, w_in=w_in, q_norm_g=q_norm_g, k_norm_g=k_norm_g, w_up_dil=w_up_dil,
                   w_up_sb=w_up_sb, gate_b=gate_b, w_out=w_out, norm2_g=norm2_g, w_ff1=w_ff1, w_ff2=w_ff2)
    moments_m = dict(norm1_g=m_norm1_g, w_in=m_w_in, q_norm_g=m_q_norm_g, k_norm_g=m_k_norm_g, w_up_dil=m_w_up_dil,
                     w_up_sb=m_w_up_sb, gate_b=m_gate_b, w_out=m_w_out, norm2_g=m_norm2_g, w_ff1=m_w_ff1, w_ff2=m_w_ff2)
    moments_v = dict(norm1_g=v_norm1_g, w_in=v_w_in, q_norm_g=v_q_norm_g, k_norm_g=v_k_norm_g, w_up_dil=v_w_up_dil,
                     w_up_sb=v_w_up_sb, gate_b=v_gate_b, w_out=v_w_out, norm2_g=v_norm2_g, w_ff1=v_w_ff1, w_ff2=v_w_ff2)
    depth = w_in.shape[0]
    seq, d_model = x.shape[1], x.shape[2]
    chip = 2 * lax.axis_index("x") + lax.axis_index("y")
    core = lax.axis_index("c")
    which = jnp.reshape(core, (1,)).astype(jnp.int32)

    chip_index = jnp.reshape(chip, (1,)).astype(jnp.int32)
    shards = [weights[n].astype(BF16) for n in BIG]
    gathered = {n: _place_piece(g, s, chip_index, name=f"gather_place_{n}")
                for n, g, s in zip(BIG, _gather_weights(shards), shards)}
    bias_buf, ((_, bias_rows),) = _pack_rows([gate_b])
    bias_slots = _exchange_small(bias_buf, name="gather_gate_b")
    bias = bias_slots[0::2, :bias_rows].reshape(N_CHIPS, depth, 2, d_model // N_CHIPS)
    bias = jnp.transpose(bias, (1, 2, 0, 3)).reshape(depth, 2, d_model)
    layers = []
    for l in range(depth):
        w = {n: gathered[n][l] for n in BIG}
        for n in ROW_SHARDED:
            w[n] = w[n].reshape(-1, w[n].shape[-1])
        w.update(norm1_g=norm1_g[l], norm2_g=norm2_g[l], q_norm_g=q_norm_g[l], k_norm_g=k_norm_g[l], gate_b=bias[l])
        layers.append(w)

    cos, sin = _rope_tables(seq)
    loss, dx, grads = _local_step(x[0], loss_target[0], layers, cos, sin)
    loss = lax.psum(loss, ("x", "y", "c"))

    reduced = []
    for l in range(depth):
        g = {n: grads[l][n] for n in BIG}
        for n in ROW_SHARDED:
            g[n] = g[n].reshape(N_CHIPS, -1, g[n].shape[-1])
        reduced.append(_reduce_to_shards(g, which, chip_index, f"l{l}"))
    final = {n: jnp.stack([reduced[l][n] for l in range(depth)]) for n in BIG}

    small_names = ("norm1_g", "norm2_g", "q_norm_g", "k_norm_g", "gate_b")
    packed, spans = _pack_rows([grads[l][n] for l in range(depth) for n in small_names])
    total = _sum_slots(_exchange_small(packed, name="exchange_small_grads"), name="sum_small_grads")
    pieces = iter(total[at:at + r] for at, r in spans)
    small = {n: [] for n in small_names}
    for l in range(depth):
        for n in small_names:
            small[n].append(next(pieces))
    for n in ("norm1_g", "norm2_g"):
        final[n] = jnp.stack([p.reshape(d_model) for p in small[n]])
    for n in ("q_norm_g", "k_norm_g"):
        final[n] = jnp.stack([p.reshape(N_GROUPS, HEAD_DIM) for p in small[n]])
    shard_cols = d_model // N_CHIPS
    final["gate_b"] = jnp.stack([lax.dynamic_slice_in_dim(p.reshape(2, d_model), chip * shard_cols, shard_cols, axis=1)
                                 for p in small["gate_b"]])

    deltas, new_m, new_v = {}, {}, {}
    for n in WEIGHTS:
        deltas[n], new_m[n], new_v[n] = _adamw(weights[n], final[n], moments_m[n], moments_v[n], name=f"adamw_{n}")
    return (loss, dx[None], *[final[n] for n in WEIGHTS], *[deltas[n] for n in WEIGHTS],
            *[new_m[n] for n in WEIGHTS], *[new_v[n] for n in WEIGHTS])
```

```python
import functools
import math

import jax
import jax.numpy as jnp
from jax import lax
from jax.experimental import pallas as pl
from jax.experimental.pallas import tpu as pltpu

F32 = jnp.float32
BF16 = jnp.bfloat16

HEAD_DIM = 128
DIL_GROUPS = ((128, 1), (512, 4), (2048, 16))
N_GROUPS = 3
HEADS_PER_GROUP = 4
GROUP_WIDTH = HEADS_PER_GROUP * HEAD_DIM
DIL_WIDTH = N_GROUPS * GROUP_WIDTH
SB_HEADS = 8
SB_WIDTH = SB_HEADS * HEAD_DIM
QKV_WIDTH = 3 * DIL_WIDTH + 3 * SB_WIDTH
BLOCK = 128
ROPE_THETA = 10000.0
EPS = 1e-6
SCALE = 1.0 / math.sqrt(HEAD_DIM)
NEG_INF = float("-inf")

ADAM_LR = 0.001
ADAM_B1 = 0.9
ADAM_B2 = 0.999
ADAM_EPS = 1e-08
ADAM_WD = 0.01
ADAM_STEP = 10

N_CHIPS = 4
MESH = pl.DeviceIdType.MESH
MIB = 1024 * 1024
VMEM_LIMIT = 56 * MIB


def _params(semantics=None, vmem=VMEM_LIMIT):
    return pltpu.CompilerParams(dimension_semantics=semantics, vmem_limit_bytes=vmem)


def _pick(n, pref):
    if n <= pref:
        return n
    t = 1 << (pref.bit_length() - 1)
    while n % t:
        t //= 2
    return t


def _tile(n, pref):
    best = 0
    for t in range(256, min(n, pref) + 1, 256):
        if n % t == 0:
            best = t
    return best if best else n


def _matmul(a, b, *, mode, name, out_dtypes=(F32,), extras=(), epilogue=None, b_sharded=False, out_sharded=False,
            tm=1024, tn=1024, tk=2048):
    if mode == "nn":
        m, kdim = a.shape
        n = b.shape[-1] * (N_CHIPS if b_sharded else 1)
        assert (b.shape[-2] if b_sharded else b.shape[0]) == kdim
    elif mode == "nt":
        m, kdim = a.shape
        n = b.shape[-2]
        assert b.shape[-1] * (N_CHIPS if b_sharded else 1) == kdim
    else:
        kdim, m = a.shape
        n = b.shape[1]
        assert b.shape[0] == kdim and not b_sharded
    shard_n = n // N_CHIPS if (out_sharded or (b_sharded and mode == "nn")) else n
    shard_k = kdim // N_CHIPS if (b_sharded and mode == "nt") else kdim
    tm = _tile(m, tm)
    tn = _tile(shard_n, tn)
    tk = _tile(shard_k, tk)
    nk = kdim // tk
    nj_n = shard_n // tn
    nj_k = shard_k // tk
    j_outer = nk == 1 and (kdim * n + m * kdim * (n // tn)) < (m * kdim + kdim * n * (m // tm))
    grid = (n // tn, m // tm, nk) if j_outer else (m // tm, n // tn, nk)

    def at(index):
        return (lambda g0, g1, k: index(g1, g0, k)) if j_outer else index

    if mode == "nn":
        a_spec = pl.BlockSpec((tm, tk), at(lambda i, j, k: (i, k)))
        if b_sharded:
            b_spec = pl.BlockSpec((None, tk, tn), at(lambda i, j, k: (j // nj_n, k, j % nj_n)))
        else:
            b_spec = pl.BlockSpec((tk, tn), at(lambda i, j, k: (k, j)))
        dims = (((1,), (0,)), ((), ()))
    elif mode == "nt":
        a_spec = pl.BlockSpec((tm, tk), at(lambda i, j, k: (i, k)))
        if b_sharded:
            b_spec = pl.BlockSpec((None, tn, tk), at(lambda i, j, k: (k // nj_k, j, k % nj_k)))
        else:
            b_spec = pl.BlockSpec((tn, tk), at(lambda i, j, k: (j, k)))
        dims = (((1,), (1,)), ((), ()))
    else:
        a_spec = pl.BlockSpec((tk, tm), at(lambda i, j, k: (k, i)))
        b_spec = pl.BlockSpec((tk, tn), at(lambda i, j, k: (k, j)))
        dims = (((0,), (0,)), ((), ()))
    if out_sharded:
        o_spec = pl.BlockSpec((None, tm, tn), at(lambda i, j, k: (j // nj_n, i, j % nj_n)))
        o_shape = (N_CHIPS, m, shard_n)
    else:
        o_spec = pl.BlockSpec((tm, tn), at(lambda i, j, k: (i, j)))
        o_shape = (m, n)
    x_spec = pl.BlockSpec((tm, tn), at(lambda i, j, k: (i, j)))
    n_extra = len(extras)
    n_out = len(out_dtypes)

    def finish(acc, extra_refs, out_refs):
        res = (acc,) if epilogue is None else epilogue(acc, *[e[...] for e in extra_refs])
        for o, r in zip(out_refs, res):
            o[...] = r.astype(o.dtype)

    def body_single(a_ref, b_ref, *rest):
        finish(lax.dot_general(a_ref[...], b_ref[...], dims, preferred_element_type=F32),
               rest[:n_extra], rest[n_extra:n_extra + n_out])

    def body_steps(a_ref, b_ref, *rest):
        acc = rest[-1]
        k = pl.program_id(2)

        @pl.when(k == 0)
        def _():
            acc[...] = jnp.zeros_like(acc)

        acc[...] += lax.dot_general(a_ref[...], b_ref[...], dims, preferred_element_type=F32)

        @pl.when(k == nk - 1)
        def _():
            finish(acc[...], rest[:n_extra], rest[n_extra:n_extra + n_out])

    outs = pl.pallas_call(
        body_single if nk == 1 else body_steps,
        name=name,
        grid=grid,
        in_specs=[a_spec, b_spec] + [x_spec] * n_extra,
        out_specs=[o_spec] * n_out,
        out_shape=[jax.ShapeDtypeStruct(o_shape, dt) for dt in out_dtypes],
        scratch_shapes=[] if nk == 1 else [pltpu.VMEM((tm, tn), F32)],
        compiler_params=_params(("parallel", "parallel", "arbitrary")),
    )(a, b, *extras)
    return outs[0] if n_out == 1 else outs


def _rmsnorm_fwd(x, g, *, name):
    s, d = x.shape
    tr = _pick(s, 256)

    def body(x_ref, g_ref, o_ref):
        xv = x_ref[...]
        r = lax.rsqrt(jnp.mean(xv * xv, axis=-1, keepdims=True) + EPS)
        o_ref[...] = (xv * r * g_ref[...]).astype(o_ref.dtype)

    return pl.pallas_call(
        body, name=name, grid=(s // tr,),
        in_specs=[pl.BlockSpec((tr, d), lambda i: (i, 0)), pl.BlockSpec((1, d), lambda i: (0, 0))],
        out_specs=pl.BlockSpec((tr, d), lambda i: (i, 0)),
        out_shape=jax.ShapeDtypeStruct((s, d), BF16),
        compiler_params=_params(("parallel",)),
    )(x, g.reshape(1, d))


def _rmsnorm_bwd(x, g, dh, dres, *, name):
    s, d = x.shape
    tr = _pick(s, 256)

    def body(x_ref, g_ref, dh_ref, dres_ref, dx_ref, dxb_ref, dg_ref):
        xv = x_ref[...]
        r = lax.rsqrt(jnp.mean(xv * xv, axis=-1, keepdims=True) + EPS)
        y = xv * r
        dhv = dh_ref[...]
        dy = dhv * g_ref[...]
        dx = dres_ref[...] + r * (dy - y * jnp.mean(dy * y, axis=-1, keepdims=True))
        dx_ref[...] = dx
        dxb_ref[...] = dx.astype(BF16)

        @pl.when(pl.program_id(0) == 0)
        def _():
            dg_ref[...] = jnp.zeros_like(dg_ref)

        dg_ref[...] += jnp.sum(dhv * y, axis=0, keepdims=True)

    row = pl.BlockSpec((tr, d), lambda i: (i, 0))
    vec = pl.BlockSpec((1, d), lambda i: (0, 0))
    dx, dxb, dg = pl.pallas_call(
        body, name=name, grid=(s // tr,),
        in_specs=[row, vec, row, row],
        out_specs=[row, row, vec],
        out_shape=[jax.ShapeDtypeStruct((s, d), F32), jax.ShapeDtypeStruct((s, d), BF16),
                   jax.ShapeDtypeStruct((1, d), F32)],
        compiler_params=_params(("arbitrary",)),
    )(x, g.reshape(1, d), dh, dres)
    return dx, dxb, dg.reshape(d)


def _loss_head(y, target):
    s, d = y.shape
    tr = _pick(s, 256)

    def body(y_ref, t_ref, dy_ref, dyb_ref, part_ref):
        err = y_ref[...] - t_ref[...]
        dy = err * (1.0 / d)
        dy_ref[...] = dy
        dyb_ref[...] = dy.astype(BF16)

        @pl.when(pl.program_id(0) == 0)
        def _():
            part_ref[...] = jnp.zeros_like(part_ref)

        part_ref[...] += jnp.sum(err * err, axis=0, keepdims=True)

    row = pl.BlockSpec((tr, d), lambda i: (i, 0))
    vec = pl.BlockSpec((1, d), lambda i: (0, 0))
    dy, dyb, part = pl.pallas_call(
        body, name="loss_head", grid=(s // tr,),
        in_specs=[row, row], out_specs=[row, row, vec],
        out_shape=[jax.ShapeDtypeStruct((s, d), F32), jax.ShapeDtypeStruct((s, d), BF16),
                   jax.ShapeDtypeStruct((1, d), F32)],
        compiler_params=_params(("arbitrary",)),
    )(y, target)
    return 0.5 * jnp.sum(part) / d, dy, dyb


def _rope_tables(s):
    half = HEAD_DIM // 2
    inv_freq = ROPE_THETA ** (-jnp.arange(half, dtype=F32) / half)
    ang = jnp.arange(s, dtype=F32)[:, None] * inv_freq[None, :]
    cos, sin = jnp.cos(ang), jnp.sin(ang)
    return jnp.concatenate([cos, cos], axis=-1), jnp.concatenate([-sin, sin], axis=-1)


PREP_ROWS = 256


def _residue_spec(r, width):
    return pl.BlockSpec((r, PREP_ROWS // r, width), lambda i: (0, i, 0))


def _to_residues(scr, tile, r, dst_ref, cols, dtype):
    scr[...] = tile
    for c in range(r):
        dst_ref[c, :, cols] = scr[pl.ds(c, PREP_ROWS // r, stride=r), :].astype(dtype)


def _from_residues(scr, src_ref, cols, r):
    for c in range(r):
        scr[pl.ds(c, PREP_ROWS // r, stride=r), :] = src_ref[c, :, cols].astype(F32)
    return scr[...]


def _qkv_prep(proj, qg, kg, cos, sin, *, name):
    s = proj.shape[0]
    tr = PREP_ROWS
    assert s % tr == 0

    def body(p_ref, qg_ref, kg_ref, cos_ref, sin_ref, o_ref, res1_ref, res2_ref, scr):
        res_refs = (None, res1_ref, res2_ref)
        c = cos_ref[...]
        sn = sin_ref[...]
        for which, g_ref in ((0, qg_ref), (1, kg_ref)):
            for g in range(N_GROUPS):
                gain = g_ref[g:g + 1, :]
                for h in range(HEADS_PER_GROUP):
                    off = which * DIL_WIDTH + g * GROUP_WIDTH + h * HEAD_DIM
                    xv = p_ref[:, off:off + HEAD_DIM]
                    r = lax.rsqrt(jnp.mean(xv * xv, axis=-1, keepdims=True) + EPS)
                    y = xv * r * gain
                    rot = y * c + pltpu.roll(y, HEAD_DIM // 2, 1) * sn
                    if g == 0:
                        o_ref[:, off:off + HEAD_DIM] = rot.astype(BF16)
                    else:
                        at = which * GROUP_WIDTH + h * HEAD_DIM
                        _to_residues(scr, rot, DIL_GROUPS[g][1], res_refs[g], slice(at, at + HEAD_DIM), BF16)
        o_ref[:, 2 * DIL_WIDTH:2 * DIL_WIDTH + GROUP_WIDTH] = p_ref[:, 2 * DIL_WIDTH:2 * DIL_WIDTH + GROUP_WIDTH].astype(BF16)
        for g in range(1, N_GROUPS):
            for h in range(HEADS_PER_GROUP):
                off = 2 * DIL_WIDTH + g * GROUP_WIDTH + h * HEAD_DIM
                at = 2 * GROUP_WIDTH + h * HEAD_DIM
                _to_residues(scr, p_ref[:, off:off + HEAD_DIM], DIL_GROUPS[g][1], res_refs[g], slice(at, at + HEAD_DIM), BF16)
        o_ref[:, 3 * DIL_WIDTH:] = p_ref[:, 3 * DIL_WIDTH:].astype(BF16)

    r1, r2 = DIL_GROUPS[1][1], DIL_GROUPS[2][1]
    return pl.pallas_call(
        body, name=name, grid=(s // tr,),
        in_specs=[pl.BlockSpec((tr, QKV_WIDTH), lambda i: (i, 0)),
                  pl.BlockSpec((N_GROUPS, HEAD_DIM), lambda i: (0, 0)),
                  pl.BlockSpec((N_GROUPS, HEAD_DIM), lambda i: (0, 0)),
                  pl.BlockSpec((tr, HEAD_DIM), lambda i: (i, 0)),
                  pl.BlockSpec((tr, HEAD_DIM), lambda i: (i, 0))],
        out_specs=[pl.BlockSpec((tr, QKV_WIDTH), lambda i: (i, 0)),
                   _residue_spec(r1, 3 * GROUP_WIDTH), _residue_spec(r2, 3 * GROUP_WIDTH)],
        out_shape=[jax.ShapeDtypeStruct((s, QKV_WIDTH), BF16),
                   jax.ShapeDtypeStruct((r1, s // r1, 3 * GROUP_WIDTH), BF16),
                   jax.ShapeDtypeStruct((r2, s // r2, 3 * GROUP_WIDTH), BF16)],
        scratch_shapes=[pltpu.VMEM((tr, HEAD_DIM), F32)],
        compiler_params=_params(("parallel",)),
    )(proj, qg, kg, cos, sin)


def _qkv_prep_bwd(proj, qg, kg, cos, sin, dqkv_dil, dqkv_sb, dgate, *, name):
    s, n_in = proj.shape
    n_gate = dgate.shape[1]
    tr = PREP_ROWS

    def body(p_ref, qg_ref, kg_ref, cos_ref, sin_ref, *rest):
        dil_refs = rest[:9]
        sb_refs = rest[9:12]
        dgate_ref = rest[12]
        o_ref, dqg_ref, dkg_ref, scr = rest[13:17]
        c = cos_ref[...]
        sn = sin_ref[...]

        def incoming(g, which, h):
            cols = slice(h * HEAD_DIM, (h + 1) * HEAD_DIM)
            if g == 0:
                return dil_refs[which][:, cols]
            return _from_residues(scr, dil_refs[3 * g + which], cols, DIL_GROUPS[g][1])

        @pl.when(pl.program_id(0) == 0)
        def _():
            dqg_ref[...] = jnp.zeros_like(dqg_ref)
            dkg_ref[...] = jnp.zeros_like(dkg_ref)

        for which, g_ref, dg_ref in ((0, qg_ref, dqg_ref), (1, kg_ref, dkg_ref)):
            for g in range(N_GROUPS):
                gain = g_ref[g:g + 1, :]
                dgain = jnp.zeros((1, HEAD_DIM), F32)
                for h in range(HEADS_PER_GROUP):
                    off = which * DIL_WIDTH + g * GROUP_WIDTH + h * HEAD_DIM
                    xv = p_ref[:, off:off + HEAD_DIM]
                    r = lax.rsqrt(jnp.mean(xv * xv, axis=-1, keepdims=True) + EPS)
                    nx = xv * r
                    dout = incoming(g, which, h)
                    dy = dout * c + pltpu.roll(dout * sn, HEAD_DIM // 2, 1)
                    dgain = dgain + jnp.sum(dy * nx, axis=0, keepdims=True)
                    dn = dy * gain
                    dxv = r * (dn - nx * jnp.mean(dn * nx, axis=-1, keepdims=True))
                    o_ref[:, off:off + HEAD_DIM] = dxv.astype(BF16)
                dg_ref[g:g + 1, :] += dgain
        for g in range(N_GROUPS):
            for h in range(HEADS_PER_GROUP):
                off = 2 * DIL_WIDTH + g * GROUP_WIDTH + h * HEAD_DIM
                o_ref[:, off:off + HEAD_DIM] = incoming(g, 2, h).astype(BF16)
        for t in range(3):
            off = 3 * DIL_WIDTH + t * SB_WIDTH
            o_ref[:, off:off + SB_WIDTH] = sb_refs[t][...].astype(BF16)
        o_ref[:, QKV_WIDTH:] = dgate_ref[...]

    grp = pl.BlockSpec((tr, GROUP_WIDTH), lambda i: (i, 0))
    sbs = pl.BlockSpec((tr, SB_WIDTH), lambda i: (i, 0))
    gain_spec = pl.BlockSpec((N_GROUPS, HEAD_DIM), lambda i: (0, 0))
    tab = pl.BlockSpec((tr, HEAD_DIM), lambda i: (i, 0))
    flat_dil = [t for grp3 in dqkv_dil for t in grp3]
    dil_specs = [grp] * 3 + [_residue_spec(DIL_GROUPS[g][1], GROUP_WIDTH) for g in (1, 2) for _ in range(3)]
    return pl.pallas_call(
        body, name=name, grid=(s // tr,),
        in_specs=[pl.BlockSpec((tr, 2 * DIL_WIDTH), lambda i: (i, 0)), gain_spec, gain_spec, tab, tab]
                 + dil_specs + [sbs] * 3 + [pl.BlockSpec((tr, n_gate), lambda i: (i, 0))],
        out_specs=[pl.BlockSpec((tr, n_in), lambda i: (i, 0)), gain_spec, gain_spec],
        out_shape=[jax.ShapeDtypeStruct((s, n_in), BF16),
                   jax.ShapeDtypeStruct((N_GROUPS, HEAD_DIM), F32),
                   jax.ShapeDtypeStruct((N_GROUPS, HEAD_DIM), F32)],
        scratch_shapes=[pltpu.VMEM((tr, HEAD_DIM), F32)],
        compiler_params=_params(("arbitrary",)),
    )(proj, qg, kg, cos, sin, *flat_dil, *dqkv_sb, dgate)


def _dil_masks(i):
    row = lax.broadcasted_iota(jnp.int32, (BLOCK, BLOCK), 0)
    col = lax.broadcasted_iota(jnp.int32, (BLOCK, BLOCK), 1)
    return col <= row, (col >= row) & (i > 0)


def _dot_nt(a, b):
    return lax.dot_general(a, b, (((1,), (1,)), ((), ())), preferred_element_type=F32)


def _dot_tn(a, b):
    return lax.dot_general(a, b, (((0,), (0,)), ((), ())), preferred_element_type=F32)


def _dot(a, b):
    return jnp.dot(a, b, preferred_element_type=F32)


def _dil_attn_fwd(src, g, cols, *, name):
    s = src.shape[0]
    r = DIL_GROUPS[g][1]
    nb = s // r // BLOCK

    def body(q_ref, kp_ref, kc_ref, vp_ref, vc_ref, o_ref, ld_ref):
        i = pl.program_id(1)
        mask_c, mask_p = _dil_masks(i)
        sls = [slice(h * HEAD_DIM, (h + 1) * HEAD_DIM) for h in range(HEADS_PER_GROUP)]
        scores = [(_dot_nt(q_ref[:, sl], kc_ref[:, sl]), _dot_nt(q_ref[:, sl], kp_ref[:, sl])) for sl in sls]
        probs = []
        for sl, (raw_c, raw_p) in zip(sls, scores):
            s_c = jnp.where(mask_c, raw_c * SCALE, NEG_INF)
            s_p = jnp.where(mask_p, raw_p * SCALE, NEG_INF)
            m = jnp.maximum(jnp.max(s_c, axis=-1, keepdims=True), jnp.max(s_p, axis=-1, keepdims=True))
            p_c = jnp.exp(s_c - m)
            p_p = jnp.exp(s_p - m)
            l = jnp.sum(p_c, axis=-1, keepdims=True) + jnp.sum(p_p, axis=-1, keepdims=True)
            inv = 1.0 / l
            probs.append(((p_c * inv).astype(BF16), (p_p * inv).astype(BF16)))
            ld_ref[:, sl] = jnp.broadcast_to(m + jnp.log(l), (BLOCK, HEAD_DIM))
        for sl, (pn_c, pn_p) in zip(sls, probs):
            o_ref[:, sl] = _dot(pn_c, vc_ref[:, sl]) + _dot(pn_p, vp_ref[:, sl])

    def col(which):
        return lambda c, i: (c * nb + i, cols[which])

    def col_prev(which):
        return lambda c, i: (c * nb + jnp.maximum(i - 1, 0), cols[which])

    blk = (BLOCK, GROUP_WIDTH)
    return pl.pallas_call(
        body, name=name, grid=(r, nb),
        in_specs=[pl.BlockSpec(blk, col(0)), pl.BlockSpec(blk, col_prev(1)), pl.BlockSpec(blk, col(1)),
                  pl.BlockSpec(blk, col_prev(2)), pl.BlockSpec(blk, col(2))],
        out_specs=[pl.BlockSpec(blk, lambda c, i: (c * nb + i, 0))] * 2,
        out_shape=[jax.ShapeDtypeStruct((s, GROUP_WIDTH), F32)] * 2,
        compiler_params=_params(("parallel", "parallel")),
    )(src, src, src, src, src)


def _dil_merge(outs, lds, *, name):
    s = outs[0].shape[0]
    tr = PREP_ROWS
    rs = [DIL_GROUPS[g][1] for g in range(N_GROUPS)]

    def body(o0, o1, o2, l0, l1, l2, yb_ref, y0_ref, y1_ref, y2_ref, e0_ref, e1_ref, e2_ref, scr_a, scr_b):
        o_refs, l_refs = (o0, o1, o2), (l0, l1, l2)
        y_refs, e_refs = (y0_ref, y1_ref, y2_ref), (e0_ref, e1_ref, e2_ref)
        for h in range(HEADS_PER_GROUP):
            cols = slice(h * HEAD_DIM, (h + 1) * HEAD_DIM)
            o = [o_refs[0][:, cols]] + [_from_residues(scr_a, o_refs[g], cols, rs[g]) for g in (1, 2)]
            a = [l_refs[0][:, cols]] + [_from_residues(scr_b, l_refs[g], cols, rs[g]) for g in (1, 2)]
            m = jnp.maximum(jnp.maximum(a[0], a[1]), a[2])
            e = [jnp.exp(t - m) for t in a]
            tot = e[0] + e[1] + e[2]
            inv = 1.0 / tot
            y = (e[0] * inv) * o[0] + (e[1] * inv) * o[1] + (e[2] * inv) * o[2]
            lse = m + jnp.log(tot)
            yb_ref[:, cols] = y.astype(BF16)
            y_refs[0][:, cols] = y
            e_refs[0][:, cols] = lse
            for g in (1, 2):
                _to_residues(scr_a, y, rs[g], y_refs[g], cols, F32)
                _to_residues(scr_b, lse, rs[g], e_refs[g], cols, F32)

    blk = pl.BlockSpec((tr, GROUP_WIDTH), lambda i: (i, 0))
    specs = [blk] + [_residue_spec(rs[g], GROUP_WIDTH) for g in (1, 2)]
    shapes = [jax.ShapeDtypeStruct((s, GROUP_WIDTH), F32)] + [
        jax.ShapeDtypeStruct((rs[g], s // rs[g], GROUP_WIDTH), F32) for g in (1, 2)]
    res = pl.pallas_call(
        body, name=name, grid=(s // tr,),
        in_specs=specs * 2, out_specs=[blk] + specs * 2,
        out_shape=[jax.ShapeDtypeStruct((s, GROUP_WIDTH), BF16)] + shapes * 2,
        scratch_shapes=[pltpu.VMEM((tr, HEAD_DIM), F32), pltpu.VMEM((tr, HEAD_DIM), F32)],
        compiler_params=_params(("parallel",)),
    )(*[t.reshape(sh.shape) for t, sh in zip(outs, shapes)], *[t.reshape(sh.shape) for t, sh in zip(lds, shapes)])
    yb, ys, lses = res[0], res[1:4], res[4:7]
    return yb, [t.reshape(s, GROUP_WIDTH) for t in ys], [t.reshape(s, GROUP_WIDTH) for t in lses]


def _dil_to_residues(x, *, name):
    s = x.shape[0]
    tr = PREP_ROWS
    rs = [DIL_GROUPS[g][1] for g in (1, 2)]

    def body(x_ref, a_ref, b_ref, scr):
        for h in range(HEADS_PER_GROUP):
            cols = slice(h * HEAD_DIM, (h + 1) * HEAD_DIM)
            tile = x_ref[:, cols]
            _to_residues(scr, tile, rs[0], a_ref, cols, F32)
            _to_residues(scr, tile, rs[1], b_ref, cols, F32)

    outs = pl.pallas_call(
        body, name=name, grid=(s // tr,),
        in_specs=[pl.BlockSpec((tr, GROUP_WIDTH), lambda i: (i, 0))],
        out_specs=[_residue_spec(r, GROUP_WIDTH) for r in rs],
        out_shape=[jax.ShapeDtypeStruct((r, s // r, GROUP_WIDTH), F32) for r in rs],
        scratch_shapes=[pltpu.VMEM((tr, HEAD_DIM), F32)],
        compiler_params=_params(("parallel",)),
    )(x)
    return [t.reshape(s, GROUP_WIDTH) for t in outs]


def _dil_attn_bwd(src, g, cols, y, lse, dy, *, name):
    s = src.shape[0]
    r = DIL_GROUPS[g][1]
    nb = s // r // BLOCK

    def body(q_ref, qn_ref, kp_ref, kc_ref, vp_ref, vc_ref, y_ref, yn_ref, l_ref, ln_ref, d_ref, dn_ref,
             dq_ref, dk_ref, dv_ref):
        i = pl.program_id(1)
        mask_c, mask_p = _dil_masks(i)
        _, mask_n = _dil_masks(jnp.where(i + 1 < nb, 1, 0))
        sls = [slice(h * HEAD_DIM, (h + 1) * HEAD_DIM) for h in range(HEADS_PER_GROUP)]
        first = []
        for sl in sls:
            q, qn = q_ref[:, sl], qn_ref[:, sl]
            kc, kp, vc, vp = kc_ref[:, sl], kp_ref[:, sl], vc_ref[:, sl], vp_ref[:, sl]
            dyb, dynb = d_ref[:, sl].astype(BF16), dn_ref[:, sl].astype(BF16)
            first.append((_dot_nt(q, kc), _dot_nt(q, kp), _dot_nt(qn, kc),
                          _dot_nt(dyb, vc), _dot_nt(dyb, vp), _dot_nt(dynb, vc)))
        second = []
        for sl, (s_c, s_p, s_n, dp_c, dp_p, dp_n) in zip(sls, first):
            delta = jnp.sum(d_ref[:, sl] * y_ref[:, sl], axis=-1, keepdims=True)
            delta_n = jnp.sum(dn_ref[:, sl] * yn_ref[:, sl], axis=-1, keepdims=True)
            lse_q, lse_n = l_ref[:, sl], ln_ref[:, sl]
            p_c = jnp.where(mask_c, jnp.exp(s_c * SCALE - lse_q), 0.0)
            p_p = jnp.where(mask_p, jnp.exp(s_p * SCALE - lse_q), 0.0)
            p_n = jnp.where(mask_n, jnp.exp(s_n * SCALE - lse_n), 0.0)
            second.append(((p_c * (dp_c - delta) * SCALE).astype(BF16), (p_p * (dp_p - delta) * SCALE).astype(BF16),
                           (p_n * (dp_n - delta_n) * SCALE).astype(BF16), p_c.astype(BF16), p_n.astype(BF16)))
        for sl, (ds_c, ds_p, ds_n, pb_c, pb_n) in zip(sls, second):
            q, qn = q_ref[:, sl], qn_ref[:, sl]
            dyb, dynb = d_ref[:, sl].astype(BF16), dn_ref[:, sl].astype(BF16)
            dq_ref[:, sl] = _dot(ds_c, kc_ref[:, sl]) + _dot(ds_p, kp_ref[:, sl])
            dk_ref[:, sl] = _dot_tn(ds_c, q) + _dot_tn(ds_n, qn)
            dv_ref[:, sl] = _dot_tn(pb_c, dyb) + _dot_tn(pb_n, dynb)

    def col(which):
        return lambda c, i: (c * nb + i, cols[which])

    def col_prev(which):
        return lambda c, i: (c * nb + jnp.maximum(i - 1, 0), cols[which])

    def col_next(which):
        return lambda c, i: (c * nb + jnp.minimum(i + 1, nb - 1), cols[which])

    blk = (BLOCK, GROUP_WIDTH)
    own = pl.BlockSpec(blk, lambda c, i: (c * nb + i, 0))
    nxt = pl.BlockSpec(blk, lambda c, i: (c * nb + jnp.minimum(i + 1, nb - 1), 0))
    return pl.pallas_call(
        body, name=name, grid=(r, nb),
        in_specs=[pl.BlockSpec(blk, col(0)), pl.BlockSpec(blk, col_next(0)),
                  pl.BlockSpec(blk, col_prev(1)), pl.BlockSpec(blk, col(1)),
                  pl.BlockSpec(blk, col_prev(2)), pl.BlockSpec(blk, col(2)),
                  own, nxt, own, nxt, own, nxt],
        out_specs=[own] * 3,
        out_shape=[jax.ShapeDtypeStruct((s, GROUP_WIDTH), F32)] * 3,
        compiler_params=_params(("parallel", "parallel")),
    )(src, src, src, src, src, src, y, y, lse, lse, dy, dy)


SB_TQ = 256
SB_GROUP = 2
SB_DEAD = -104.0
SB_NH_FWD = 4
SB_NH_BWD = 2


def _split_dot(x, w):
    hi = x.astype(BF16)
    lo = (x - hi.astype(F32)).astype(BF16)
    return _dot(hi, w) + _dot(lo, w)


def _sb_consts():
    j = lax.broadcasted_iota(jnp.int32, (BLOCK, BLOCK), 0)
    k = lax.broadcasted_iota(jnp.int32, (BLOCK, BLOCK), 1)
    ones = jnp.ones((BLOCK, BLOCK), F32)
    after = jnp.concatenate([(j > k).astype(F32), ones], axis=1).astype(BF16)
    before = jnp.concatenate([(j < k).astype(F32), ones], axis=1).astype(BF16)
    return after, before


def _sb_mask(row0, col0, tq):
    row = row0 + lax.broadcasted_iota(jnp.int32, (tq, BLOCK), 0)
    col = col0 + lax.broadcasted_iota(jnp.int32, (tq, BLOCK), 1)
    return col < row


def _softplus(z):
    return jnp.maximum(z, 0.0) + jnp.log1p(jnp.exp(-jnp.abs(z)))


def _sb_attn_fwd(qkv, *, name):
    s = qkv.shape[0]
    tq = _pick(s, SB_TQ)
    nq = s // tq
    per = tq // BLOCK
    base = 3 * DIL_WIDTH // HEAD_DIM
    after, _ = _sb_consts()

    nh = SB_NH_FWD
    wide = nh * HEAD_DIM

    def body(q_ref, k_ref, v_ref, w_ref, o_ref, rc_ref, trips_ref, acc_ref, run_ref):
        qi = pl.program_id(1)
        w = w_ref[...]
        acc_ref[...] = jnp.zeros_like(acc_ref)
        run_ref[...] = jnp.zeros_like(run_ref)
        rc_ref[...] = jnp.zeros_like(rc_ref)
        lane = lax.broadcasted_iota(jnp.int32, (tq, BLOCK), 1)
        n_chunks = (qi + 1) * per

        def chunks(js, masked):
            sls = [slice(h * HEAD_DIM, (h + 1) * HEAD_DIM) for h in range(nh)]
            work = [(sl, j, pl.multiple_of(j * BLOCK, BLOCK), _sb_mask(qi * tq, j * BLOCK, tq) if masked else None)
                    for j in js for sl in sls]
            zs = [_dot_nt(q_ref[:, sl], k_ref[pl.ds(start, BLOCK), sl]) * SCALE for sl, _, start, _ in work]
            logits, sums = [], []
            for z, (_, _, _, mask) in zip(zs, work):
                sp = _softplus(z)
                logits.append(z - sp)
                sums.append(_split_dot(jnp.where(mask, -sp, 0.0) if masked else -sp, w))
            for (sl, j, start, mask), lg, sm in zip(work, logits, sums):
                run = run_ref[:, sl]
                a = jnp.exp(lg + run + sm[:, :BLOCK])
                if masked:
                    a = jnp.where(mask, a, 0.0)
                acc_ref[:, sl] += _dot(a.astype(BF16), v_ref[pl.ds(start, BLOCK), sl])
                rc_ref[:, sl] = jnp.where(lane == j, run, rc_ref[:, sl])
                run_ref[:, sl] = run + sm[:, BLOCK:]

        def diagonal(t, carry):
            last = n_chunks - 1 - SB_GROUP * t
            chunks([last - g for g in range(SB_GROUP)], True)
            return carry

        def alive():
            return (jnp.max(run_ref[...]) > SB_DEAD).astype(jnp.int32)

        def more(carry):
            t, live = carry
            return jnp.logical_and(t < n_left, live > 0)

        def left(carry):
            t, _ = carry
            last = n_chunks - 1 - per - SB_GROUP * t
            chunks([last - g for g in range(SB_GROUP)], False)
            return t + 1, alive()

        lax.fori_loop(0, per // SB_GROUP, diagonal, 0)
        n_left = (n_chunks - per) // SB_GROUP
        trips, _ = lax.while_loop(more, left, (jnp.int32(0), alive()))
        trips_ref[pl.program_id(0), qi] = trips
        o_ref[...] = acc_ref[...].astype(o_ref.dtype)

    head = lambda off: (lambda h, i: (0, (base + off) // nh + h))
    once = pl.Buffered(1)
    return pl.pallas_call(
        body, name=name, grid=(SB_HEADS // nh, nq),
        in_specs=[pl.BlockSpec((tq, wide), lambda h, i: (i, base // nh + h)),
                  pl.BlockSpec((s, wide), head(SB_HEADS), pipeline_mode=once),
                  pl.BlockSpec((s, wide), head(2 * SB_HEADS), pipeline_mode=once),
                  pl.BlockSpec((BLOCK, 2 * BLOCK), lambda h, i: (0, 0))],
        out_specs=[pl.BlockSpec((tq, wide), lambda h, i: (i, h))] * 2 + [pl.BlockSpec(memory_space=pltpu.SMEM)],
        out_shape=[jax.ShapeDtypeStruct((s, SB_WIDTH), BF16), jax.ShapeDtypeStruct((s, SB_WIDTH), F32),
                   jax.ShapeDtypeStruct((SB_HEADS // nh, nq), jnp.int32)],
        scratch_shapes=[pltpu.VMEM((tq, wide), F32), pltpu.VMEM((tq, wide), F32)],
        compiler_params=_params(("arbitrary", "arbitrary")),
    )(qkv, qkv, qkv, after)


def _sb_attn_bwd(qkv, rc, trips, do, *, name):
    s = qkv.shape[0]
    tq = _pick(s, SB_TQ)
    nq = s // tq
    per = tq // BLOCK
    base = 3 * DIL_WIDTH // HEAD_DIM
    _, before = _sb_consts()

    nh = SB_NH_BWD
    wide = nh * HEAD_DIM

    def body(trips_ref, q_ref, k_ref, v_ref, w_ref, rc_ref, do_ref, dq_ref, dk_ref, dv_ref, acc_ref, pre_ref, dob_ref):
        qi = pl.program_id(1)

        @pl.when(qi == 0)
        def _():
            dk_ref[...] = jnp.zeros_like(dk_ref)
            dv_ref[...] = jnp.zeros_like(dv_ref)

        w = w_ref[...]
        wa = (lax.broadcasted_iota(jnp.int32, (BLOCK, BLOCK), 0)
              > lax.broadcasted_iota(jnp.int32, (BLOCK, BLOCK), 1)).astype(BF16)
        dob_ref[...] = do_ref[...].astype(BF16)
        acc_ref[...] = jnp.zeros_like(acc_ref)
        pre_ref[...] = jnp.zeros_like(pre_ref)
        lane = lax.broadcasted_iota(jnp.int32, (tq, BLOCK), 1)
        n_chunks = (qi + 1) * per

        def chunks(js, masked):
            sls = [slice(h * HEAD_DIM, (h + 1) * HEAD_DIM) for h in range(nh)]
            work = [(sl, j, pl.multiple_of(j * BLOCK, BLOCK), _sb_mask(qi * tq, j * BLOCK, tq) if masked else None)
                    for j in js for sl in sls]
            zs = [_dot_nt(q_ref[:, sl], k_ref[pl.ds(start, BLOCK), sl]) * SCALE for sl, _, start, _ in work]
            das = [_dot_nt(dob_ref[:, sl], v_ref[pl.ds(start, BLOCK), sl]) for sl, _, start, _ in work]
            sigs, betweens = [], []
            for z, (_, _, _, mask) in zip(zs, work):
                sp = _softplus(z)
                sigs.append(jnp.exp(z - sp))
                betweens.append(_split_dot(jnp.where(mask, -sp, 0.0) if masked else -sp, wa))
            es, avs, sums = [], [], []
            for (sl, j, _, mask), sig, bt, da in zip(work, sigs, betweens, das):
                run = jnp.sum(jnp.where(lane == j, rc_ref[:, sl], 0.0), axis=-1, keepdims=True)
                a = sig * jnp.exp(run + bt)
                if masked:
                    a = jnp.where(mask, a, 0.0)
                e = a * da
                es.append(e)
                avs.append(a.astype(BF16))
                sums.append(_split_dot(e, w))
            for (sl, _, start, mask), sig, e, ab, sm in zip(work, sigs, es, avs, sums):
                pre = pre_ref[:, sl]
                dz = (e * (1.0 - sig) - sig * (pre + sm[:, :BLOCK])) * SCALE
                if masked:
                    dz = jnp.where(mask, dz, 0.0)
                dz = dz.astype(BF16)
                pre_ref[:, sl] = pre + sm[:, BLOCK:]
                acc_ref[:, sl] += _dot(dz, k_ref[pl.ds(start, BLOCK), sl])
                dk_ref[pl.ds(start, BLOCK), sl] += _dot_tn(dz, q_ref[:, sl])
                dv_ref[pl.ds(start, BLOCK), sl] += _dot_tn(ab, dob_ref[:, sl])

        def left(t, carry):
            chunks([SB_GROUP * t + g for g in range(SB_GROUP)], False)
            return carry

        def diagonal(t, carry):
            chunks([n_chunks - per + SB_GROUP * t + g for g in range(SB_GROUP)], True)
            return carry

        n_left = (n_chunks - per) // SB_GROUP
        ran = trips_ref[pl.program_id(0) // (SB_NH_FWD // nh), qi]
        lax.fori_loop(n_left - ran, n_left, left, 0)
        lax.fori_loop(0, per // SB_GROUP, diagonal, 0)
        dq_ref[...] = acc_ref[...]

    head = lambda off: (lambda h, i, t: (0, (base + off) // nh + h))
    blk = pl.BlockSpec((tq, wide), lambda h, i, t: (i, h))
    whole = pl.BlockSpec((s, wide), lambda h, i, t: (0, h))
    once = pl.Buffered(1)
    grid_spec = pltpu.PrefetchScalarGridSpec(
        num_scalar_prefetch=1, grid=(SB_HEADS // nh, nq),
        in_specs=[pl.BlockSpec((tq, wide), lambda h, i, t: (i, base // nh + h)),
                  pl.BlockSpec((s, wide), head(SB_HEADS), pipeline_mode=once),
                  pl.BlockSpec((s, wide), head(2 * SB_HEADS), pipeline_mode=once),
                  pl.BlockSpec((BLOCK, 2 * BLOCK), lambda h, i, t: (0, 0)),
                  blk, blk],
        out_specs=[blk, whole, whole],
        scratch_shapes=[pltpu.VMEM((tq, wide), F32), pltpu.VMEM((tq, wide), F32), pltpu.VMEM((tq, wide), BF16)],
    )
    return pl.pallas_call(
        body, name=name, grid_spec=grid_spec,
        out_shape=[jax.ShapeDtypeStruct((s, SB_WIDTH), F32)] * 3,
        compiler_params=_params(("parallel", "arbitrary")),
    )(trips, qkv, qkv, qkv, before, rc, do)


GATE_TC = 512


def _gate_fwd(proj, gate_b, up_dil, up_sb, *, name):
    s, d = up_dil.shape
    tr = _pick(s, 512)
    g0 = QKV_WIDTH // GATE_TC
    nc = d // GATE_TC

    def body(ga_ref, gb_ref, b_ref, ud_ref, us_ref, o_ref):
        ga = jax.nn.sigmoid(ga_ref[...] + b_ref[0:1, :])
        gb = jax.nn.sigmoid(gb_ref[...] + b_ref[1:2, :])
        o_ref[...] = (ga * ud_ref[...] + gb * us_ref[...]).astype(o_ref.dtype)

    blk = pl.BlockSpec((tr, GATE_TC), lambda i, j: (i, j))
    return pl.pallas_call(
        body, name=name, grid=(s // tr, nc),
        in_specs=[pl.BlockSpec((tr, GATE_TC), lambda i, j: (i, g0 + j)),
                  pl.BlockSpec((tr, GATE_TC), lambda i, j: (i, g0 + nc + j)),
                  pl.BlockSpec((2, GATE_TC), lambda i, j: (0, j)), blk, blk],
        out_specs=blk,
        out_shape=jax.ShapeDtypeStruct((s, d), BF16),
        compiler_params=_params(("parallel", "parallel")),
    )(proj, proj, gate_b, up_dil, up_sb)


def _gate_bwd(proj, gate_b, up_dil, up_sb, dmixed, *, name):
    s, d = up_dil.shape
    tr = _pick(s, 512)
    g0 = QKV_WIDTH // GATE_TC
    nc = d // GATE_TC

    def body(ga_ref, gb_ref, b_ref, ud_ref, us_ref, dm_ref, dud_ref, dus_ref, dga_ref, dgb_ref, db_ref):
        ga = jax.nn.sigmoid(ga_ref[...] + b_ref[0:1, :])
        gb = jax.nn.sigmoid(gb_ref[...] + b_ref[1:2, :])
        dm = dm_ref[...]
        dud_ref[...] = (dm * ga).astype(BF16)
        dus_ref[...] = (dm * gb).astype(BF16)
        dla = dm * ud_ref[...] * ga * (1.0 - ga)
        dlb = dm * us_ref[...] * gb * (1.0 - gb)
        dga_ref[...] = dla.astype(BF16)
        dgb_ref[...] = dlb.astype(BF16)

        @pl.when(pl.program_id(1) == 0)
        def _():
            db_ref[...] = jnp.zeros_like(db_ref)

        db_ref[0:1, :] += jnp.sum(dla, axis=0, keepdims=True)
        db_ref[1:2, :] += jnp.sum(dlb, axis=0, keepdims=True)

    blk = pl.BlockSpec((tr, GATE_TC), lambda j, i: (i, j))
    dud, dus, dga, dgb, db = pl.pallas_call(
        body, name=name, grid=(nc, s // tr),
        in_specs=[pl.BlockSpec((tr, GATE_TC), lambda j, i: (i, g0 + j)),
                  pl.BlockSpec((tr, GATE_TC), lambda j, i: (i, g0 + nc + j)),
                  pl.BlockSpec((2, GATE_TC), lambda j, i: (0, j)), blk, blk, blk],
        out_specs=[blk, blk, blk, blk, pl.BlockSpec((2, GATE_TC), lambda j, i: (0, j))],
        out_shape=[jax.ShapeDtypeStruct((s, d), BF16)] * 4 + [jax.ShapeDtypeStruct((2, d), F32)],
        compiler_params=_params(("parallel", "arbitrary")),
    )(proj, proj, gate_b, up_dil, up_sb, dmixed)
    return dud, dus, jnp.concatenate([dga, dgb], axis=1), db


def _adamw(w, g, m, v, *, name):
    shape = w.shape
    cols = shape[-1]
    rows = w.size // cols
    tr = _pick(rows, max(8, (512 * 1024) // cols))
    if rows % tr or (tr % 8 and tr != rows):
        tr = rows
    c1 = 1.0 - ADAM_B1 ** ADAM_STEP
    c2 = 1.0 - ADAM_B2 ** ADAM_STEP

    def body(w_ref, g_ref, m_ref, v_ref, d_ref, nm_ref, nv_ref):
        gv = g_ref[...]
        nm = ADAM_B1 * m_ref[...] + (1.0 - ADAM_B1) * gv
        nv = ADAM_B2 * v_ref[...] + (1.0 - ADAM_B2) * (gv * gv)
        d_ref[...] = -ADAM_LR * ((nm / c1) / (jnp.sqrt(nv / c2) + ADAM_EPS) + ADAM_WD * w_ref[...])
        nm_ref[...] = nm
        nv_ref[...] = nv

    blk = pl.BlockSpec((tr, cols), lambda i: (i, 0))
    outs = pl.pallas_call(
        body, name=name, grid=(rows // tr,),
        in_specs=[blk] * 4, out_specs=[blk] * 3,
        out_shape=[jax.ShapeDtypeStruct((rows, cols), F32)] * 3,
        compiler_params=_params(("parallel",)),
    )(*(t.reshape(rows, cols) for t in (w, g, m, v)))
    return tuple(t.reshape(shape) for t in outs)


def _add_half(full, other, which, *, name):
    n, rh, c = other.shape
    tr = _pick(rh, max(8, (1024 * 1024) // c))
    nb = rh // tr

    def body(which_ref, a_ref, b_ref, o_ref):
        o_ref[...] = (a_ref[...] + b_ref[...]).astype(o_ref.dtype)

    grid_spec = pltpu.PrefetchScalarGridSpec(
        num_scalar_prefetch=1, grid=(n, nb),
        in_specs=[pl.BlockSpec((None, tr, c), lambda s, i, w: (s, w[0] * nb + i, 0)),
                  pl.BlockSpec((None, tr, c), lambda s, i, w: (s, i, 0))],
        out_specs=pl.BlockSpec((None, tr, c), lambda s, i, w: (s, i, 0)),
    )
    return pl.pallas_call(
        body, name=name, grid_spec=grid_spec,
        out_shape=jax.ShapeDtypeStruct((n, rh, c), BF16),
        compiler_params=_params(("parallel", "parallel")),
    )(which, full, other)


def _sum_chips(own, parts, chip, *, name):
    n, r, c = parts.shape
    tr = _pick(r, max(16, (1024 * 1024) // c))

    def body(chip_ref, own_ref, p1_ref, p2_ref, p3_ref, o_ref):
        acc = own_ref[...].astype(F32)
        for p in (p1_ref, p2_ref, p3_ref):
            acc = acc + p[...].astype(F32)
        o_ref[...] = acc

    def slot(k):
        return pl.BlockSpec((None, tr, c), lambda i, o: (o[0] ^ k, i, 0))

    grid_spec = pltpu.PrefetchScalarGridSpec(
        num_scalar_prefetch=1, grid=(r // tr,),
        in_specs=[slot(0), slot(1), slot(2), slot(3)],
        out_specs=pl.BlockSpec((tr, c), lambda i, o: (i, 0)),
    )
    return pl.pallas_call(
        body, name=name, grid_spec=grid_spec,
        out_shape=jax.ShapeDtypeStruct((r, c), F32),
        compiler_params=_params(("parallel",)),
    )(chip, own, parts, parts, parts)


def _place_piece(full, piece, index, *, name):
    a, n, r, c = full.shape
    tr = _pick(r, max(16, (1024 * 1024) // c))

    def body(index_ref, piece_ref, full_ref, o_ref):
        o_ref[...] = piece_ref[...]

    grid_spec = pltpu.PrefetchScalarGridSpec(
        num_scalar_prefetch=1, grid=(a, r // tr),
        in_specs=[pl.BlockSpec((None, tr, c), lambda l, i, k: (l, i, 0)), ANY],
        out_specs=pl.BlockSpec((None, None, tr, c), lambda l, i, k: (l, k[0], i, 0)),
    )
    return pl.pallas_call(
        body, name=name, grid_spec=grid_spec,
        out_shape=jax.ShapeDtypeStruct(full.shape, full.dtype),
        input_output_aliases={2: 0},
        compiler_params=_params(("parallel", "parallel")),
    )(index, piece, full)


ANY = pl.BlockSpec(memory_space=pl.ANY)


def _place():
    x, y, c = lax.axis_index("x"), lax.axis_index("y"), lax.axis_index("c")
    chips = [(1 - x, y), (x, 1 - y), (1 - x, 1 - y)]
    return x, y, c, chips


def _remote(src, dst, send_sem, recv_sem, to):
    return pltpu.make_async_remote_copy(src_ref=src, dst_ref=dst, send_sem=send_sem, recv_sem=recv_sem,
                                        device_id=to, device_id_type=MESH)


def _gather_weights(shards):
    n = len(shards)

    def body(*refs):
        ins, outs = refs[:n], refs[n:2 * n]
        send_sems, recv_sems, fwd_send, fwd_recv = refs[2 * n:]
        x, y, c, chips = _place()
        me = 2 * x + y
        sibling = (x, y, 1 - c)

        def half(ref, chip, which):
            rh = ref.shape[2] // 2
            return ref.at[:, chip, pl.ds(which * rh, rh), :]

        def src_half(ref, which):
            rh = ref.shape[1] // 2
            return ref.at[:, pl.ds(which * rh, rh), :]

        first = []
        for t in range(n):
            for k, (px, py) in enumerate(chips):
                cp = _remote(src_half(ins[t], c), half(outs[t], me, c), send_sems.at[t, k], recv_sems.at[t, k],
                             (px, py, c))
                cp.start()
                first.append(cp)
        passed = []
        for t in range(n):
            for k, (px, py) in enumerate(chips):
                theirs = 2 * px + py
                landed = half(outs[t], theirs, c)
                _remote(landed, landed, send_sems.at[t, k], recv_sems.at[t, k], (px, py, c)).wait_recv()
                cp = _remote(landed, landed, fwd_send.at[t, k], fwd_recv.at[t, k], sibling)
                cp.start()
                passed.append(cp)
        for t in range(n):
            for k, (px, py) in enumerate(chips):
                theirs = 2 * px + py
                other = half(outs[t], theirs, 1 - c)
                _remote(other, other, fwd_send.at[t, k], fwd_recv.at[t, k], sibling).wait_recv()
        for cp in first + passed:
            cp.wait_send()

    return pl.pallas_call(
        body, name="gather_weights",
        in_specs=[ANY] * n, out_specs=[ANY] * n,
        out_shape=[jax.ShapeDtypeStruct((w.shape[0], N_CHIPS) + w.shape[1:], w.dtype) for w in shards],
        scratch_shapes=[pltpu.SemaphoreType.DMA((n, 3)), pltpu.SemaphoreType.DMA((n, 3)),
                        pltpu.SemaphoreType.DMA((n, 3)), pltpu.SemaphoreType.DMA((n, 3))],
        compiler_params=pltpu.CompilerParams(has_side_effects=True),
    )(*shards)


def _pair_swap_halves(grads, *, name):
    n = len(grads)

    def body(*refs):
        ins, outs = refs[:n], refs[n:2 * n]
        send_sems, recv_sems = refs[2 * n:]
        x, y, c, _ = _place()
        sibling = (x, y, 1 - c)
        cps = []
        for t in range(n):
            rh = ins[t].shape[1] // 2
            cp = _remote(ins[t].at[:, pl.ds((1 - c) * rh, rh), :], outs[t], send_sems.at[t], recv_sems.at[t], sibling)
            cp.start()
            cps.append(cp)
        for cp in cps:
            cp.wait()

    return pl.pallas_call(
        body, name=name,
        in_specs=[ANY] * n, out_specs=[ANY] * n,
        out_shape=[jax.ShapeDtypeStruct((g.shape[0], g.shape[1] // 2, g.shape[2]), g.dtype) for g in grads],
        scratch_shapes=[pltpu.SemaphoreType.DMA((n,)), pltpu.SemaphoreType.DMA((n,))],
        compiler_params=pltpu.CompilerParams(has_side_effects=True),
    )(*grads)


def _chip_exchange(parts, *, name):
    n = len(parts)

    def body(*refs):
        ins, outs = refs[:n], refs[n:2 * n]
        send_sems, recv_sems = refs[2 * n:]
        x, y, c, chips = _place()
        me = 2 * x + y
        cps = []
        for t in range(n):
            for k, (px, py) in enumerate(chips):
                cp = _remote(ins[t].at[2 * px + py], outs[t].at[me], send_sems.at[t, k], recv_sems.at[t, k], (px, py, c))
                cp.start()
                cps.append(cp)
        for t in range(n):
            for k, (px, py) in enumerate(chips):
                theirs = outs[t].at[2 * px + py]
                _remote(theirs, theirs, send_sems.at[t, k], recv_sems.at[t, k], (px, py, c)).wait_recv()
        for cp in cps:
            cp.wait_send()

    return pl.pallas_call(
        body, name=name,
        in_specs=[ANY] * n, out_specs=[ANY] * n,
        out_shape=[jax.ShapeDtypeStruct(p.shape, p.dtype) for p in parts],
        scratch_shapes=[pltpu.SemaphoreType.DMA((n, 3)), pltpu.SemaphoreType.DMA((n, 3))],
        compiler_params=pltpu.CompilerParams(has_side_effects=True),
    )(*parts)


def _pair_join_halves(halves, *, name):
    n = len(halves)

    def body(*refs):
        ins, outs = refs[:n], refs[n:2 * n]
        send_sems, recv_sems = refs[2 * n:]
        x, y, c, _ = _place()
        sibling = (x, y, 1 - c)
        cps = []
        for t in range(n):
            rh = ins[t].shape[0]
            cp = _remote(ins[t], outs[t].at[pl.ds(c * rh, rh), :], send_sems.at[t], recv_sems.at[t], sibling)
            cp.start()
            cps.append(cp)
        for t, cp in enumerate(cps):
            rh = ins[t].shape[0]
            theirs = outs[t].at[pl.ds((1 - c) * rh, rh), :]
            _remote(theirs, theirs, send_sems.at[t], recv_sems.at[t], sibling).wait_recv()
            cp.wait_send()

    return pl.pallas_call(
        body, name=name,
        in_specs=[ANY] * n, out_specs=[ANY] * n,
        out_shape=[jax.ShapeDtypeStruct((2 * h.shape[0], h.shape[1]), h.dtype) for h in halves],
        scratch_shapes=[pltpu.SemaphoreType.DMA((n,)), pltpu.SemaphoreType.DMA((n,))],
        compiler_params=pltpu.CompilerParams(has_side_effects=True),
    )(*halves)


def _exchange_small(buf, *, name):
    rows, lanes = buf.shape

    def body(in_ref, out_ref, send_sems, recv_sems, local_sem):
        x, y, c = lax.axis_index("x"), lax.axis_index("y"), lax.axis_index("c")
        me = 4 * x + 2 * y + c
        lc = pltpu.make_async_copy(in_ref, out_ref.at[me], local_sem)
        lc.start()
        cps = []
        for k in range(1, 8):
            fx, fy, fc = (k >> 2) & 1, (k >> 1) & 1, k & 1
            to = (x ^ fx, y ^ fy, c ^ fc)
            cp = _remote(in_ref, out_ref.at[me], send_sems.at[k - 1], recv_sems.at[k - 1], to)
            cp.start()
            cps.append(cp)
        for k in range(1, 8):
            fx, fy, fc = (k >> 2) & 1, (k >> 1) & 1, k & 1
            theirs = out_ref.at[4 * (x ^ fx) + 2 * (y ^ fy) + (c ^ fc)]
            _remote(theirs, theirs, send_sems.at[k - 1], recv_sems.at[k - 1], (x ^ fx, y ^ fy, c ^ fc)).wait_recv()
        for cp in cps:
            cp.wait_send()
        lc.wait()

    return pl.pallas_call(
        body, name=name,
        in_specs=[ANY], out_specs=ANY,
        out_shape=jax.ShapeDtypeStruct((8, rows, lanes), buf.dtype),
        scratch_shapes=[pltpu.SemaphoreType.DMA((7,)), pltpu.SemaphoreType.DMA((7,)), pltpu.SemaphoreType.DMA],
        compiler_params=pltpu.CompilerParams(has_side_effects=True),
    )(buf)


def _sum_slots(slots, *, name):
    n, rows, lanes = slots.shape

    def body(s_ref, o_ref):
        acc = s_ref[0]
        for t in range(1, n):
            acc = acc + s_ref[t]
        o_ref[...] = acc

    return pl.pallas_call(
        body, name=name,
        out_shape=jax.ShapeDtypeStruct((rows, lanes), slots.dtype),
    )(slots)


def _relu2(acc):
    r = jnp.maximum(acc, 0.0)
    return acc, r * r


def _layer_fwd(x, w, cos, sin, tag):
    h = _rmsnorm_fwd(x, w["norm1_g"], name=f"norm1_{tag}")
    proj = _matmul(h, w["w_in"], mode="nn", b_sharded=True, tm=512, name=f"proj_{tag}")
    qkv, res1, res2 = _qkv_prep(proj, w["q_norm_g"], w["k_norm_g"], cos, sin, name=f"qkv_prep_{tag}")
    dil_src = [(qkv, (0, 3, 6))] + [(t.reshape(-1, 3 * GROUP_WIDTH), (0, 1, 2)) for t in (res1, res2)]
    outs, lds = [], []
    for g, (src, cols) in enumerate(dil_src):
        o_g, ld_g = _dil_attn_fwd(src, g, cols, name=f"dil{g}_fwd_{tag}")
        outs.append(o_g)
        lds.append(ld_g)
    yb, y, lse = _dil_merge(outs, lds, name=f"dil_merge_{tag}")
    o_sb, rc, trips = _sb_attn_fwd(qkv, name=f"sb_fwd_{tag}")
    up_dil = _matmul(yb, w["w_up_dil"], mode="nn", b_sharded=True, name=f"up_dil_{tag}")
    up_sb = _matmul(o_sb, w["w_up_sb"], mode="nn", b_sharded=True, name=f"up_sb_{tag}")
    mixed = _gate_fwd(proj, w["gate_b"], up_dil, up_sb, name=f"gate_{tag}")
    x1 = _matmul(mixed, w["w_out"], mode="nn", extras=(x,), epilogue=lambda acc, res: (acc + res,), name=f"out_{tag}")
    h2 = _rmsnorm_fwd(x1, w["norm2_g"], name=f"norm2_{tag}")
    u, a = _matmul(h2, w["w_ff1"], mode="nn", b_sharded=True, out_dtypes=(F32, BF16), epilogue=_relu2,
                   name=f"ff1_{tag}")
    x2 = _matmul(a, w["w_ff2"], mode="nn", extras=(x1,), epilogue=lambda acc, res: (acc + res,), name=f"ff2_{tag}")
    saved = dict(x=x, h=h, proj=proj, qkv=qkv, dil_src=dil_src, y=y, yb=yb, lse=lse, o_sb=o_sb, rc=rc, trips=trips, up_dil=up_dil, up_sb=up_sb,
                 mixed=mixed, x1=x1, h2=h2, u=u, a=a)
    return x2, saved


def _layer_bwd(dx, dxb, w, sv, cos, sin, tag):
    grads = {}
    du = _matmul(dxb, w["w_ff2"], mode="nt", extras=(sv["u"],), out_dtypes=(BF16,),
                 epilogue=lambda acc, u: (acc * (2.0 * jnp.maximum(u, 0.0)),), name=f"d_a_{tag}")
    grads["w_ff2"] = _matmul(sv["a"], dxb, mode="tn", name=f"dw_ff2_{tag}")
    dh2 = _matmul(du, w["w_ff1"], mode="nt", b_sharded=True, name=f"d_h2_{tag}")
    grads["w_ff1"] = _matmul(sv["h2"], du, mode="tn", out_sharded=True, name=f"dw_ff1_{tag}")
    dx1, dx1b, grads["norm2_g"] = _rmsnorm_bwd(sv["x1"], w["norm2_g"], dh2, dx, name=f"norm2_bwd_{tag}")
    dmixed = _matmul(dx1b, w["w_out"], mode="nt", name=f"d_mixed_{tag}")
    grads["w_out"] = _matmul(sv["mixed"], dx1b, mode="tn", name=f"dw_out_{tag}")
    dud, dus, dgate, grads["gate_b"] = _gate_bwd(sv["proj"], w["gate_b"], sv["up_dil"], sv["up_sb"], dmixed,
                                                 name=f"gate_bwd_{tag}")
    dy_dil = _matmul(dud, w["w_up_dil"], mode="nt", b_sharded=True, name=f"d_ydil_{tag}")
    grads["w_up_dil"] = _matmul(sv["yb"], dud, mode="tn", out_sharded=True, name=f"dw_up_dil_{tag}")
    dy_sb = _matmul(dus, w["w_up_sb"], mode="nt", b_sharded=True, name=f"d_ysb_{tag}")
    grads["w_up_sb"] = _matmul(sv["o_sb"], dus, mode="tn", out_sharded=True, name=f"dw_up_sb_{tag}")
    dys = [dy_dil] + _dil_to_residues(dy_dil, name=f"dil_dy_{tag}")
    d_dil = [_dil_attn_bwd(src, g, cols, sv["y"][g], sv["lse"][g], dys[g], name=f"dil{g}_bwd_{tag}")
             for g, (src, cols) in enumerate(sv["dil_src"])]
    for g in (1, 2):
        d_dil[g] = [t.reshape(DIL_GROUPS[g][1], -1, GROUP_WIDTH) for t in d_dil[g]]
    d_sb = _sb_attn_bwd(sv["qkv"], sv["rc"], sv["trips"], dy_sb, name=f"sb_bwd_{tag}")
    dproj, grads["q_norm_g"], grads["k_norm_g"] = _qkv_prep_bwd(
        sv["proj"], w["q_norm_g"], w["k_norm_g"], cos, sin, d_dil, d_sb, dgate, name=f"qkv_prep_bwd_{tag}")
    dh = _matmul(dproj, w["w_in"], mode="nt", b_sharded=True, tm=512, tn=2048, name=f"d_h_{tag}")
    grads["w_in"] = _matmul(sv["h"], dproj, mode="tn", out_sharded=True, tm=512, tk=1024, name=f"dw_in_{tag}")
    dx0, dx0b, grads["norm1_g"] = _rmsnorm_bwd(sv["x"], w["norm1_g"], dh, dx1, name=f"norm1_bwd_{tag}")
    return dx0, dx0b, grads


def _local_step(x, target, layers, cos, sin):
    saved = []
    for l, w in enumerate(layers):
        x, sv = _layer_fwd(x, w, cos, sin, f"l{l}")
        saved.append(sv)
    loss, dx, dxb = _loss_head(x, target)
    grads = [None] * len(layers)
    for l in reversed(range(len(layers))):
        dx, dxb, grads[l] = _layer_bwd(dx, dxb, layers[l], saved[l], cos, sin, f"l{l}")
    return loss, dx, grads


BIG = ("w_in", "w_up_dil", "w_up_sb", "w_out", "w_ff1", "w_ff2")
ROW_SHARDED = ("w_out", "w_ff2")
WEIGHTS = ("norm1_g", "w_in", "q_norm_g", "k_norm_g", "w_up_dil", "w_up_sb", "gate_b", "w_out", "norm2_g", "w_ff1", "w_ff2")


def _reduce_to_shards(grads, which, chip, tag):
    names = list(grads)
    full = [grads[n] for n in names]
    theirs = _pair_swap_halves(full, name=f"rs_pair_{tag}")
    pair = [_add_half(f, t, which, name=f"rs_pair_sum_{n}_{tag}") for n, f, t in zip(names, full, theirs)]
    parts = _chip_exchange(pair, name=f"rs_chips_{tag}")
    halves = [_sum_chips(p, q, chip, name=f"rs_chip_sum_{n}_{tag}") for n, p, q in zip(names, pair, parts)]
    joined = _pair_join_halves(halves, name=f"rs_join_{tag}")
    out = {}
    for n, j, h in zip(names, joined, halves):
        r, c = j.shape
        out[n] = _place_piece(j.reshape(1, 2, r // 2, c), h[None], which, name=f"rs_place_{n}_{tag}").reshape(r, c)
    return out


def _pack_rows(vecs):
    rows, spans, at = [], [], 0
    for v in vecs:
        r = v.size // 128
        padded = -(-r // 8) * 8
        rows.append(jnp.pad(v.reshape(r, 128), ((0, padded - r), (0, 0))))
        spans.append((at, r))
        at += padded
    return jnp.concatenate(rows, axis=0), spans


def kernel(x, norm1_g, w_in, q_norm_g, k_norm_g, w_up_dil, w_up_sb, gate_b, w_out, norm2_g, w_ff1, w_ff2, loss_target, m_norm1_g, m_w_in, m_q_norm_g, m_k_norm_g, m_w_up_dil, m_w_up_sb, m_gate_b, m_w_out, m_norm2_g, m_w_ff1, m_w_ff2, v_norm1_g, v_w_in, v_q_norm_g, v_k_norm_g, v_w_up_dil, v_w_up_sb, v_gate_b, v_w_out, v_norm2_g, v_w_ff1, v_w_ff2):
    weights = dict(norm1_g=norm1_g, w_in=w_in, q_norm_g=q_norm_g, k_norm_g=k_norm_g, w_up_dil=w_up_dil,
                   w_up_sb=w_up_sb, gate_b=gate_b, w_out=w_out, norm2_g=norm2_g, w_ff1=w_ff1, w_ff2=w_ff2)
    moments_m = dict(norm1_g=m_norm1_g, w_in=m_w_in, q_norm_g=m_q_norm_g, k_norm_g=m_k_norm_g, w_up_dil=m_w_up_dil,
                     w_up_sb=m_w_up_sb, gate_b=m_gate_b, w_out=m_w_out, norm2_g=m_norm2_g, w_ff1=m_w_ff1, w_ff2=m_w_ff2)
    moments_v = dict(norm1_g=v_norm1_g, w_in=v_w_in, q_norm_g=v_q_norm_g, k_norm_g=v_k_norm_g, w_up_dil=v_w_up_dil,
                     w_up_sb=v_w_up_sb, gate_b=v_gate_b, w_out=v_w_out, norm2_g=v_norm2_g, w_ff1=v_w_ff1, w_ff2=v_w_ff2)
    depth = w_in.shape[0]
    seq, d_model = x.shape[1], x.shape[2]
    chip = 2 * lax.axis_index("x") + lax.axis_index("y")
    core = lax.axis_index("c")
    which = jnp.reshape(core, (1,)).astype(jnp.int32)

    chip_index = jnp.reshape(chip, (1,)).astype(jnp.int32)
    shards = [weights[n].astype(BF16) for n in BIG]
    gathered = {n: _place_piece(g, s, chip_index, name=f"gather_place_{n}")
                for n, g, s in zip(BIG, _gather_weights(shards), shards)}
    bias_buf, ((_, bias_rows),) = _pack_rows([gate_b])
    bias_slots = _exchange_small(bias_buf, name="gather_gate_b")
    bias = bias_slots[0::2, :bias_rows].reshape(N_CHIPS, depth, 2, d_model // N_CHIPS)
    bias = jnp.transpose(bias, (1, 2, 0, 3)).reshape(depth, 2, d_model)
    layers = []
    for l in range(depth):
        w = {n: gathered[n][l] for n in BIG}
        for n in ROW_SHARDED:
            w[n] = w[n].reshape(-1, w[n].shape[-1])
        w.update(norm1_g=norm1_g[l], norm2_g=norm2_g[l], q_norm_g=q_norm_g[l], k_norm_g=k_norm_g[l], gate_b=bias[l])
        layers.append(w)

    cos, sin = _rope_tables(seq)
    loss, dx, grads = _local_step(x[0], loss_target[0], layers, cos, sin)
    loss = lax.psum(loss, ("x", "y", "c"))

    reduced = []
    for l in range(depth):
        g = {n: grads[l][n] for n in BIG}
        for n in ROW_SHARDED:
            g[n] = g[n].reshape(N_CHIPS, -1, g[n].shape[-1])
        reduced.append(_reduce_to_shards(g, which, chip_index, f"l{l}"))
    final = {n: jnp.stack([reduced[l][n] for l in range(depth)]) for n in BIG}

    small_names = ("norm1_g", "norm2_g", "q_norm_g", "k_norm_g", "gate_b")
    packed, spans = _pack_rows([grads[l][n] for l in range(depth) for n in small_names])
    total = _sum_slots(_exchange_small(packed, name="exchange_small_grads"), name="sum_small_grads")
    pieces = iter(total[at:at + r] for at, r in spans)
    small = {n: [] for n in small_names}
    for l in range(depth):
        for n in small_names:
            small[n].append(next(pieces))
    for n in ("norm1_g", "norm2_g"):
        final[n] = jnp.stack([p.reshape(d_model) for p in small[n]])
    for n in ("q_norm_g", "k_norm_g"):
        final[n] = jnp.stack([p.reshape(N_GROUPS, HEAD_DIM) for p in small[n]])
    shard_cols = d_model // N_CHIPS
    final["gate_b"] = jnp.stack([lax.dynamic_slice_in_dim(p.reshape(2, d_model), chip * shard_cols, shard_cols, axis=1)
                                 for p in small["gate_b"]])

    deltas, new_m, new_v = {}, {}, {}
    for n in WEIGHTS:
        deltas[n], new_m[n], new_v[n] = _adamw(weights[n], final[n], moments_m[n], moments_v[n], name=f"adamw_{n}")
    return (loss, dx[None], *[final[n] for n in WEIGHTS], *[deltas[n] for n in WEIGHTS],
            *[new_m[n] for n in WEIGHTS], *[new_v[n] for n in WEIGHTS])
```

```python
import functools
import math

import jax
import jax.numpy as jnp
from jax import lax
from jax.experimental import pallas as pl
from jax.experimental.pallas import tpu as pltpu

F32 = jnp.float32
BF16 = jnp.bfloat16

HEAD_DIM = 128
DIL_GROUPS = ((128, 1), (512, 4), (2048, 16))
N_GROUPS = 3
HEADS_PER_GROUP = 4
GROUP_WIDTH = HEADS_PER_GROUP * HEAD_DIM
DIL_WIDTH = N_GROUPS * GROUP_WIDTH
SB_HEADS = 8
SB_WIDTH = SB_HEADS * HEAD_DIM
QKV_WIDTH = 3 * DIL_WIDTH + 3 * SB_WIDTH
BLOCK = 128
ROPE_THETA = 10000.0
EPS = 1e-6
SCALE = 1.0 / math.sqrt(HEAD_DIM)
NEG_INF = float("-inf")

ADAM_LR = 0.001
ADAM_B1 = 0.9
ADAM_B2 = 0.999
ADAM_EPS = 1e-08
ADAM_WD = 0.01
ADAM_STEP = 10

N_CHIPS = 4
MESH = pl.DeviceIdType.MESH
MIB = 1024 * 1024
VMEM_LIMIT = 56 * MIB


def _params(semantics=None, vmem=VMEM_LIMIT):
    return pltpu.CompilerParams(dimension_semantics=semantics, vmem_limit_bytes=vmem)


def _pick(n, pref):
    if n <= pref:
        return n
    t = 1 << (pref.bit_length() - 1)
    while n % t:
        t //= 2
    return t


def _tile(n, pref):
    best = 0
    for t in range(256, min(n, pref) + 1, 256):
        if n % t == 0:
            best = t
    return best if best else n


def _matmul(a, b, *, mode, name, out_dtypes=(F32,), extras=(), epilogue=None, b_sharded=False, out_sharded=False,
            tm=1024, tn=1024, tk=2048, side=None):
    if mode == "nn":
        m, kdim = a.shape
        n = b.shape[-1] * (N_CHIPS if b_sharded else 1)
        assert (b.shape[-2] if b_sharded else b.shape[0]) == kdim
    elif mode == "nt":
        m, kdim = a.shape
        n = b.shape[-2]
        assert b.shape[-1] * (N_CHIPS if b_sharded else 1) == kdim
    else:
        kdim, m = a.shape
        n = b.shape[1]
        assert b.shape[0] == kdim and not b_sharded
    shard_n = n // N_CHIPS if (out_sharded or (b_sharded and mode == "nn")) else n
    shard_k = kdim // N_CHIPS if (b_sharded and mode == "nt") else kdim
    tm = _tile(m, tm)
    tn = _tile(shard_n, tn)
    tk = _tile(shard_k, tk)
    nk = kdim // tk
    nj_n = shard_n // tn
    nj_k = shard_k // tk
    j_outer = nk == 1 and (kdim * n + m * kdim * (n // tn)) < (m * kdim + kdim * n * (m // tm))
    grid = (n // tn, m // tm, nk) if j_outer else (m // tm, n // tn, nk)

    def at(index):
        return (lambda g0, g1, k: index(g1, g0, k)) if j_outer else index

    if mode == "nn":
        a_spec = pl.BlockSpec((tm, tk), at(lambda i, j, k: (i, k)))
        if b_sharded:
            b_spec = pl.BlockSpec((None, tk, tn), at(lambda i, j, k: (j // nj_n, k, j % nj_n)))
        else:
            b_spec = pl.BlockSpec((tk, tn), at(lambda i, j, k: (k, j)))
        dims = (((1,), (0,)), ((), ()))
    elif mode == "nt":
        a_spec = pl.BlockSpec((tm, tk), at(lambda i, j, k: (i, k)))
        if b_sharded:
            b_spec = pl.BlockSpec((None, tn, tk), at(lambda i, j, k: (k // nj_k, j, k % nj_k)))
        else:
            b_spec = pl.BlockSpec((tn, tk), at(lambda i, j, k: (j, k)))
        dims = (((1,), (1,)), ((), ()))
    else:
        a_spec = pl.BlockSpec((tk, tm), at(lambda i, j, k: (k, i)))
        b_spec = pl.BlockSpec((tk, tn), at(lambda i, j, k: (k, j)))
        dims = (((0,), (0,)), ((), ()))
    if out_sharded:
        o_spec = pl.BlockSpec((None, tm, tn), at(lambda i, j, k: (j // nj_n, i, j % nj_n)))
        o_shape = (N_CHIPS, m, shard_n)
    else:
        o_spec = pl.BlockSpec((tm, tn), at(lambda i, j, k: (i, j)))
        o_shape = (m, n)
    x_spec = pl.BlockSpec((tm, tn), at(lambda i, j, k: (i, j)))
    n_extra = len(extras)
    n_out = len(out_dtypes)

    side_ins, side_shapes, side_copies, n_copies = side if side else ((), (), None, 0)
    n_side_in, n_side_out = len(side_ins), len(side_shapes)

    def finish(acc, extra_refs, out_refs):
        res = (acc,) if epilogue is None else epilogue(acc, *[e[...] for e in extra_refs])
        for o, r in zip(out_refs, res):
            o[...] = r.astype(o.dtype)

    def body(a_ref, b_ref, *rest):
        extra_refs = rest[:n_extra]
        side_in_refs = rest[n_extra:n_extra + n_side_in]
        rest = rest[n_extra + n_side_in:]
        out_refs, side_out_refs, scratch = rest[:n_out], rest[n_out:n_out + n_side_out], rest[n_out + n_side_out:]
        if side:
            step = (pl.program_id(0) * grid[1] + pl.program_id(1)) * grid[2] + pl.program_id(2)
            send_sems, recv_sems = scratch[-2:]

            @pl.when(step == 0)
            def _():
                sends, _ = side_copies(side_in_refs, side_out_refs, send_sems, recv_sems)
                for cp in sends:
                    cp.start()

        if nk == 1:
            finish(lax.dot_general(a_ref[...], b_ref[...], dims, preferred_element_type=F32), extra_refs, out_refs)
        else:
            acc = scratch[0]
            k = pl.program_id(2)

            @pl.when(k == 0)
            def _():
                acc[...] = jnp.zeros_like(acc)

            acc[...] += lax.dot_general(a_ref[...], b_ref[...], dims, preferred_element_type=F32)

            @pl.when(k == nk - 1)
            def _():
                finish(acc[...], extra_refs, out_refs)

        if side:
            @pl.when(step == grid[0] * grid[1] * grid[2] - 1)
            def _():
                sends, recvs = side_copies(side_in_refs, side_out_refs, send_sems, recv_sems)
                for cp in recvs:
                    cp.wait_recv()
                for cp in sends:
                    cp.wait_send()

    scratch_shapes = [] if nk == 1 else [pltpu.VMEM((tm, tn), F32)]
    if side:
        scratch_shapes += [pltpu.SemaphoreType.DMA((n_copies,)), pltpu.SemaphoreType.DMA((n_copies,))]
    outs = pl.pallas_call(
        body,
        name=name,
        grid=grid,
        in_specs=[a_spec, b_spec] + [x_spec] * n_extra + [ANY] * n_side_in,
        out_specs=[o_spec] * n_out + [ANY] * n_side_out,
        out_shape=[jax.ShapeDtypeStruct(o_shape, dt) for dt in out_dtypes] + list(side_shapes),
        scratch_shapes=scratch_shapes,
        compiler_params=_params(("arbitrary",) * 3 if side else ("parallel", "parallel", "arbitrary")),
    )(a, b, *extras, *side_ins)
    if side:
        return (outs[0] if n_out == 1 else outs[:n_out]), outs[n_out:]
    return outs[0] if n_out == 1 else outs


def _rmsnorm_fwd(x, g, *, name):
    s, d = x.shape
    tr = _pick(s, 256)

    def body(x_ref, g_ref, o_ref):
        xv = x_ref[...]
        r = lax.rsqrt(jnp.mean(xv * xv, axis=-1, keepdims=True) + EPS)
        o_ref[...] = (xv * r * g_ref[...]).astype(o_ref.dtype)

    return pl.pallas_call(
        body, name=name, grid=(s // tr,),
        in_specs=[pl.BlockSpec((tr, d), lambda i: (i, 0)), pl.BlockSpec((1, d), lambda i: (0, 0))],
        out_specs=pl.BlockSpec((tr, d), lambda i: (i, 0)),
        out_shape=jax.ShapeDtypeStruct((s, d), BF16),
        compiler_params=_params(("parallel",)),
    )(x, g.reshape(1, d))


def _rmsnorm_bwd(x, g, dh, dres, *, name):
    s, d = x.shape
    tr = _pick(s, 256)

    def body(x_ref, g_ref, dh_ref, dres_ref, dx_ref, dxb_ref, dg_ref):
        xv = x_ref[...]
        r = lax.rsqrt(jnp.mean(xv * xv, axis=-1, keepdims=True) + EPS)
        y = xv * r
        dhv = dh_ref[...]
        dy = dhv * g_ref[...]
        dx = dres_ref[...] + r * (dy - y * jnp.mean(dy * y, axis=-1, keepdims=True))
        dx_ref[...] = dx
        dxb_ref[...] = dx.astype(BF16)

        @pl.when(pl.program_id(0) == 0)
        def _():
            dg_ref[...] = jnp.zeros_like(dg_ref)

        dg_ref[...] += jnp.sum(dhv * y, axis=0, keepdims=True)

    row = pl.BlockSpec((tr, d), lambda i: (i, 0))
    vec = pl.BlockSpec((1, d), lambda i: (0, 0))
    dx, dxb, dg = pl.pallas_call(
        body, name=name, grid=(s // tr,),
        in_specs=[row, vec, row, row],
        out_specs=[row, row, vec],
        out_shape=[jax.ShapeDtypeStruct((s, d), F32), jax.ShapeDtypeStruct((s, d), BF16),
                   jax.ShapeDtypeStruct((1, d), F32)],
        compiler_params=_params(("arbitrary",)),
    )(x, g.reshape(1, d), dh, dres)
    return dx, dxb, dg.reshape(d)


def _loss_head(y, target):
    s, d = y.shape
    tr = _pick(s, 256)

    def body(y_ref, t_ref, dy_ref, dyb_ref, part_ref):
        err = y_ref[...] - t_ref[...]
        dy = err * (1.0 / d)
        dy_ref[...] = dy
        dyb_ref[...] = dy.astype(BF16)

        @pl.when(pl.program_id(0) == 0)
        def _():
            part_ref[...] = jnp.zeros_like(part_ref)

        part_ref[...] += jnp.sum(err * err, axis=0, keepdims=True)

    row = pl.BlockSpec((tr, d), lambda i: (i, 0))
    vec = pl.BlockSpec((1, d), lambda i: (0, 0))
    dy, dyb, part = pl.pallas_call(
        body, name="loss_head", grid=(s // tr,),
        in_specs=[row, row], out_specs=[row, row, vec],
        out_shape=[jax.ShapeDtypeStruct((s, d), F32), jax.ShapeDtypeStruct((s, d), BF16),
                   jax.ShapeDtypeStruct((1, d), F32)],
        compiler_params=_params(("arbitrary",)),
    )(y, target)
    return 0.5 * jnp.sum(part) / d, dy, dyb


def _rope_tables(s):
    half = HEAD_DIM // 2
    inv_freq = ROPE_THETA ** (-jnp.arange(half, dtype=F32) / half)
    ang = jnp.arange(s, dtype=F32)[:, None] * inv_freq[None, :]
    cos, sin = jnp.cos(ang), jnp.sin(ang)
    return jnp.concatenate([cos, cos], axis=-1), jnp.concatenate([-sin, sin], axis=-1)


PREP_ROWS = 256


def _residue_spec(r, width):
    return pl.BlockSpec((r, PREP_ROWS // r, width), lambda i: (0, i, 0))


def _to_residues(scr, tile, r, dst_ref, cols, dtype):
    scr[...] = tile
    for c in range(r):
        dst_ref[c, :, cols] = scr[pl.ds(c, PREP_ROWS // r, stride=r), :].astype(dtype)


def _from_residues(scr, src_ref, cols, r):
    for c in range(r):
        scr[pl.ds(c, PREP_ROWS // r, stride=r), :] = src_ref[c, :, cols].astype(F32)
    return scr[...]


def _qkv_prep(proj, qg, kg, cos, sin, *, name):
    s = proj.shape[0]
    tr = PREP_ROWS
    assert s % tr == 0

    def body(p_ref, qg_ref, kg_ref, cos_ref, sin_ref, o_ref, res1_ref, res2_ref, scr):
        res_refs = (None, res1_ref, res2_ref)
        c = cos_ref[...]
        sn = sin_ref[...]
        for which, g_ref in ((0, qg_ref), (1, kg_ref)):
            for g in range(N_GROUPS):
                gain = g_ref[g:g + 1, :]
                for h in range(HEADS_PER_GROUP):
                    off = which * DIL_WIDTH + g * GROUP_WIDTH + h * HEAD_DIM
                    xv = p_ref[:, off:off + HEAD_DIM]
                    r = lax.rsqrt(jnp.mean(xv * xv, axis=-1, keepdims=True) + EPS)
                    y = xv * r * gain
                    rot = y * c + pltpu.roll(y, HEAD_DIM // 2, 1) * sn
                    if g == 0:
                        o_ref[:, off:off + HEAD_DIM] = rot.astype(BF16)
                    else:
                        at = which * GROUP_WIDTH + h * HEAD_DIM
                        _to_residues(scr, rot, DIL_GROUPS[g][1], res_refs[g], slice(at, at + HEAD_DIM), BF16)
        o_ref[:, 2 * DIL_WIDTH:2 * DIL_WIDTH + GROUP_WIDTH] = p_ref[:, 2 * DIL_WIDTH:2 * DIL_WIDTH + GROUP_WIDTH].astype(BF16)
        for g in range(1, N_GROUPS):
            for h in range(HEADS_PER_GROUP):
                off = 2 * DIL_WIDTH + g * GROUP_WIDTH + h * HEAD_DIM
                at = 2 * GROUP_WIDTH + h * HEAD_DIM
                _to_residues(scr, p_ref[:, off:off + HEAD_DIM], DIL_GROUPS[g][1], res_refs[g], slice(at, at + HEAD_DIM), BF16)
        o_ref[:, 3 * DIL_WIDTH:] = p_ref[:, 3 * DIL_WIDTH:].astype(BF16)

    r1, r2 = DIL_GROUPS[1][1], DIL_GROUPS[2][1]
    return pl.pallas_call(
        body, name=name, grid=(s // tr,),
        in_specs=[pl.BlockSpec((tr, QKV_WIDTH), lambda i: (i, 0)),
                  pl.BlockSpec((N_GROUPS, HEAD_DIM), lambda i: (0, 0)),
                  pl.BlockSpec((N_GROUPS, HEAD_DIM), lambda i: (0, 0)),
                  pl.BlockSpec((tr, HEAD_DIM), lambda i: (i, 0)),
                  pl.BlockSpec((tr, HEAD_DIM), lambda i: (i, 0))],
        out_specs=[pl.BlockSpec((tr, QKV_WIDTH), lambda i: (i, 0)),
                   _residue_spec(r1, 3 * GROUP_WIDTH), _residue_spec(r2, 3 * GROUP_WIDTH)],
        out_shape=[jax.ShapeDtypeStruct((s, QKV_WIDTH), BF16),
                   jax.ShapeDtypeStruct((r1, s // r1, 3 * GROUP_WIDTH), BF16),
                   jax.ShapeDtypeStruct((r2, s // r2, 3 * GROUP_WIDTH), BF16)],
        scratch_shapes=[pltpu.VMEM((tr, HEAD_DIM), F32)],
        compiler_params=_params(("parallel",)),
    )(proj, qg, kg, cos, sin)


def _qkv_prep_bwd(proj, qg, kg, cos, sin, dqkv_dil, dqkv_sb, dgate, *, name):
    s, n_in = proj.shape
    n_gate = dgate.shape[1]
    tr = PREP_ROWS

    def body(p_ref, qg_ref, kg_ref, cos_ref, sin_ref, *rest):
        dil_refs = rest[:9]
        sb_refs = rest[9:12]
        dgate_ref = rest[12]
        o_ref, dqg_ref, dkg_ref, scr = rest[13:17]
        c = cos_ref[...]
        sn = sin_ref[...]

        def incoming(g, which, h):
            cols = slice(h * HEAD_DIM, (h + 1) * HEAD_DIM)
            if g == 0:
                return dil_refs[which][:, cols]
            return _from_residues(scr, dil_refs[3 * g + which], cols, DIL_GROUPS[g][1])

        @pl.when(pl.program_id(0) == 0)
        def _():
            dqg_ref[...] = jnp.zeros_like(dqg_ref)
            dkg_ref[...] = jnp.zeros_like(dkg_ref)

        for which, g_ref, dg_ref in ((0, qg_ref, dqg_ref), (1, kg_ref, dkg_ref)):
            for g in range(N_GROUPS):
                gain = g_ref[g:g + 1, :]
                dgain = jnp.zeros((1, HEAD_DIM), F32)
                for h in range(HEADS_PER_GROUP):
                    off = which * DIL_WIDTH + g * GROUP_WIDTH + h * HEAD_DIM
                    xv = p_ref[:, off:off + HEAD_DIM]
                    r = lax.rsqrt(jnp.mean(xv * xv, axis=-1, keepdims=True) + EPS)
                    nx = xv * r
                    dout = incoming(g, which, h)
                    dy = dout * c + pltpu.roll(dout * sn, HEAD_DIM // 2, 1)
                    dgain = dgain + jnp.sum(dy * nx, axis=0, keepdims=True)
                    dn = dy * gain
                    dxv = r * (dn - nx * jnp.mean(dn * nx, axis=-1, keepdims=True))
                    o_ref[:, off:off + HEAD_DIM] = dxv.astype(BF16)
                dg_ref[g:g + 1, :] += dgain
        for g in range(N_GROUPS):
            for h in range(HEADS_PER_GROUP):
                off = 2 * DIL_WIDTH + g * GROUP_WIDTH + h * HEAD_DIM
                o_ref[:, off:off + HEAD_DIM] = incoming(g, 2, h).astype(BF16)
        for t in range(3):
            off = 3 * DIL_WIDTH + t * SB_WIDTH
            o_ref[:, off:off + SB_WIDTH] = sb_refs[t][...].astype(BF16)
        o_ref[:, QKV_WIDTH:] = dgate_ref[...]

    grp = pl.BlockSpec((tr, GROUP_WIDTH), lambda i: (i, 0))
    sbs = pl.BlockSpec((tr, SB_WIDTH), lambda i: (i, 0))
    gain_spec = pl.BlockSpec((N_GROUPS, HEAD_DIM), lambda i: (0, 0))
    tab = pl.BlockSpec((tr, HEAD_DIM), lambda i: (i, 0))
    flat_dil = [t for grp3 in dqkv_dil for t in grp3]
    dil_specs = [grp] * 3 + [_residue_spec(DIL_GROUPS[g][1], GROUP_WIDTH) for g in (1, 2) for _ in range(3)]
    return pl.pallas_call(
        body, name=name, grid=(s // tr,),
        in_specs=[pl.BlockSpec((tr, 2 * DIL_WIDTH), lambda i: (i, 0)), gain_spec, gain_spec, tab, tab]
                 + dil_specs + [sbs] * 3 + [pl.BlockSpec((tr, n_gate), lambda i: (i, 0))],
        out_specs=[pl.BlockSpec((tr, n_in), lambda i: (i, 0)), gain_spec, gain_spec],
        out_shape=[jax.ShapeDtypeStruct((s, n_in), BF16),
                   jax.ShapeDtypeStruct((N_GROUPS, HEAD_DIM), F32),
                   jax.ShapeDtypeStruct((N_GROUPS, HEAD_DIM), F32)],
        scratch_shapes=[pltpu.VMEM((tr, HEAD_DIM), F32)],
        compiler_params=_params(("arbitrary",)),
    )(proj, qg, kg, cos, sin, *flat_dil, *dqkv_sb, dgate)


def _dil_masks(i):
    row = lax.broadcasted_iota(jnp.int32, (BLOCK, BLOCK), 0)
    col = lax.broadcasted_iota(jnp.int32, (BLOCK, BLOCK), 1)
    return col <= row, (col >= row) & (i > 0)


def _dot_nt(a, b):
    return lax.dot_general(a, b, (((1,), (1,)), ((), ())), preferred_element_type=F32)


def _dot_tn(a, b):
    return lax.dot_general(a, b, (((0,), (0,)), ((), ())), preferred_element_type=F32)


def _dot(a, b):
    return jnp.dot(a, b, preferred_element_type=F32)


def _dil_attn_fwd(src, g, cols, *, name):
    s = src.shape[0]
    r = DIL_GROUPS[g][1]
    nb = s // r // BLOCK

    def body(q_ref, kp_ref, kc_ref, vp_ref, vc_ref, o_ref, ld_ref):
        i = pl.program_id(1)
        mask_c, mask_p = _dil_masks(i)
        sls = [slice(h * HEAD_DIM, (h + 1) * HEAD_DIM) for h in range(HEADS_PER_GROUP)]
        scores = [(_dot_nt(q_ref[:, sl], kc_ref[:, sl]), _dot_nt(q_ref[:, sl], kp_ref[:, sl])) for sl in sls]
        probs = []
        for sl, (raw_c, raw_p) in zip(sls, scores):
            s_c = jnp.where(mask_c, raw_c * SCALE, NEG_INF)
            s_p = jnp.where(mask_p, raw_p * SCALE, NEG_INF)
            m = jnp.maximum(jnp.max(s_c, axis=-1, keepdims=True), jnp.max(s_p, axis=-1, keepdims=True))
            p_c = jnp.exp(s_c - m)
            p_p = jnp.exp(s_p - m)
            l = jnp.sum(p_c, axis=-1, keepdims=True) + jnp.sum(p_p, axis=-1, keepdims=True)
            inv = 1.0 / l
            probs.append(((p_c * inv).astype(BF16), (p_p * inv).astype(BF16)))
            ld_ref[:, sl] = jnp.broadcast_to(m + jnp.log(l), (BLOCK, HEAD_DIM))
        for sl, (pn_c, pn_p) in zip(sls, probs):
            o_ref[:, sl] = _dot(pn_c, vc_ref[:, sl]) + _dot(pn_p, vp_ref[:, sl])

    def col(which):
        return lambda c, i: (c * nb + i, cols[which])

    def col_prev(which):
        return lambda c, i: (c * nb + jnp.maximum(i - 1, 0), cols[which])

    blk = (BLOCK, GROUP_WIDTH)
    return pl.pallas_call(
        body, name=name, grid=(r, nb),
        in_specs=[pl.BlockSpec(blk, col(0)), pl.BlockSpec(blk, col_prev(1)), pl.BlockSpec(blk, col(1)),
                  pl.BlockSpec(blk, col_prev(2)), pl.BlockSpec(blk, col(2))],
        out_specs=[pl.BlockSpec(blk, lambda c, i: (c * nb + i, 0))] * 2,
        out_shape=[jax.ShapeDtypeStruct((s, GROUP_WIDTH), F32)] * 2,
        compiler_params=_params(("parallel", "parallel")),
    )(src, src, src, src, src)


def _dil_merge(outs, lds, *, name):
    s = outs[0].shape[0]
    tr = PREP_ROWS
    rs = [DIL_GROUPS[g][1] for g in range(N_GROUPS)]

    def body(o0, o1, o2, l0, l1, l2, yb_ref, y0_ref, y1_ref, y2_ref, e0_ref, e1_ref, e2_ref, scr_a, scr_b):
        o_refs, l_refs = (o0, o1, o2), (l0, l1, l2)
        y_refs, e_refs = (y0_ref, y1_ref, y2_ref), (e0_ref, e1_ref, e2_ref)
        for h in range(HEADS_PER_GROUP):
            cols = slice(h * HEAD_DIM, (h + 1) * HEAD_DIM)
            o = [o_refs[0][:, cols]] + [_from_residues(scr_a, o_refs[g], cols, rs[g]) for g in (1, 2)]
            a = [l_refs[0][:, cols]] + [_from_residues(scr_b, l_refs[g], cols, rs[g]) for g in (1, 2)]
            m = jnp.maximum(jnp.maximum(a[0], a[1]), a[2])
            e = [jnp.exp(t - m) for t in a]
            tot = e[0] + e[1] + e[2]
            inv = 1.0 / tot
            y = (e[0] * inv) * o[0] + (e[1] * inv) * o[1] + (e[2] * inv) * o[2]
            lse = m + jnp.log(tot)
            yb_ref[:, cols] = y.astype(BF16)
            y_refs[0][:, cols] = y
            e_refs[0][:, cols] = lse
            for g in (1, 2):
                _to_residues(scr_a, y, rs[g], y_refs[g], cols, F32)
                _to_residues(scr_b, lse, rs[g], e_refs[g], cols, F32)

    blk = pl.BlockSpec((tr, GROUP_WIDTH), lambda i: (i, 0))
    specs = [blk] + [_residue_spec(rs[g], GROUP_WIDTH) for g in (1, 2)]
    shapes = [jax.ShapeDtypeStruct((s, GROUP_WIDTH), F32)] + [
        jax.ShapeDtypeStruct((rs[g], s // rs[g], GROUP_WIDTH), F32) for g in (1, 2)]
    res = pl.pallas_call(
        body, name=name, grid=(s // tr,),
        in_specs=specs * 2, out_specs=[blk] + specs * 2,
        out_shape=[jax.ShapeDtypeStruct((s, GROUP_WIDTH), BF16)] + shapes * 2,
        scratch_shapes=[pltpu.VMEM((tr, HEAD_DIM), F32), pltpu.VMEM((tr, HEAD_DIM), F32)],
        compiler_params=_params(("parallel",)),
    )(*[t.reshape(sh.shape) for t, sh in zip(outs, shapes)], *[t.reshape(sh.shape) for t, sh in zip(lds, shapes)])
    yb, ys, lses = res[0], res[1:4], res[4:7]
    return yb, [t.reshape(s, GROUP_WIDTH) for t in ys], [t.reshape(s, GROUP_WIDTH) for t in lses]


def _dil_to_residues(x, *, name):
    s = x.shape[0]
    tr = PREP_ROWS
    rs = [DIL_GROUPS[g][1] for g in (1, 2)]

    def body(x_ref, a_ref, b_ref, scr):
        for h in range(HEADS_PER_GROUP):
            cols = slice(h * HEAD_DIM, (h + 1) * HEAD_DIM)
            tile = x_ref[:, cols]
            _to_residues(scr, tile, rs[0], a_ref, cols, F32)
            _to_residues(scr, tile, rs[1], b_ref, cols, F32)

    outs = pl.pallas_call(
        body, name=name, grid=(s // tr,),
        in_specs=[pl.BlockSpec((tr, GROUP_WIDTH), lambda i: (i, 0))],
        out_specs=[_residue_spec(r, GROUP_WIDTH) for r in rs],
        out_shape=[jax.ShapeDtypeStruct((r, s // r, GROUP_WIDTH), F32) for r in rs],
        scratch_shapes=[pltpu.VMEM((tr, HEAD_DIM), F32)],
        compiler_params=_params(("parallel",)),
    )(x)
    return [t.reshape(s, GROUP_WIDTH) for t in outs]


def _dil_attn_bwd(src, g, cols, y, lse, dy, *, name):
    s = src.shape[0]
    r = DIL_GROUPS[g][1]
    nb = s // r // BLOCK

    def body(q_ref, qn_ref, kp_ref, kc_ref, vp_ref, vc_ref, y_ref, yn_ref, l_ref, ln_ref, d_ref, dn_ref,
             dq_ref, dk_ref, dv_ref):
        i = pl.program_id(1)
        mask_c, mask_p = _dil_masks(i)
        _, mask_n = _dil_masks(jnp.where(i + 1 < nb, 1, 0))
        sls = [slice(h * HEAD_DIM, (h + 1) * HEAD_DIM) for h in range(HEADS_PER_GROUP)]
        first = []
        for sl in sls:
            q, qn = q_ref[:, sl], qn_ref[:, sl]
            kc, kp, vc, vp = kc_ref[:, sl], kp_ref[:, sl], vc_ref[:, sl], vp_ref[:, sl]
            dyb, dynb = d_ref[:, sl].astype(BF16), dn_ref[:, sl].astype(BF16)
            first.append((_dot_nt(q, kc), _dot_nt(q, kp), _dot_nt(qn, kc),
                          _dot_nt(dyb, vc), _dot_nt(dyb, vp), _dot_nt(dynb, vc)))
        second = []
        for sl, (s_c, s_p, s_n, dp_c, dp_p, dp_n) in zip(sls, first):
            delta = jnp.sum(d_ref[:, sl] * y_ref[:, sl], axis=-1, keepdims=True)
            delta_n = jnp.sum(dn_ref[:, sl] * yn_ref[:, sl], axis=-1, keepdims=True)
            lse_q, lse_n = l_ref[:, sl], ln_ref[:, sl]
            p_c = jnp.where(mask_c, jnp.exp(s_c * SCALE - lse_q), 0.0)
            p_p = jnp.where(mask_p, jnp.exp(s_p * SCALE - lse_q), 0.0)
            p_n = jnp.where(mask_n, jnp.exp(s_n * SCALE - lse_n), 0.0)
            second.append(((p_c * (dp_c - delta) * SCALE).astype(BF16), (p_p * (dp_p - delta) * SCALE).astype(BF16),
                           (p_n * (dp_n - delta_n) * SCALE).astype(BF16), p_c.astype(BF16), p_n.astype(BF16)))
        for sl, (ds_c, ds_p, ds_n, pb_c, pb_n) in zip(sls, second):
            q, qn = q_ref[:, sl], qn_ref[:, sl]
            dyb, dynb = d_ref[:, sl].astype(BF16), dn_ref[:, sl].astype(BF16)
            dq_ref[:, sl] = _dot(ds_c, kc_ref[:, sl]) + _dot(ds_p, kp_ref[:, sl])
            dk_ref[:, sl] = _dot_tn(ds_c, q) + _dot_tn(ds_n, qn)
            dv_ref[:, sl] = _dot_tn(pb_c, dyb) + _dot_tn(pb_n, dynb)

    def col(which):
        return lambda c, i: (c * nb + i, cols[which])

    def col_prev(which):
        return lambda c, i: (c * nb + jnp.maximum(i - 1, 0), cols[which])

    def col_next(which):
        return lambda c, i: (c * nb + jnp.minimum(i + 1, nb - 1), cols[which])

    blk = (BLOCK, GROUP_WIDTH)
    own = pl.BlockSpec(blk, lambda c, i: (c * nb + i, 0))
    nxt = pl.BlockSpec(blk, lambda c, i: (c * nb + jnp.minimum(i + 1, nb - 1), 0))
    return pl.pallas_call(
        body, name=name, grid=(r, nb),
        in_specs=[pl.BlockSpec(blk, col(0)), pl.BlockSpec(blk, col_next(0)),
                  pl.BlockSpec(blk, col_prev(1)), pl.BlockSpec(blk, col(1)),
                  pl.BlockSpec(blk, col_prev(2)), pl.BlockSpec(blk, col(2)),
                  own, nxt, own, nxt, own, nxt],
        out_specs=[own] * 3,
        out_shape=[jax.ShapeDtypeStruct((s, GROUP_WIDTH), F32)] * 3,
        compiler_params=_params(("parallel", "parallel")),
    )(src, src, src, src, src, src, y, y, lse, lse, dy, dy)


SB_TQ = 256
SB_GROUP = 2
SB_DEAD = -104.0
SB_NH_FWD = 4
SB_NH_BWD = 2


def _split_dot(x, w):
    hi = x.astype(BF16)
    lo = (x - hi.astype(F32)).astype(BF16)
    return _dot(hi, w) + _dot(lo, w)


def _sb_consts():
    j = lax.broadcasted_iota(jnp.int32, (BLOCK, BLOCK), 0)
    k = lax.broadcasted_iota(jnp.int32, (BLOCK, BLOCK), 1)
    ones = jnp.ones((BLOCK, BLOCK), F32)
    after = jnp.concatenate([(j > k).astype(F32), ones], axis=1).astype(BF16)
    before = jnp.concatenate([(j < k).astype(F32), ones], axis=1).astype(BF16)
    return after, before


def _sb_mask(row0, col0, tq):
    row = row0 + lax.broadcasted_iota(jnp.int32, (tq, BLOCK), 0)
    col = col0 + lax.broadcasted_iota(jnp.int32, (tq, BLOCK), 1)
    return col < row


def _softplus(z):
    return jnp.maximum(z, 0.0) + jnp.log1p(jnp.exp(-jnp.abs(z)))


def _sb_attn_fwd(qkv, *, name):
    s = qkv.shape[0]
    tq = _pick(s, SB_TQ)
    nq = s // tq
    per = tq // BLOCK
    base = 3 * DIL_WIDTH // HEAD_DIM
    after, _ = _sb_consts()

    nh = SB_NH_FWD
    wide = nh * HEAD_DIM

    def body(q_ref, k_ref, v_ref, w_ref, o_ref, rc_ref, trips_ref, acc_ref, run_ref):
        qi = pl.program_id(1)
        w = w_ref[...]
        acc_ref[...] = jnp.zeros_like(acc_ref)
        run_ref[...] = jnp.zeros_like(run_ref)
        rc_ref[...] = jnp.zeros_like(rc_ref)
        lane = lax.broadcasted_iota(jnp.int32, (tq, BLOCK), 1)
        n_chunks = (qi + 1) * per

        def chunks(js, masked):
            sls = [slice(h * HEAD_DIM, (h + 1) * HEAD_DIM) for h in range(nh)]
            work = [(sl, j, pl.multiple_of(j * BLOCK, BLOCK), _sb_mask(qi * tq, j * BLOCK, tq) if masked else None)
                    for j in js for sl in sls]
            zs = [_dot_nt(q_ref[:, sl], k_ref[pl.ds(start, BLOCK), sl]) * SCALE for sl, _, start, _ in work]
            logits, sums = [], []
            for z, (_, _, _, mask) in zip(zs, work):
                sp = _softplus(z)
                logits.append(z - sp)
                sums.append(_split_dot(jnp.where(mask, -sp, 0.0) if masked else -sp, w))
            for (sl, j, start, mask), lg, sm in zip(work, logits, sums):
                run = run_ref[:, sl]
                a = jnp.exp(lg + run + sm[:, :BLOCK])
                if masked:
                    a = jnp.where(mask, a, 0.0)
                acc_ref[:, sl] += _dot(a.astype(BF16), v_ref[pl.ds(start, BLOCK), sl])
                rc_ref[:, sl] = jnp.where(lane == j, run, rc_ref[:, sl])
                run_ref[:, sl] = run + sm[:, BLOCK:]

        def diagonal(t, carry):
            last = n_chunks - 1 - SB_GROUP * t
            chunks([last - g for g in range(SB_GROUP)], True)
            return carry

        def alive():
            return (jnp.max(run_ref[...]) > SB_DEAD).astype(jnp.int32)

        def more(carry):
            t, live = carry
            return jnp.logical_and(t < n_left, live > 0)

        def left(carry):
            t, _ = carry
            last = n_chunks - 1 - per - SB_GROUP * t
            chunks([last - g for g in range(SB_GROUP)], False)
            return t + 1, alive()

        lax.fori_loop(0, per // SB_GROUP, diagonal, 0)
        n_left = (n_chunks - per) // SB_GROUP
        trips, _ = lax.while_loop(more, left, (jnp.int32(0), alive()))
        trips_ref[pl.program_id(0), qi] = trips
        o_ref[...] = acc_ref[...].astype(o_ref.dtype)

    head = lambda off: (lambda h, i: (0, (base + off) // nh + h))
    once = pl.Buffered(1)
    return pl.pallas_call(
        body, name=name, grid=(SB_HEADS // nh, nq),
        in_specs=[pl.BlockSpec((tq, wide), lambda h, i: (i, base // nh + h)),
                  pl.BlockSpec((s, wide), head(SB_HEADS), pipeline_mode=once),
                  pl.BlockSpec((s, wide), head(2 * SB_HEADS), pipeline_mode=once),
                  pl.BlockSpec((BLOCK, 2 * BLOCK), lambda h, i: (0, 0))],
        out_specs=[pl.BlockSpec((tq, wide), lambda h, i: (i, h))] * 2 + [pl.BlockSpec(memory_space=pltpu.SMEM)],
        out_shape=[jax.ShapeDtypeStruct((s, SB_WIDTH), BF16), jax.ShapeDtypeStruct((s, SB_WIDTH), F32),
                   jax.ShapeDtypeStruct((SB_HEADS // nh, nq), jnp.int32)],
        scratch_shapes=[pltpu.VMEM((tq, wide), F32), pltpu.VMEM((tq, wide), F32)],
        compiler_params=_params(("arbitrary", "arbitrary")),
    )(qkv, qkv, qkv, after)


def _sb_attn_bwd(qkv, rc, trips, do, *, name):
    s = qkv.shape[0]
    tq = _pick(s, SB_TQ)
    nq = s // tq
    per = tq // BLOCK
    base = 3 * DIL_WIDTH // HEAD_DIM
    _, before = _sb_consts()

    nh = SB_NH_BWD
    wide = nh * HEAD_DIM

    def body(trips_ref, q_ref, k_ref, v_ref, w_ref, rc_ref, do_ref, dq_ref, dk_ref, dv_ref, acc_ref, pre_ref, dob_ref):
        qi = pl.program_id(1)

        @pl.when(qi == 0)
        def _():
            dk_ref[...] = jnp.zeros_like(dk_ref)
            dv_ref[...] = jnp.zeros_like(dv_ref)

        w = w_ref[...]
        wa = (lax.broadcasted_iota(jnp.int32, (BLOCK, BLOCK), 0)
              > lax.broadcasted_iota(jnp.int32, (BLOCK, BLOCK), 1)).astype(BF16)
        dob_ref[...] = do_ref[...].astype(BF16)
        acc_ref[...] = jnp.zeros_like(acc_ref)
        pre_ref[...] = jnp.zeros_like(pre_ref)
        lane = lax.broadcasted_iota(jnp.int32, (tq, BLOCK), 1)
        n_chunks = (qi + 1) * per

        def chunks(js, masked):
            sls = [slice(h * HEAD_DIM, (h + 1) * HEAD_DIM) for h in range(nh)]
            work = [(sl, j, pl.multiple_of(j * BLOCK, BLOCK), _sb_mask(qi * tq, j * BLOCK, tq) if masked else None)
                    for j in js for sl in sls]
            zs = [_dot_nt(q_ref[:, sl], k_ref[pl.ds(start, BLOCK), sl]) * SCALE for sl, _, start, _ in work]
            das = [_dot_nt(dob_ref[:, sl], v_ref[pl.ds(start, BLOCK), sl]) for sl, _, start, _ in work]
            sigs, betweens = [], []
            for z, (_, _, _, mask) in zip(zs, work):
                sp = _softplus(z)
                sigs.append(jnp.exp(z - sp))
                betweens.append(_split_dot(jnp.where(mask, -sp, 0.0) if masked else -sp, wa))
            es, avs, sums = [], [], []
            for (sl, j, _, mask), sig, bt, da in zip(work, sigs, betweens, das):
                run = jnp.sum(jnp.where(lane == j, rc_ref[:, sl], 0.0), axis=-1, keepdims=True)
                a = sig * jnp.exp(run + bt)
                if masked:
                    a = jnp.where(mask, a, 0.0)
                e = a * da
                es.append(e)
                avs.append(a.astype(BF16))
                sums.append(_split_dot(e, w))
            for (sl, _, start, mask), sig, e, ab, sm in zip(work, sigs, es, avs, sums):
                pre = pre_ref[:, sl]
                dz = (e * (1.0 - sig) - sig * (pre + sm[:, :BLOCK])) * SCALE
                if masked:
                    dz = jnp.where(mask, dz, 0.0)
                dz = dz.astype(BF16)
                pre_ref[:, sl] = pre + sm[:, BLOCK:]
                acc_ref[:, sl] += _dot(dz, k_ref[pl.ds(start, BLOCK), sl])
                dk_ref[pl.ds(start, BLOCK), sl] += _dot_tn(dz, q_ref[:, sl])
                dv_ref[pl.ds(start, BLOCK), sl] += _dot_tn(ab, dob_ref[:, sl])

        def left(t, carry):
            chunks([SB_GROUP * t + g for g in range(SB_GROUP)], False)
            return carry

        def diagonal(t, carry):
            chunks([n_chunks - per + SB_GROUP * t + g for g in range(SB_GROUP)], True)
            return carry

        n_left = (n_chunks - per) // SB_GROUP
        ran = trips_ref[pl.program_id(0) // (SB_NH_FWD // nh), qi]
        lax.fori_loop(n_left - ran, n_left, left, 0)
        lax.fori_loop(0, per // SB_GROUP, diagonal, 0)
        dq_ref[...] = acc_ref[...]

    head = lambda off: (lambda h, i, t: (0, (base + off) // nh + h))
    blk = pl.BlockSpec((tq, wide), lambda h, i, t: (i, h))
    whole = pl.BlockSpec((s, wide), lambda h, i, t: (0, h))
    once = pl.Buffered(1)
    grid_spec = pltpu.PrefetchScalarGridSpec(
        num_scalar_prefetch=1, grid=(SB_HEADS // nh, nq),
        in_specs=[pl.BlockSpec((tq, wide), lambda h, i, t: (i, base // nh + h)),
                  pl.BlockSpec((s, wide), head(SB_HEADS), pipeline_mode=once),
                  pl.BlockSpec((s, wide), head(2 * SB_HEADS), pipeline_mode=once),
                  pl.BlockSpec((BLOCK, 2 * BLOCK), lambda h, i, t: (0, 0)),
                  blk, blk],
        out_specs=[blk, whole, whole],
        scratch_shapes=[pltpu.VMEM((tq, wide), F32), pltpu.VMEM((tq, wide), F32), pltpu.VMEM((tq, wide), BF16)],
    )
    return pl.pallas_call(
        body, name=name, grid_spec=grid_spec,
        out_shape=[jax.ShapeDtypeStruct((s, SB_WIDTH), F32)] * 3,
        compiler_params=_params(("parallel", "arbitrary")),
    )(trips, qkv, qkv, qkv, before, rc, do)


GATE_TC = 512


def _gate_fwd(proj, gate_b, up_dil, up_sb, *, name):
    s, d = up_dil.shape
    tr = _pick(s, 512)
    g0 = QKV_WIDTH // GATE_TC
    nc = d // GATE_TC

    def body(ga_ref, gb_ref, b_ref, ud_ref, us_ref, o_ref):
        ga = jax.nn.sigmoid(ga_ref[...] + b_ref[0:1, :])
        gb = jax.nn.sigmoid(gb_ref[...] + b_ref[1:2, :])
        o_ref[...] = (ga * ud_ref[...] + gb * us_ref[...]).astype(o_ref.dtype)

    blk = pl.BlockSpec((tr, GATE_TC), lambda i, j: (i, j))
    return pl.pallas_call(
        body, name=name, grid=(s // tr, nc),
        in_specs=[pl.BlockSpec((tr, GATE_TC), lambda i, j: (i, g0 + j)),
                  pl.BlockSpec((tr, GATE_TC), lambda i, j: (i, g0 + nc + j)),
                  pl.BlockSpec((2, GATE_TC), lambda i, j: (0, j)), blk, blk],
        out_specs=blk,
        out_shape=jax.ShapeDtypeStruct((s, d), BF16),
        compiler_params=_params(("parallel", "parallel")),
    )(proj, proj, gate_b, up_dil, up_sb)


def _gate_bwd(proj, gate_b, up_dil, up_sb, dmixed, *, name):
    s, d = up_dil.shape
    tr = _pick(s, 512)
    g0 = QKV_WIDTH // GATE_TC
    nc = d // GATE_TC

    def body(ga_ref, gb_ref, b_ref, ud_ref, us_ref, dm_ref, dud_ref, dus_ref, dga_ref, dgb_ref, db_ref):
        ga = jax.nn.sigmoid(ga_ref[...] + b_ref[0:1, :])
        gb = jax.nn.sigmoid(gb_ref[...] + b_ref[1:2, :])
        dm = dm_ref[...]
        dud_ref[...] = (dm * ga).astype(BF16)
        dus_ref[...] = (dm * gb).astype(BF16)
        dla = dm * ud_ref[...] * ga * (1.0 - ga)
        dlb = dm * us_ref[...] * gb * (1.0 - gb)
        dga_ref[...] = dla.astype(BF16)
        dgb_ref[...] = dlb.astype(BF16)

        @pl.when(pl.program_id(1) == 0)
        def _():
            db_ref[...] = jnp.zeros_like(db_ref)

        db_ref[0:1, :] += jnp.sum(dla, axis=0, keepdims=True)
        db_ref[1:2, :] += jnp.sum(dlb, axis=0, keepdims=True)

    blk = pl.BlockSpec((tr, GATE_TC), lambda j, i: (i, j))
    dud, dus, dga, dgb, db = pl.pallas_call(
        body, name=name, grid=(nc, s // tr),
        in_specs=[pl.BlockSpec((tr, GATE_TC), lambda j, i: (i, g0 + j)),
                  pl.BlockSpec((tr, GATE_TC), lambda j, i: (i, g0 + nc + j)),
                  pl.BlockSpec((2, GATE_TC), lambda j, i: (0, j)), blk, blk, blk],
        out_specs=[blk, blk, blk, blk, pl.BlockSpec((2, GATE_TC), lambda j, i: (0, j))],
        out_shape=[jax.ShapeDtypeStruct((s, d), BF16)] * 4 + [jax.ShapeDtypeStruct((2, d), F32)],
        compiler_params=_params(("parallel", "arbitrary")),
    )(proj, proj, gate_b, up_dil, up_sb, dmixed)
    return dud, dus, jnp.concatenate([dga, dgb], axis=1), db


def _adamw(w, g, m, v, *, name):
    shape = w.shape
    cols = shape[-1]
    rows = w.size // cols
    tr = _pick(rows, max(8, (512 * 1024) // cols))
    if rows % tr or (tr % 8 and tr != rows):
        tr = rows
    c1 = 1.0 - ADAM_B1 ** ADAM_STEP
    c2 = 1.0 - ADAM_B2 ** ADAM_STEP

    def body(w_ref, g_ref, m_ref, v_ref, d_ref, nm_ref, nv_ref):
        gv = g_ref[...]
        nm = ADAM_B1 * m_ref[...] + (1.0 - ADAM_B1) * gv
        nv = ADAM_B2 * v_ref[...] + (1.0 - ADAM_B2) * (gv * gv)
        d_ref[...] = -ADAM_LR * ((nm / c1) / (jnp.sqrt(nv / c2) + ADAM_EPS) + ADAM_WD * w_ref[...])
        nm_ref[...] = nm
        nv_ref[...] = nv

    blk = pl.BlockSpec((tr, cols), lambda i: (i, 0))
    outs = pl.pallas_call(
        body, name=name, grid=(rows // tr,),
        in_specs=[blk] * 4, out_specs=[blk] * 3,
        out_shape=[jax.ShapeDtypeStruct((rows, cols), F32)] * 3,
        compiler_params=_params(("parallel",)),
    )(*(t.reshape(rows, cols) for t in (w, g, m, v)))
    return tuple(t.reshape(shape) for t in outs)


def _add_half(full, other, which, *, name):
    n, rh, c = other.shape
    tr = _pick(rh, max(8, (1024 * 1024) // c))
    nb = rh // tr

    def body(which_ref, a_ref, b_ref, o_ref):
        o_ref[...] = (a_ref[...] + b_ref[...]).astype(o_ref.dtype)

    grid_spec = pltpu.PrefetchScalarGridSpec(
        num_scalar_prefetch=1, grid=(n, nb),
        in_specs=[pl.BlockSpec((None, tr, c), lambda s, i, w: (s, w[0] * nb + i, 0)),
                  pl.BlockSpec((None, tr, c), lambda s, i, w: (s, i, 0))],
        out_specs=pl.BlockSpec((None, tr, c), lambda s, i, w: (s, i, 0)),
    )
    return pl.pallas_call(
        body, name=name, grid_spec=grid_spec,
        out_shape=jax.ShapeDtypeStruct((n, rh, c), BF16),
        compiler_params=_params(("parallel", "parallel")),
    )(which, full, other)


def _sum_chips(own, parts, chip, *, name):
    n, r, c = parts.shape
    tr = _pick(r, max(16, (1024 * 1024) // c))

    def body(chip_ref, own_ref, p1_ref, p2_ref, p3_ref, o_ref):
        acc = own_ref[...].astype(F32)
        for p in (p1_ref, p2_ref, p3_ref):
            acc = acc + p[...].astype(F32)
        o_ref[...] = acc

    def slot(k):
        return pl.BlockSpec((None, tr, c), lambda i, o: (o[0] ^ k, i, 0))

    grid_spec = pltpu.PrefetchScalarGridSpec(
        num_scalar_prefetch=1, grid=(r // tr,),
        in_specs=[slot(0), slot(1), slot(2), slot(3)],
        out_specs=pl.BlockSpec((tr, c), lambda i, o: (i, 0)),
    )
    return pl.pallas_call(
        body, name=name, grid_spec=grid_spec,
        out_shape=jax.ShapeDtypeStruct((r, c), F32),
        compiler_params=_params(("parallel",)),
    )(chip, own, parts, parts, parts)


def _place_piece(full, piece, index, *, name):
    a, n, r, c = full.shape
    tr = _pick(r, max(16, (1024 * 1024) // c))

    def body(index_ref, piece_ref, full_ref, o_ref):
        o_ref[...] = piece_ref[...]

    grid_spec = pltpu.PrefetchScalarGridSpec(
        num_scalar_prefetch=1, grid=(a, r // tr),
        in_specs=[pl.BlockSpec((None, tr, c), lambda l, i, k: (l, i, 0)), ANY],
        out_specs=pl.BlockSpec((None, None, tr, c), lambda l, i, k: (l, k[0], i, 0)),
    )
    return pl.pallas_call(
        body, name=name, grid_spec=grid_spec,
        out_shape=jax.ShapeDtypeStruct(full.shape, full.dtype),
        input_output_aliases={2: 0},
        compiler_params=_params(("parallel", "parallel")),
    )(index, piece, full)


ANY = pl.BlockSpec(memory_space=pl.ANY)


def _place():
    x, y, c = lax.axis_index("x"), lax.axis_index("y"), lax.axis_index("c")
    chips = [(1 - x, y), (x, 1 - y), (1 - x, 1 - y)]
    return x, y, c, chips


def _remote(src, dst, send_sem, recv_sem, to):
    return pltpu.make_async_remote_copy(src_ref=src, dst_ref=dst, send_sem=send_sem, recv_sem=recv_sem,
                                        device_id=to, device_id_type=MESH)


def _half(ref, chip, which):
    rh = ref.shape[1] // 2
    return ref.at[chip, pl.ds(which * rh, rh), :]


def _gather_ici_copies(layer):
    def copies(ins, outs, send_sems, recv_sems):
        x, y, c, chips = _place()
        me = 2 * x + y
        sends, recvs = [], []
        for t in range(len(ins)):
            rh = ins[t].shape[1] // 2
            for k, (px, py) in enumerate(chips):
                sems = (send_sems.at[3 * t + k], recv_sems.at[3 * t + k])
                sends.append(_remote(ins[t].at[layer, pl.ds(c * rh, rh), :], _half(outs[t], me, c), *sems, (px, py, c)))
                landed = _half(outs[t], 2 * px + py, c)
                recvs.append(_remote(landed, landed, *sems, (px, py, c)))
        return sends, recvs
    return copies


def _gather_pass_copies(ins, outs, send_sems, recv_sems):
    x, y, c, chips = _place()
    sibling = (x, y, 1 - c)
    sends, recvs = [], []
    for t in range(len(ins)):
        for k, (px, py) in enumerate(chips):
            sems = (send_sems.at[3 * t + k], recv_sems.at[3 * t + k])
            sends.append(_remote(_half(ins[t], 2 * px + py, c), _half(outs[t], 2 * px + py, c), *sems, sibling))
            other = _half(outs[t], 2 * px + py, 1 - c)
            recvs.append(_remote(other, other, *sems, sibling))
    return sends, recvs


def _gathered_shapes(shards):
    return [jax.ShapeDtypeStruct((N_CHIPS,) + w.shape[1:], w.dtype) for w in shards]


def _gather_ici_side(shards, layer):
    return (tuple(shards), _gathered_shapes(shards), _gather_ici_copies(layer), 3 * len(shards))


def _gather_layer(shards, layer, *, name):
    n = len(shards)
    ici = _gather_ici_copies(layer)

    def body(*refs):
        ins, outs = refs[:n], refs[n:2 * n]
        send_sems, recv_sems, pass_send, pass_recv = refs[2 * n:]
        sends, recvs = ici(ins, outs, send_sems, recv_sems)
        passes, arrivals = _gather_pass_copies(outs, outs, pass_send, pass_recv)
        for cp in sends:
            cp.start()
        for landed, onward in zip(recvs, passes):
            landed.wait_recv()
            onward.start()
        for cp in arrivals:
            cp.wait_recv()
        for cp in sends + passes:
            cp.wait_send()

    return pl.pallas_call(
        body, name=name,
        in_specs=[ANY] * n, out_specs=[ANY] * n,
        out_shape=_gathered_shapes(shards),
        scratch_shapes=[pltpu.SemaphoreType.DMA((3 * n,)) for _ in range(4)],
        compiler_params=pltpu.CompilerParams(has_side_effects=True),
    )(*shards)


def _gather_pass(landed, *, name):
    n = len(landed)

    def body(*refs):
        ins, outs = refs[:n], refs[n:2 * n]
        send_sems, recv_sems = refs[2 * n:]
        passes, arrivals = _gather_pass_copies(ins, outs, send_sems, recv_sems)
        for cp in passes:
            cp.start()
        for cp in arrivals:
            cp.wait_recv()
        for cp in passes:
            cp.wait_send()

    return pl.pallas_call(
        body, name=name,
        in_specs=[ANY] * n, out_specs=[ANY] * n,
        out_shape=[jax.ShapeDtypeStruct(w.shape, w.dtype) for w in landed],
        input_output_aliases={t: t for t in range(n)},
        scratch_shapes=[pltpu.SemaphoreType.DMA((3 * n,)), pltpu.SemaphoreType.DMA((3 * n,))],
        compiler_params=pltpu.CompilerParams(has_side_effects=True),
    )(*landed)


def _pair_swap_halves(grads, *, name):
    n = len(grads)

    def body(*refs):
        ins, outs = refs[:n], refs[n:2 * n]
        send_sems, recv_sems = refs[2 * n:]
        x, y, c, _ = _place()
        sibling = (x, y, 1 - c)
        cps = []
        for t in range(n):
            rh = ins[t].shape[1] // 2
            cp = _remote(ins[t].at[:, pl.ds((1 - c) * rh, rh), :], outs[t], send_sems.at[t], recv_sems.at[t], sibling)
            cp.start()
            cps.append(cp)
        for cp in cps:
            cp.wait()

    return pl.pallas_call(
        body, name=name,
        in_specs=[ANY] * n, out_specs=[ANY] * n,
        out_shape=[jax.ShapeDtypeStruct((g.shape[0], g.shape[1] // 2, g.shape[2]), g.dtype) for g in grads],
        scratch_shapes=[pltpu.SemaphoreType.DMA((n,)), pltpu.SemaphoreType.DMA((n,))],
        compiler_params=pltpu.CompilerParams(has_side_effects=True),
    )(*grads)


def _chip_copies(ins, outs, send_sems, recv_sems):
    x, y, c, chips = _place()
    me = 2 * x + y
    sends, recvs = [], []
    for t in range(len(ins)):
        for k, (px, py) in enumerate(chips):
            sems = (send_sems.at[3 * t + k], recv_sems.at[3 * t + k])
            sends.append(_remote(ins[t].at[2 * px + py], outs[t].at[me], *sems, (px, py, c)))
            theirs = outs[t].at[2 * px + py]
            recvs.append(_remote(theirs, theirs, *sems, (px, py, c)))
    return sends, recvs


def _chip_side(parts):
    return (tuple(parts), [jax.ShapeDtypeStruct(p.shape, p.dtype) for p in parts], _chip_copies, 3 * len(parts))


def _chip_exchange(parts, *, name):
    n = len(parts)

    def body(*refs):
        sends, recvs = _chip_copies(refs[:n], refs[n:2 * n], *refs[2 * n:])
        for cp in sends:
            cp.start()
        for cp in recvs:
            cp.wait_recv()
        for cp in sends:
            cp.wait_send()

    return pl.pallas_call(
        body, name=name,
        in_specs=[ANY] * n, out_specs=[ANY] * n,
        out_shape=[jax.ShapeDtypeStruct(p.shape, p.dtype) for p in parts],
        scratch_shapes=[pltpu.SemaphoreType.DMA((3 * n,)), pltpu.SemaphoreType.DMA((3 * n,))],
        compiler_params=pltpu.CompilerParams(has_side_effects=True),
    )(*parts)


def _pair_join_halves(halves, *, name):
    n = len(halves)

    def body(*refs):
        ins, outs = refs[:n], refs[n:2 * n]
        send_sems, recv_sems = refs[2 * n:]
        x, y, c, _ = _place()
        sibling = (x, y, 1 - c)
        cps = []
        for t in range(n):
            rh = ins[t].shape[0]
            cp = _remote(ins[t], outs[t].at[pl.ds(c * rh, rh), :], send_sems.at[t], recv_sems.at[t], sibling)
            cp.start()
            cps.append(cp)
        for t, cp in enumerate(cps):
            rh = ins[t].shape[0]
            theirs = outs[t].at[pl.ds((1 - c) * rh, rh), :]
            _remote(theirs, theirs, send_sems.at[t], recv_sems.at[t], sibling).wait_recv()
            cp.wait_send()

    return pl.pallas_call(
        body, name=name,
        in_specs=[ANY] * n, out_specs=[ANY] * n,
        out_shape=[jax.ShapeDtypeStruct((2 * h.shape[0], h.shape[1]), h.dtype) for h in halves],
        scratch_shapes=[pltpu.SemaphoreType.DMA((n,)), pltpu.SemaphoreType.DMA((n,))],
        compiler_params=pltpu.CompilerParams(has_side_effects=True),
    )(*halves)


def _exchange_small(buf, *, name):
    rows, lanes = buf.shape

    def body(in_ref, out_ref, send_sems, recv_sems, local_sem):
        x, y, c = lax.axis_index("x"), lax.axis_index("y"), lax.axis_index("c")
        me = 4 * x + 2 * y + c
        lc = pltpu.make_async_copy(in_ref, out_ref.at[me], local_sem)
        lc.start()
        cps = []
        for k in range(1, 8):
            fx, fy, fc = (k >> 2) & 1, (k >> 1) & 1, k & 1
            to = (x ^ fx, y ^ fy, c ^ fc)
            cp = _remote(in_ref, out_ref.at[me], send_sems.at[k - 1], recv_sems.at[k - 1], to)
            cp.start()
            cps.append(cp)
        for k in range(1, 8):
            fx, fy, fc = (k >> 2) & 1, (k >> 1) & 1, k & 1
            theirs = out_ref.at[4 * (x ^ fx) + 2 * (y ^ fy) + (c ^ fc)]
            _remote(theirs, theirs, send_sems.at[k - 1], recv_sems.at[k - 1], (x ^ fx, y ^ fy, c ^ fc)).wait_recv()
        for cp in cps:
            cp.wait_send()
        lc.wait()

    return pl.pallas_call(
        body, name=name,
        in_specs=[ANY], out_specs=ANY,
        out_shape=jax.ShapeDtypeStruct((8, rows, lanes), buf.dtype),
        scratch_shapes=[pltpu.SemaphoreType.DMA((7,)), pltpu.SemaphoreType.DMA((7,)), pltpu.SemaphoreType.DMA],
        compiler_params=pltpu.CompilerParams(has_side_effects=True),
    )(buf)


def _sum_slots(slots, *, name):
    n, rows, lanes = slots.shape

    def body(s_ref, o_ref):
        acc = s_ref[0]
        for t in range(1, n):
            acc = acc + s_ref[t]
        o_ref[...] = acc

    return pl.pallas_call(
        body, name=name,
        out_shape=jax.ShapeDtypeStruct((rows, lanes), slots.dtype),
    )(slots)


def _relu2(acc):
    r = jnp.maximum(acc, 0.0)
    return acc, r * r


def _with_side(result, side):
    return result if side else (result, ())


def _layer_fwd(x, w, cos, sin, tag, sides=None):
    sides = sides or {}
    got = {}
    h = _rmsnorm_fwd(x, w["norm1_g"], name=f"norm1_{tag}")
    proj, got["proj"] = _with_side(_matmul(h, w["w_in"], mode="nn", b_sharded=True, tm=512, side=sides.get("proj"),
                                           name=f"proj_{tag}"), sides.get("proj"))
    qkv, res1, res2 = _qkv_prep(proj, w["q_norm_g"], w["k_norm_g"], cos, sin, name=f"qkv_prep_{tag}")
    dil_src = [(qkv, (0, 3, 6))] + [(t.reshape(-1, 3 * GROUP_WIDTH), (0, 1, 2)) for t in (res1, res2)]
    outs, lds = [], []
    for g, (src, cols) in enumerate(dil_src):
        o_g, ld_g = _dil_attn_fwd(src, g, cols, name=f"dil{g}_fwd_{tag}")
        outs.append(o_g)
        lds.append(ld_g)
    yb, y, lse = _dil_merge(outs, lds, name=f"dil_merge_{tag}")
    o_sb, rc, trips = _sb_attn_fwd(qkv, name=f"sb_fwd_{tag}")
    up_dil = _matmul(yb, w["w_up_dil"], mode="nn", b_sharded=True, name=f"up_dil_{tag}")
    up_sb = _matmul(o_sb, w["w_up_sb"], mode="nn", b_sharded=True, name=f"up_sb_{tag}")
    mixed = _gate_fwd(proj, w["gate_b"], up_dil, up_sb, name=f"gate_{tag}")
    x1 = _matmul(mixed, w["w_out"], mode="nn", extras=(x,), epilogue=lambda acc, res: (acc + res,), name=f"out_{tag}")
    h2 = _rmsnorm_fwd(x1, w["norm2_g"], name=f"norm2_{tag}")
    (u, a), got["ff1"] = _with_side(_matmul(h2, w["w_ff1"], mode="nn", b_sharded=True, out_dtypes=(F32, BF16),
                                            epilogue=_relu2, side=sides.get("ff1"), name=f"ff1_{tag}"), sides.get("ff1"))
    x2, got["ff2"] = _with_side(_matmul(a, w["w_ff2"], mode="nn", extras=(x1,), epilogue=lambda acc, res: (acc + res,),
                                        side=sides.get("ff2"), name=f"ff2_{tag}"), sides.get("ff2"))
    saved = dict(x=x, h=h, proj=proj, qkv=qkv, dil_src=dil_src, y=y, yb=yb, lse=lse, o_sb=o_sb, rc=rc, trips=trips, up_dil=up_dil, up_sb=up_sb,
                 mixed=mixed, x1=x1, h2=h2, u=u, a=a)
    return x2, saved, got


def _layer_bwd(dx, dxb, w, sv, cos, sin, tag, sides=None):
    sides = sides or {}
    grads, got = {}, {}
    du, got["d_a"] = _with_side(_matmul(dxb, w["w_ff2"], mode="nt", extras=(sv["u"],), out_dtypes=(BF16,),
                                        epilogue=lambda acc, u: (acc * (2.0 * jnp.maximum(u, 0.0)),),
                                        side=sides.get("d_a"), name=f"d_a_{tag}"), sides.get("d_a"))
    grads["w_ff2"], got["dw_ff2"] = _with_side(_matmul(sv["a"], dxb, mode="tn", side=sides.get("dw_ff2"),
                                                       name=f"dw_ff2_{tag}"), sides.get("dw_ff2"))
    dh2, got["d_h2"] = _with_side(_matmul(du, w["w_ff1"], mode="nt", b_sharded=True, side=sides.get("d_h2"),
                                          name=f"d_h2_{tag}"), sides.get("d_h2"))
    grads["w_ff1"] = _matmul(sv["h2"], du, mode="tn", out_sharded=True, name=f"dw_ff1_{tag}")
    dx1, dx1b, grads["norm2_g"] = _rmsnorm_bwd(sv["x1"], w["norm2_g"], dh2, dx, name=f"norm2_bwd_{tag}")
    dmixed = _matmul(dx1b, w["w_out"], mode="nt", name=f"d_mixed_{tag}")
    grads["w_out"] = _matmul(sv["mixed"], dx1b, mode="tn", name=f"dw_out_{tag}")
    dud, dus, dgate, grads["gate_b"] = _gate_bwd(sv["proj"], w["gate_b"], sv["up_dil"], sv["up_sb"], dmixed,
                                                 name=f"gate_bwd_{tag}")
    dy_dil = _matmul(dud, w["w_up_dil"], mode="nt", b_sharded=True, name=f"d_ydil_{tag}")
    grads["w_up_dil"] = _matmul(sv["yb"], dud, mode="tn", out_sharded=True, name=f"dw_up_dil_{tag}")
    dy_sb = _matmul(dus, w["w_up_sb"], mode="nt", b_sharded=True, name=f"d_ysb_{tag}")
    grads["w_up_sb"] = _matmul(sv["o_sb"], dus, mode="tn", out_sharded=True, name=f"dw_up_sb_{tag}")
    dys = [dy_dil] + _dil_to_residues(dy_dil, name=f"dil_dy_{tag}")
    d_dil = [_dil_attn_bwd(src, g, cols, sv["y"][g], sv["lse"][g], dys[g], name=f"dil{g}_bwd_{tag}")
             for g, (src, cols) in enumerate(sv["dil_src"])]
    for g in (1, 2):
        d_dil[g] = [t.reshape(DIL_GROUPS[g][1], -1, GROUP_WIDTH) for t in d_dil[g]]
    d_sb = _sb_attn_bwd(sv["qkv"], sv["rc"], sv["trips"], dy_sb, name=f"sb_bwd_{tag}")
    dproj, grads["q_norm_g"], grads["k_norm_g"] = _qkv_prep_bwd(
        sv["proj"], w["q_norm_g"], w["k_norm_g"], cos, sin, d_dil, d_sb, dgate, name=f"qkv_prep_bwd_{tag}")
    dh = _matmul(dproj, w["w_in"], mode="nt", b_sharded=True, tm=512, tn=2048, name=f"d_h_{tag}")
    grads["w_in"] = _matmul(sv["h"], dproj, mode="tn", out_sharded=True, tm=512, tk=1024, name=f"dw_in_{tag}")
    dx0, dx0b, grads["norm1_g"] = _rmsnorm_bwd(sv["x"], w["norm1_g"], dh, dx1, name=f"norm1_bwd_{tag}")
    return dx0, dx0b, grads, got


BIG = ("w_in", "w_up_dil", "w_up_sb", "w_out", "w_ff1", "w_ff2")
ROW_SHARDED = ("w_out", "w_ff2")
WEIGHTS = ("norm1_g", "w_in", "q_norm_g", "k_norm_g", "w_up_dil", "w_up_sb", "gate_b", "w_out", "norm2_g", "w_ff1", "w_ff2")


def _reduce_begin(grads, which, tag):
    names = list(grads)
    full = [grads[n] for n in names]
    theirs = _pair_swap_halves(full, name=f"rs_pair_{tag}")
    return {n: _add_half(f, t, which, name=f"rs_pair_sum_{n}_{tag}") for n, f, t in zip(names, full, theirs)}


def _reduce_end(pair, parts, which, chip, tag):
    names = list(pair)
    halves = [_sum_chips(pair[n], parts[n], chip, name=f"rs_chip_sum_{n}_{tag}") for n in names]
    joined = _pair_join_halves(halves, name=f"rs_join_{tag}")
    out = {}
    for n, j, h in zip(names, joined, halves):
        r, c = j.shape
        out[n] = _place_piece(j.reshape(1, 2, r // 2, c), h[None], which, name=f"rs_place_{n}_{tag}").reshape(r, c)
    return out


def _pack_rows(vecs):
    rows, spans, at = [], [], 0
    for v in vecs:
        r = v.size // 128
        padded = -(-r // 8) * 8
        rows.append(jnp.pad(v.reshape(r, 128), ((0, padded - r), (0, 0))))
        spans.append((at, r))
        at += padded
    return jnp.concatenate(rows, axis=0), spans


def kernel(x, norm1_g, w_in, q_norm_g, k_norm_g, w_up_dil, w_up_sb, gate_b, w_out, norm2_g, w_ff1, w_ff2, loss_target, m_norm1_g, m_w_in, m_q_norm_g, m_k_norm_g, m_w_up_dil, m_w_up_sb, m_gate_b, m_w_out, m_norm2_g, m_w_ff1, m_w_ff2, v_norm1_g, v_w_in, v_q_norm_g, v_k_norm_g, v_w_up_dil, v_w_up_sb, v_gate_b, v_w_out, v_norm2_g, v_w_ff1, v_w_ff2):
    weights = dict(norm1_g=norm1_g, w_in=w_in, q_norm_g=q_norm_g, k_norm_g=k_norm_g, w_up_dil=w_up_dil,
                   w_up_sb=w_up_sb, gate_b=gate_b, w_out=w_out, norm2_g=norm2_g, w_ff1=w_ff1, w_ff2=w_ff2)
    moments_m = dict(norm1_g=m_norm1_g, w_in=m_w_in, q_norm_g=m_q_norm_g, k_norm_g=m_k_norm_g, w_up_dil=m_w_up_dil,
                     w_up_sb=m_w_up_sb, gate_b=m_gate_b, w_out=m_w_out, norm2_g=m_norm2_g, w_ff1=m_w_ff1, w_ff2=m_w_ff2)
    moments_v = dict(norm1_g=v_norm1_g, w_in=v_w_in, q_norm_g=v_q_norm_g, k_norm_g=v_k_norm_g, w_up_dil=v_w_up_dil,
                     w_up_sb=v_w_up_sb, gate_b=v_gate_b, w_out=v_w_out, norm2_g=v_norm2_g, w_ff1=v_w_ff1, w_ff2=v_w_ff2)
    depth = w_in.shape[0]
    seq, d_model = x.shape[1], x.shape[2]
    chip = 2 * lax.axis_index("x") + lax.axis_index("y")
    core = lax.axis_index("c")
    which = jnp.reshape(core, (1,)).astype(jnp.int32)

    chip_index = jnp.reshape(chip, (1,)).astype(jnp.int32)
    shards = {n: weights[n].astype(BF16) for n in BIG}
    bias_buf, ((_, bias_rows),) = _pack_rows([gate_b])
    bias_slots = _exchange_small(bias_buf, name="gather_gate_b")
    bias = bias_slots[0::2, :bias_rows].reshape(N_CHIPS, depth, 2, d_model // N_CHIPS)
    bias = jnp.transpose(bias, (1, 2, 0, 3)).reshape(depth, 2, d_model)

    def layer_weights(l, landed):
        w = {}
        for n in BIG:
            w[n] = _place_piece(landed[n][None], shards[n][l:l + 1], chip_index, name=f"gather_place_{n}_l{l}")[0]
        for n in ROW_SHARDED:
            w[n] = w[n].reshape(-1, w[n].shape[-1])
        w.update(norm1_g=norm1_g[l], norm2_g=norm2_g[l], q_norm_g=q_norm_g[l], k_norm_g=k_norm_g[l], gate_b=bias[l])
        return w

    fwd_riders = {"proj": ("w_in", "w_up_dil", "w_up_sb", "w_out"), "ff1": ("w_ff1",), "ff2": ("w_ff2",)}
    bwd_riders = {"d_a": ("w_in",), "dw_ff2": ("w_ff1", "w_out", "w_up_dil", "w_up_sb"), "d_h2": ("w_ff2",)}

    cos, sin = _rope_tables(seq)
    landed = dict(zip(BIG, _gather_layer([shards[n] for n in BIG], 0, name="gather_l0")))
    act = x[0]
    layers, saved = [], []
    for l in range(depth):
        w = layer_weights(l, landed)
        sides = None
        if l + 1 < depth:
            sides = {k: _gather_ici_side([shards[n] for n in names], l + 1) for k, names in fwd_riders.items()}
        act, sv, got = _layer_fwd(act, w, cos, sin, f"l{l}", sides)
        if l + 1 < depth:
            arrived = {n: t for k, names in fwd_riders.items() for n, t in zip(names, got[k])}
            landed = dict(zip(BIG, _gather_pass([arrived[n] for n in BIG], name=f"gather_pass_l{l + 1}")))
        layers.append(w)
        saved.append(sv)
    loss, dx, dxb = _loss_head(act, loss_target[0])
    loss = lax.psum(loss, ("x", "y", "c"))

    grads, reduced = [None] * depth, [None] * depth
    pair = None
    for l in reversed(range(depth)):
        sides = {k: _chip_side([pair[n] for n in names]) for k, names in bwd_riders.items()} if pair else None
        dx, dxb, grads[l], got = _layer_bwd(dx, dxb, layers[l], saved[l], cos, sin, f"l{l}", sides)
        if pair:
            parts = {n: t for k, names in bwd_riders.items() for n, t in zip(names, got[k])}
            reduced[l + 1] = _reduce_end(pair, parts, which, chip_index, f"l{l + 1}")
        g = {n: grads[l][n] for n in BIG}
        for n in ROW_SHARDED:
            g[n] = g[n].reshape(N_CHIPS, -1, g[n].shape[-1])
        pair = _reduce_begin(g, which, f"l{l}")
    parts = dict(zip(BIG, _chip_exchange([pair[n] for n in BIG], name="rs_chips_l0")))
    reduced[0] = _reduce_end(pair, parts, which, chip_index, "l0")
    final = {n: jnp.stack([reduced[l][n] for l in range(depth)]) for n in BIG}

    small_names = ("norm1_g", "norm2_g", "q_norm_g", "k_norm_g", "gate_b")
    packed, spans = _pack_rows([grads[l][n] for l in range(depth) for n in small_names])
    total = _sum_slots(_exchange_small(packed, name="exchange_small_grads"), name="sum_small_grads")
    pieces = iter(total[at:at + r] for at, r in spans)
    small = {n: [] for n in small_names}
    for l in range(depth):
        for n in small_names:
            small[n].append(next(pieces))
    for n in ("norm1_g", "norm2_g"):
        final[n] = jnp.stack([p.reshape(d_model) for p in small[n]])
    for n in ("q_norm_g", "k_norm_g"):
        final[n] = jnp.stack([p.reshape(N_GROUPS, HEAD_DIM) for p in small[n]])
    shard_cols = d_model // N_CHIPS
    final["gate_b"] = jnp.stack([lax.dynamic_slice_in_dim(p.reshape(2, d_model), chip * shard_cols, shard_cols, axis=1)
                                 for p in small["gate_b"]])

    deltas, new_m, new_v = {}, {}, {}
    for n in WEIGHTS:
        deltas[n], new_m[n], new_v[n] = _adamw(weights[n], final[n], moments_m[n], moments_v[n], name=f"adamw_{n}")
    return (loss, dx[None], *[final[n] for n in WEIGHTS], *[deltas[n] for n in WEIGHTS],
            *[new_m[n] for n in WEIGHTS], *[new_v[n] for n in WEIGHTS])
```

```python
import functools
import math

import jax
import jax.numpy as jnp
from jax import lax
from jax.experimental import pallas as pl
from jax.experimental.pallas import tpu as pltpu

F32 = jnp.float32
BF16 = jnp.bfloat16

HEAD_DIM = 128
DIL_GROUPS = ((128, 1), (512, 4), (2048, 16))
N_GROUPS = 3
HEADS_PER_GROUP = 4
GROUP_WIDTH = HEADS_PER_GROUP * HEAD_DIM
DIL_WIDTH = N_GROUPS * GROUP_WIDTH
SB_HEADS = 8
SB_WIDTH = SB_HEADS * HEAD_DIM
QKV_WIDTH = 3 * DIL_WIDTH + 3 * SB_WIDTH
BLOCK = 128
ROPE_THETA = 10000.0
EPS = 1e-6
SCALE = 1.0 / math.sqrt(HEAD_DIM)
NEG_INF = float("-inf")

ADAM_LR = 0.001
ADAM_B1 = 0.9
ADAM_B2 = 0.999
ADAM_EPS = 1e-08
ADAM_WD = 0.01
ADAM_STEP = 10

N_CHIPS = 4
MESH = pl.DeviceIdType.MESH
MIB = 1024 * 1024
VMEM_LIMIT = 56 * MIB


def _params(semantics=None, vmem=VMEM_LIMIT):
    return pltpu.CompilerParams(dimension_semantics=semantics, vmem_limit_bytes=vmem)


def _pick(n, pref):
    if n <= pref:
        return n
    t = 1 << (pref.bit_length() - 1)
    while n % t:
        t //= 2
    return t


def _tile(n, pref):
    best = 0
    for t in range(256, min(n, pref) + 1, 256):
        if n % t == 0:
            best = t
    return best if best else n


def _matmul(a, b, *, mode, name, out_dtypes=(F32,), extras=(), epilogue=None, b_sharded=False, out_sharded=False,
            tm=1024, tn=1024, tk=2048, side=None):
    if mode == "nn":
        m, kdim = a.shape
        n = b.shape[-1] * (N_CHIPS if b_sharded else 1)
        assert (b.shape[-2] if b_sharded else b.shape[0]) == kdim
    elif mode == "nt":
        m, kdim = a.shape
        n = b.shape[-2]
        assert b.shape[-1] * (N_CHIPS if b_sharded else 1) == kdim
    else:
        kdim, m = a.shape
        n = b.shape[1]
        assert b.shape[0] == kdim and not b_sharded
    shard_n = n // N_CHIPS if (out_sharded or (b_sharded and mode == "nn")) else n
    shard_k = kdim // N_CHIPS if (b_sharded and mode == "nt") else kdim
    tm = _tile(m, tm)
    tn = _tile(shard_n, tn)
    tk = _tile(shard_k, tk)
    nk = kdim // tk
    nj_n = shard_n // tn
    nj_k = shard_k // tk
    j_outer = nk == 1 and (kdim * n + m * kdim * (n // tn)) < (m * kdim + kdim * n * (m // tm))
    grid = (n // tn, m // tm, nk) if j_outer else (m // tm, n // tn, nk)

    def at(index):
        return (lambda g0, g1, k: index(g1, g0, k)) if j_outer else index

    if mode == "nn":
        a_spec = pl.BlockSpec((tm, tk), at(lambda i, j, k: (i, k)))
        if b_sharded:
            b_spec = pl.BlockSpec((None, tk, tn), at(lambda i, j, k: (j // nj_n, k, j % nj_n)))
        else:
            b_spec = pl.BlockSpec((tk, tn), at(lambda i, j, k: (k, j)))
        dims = (((1,), (0,)), ((), ()))
    elif mode == "nt":
        a_spec = pl.BlockSpec((tm, tk), at(lambda i, j, k: (i, k)))
        if b_sharded:
            b_spec = pl.BlockSpec((None, tn, tk), at(lambda i, j, k: (k // nj_k, j, k % nj_k)))
        else:
            b_spec = pl.BlockSpec((tn, tk), at(lambda i, j, k: (j, k)))
        dims = (((1,), (1,)), ((), ()))
    else:
        a_spec = pl.BlockSpec((tk, tm), at(lambda i, j, k: (k, i)))
        b_spec = pl.BlockSpec((tk, tn), at(lambda i, j, k: (k, j)))
        dims = (((0,), (0,)), ((), ()))
    if out_sharded:
        o_spec = pl.BlockSpec((None, tm, tn), at(lambda i, j, k: (j // nj_n, i, j % nj_n)))
        o_shape = (N_CHIPS, m, shard_n)
    else:
        o_spec = pl.BlockSpec((tm, tn), at(lambda i, j, k: (i, j)))
        o_shape = (m, n)
    x_spec = pl.BlockSpec((tm, tn), at(lambda i, j, k: (i, j)))
    n_extra = len(extras)
    n_out = len(out_dtypes)

    side_ins, side_shapes, side_copies, n_copies = side if side else ((), (), None, 0)
    n_side_in, n_side_out = len(side_ins), len(side_shapes)

    def finish(acc, extra_refs, out_refs):
        res = (acc,) if epilogue is None else epilogue(acc, *[e[...] for e in extra_refs])
        for o, r in zip(out_refs, res):
            o[...] = r.astype(o.dtype)

    def body(a_ref, b_ref, *rest):
        extra_refs = rest[:n_extra]
        side_in_refs = rest[n_extra:n_extra + n_side_in]
        rest = rest[n_extra + n_side_in:]
        out_refs, side_out_refs, scratch = rest[:n_out], rest[n_out:n_out + n_side_out], rest[n_out + n_side_out:]
        if side:
            step = (pl.program_id(0) * grid[1] + pl.program_id(1)) * grid[2] + pl.program_id(2)
            send_sems, recv_sems = scratch[-2:]

            @pl.when(step == 0)
            def _():
                sends, _ = side_copies(side_in_refs, side_out_refs, send_sems, recv_sems)
                for cp in sends:
                    cp.start()

        if nk == 1:
            finish(lax.dot_general(a_ref[...], b_ref[...], dims, preferred_element_type=F32), extra_refs, out_refs)
        else:
            acc = scratch[0]
            k = pl.program_id(2)

            @pl.when(k == 0)
            def _():
                acc[...] = jnp.zeros_like(acc)

            acc[...] += lax.dot_general(a_ref[...], b_ref[...], dims, preferred_element_type=F32)

            @pl.when(k == nk - 1)
            def _():
                finish(acc[...], extra_refs, out_refs)

        if side:
            @pl.when(step == grid[0] * grid[1] * grid[2] - 1)
            def _():
                sends, recvs = side_copies(side_in_refs, side_out_refs, send_sems, recv_sems)
                for cp in recvs:
                    cp.wait_recv()
                for cp in sends:
                    cp.wait_send()

    scratch_shapes = [] if nk == 1 else [pltpu.VMEM((tm, tn), F32)]
    if side:
        scratch_shapes += [pltpu.SemaphoreType.DMA((n_copies,)), pltpu.SemaphoreType.DMA((n_copies,))]
    outs = pl.pallas_call(
        body,
        name=name,
        grid=grid,
        in_specs=[a_spec, b_spec] + [x_spec] * n_extra + [ANY] * n_side_in,
        out_specs=[o_spec] * n_out + [ANY] * n_side_out,
        out_shape=[jax.ShapeDtypeStruct(o_shape, dt) for dt in out_dtypes] + list(side_shapes),
        scratch_shapes=scratch_shapes,
        compiler_params=_params(("arbitrary",) * 3 if side else ("parallel", "parallel", "arbitrary")),
    )(a, b, *extras, *side_ins)
    if side:
        return (outs[0] if n_out == 1 else outs[:n_out]), outs[n_out:]
    return outs[0] if n_out == 1 else outs


def _rmsnorm_fwd(x, g, *, name):
    s, d = x.shape
    tr = _pick(s, 256)

    def body(x_ref, g_ref, o_ref):
        xv = x_ref[...]
        r = lax.rsqrt(jnp.mean(xv * xv, axis=-1, keepdims=True) + EPS)
        o_ref[...] = (xv * r * g_ref[...]).astype(o_ref.dtype)

    return pl.pallas_call(
        body, name=name, grid=(s // tr,),
        in_specs=[pl.BlockSpec((tr, d), lambda i: (i, 0)), pl.BlockSpec((1, d), lambda i: (0, 0))],
        out_specs=pl.BlockSpec((tr, d), lambda i: (i, 0)),
        out_shape=jax.ShapeDtypeStruct((s, d), BF16),
        compiler_params=_params(("parallel",)),
    )(x, g.reshape(1, d))


def _rmsnorm_bwd(x, g, dh, dres, *, name):
    s, d = x.shape
    tr = _pick(s, 256)

    def body(x_ref, g_ref, dh_ref, dres_ref, dx_ref, dxb_ref, dg_ref):
        xv = x_ref[...]
        r = lax.rsqrt(jnp.mean(xv * xv, axis=-1, keepdims=True) + EPS)
        y = xv * r
        dhv = dh_ref[...]
        dy = dhv * g_ref[...]
        dx = dres_ref[...] + r * (dy - y * jnp.mean(dy * y, axis=-1, keepdims=True))
        dx_ref[...] = dx
        dxb_ref[...] = dx.astype(BF16)

        @pl.when(pl.program_id(0) == 0)
        def _():
            dg_ref[...] = jnp.zeros_like(dg_ref)

        dg_ref[...] += jnp.sum(dhv * y, axis=0, keepdims=True)

    row = pl.BlockSpec((tr, d), lambda i: (i, 0))
    vec = pl.BlockSpec((1, d), lambda i: (0, 0))
    dx, dxb, dg = pl.pallas_call(
        body, name=name, grid=(s // tr,),
        in_specs=[row, vec, row, row],
        out_specs=[row, row, vec],
        out_shape=[jax.ShapeDtypeStruct((s, d), F32), jax.ShapeDtypeStruct((s, d), BF16),
                   jax.ShapeDtypeStruct((1, d), F32)],
        compiler_params=_params(("arbitrary",)),
    )(x, g.reshape(1, d), dh, dres)
    return dx, dxb, dg.reshape(d)


def _loss_head(y, target):
    s, d = y.shape
    tr = _pick(s, 256)

    def body(y_ref, t_ref, dy_ref, dyb_ref, part_ref):
        err = y_ref[...] - t_ref[...]
        dy = err * (1.0 / d)
        dy_ref[...] = dy
        dyb_ref[...] = dy.astype(BF16)

        @pl.when(pl.program_id(0) == 0)
        def _():
            part_ref[...] = jnp.zeros_like(part_ref)

        part_ref[...] += jnp.sum(err * err, axis=0, keepdims=True)

    row = pl.BlockSpec((tr, d), lambda i: (i, 0))
    vec = pl.BlockSpec((1, d), lambda i: (0, 0))
    dy, dyb, part = pl.pallas_call(
        body, name="loss_head", grid=(s // tr,),
        in_specs=[row, row], out_specs=[row, row, vec],
        out_shape=[jax.ShapeDtypeStruct((s, d), F32), jax.ShapeDtypeStruct((s, d), BF16),
                   jax.ShapeDtypeStruct((1, d), F32)],
        compiler_params=_params(("arbitrary",)),
    )(y, target)
    return 0.5 * jnp.sum(part) / d, dy, dyb


def _rope_tables(s):
    half = HEAD_DIM // 2
    inv_freq = ROPE_THETA ** (-jnp.arange(half, dtype=F32) / half)
    ang = jnp.arange(s, dtype=F32)[:, None] * inv_freq[None, :]
    cos, sin = jnp.cos(ang), jnp.sin(ang)
    return jnp.concatenate([cos, cos], axis=-1), jnp.concatenate([-sin, sin], axis=-1)


PREP_ROWS = 256


def _residue_spec(r, width):
    return pl.BlockSpec((r, PREP_ROWS // r, width), lambda i: (0, i, 0))


def _to_residues(scr, tile, r, dst_ref, cols, dtype):
    scr[...] = tile
    for c in range(r):
        dst_ref[c, :, cols] = scr[pl.ds(c, PREP_ROWS // r, stride=r), :].astype(dtype)


def _from_residues(scr, src_ref, cols, r):
    for c in range(r):
        scr[pl.ds(c, PREP_ROWS // r, stride=r), :] = src_ref[c, :, cols].astype(F32)
    return scr[...]


def _qkv_prep(proj, qg, kg, cos, sin, *, name):
    s = proj.shape[0]
    tr = PREP_ROWS
    assert s % tr == 0

    def body(p_ref, qg_ref, kg_ref, cos_ref, sin_ref, o_ref, res1_ref, res2_ref, scr):
        res_refs = (None, res1_ref, res2_ref)
        c = cos_ref[...]
        sn = sin_ref[...]
        for which, g_ref in ((0, qg_ref), (1, kg_ref)):
            for g in range(N_GROUPS):
                gain = g_ref[g:g + 1, :]
                for h in range(HEADS_PER_GROUP):
                    off = which * DIL_WIDTH + g * GROUP_WIDTH + h * HEAD_DIM
                    xv = p_ref[:, off:off + HEAD_DIM]
                    r = lax.rsqrt(jnp.mean(xv * xv, axis=-1, keepdims=True) + EPS)
                    y = xv * r * gain
                    rot = y * c + pltpu.roll(y, HEAD_DIM // 2, 1) * sn
                    if g == 0:
                        o_ref[:, off:off + HEAD_DIM] = rot.astype(BF16)
                    else:
                        at = which * GROUP_WIDTH + h * HEAD_DIM
                        _to_residues(scr, rot, DIL_GROUPS[g][1], res_refs[g], slice(at, at + HEAD_DIM), BF16)
        o_ref[:, 2 * DIL_WIDTH:2 * DIL_WIDTH + GROUP_WIDTH] = p_ref[:, 2 * DIL_WIDTH:2 * DIL_WIDTH + GROUP_WIDTH].astype(BF16)
        for g in range(1, N_GROUPS):
            for h in range(HEADS_PER_GROUP):
                off = 2 * DIL_WIDTH + g * GROUP_WIDTH + h * HEAD_DIM
                at = 2 * GROUP_WIDTH + h * HEAD_DIM
                _to_residues(scr, p_ref[:, off:off + HEAD_DIM], DIL_GROUPS[g][1], res_refs[g], slice(at, at + HEAD_DIM), BF16)
        o_ref[:, 3 * DIL_WIDTH:] = p_ref[:, 3 * DIL_WIDTH:].astype(BF16)

    r1, r2 = DIL_GROUPS[1][1], DIL_GROUPS[2][1]
    return pl.pallas_call(
        body, name=name, grid=(s // tr,),
        in_specs=[pl.BlockSpec((tr, QKV_WIDTH), lambda i: (i, 0)),
                  pl.BlockSpec((N_GROUPS, HEAD_DIM), lambda i: (0, 0)),
                  pl.BlockSpec((N_GROUPS, HEAD_DIM), lambda i: (0, 0)),
                  pl.BlockSpec((tr, HEAD_DIM), lambda i: (i, 0)),
                  pl.BlockSpec((tr, HEAD_DIM), lambda i: (i, 0))],
        out_specs=[pl.BlockSpec((tr, QKV_WIDTH), lambda i: (i, 0)),
                   _residue_spec(r1, 3 * GROUP_WIDTH), _residue_spec(r2, 3 * GROUP_WIDTH)],
        out_shape=[jax.ShapeDtypeStruct((s, QKV_WIDTH), BF16),
                   jax.ShapeDtypeStruct((r1, s // r1, 3 * GROUP_WIDTH), BF16),
                   jax.ShapeDtypeStruct((r2, s // r2, 3 * GROUP_WIDTH), BF16)],
        scratch_shapes=[pltpu.VMEM((tr, HEAD_DIM), F32)],
        compiler_params=_params(("parallel",)),
    )(proj, qg, kg, cos, sin)


def _qkv_prep_bwd(proj, qg, kg, cos, sin, dqkv_dil, dqkv_sb, dgate, *, name):
    s, n_in = proj.shape
    n_gate = dgate.shape[1]
    tr = PREP_ROWS

    def body(p_ref, qg_ref, kg_ref, cos_ref, sin_ref, *rest):
        dil_refs = rest[:9]
        sb_refs = rest[9:12]
        dgate_ref = rest[12]
        o_ref, dqg_ref, dkg_ref, scr = rest[13:17]
        c = cos_ref[...]
        sn = sin_ref[...]

        def incoming(g, which, h):
            cols = slice(h * HEAD_DIM, (h + 1) * HEAD_DIM)
            if g == 0:
                return dil_refs[which][:, cols]
            return _from_residues(scr, dil_refs[3 * g + which], cols, DIL_GROUPS[g][1])

        @pl.when(pl.program_id(0) == 0)
        def _():
            dqg_ref[...] = jnp.zeros_like(dqg_ref)
            dkg_ref[...] = jnp.zeros_like(dkg_ref)

        for which, g_ref, dg_ref in ((0, qg_ref, dqg_ref), (1, kg_ref, dkg_ref)):
            for g in range(N_GROUPS):
                gain = g_ref[g:g + 1, :]
                dgain = jnp.zeros((1, HEAD_DIM), F32)
                for h in range(HEADS_PER_GROUP):
                    off = which * DIL_WIDTH + g * GROUP_WIDTH + h * HEAD_DIM
                    xv = p_ref[:, off:off + HEAD_DIM]
                    r = lax.rsqrt(jnp.mean(xv * xv, axis=-1, keepdims=True) + EPS)
                    nx = xv * r
                    dout = incoming(g, which, h)
                    dy = dout * c + pltpu.roll(dout * sn, HEAD_DIM // 2, 1)
                    dgain = dgain + jnp.sum(dy * nx, axis=0, keepdims=True)
                    dn = dy * gain
                    dxv = r * (dn - nx * jnp.mean(dn * nx, axis=-1, keepdims=True))
                    o_ref[:, off:off + HEAD_DIM] = dxv.astype(BF16)
                dg_ref[g:g + 1, :] += dgain
        for g in range(N_GROUPS):
            for h in range(HEADS_PER_GROUP):
                off = 2 * DIL_WIDTH + g * GROUP_WIDTH + h * HEAD_DIM
                o_ref[:, off:off + HEAD_DIM] = incoming(g, 2, h).astype(BF16)
        for t in range(3):
            off = 3 * DIL_WIDTH + t * SB_WIDTH
            o_ref[:, off:off + SB_WIDTH] = sb_refs[t][...].astype(BF16)
        o_ref[:, QKV_WIDTH:] = dgate_ref[...]

    grp = pl.BlockSpec((tr, GROUP_WIDTH), lambda i: (i, 0))
    sbs = pl.BlockSpec((tr, SB_WIDTH), lambda i: (i, 0))
    gain_spec = pl.BlockSpec((N_GROUPS, HEAD_DIM), lambda i: (0, 0))
    tab = pl.BlockSpec((tr, HEAD_DIM), lambda i: (i, 0))
    flat_dil = [t for grp3 in dqkv_dil for t in grp3]
    dil_specs = [grp] * 3 + [_residue_spec(DIL_GROUPS[g][1], GROUP_WIDTH) for g in (1, 2) for _ in range(3)]
    return pl.pallas_call(
        body, name=name, grid=(s // tr,),
        in_specs=[pl.BlockSpec((tr, 2 * DIL_WIDTH), lambda i: (i, 0)), gain_spec, gain_spec, tab, tab]
                 + dil_specs + [sbs] * 3 + [pl.BlockSpec((tr, n_gate), lambda i: (i, 0))],
        out_specs=[pl.BlockSpec((tr, n_in), lambda i: (i, 0)), gain_spec, gain_spec],
        out_shape=[jax.ShapeDtypeStruct((s, n_in), BF16),
                   jax.ShapeDtypeStruct((N_GROUPS, HEAD_DIM), F32),
                   jax.ShapeDtypeStruct((N_GROUPS, HEAD_DIM), F32)],
        scratch_shapes=[pltpu.VMEM((tr, HEAD_DIM), F32)],
        compiler_params=_params(("arbitrary",)),
    )(proj, qg, kg, cos, sin, *flat_dil, *dqkv_sb, dgate)


def _dil_masks(i):
    row = lax.broadcasted_iota(jnp.int32, (BLOCK, BLOCK), 0)
    col = lax.broadcasted_iota(jnp.int32, (BLOCK, BLOCK), 1)
    return col <= row, (col >= row) & (i > 0)


def _dot_nt(a, b):
    return lax.dot_general(a, b, (((1,), (1,)), ((), ())), preferred_element_type=F32)


def _dot_tn(a, b):
    return lax.dot_general(a, b, (((0,), (0,)), ((), ())), preferred_element_type=F32)


def _dot(a, b):
    return jnp.dot(a, b, preferred_element_type=F32)


def _dil_attn_fwd(src, g, cols, *, name):
    s = src.shape[0]
    r = DIL_GROUPS[g][1]
    nb = s // r // BLOCK

    def body(q_ref, kp_ref, kc_ref, vp_ref, vc_ref, o_ref, ld_ref):
        i = pl.program_id(1)
        mask_c, mask_p = _dil_masks(i)
        sls = [slice(h * HEAD_DIM, (h + 1) * HEAD_DIM) for h in range(HEADS_PER_GROUP)]
        scores = [(_dot_nt(q_ref[:, sl], kc_ref[:, sl]), _dot_nt(q_ref[:, sl], kp_ref[:, sl])) for sl in sls]
        probs = []
        for sl, (raw_c, raw_p) in zip(sls, scores):
            s_c = jnp.where(mask_c, raw_c * SCALE, NEG_INF)
            s_p = jnp.where(mask_p, raw_p * SCALE, NEG_INF)
            m = jnp.maximum(jnp.max(s_c, axis=-1, keepdims=True), jnp.max(s_p, axis=-1, keepdims=True))
            p_c = jnp.exp(s_c - m)
            p_p = jnp.exp(s_p - m)
            l = jnp.sum(p_c, axis=-1, keepdims=True) + jnp.sum(p_p, axis=-1, keepdims=True)
            inv = 1.0 / l
            probs.append(((p_c * inv).astype(BF16), (p_p * inv).astype(BF16)))
            ld_ref[:, sl] = jnp.broadcast_to(m + jnp.log(l), (BLOCK, HEAD_DIM))
        for sl, (pn_c, pn_p) in zip(sls, probs):
            o_ref[:, sl] = _dot(pn_c, vc_ref[:, sl]) + _dot(pn_p, vp_ref[:, sl])

    def col(which):
        return lambda c, i: (c * nb + i, cols[which])

    def col_prev(which):
        return lambda c, i: (c * nb + jnp.maximum(i - 1, 0), cols[which])

    blk = (BLOCK, GROUP_WIDTH)
    return pl.pallas_call(
        body, name=name, grid=(r, nb),
        in_specs=[pl.BlockSpec(blk, col(0)), pl.BlockSpec(blk, col_prev(1)), pl.BlockSpec(blk, col(1)),
                  pl.BlockSpec(blk, col_prev(2)), pl.BlockSpec(blk, col(2))],
        out_specs=[pl.BlockSpec(blk, lambda c, i: (c * nb + i, 0))] * 2,
        out_shape=[jax.ShapeDtypeStruct((s, GROUP_WIDTH), F32)] * 2,
        compiler_params=_params(("parallel", "parallel")),
    )(src, src, src, src, src)


def _dil_merge(outs, lds, *, name):
    s = outs[0].shape[0]
    tr = PREP_ROWS
    rs = [DIL_GROUPS[g][1] for g in range(N_GROUPS)]

    def body(o0, o1, o2, l0, l1, l2, yb_ref, y0_ref, y1_ref, y2_ref, e0_ref, e1_ref, e2_ref, scr_a, scr_b):
        o_refs, l_refs = (o0, o1, o2), (l0, l1, l2)
        y_refs, e_refs = (y0_ref, y1_ref, y2_ref), (e0_ref, e1_ref, e2_ref)
        for h in range(HEADS_PER_GROUP):
            cols = slice(h * HEAD_DIM, (h + 1) * HEAD_DIM)
            o = [o_refs[0][:, cols]] + [_from_residues(scr_a, o_refs[g], cols, rs[g]) for g in (1, 2)]
            a = [l_refs[0][:, cols]] + [_from_residues(scr_b, l_refs[g], cols, rs[g]) for g in (1, 2)]
            m = jnp.maximum(jnp.maximum(a[0], a[1]), a[2])
            e = [jnp.exp(t - m) for t in a]
            tot = e[0] + e[1] + e[2]
            inv = 1.0 / tot
            y = (e[0] * inv) * o[0] + (e[1] * inv) * o[1] + (e[2] * inv) * o[2]
            lse = m + jnp.log(tot)
            yb_ref[:, cols] = y.astype(BF16)
            y_refs[0][:, cols] = y
            e_refs[0][:, cols] = lse
            for g in (1, 2):
                _to_residues(scr_a, y, rs[g], y_refs[g], cols, F32)
                _to_residues(scr_b, lse, rs[g], e_refs[g], cols, F32)

    blk = pl.BlockSpec((tr, GROUP_WIDTH), lambda i: (i, 0))
    specs = [blk] + [_residue_spec(rs[g], GROUP_WIDTH) for g in (1, 2)]
    shapes = [jax.ShapeDtypeStruct((s, GROUP_WIDTH), F32)] + [
        jax.ShapeDtypeStruct((rs[g], s // rs[g], GROUP_WIDTH), F32) for g in (1, 2)]
    res = pl.pallas_call(
        body, name=name, grid=(s // tr,),
        in_specs=specs * 2, out_specs=[blk] + specs * 2,
        out_shape=[jax.ShapeDtypeStruct((s, GROUP_WIDTH), BF16)] + shapes * 2,
        scratch_shapes=[pltpu.VMEM((tr, HEAD_DIM), F32), pltpu.VMEM((tr, HEAD_DIM), F32)],
        compiler_params=_params(("parallel",)),
    )(*[t.reshape(sh.shape) for t, sh in zip(outs, shapes)], *[t.reshape(sh.shape) for t, sh in zip(lds, shapes)])
    yb, ys, lses = res[0], res[1:4], res[4:7]
    return yb, [t.reshape(s, GROUP_WIDTH) for t in ys], [t.reshape(s, GROUP_WIDTH) for t in lses]


def _dil_to_residues(x, *, name):
    s = x.shape[0]
    tr = PREP_ROWS
    rs = [DIL_GROUPS[g][1] for g in (1, 2)]

    def body(x_ref, a_ref, b_ref, scr):
        for h in range(HEADS_PER_GROUP):
            cols = slice(h * HEAD_DIM, (h + 1) * HEAD_DIM)
            tile = x_ref[:, cols]
            _to_residues(scr, tile, rs[0], a_ref, cols, F32)
            _to_residues(scr, tile, rs[1], b_ref, cols, F32)

    outs = pl.pallas_call(
        body, name=name, grid=(s // tr,),
        in_specs=[pl.BlockSpec((tr, GROUP_WIDTH), lambda i: (i, 0))],
        out_specs=[_residue_spec(r, GROUP_WIDTH) for r in rs],
        out_shape=[jax.ShapeDtypeStruct((r, s // r, GROUP_WIDTH), F32) for r in rs],
        scratch_shapes=[pltpu.VMEM((tr, HEAD_DIM), F32)],
        compiler_params=_params(("parallel",)),
    )(x)
    return [t.reshape(s, GROUP_WIDTH) for t in outs]


def _dil_attn_bwd(src, g, cols, y, lse, dy, *, name):
    s = src.shape[0]
    r = DIL_GROUPS[g][1]
    nb = s // r // BLOCK

    def body(q_ref, qn_ref, kp_ref, kc_ref, vp_ref, vc_ref, y_ref, yn_ref, l_ref, ln_ref, d_ref, dn_ref,
             dq_ref, dk_ref, dv_ref):
        i = pl.program_id(1)
        mask_c, mask_p = _dil_masks(i)
        _, mask_n = _dil_masks(jnp.where(i + 1 < nb, 1, 0))
        sls = [slice(h * HEAD_DIM, (h + 1) * HEAD_DIM) for h in range(HEADS_PER_GROUP)]
        first = []
        for sl in sls:
            q, qn = q_ref[:, sl], qn_ref[:, sl]
            kc, kp, vc, vp = kc_ref[:, sl], kp_ref[:, sl], vc_ref[:, sl], vp_ref[:, sl]
            dyb, dynb = d_ref[:, sl].astype(BF16), dn_ref[:, sl].astype(BF16)
            first.append((_dot_nt(q, kc), _dot_nt(q, kp), _dot_nt(qn, kc),
                          _dot_nt(dyb, vc), _dot_nt(dyb, vp), _dot_nt(dynb, vc)))
        second = []
        for sl, (s_c, s_p, s_n, dp_c, dp_p, dp_n) in zip(sls, first):
            delta = jnp.sum(d_ref[:, sl] * y_ref[:, sl], axis=-1, keepdims=True)
            delta_n = jnp.sum(dn_ref[:, sl] * yn_ref[:, sl], axis=-1, keepdims=True)
            lse_q, lse_n = l_ref[:, sl], ln_ref[:, sl]
            p_c = jnp.where(mask_c, jnp.exp(s_c * SCALE - lse_q), 0.0)
            p_p = jnp.where(mask_p, jnp.exp(s_p * SCALE - lse_q), 0.0)
            p_n = jnp.where(mask_n, jnp.exp(s_n * SCALE - lse_n), 0.0)
            second.append(((p_c * (dp_c - delta) * SCALE).astype(BF16), (p_p * (dp_p - delta) * SCALE).astype(BF16),
                           (p_n * (dp_n - delta_n) * SCALE).astype(BF16), p_c.astype(BF16), p_n.astype(BF16)))
        for sl, (ds_c, ds_p, ds_n, pb_c, pb_n) in zip(sls, second):
            q, qn = q_ref[:, sl], qn_ref[:, sl]
            dyb, dynb = d_ref[:, sl].astype(BF16), dn_ref[:, sl].astype(BF16)
            dq_ref[:, sl] = _dot(ds_c, kc_ref[:, sl]) + _dot(ds_p, kp_ref[:, sl])
            dk_ref[:, sl] = _dot_tn(ds_c, q) + _dot_tn(ds_n, qn)
            dv_ref[:, sl] = _dot_tn(pb_c, dyb) + _dot_tn(pb_n, dynb)

    def col(which):
        return lambda c, i: (c * nb + i, cols[which])

    def col_prev(which):
        return lambda c, i: (c * nb + jnp.maximum(i - 1, 0), cols[which])

    def col_next(which):
        return lambda c, i: (c * nb + jnp.minimum(i + 1, nb - 1), cols[which])

    blk = (BLOCK, GROUP_WIDTH)
    own = pl.BlockSpec(blk, lambda c, i: (c * nb + i, 0))
    nxt = pl.BlockSpec(blk, lambda c, i: (c * nb + jnp.minimum(i + 1, nb - 1), 0))
    return pl.pallas_call(
        body, name=name, grid=(r, nb),
        in_specs=[pl.BlockSpec(blk, col(0)), pl.BlockSpec(blk, col_next(0)),
                  pl.BlockSpec(blk, col_prev(1)), pl.BlockSpec(blk, col(1)),
                  pl.BlockSpec(blk, col_prev(2)), pl.BlockSpec(blk, col(2)),
                  own, nxt, own, nxt, own, nxt],
        out_specs=[own] * 3,
        out_shape=[jax.ShapeDtypeStruct((s, GROUP_WIDTH), F32)] * 3,
        compiler_params=_params(("parallel", "parallel")),
    )(src, src, src, src, src, src, y, y, lse, lse, dy, dy)


SB_TQ = 256
SB_GROUP = 2
SB_DEAD = -104.0
SB_NH_FWD = 4
SB_NH_BWD = 2


def _split_dot(x, w):
    hi = x.astype(BF16)
    lo = (x - hi.astype(F32)).astype(BF16)
    return _dot(hi, w) + _dot(lo, w)


def _sb_consts():
    j = lax.broadcasted_iota(jnp.int32, (BLOCK, BLOCK), 0)
    k = lax.broadcasted_iota(jnp.int32, (BLOCK, BLOCK), 1)
    ones = jnp.ones((BLOCK, BLOCK), F32)
    after = jnp.concatenate([(j > k).astype(F32), ones], axis=1).astype(BF16)
    before = jnp.concatenate([(j < k).astype(F32), ones], axis=1).astype(BF16)
    return after, before


def _sb_mask(row0, col0, tq):
    row = row0 + lax.broadcasted_iota(jnp.int32, (tq, BLOCK), 0)
    col = col0 + lax.broadcasted_iota(jnp.int32, (tq, BLOCK), 1)
    return col < row


def _softplus(z):
    return jnp.maximum(z, 0.0) + jnp.log1p(jnp.exp(-jnp.abs(z)))


def _sb_attn_fwd(qkv, *, name):
    s = qkv.shape[0]
    tq = _pick(s, SB_TQ)
    nq = s // tq
    per = tq // BLOCK
    base = 3 * DIL_WIDTH // HEAD_DIM
    after, _ = _sb_consts()

    nh = SB_NH_FWD
    wide = nh * HEAD_DIM

    def body(q_ref, k_ref, v_ref, w_ref, o_ref, rc_ref, trips_ref, acc_ref, run_ref):
        qi = pl.program_id(1)
        w = w_ref[...]
        acc_ref[...] = jnp.zeros_like(acc_ref)
        run_ref[...] = jnp.zeros_like(run_ref)
        rc_ref[...] = jnp.zeros_like(rc_ref)
        lane = lax.broadcasted_iota(jnp.int32, (tq, BLOCK), 1)
        n_chunks = (qi + 1) * per

        def chunks(js, masked):
            sls = [slice(h * HEAD_DIM, (h + 1) * HEAD_DIM) for h in range(nh)]
            work = [(sl, j, pl.multiple_of(j * BLOCK, BLOCK), _sb_mask(qi * tq, j * BLOCK, tq) if masked else None)
                    for j in js for sl in sls]
            zs = [_dot_nt(q_ref[:, sl], k_ref[pl.ds(start, BLOCK), sl]) * SCALE for sl, _, start, _ in work]
            logits, sums = [], []
            for z, (_, _, _, mask) in zip(zs, work):
                sp = _softplus(z)
                logits.append(z - sp)
                sums.append(_split_dot(jnp.where(mask, -sp, 0.0) if masked else -sp, w))
            for (sl, j, start, mask), lg, sm in zip(work, logits, sums):
                run = run_ref[:, sl]
                a = jnp.exp(lg + run + sm[:, :BLOCK])
                if masked:
                    a = jnp.where(mask, a, 0.0)
                acc_ref[:, sl] += _dot(a.astype(BF16), v_ref[pl.ds(start, BLOCK), sl])
                rc_ref[:, sl] = jnp.where(lane == j, run, rc_ref[:, sl])
                run_ref[:, sl] = run + sm[:, BLOCK:]

        def diagonal(t, carry):
            last = n_chunks - 1 - SB_GROUP * t
            chunks([last - g for g in range(SB_GROUP)], True)
            return carry

        def alive():
            return (jnp.max(run_ref[...]) > SB_DEAD).astype(jnp.int32)

        def more(carry):
            t, live = carry
            return jnp.logical_and(t < n_left, live > 0)

        def left(carry):
            t, _ = carry
            last = n_chunks - 1 - per - SB_GROUP * t
            chunks([last - g for g in range(SB_GROUP)], False)
            return t + 1, alive()

        lax.fori_loop(0, per // SB_GROUP, diagonal, 0)
        n_left = (n_chunks - per) // SB_GROUP
        trips, _ = lax.while_loop(more, left, (jnp.int32(0), alive()))
        trips_ref[pl.program_id(0), qi] = trips
        o_ref[...] = acc_ref[...].astype(o_ref.dtype)

    head = lambda off: (lambda h, i: (0, (base + off) // nh + h))
    once = pl.Buffered(1)
    return pl.pallas_call(
        body, name=name, grid=(SB_HEADS // nh, nq),
        in_specs=[pl.BlockSpec((tq, wide), lambda h, i: (i, base // nh + h)),
                  pl.BlockSpec((s, wide), head(SB_HEADS), pipeline_mode=once),
                  pl.BlockSpec((s, wide), head(2 * SB_HEADS), pipeline_mode=once),
                  pl.BlockSpec((BLOCK, 2 * BLOCK), lambda h, i: (0, 0))],
        out_specs=[pl.BlockSpec((tq, wide), lambda h, i: (i, h))] * 2 + [pl.BlockSpec(memory_space=pltpu.SMEM)],
        out_shape=[jax.ShapeDtypeStruct((s, SB_WIDTH), BF16), jax.ShapeDtypeStruct((s, SB_WIDTH), F32),
                   jax.ShapeDtypeStruct((SB_HEADS // nh, nq), jnp.int32)],
        scratch_shapes=[pltpu.VMEM((tq, wide), F32), pltpu.VMEM((tq, wide), F32)],
        compiler_params=_params(("arbitrary", "arbitrary")),
    )(qkv, qkv, qkv, after)


def _sb_attn_bwd(qkv, rc, trips, do, *, name):
    s = qkv.shape[0]
    tq = _pick(s, SB_TQ)
    nq = s // tq
    per = tq // BLOCK
    base = 3 * DIL_WIDTH // HEAD_DIM
    _, before = _sb_consts()

    nh = SB_NH_BWD
    wide = nh * HEAD_DIM

    def body(trips_ref, q_ref, k_ref, v_ref, w_ref, rc_ref, do_ref, dq_ref, dk_ref, dv_ref, acc_ref, pre_ref, dob_ref):
        qi = pl.program_id(1)

        @pl.when(qi == 0)
        def _():
            dk_ref[...] = jnp.zeros_like(dk_ref)
            dv_ref[...] = jnp.zeros_like(dv_ref)

        w = w_ref[...]
        wa = (lax.broadcasted_iota(jnp.int32, (BLOCK, BLOCK), 0)
              > lax.broadcasted_iota(jnp.int32, (BLOCK, BLOCK), 1)).astype(BF16)
        dob_ref[...] = do_ref[...].astype(BF16)
        acc_ref[...] = jnp.zeros_like(acc_ref)
        pre_ref[...] = jnp.zeros_like(pre_ref)
        lane = lax.broadcasted_iota(jnp.int32, (tq, BLOCK), 1)
        n_chunks = (qi + 1) * per

        def chunks(js, masked):
            sls = [slice(h * HEAD_DIM, (h + 1) * HEAD_DIM) for h in range(nh)]
            work = [(sl, j, pl.multiple_of(j * BLOCK, BLOCK), _sb_mask(qi * tq, j * BLOCK, tq) if masked else None)
                    for j in js for sl in sls]
            zs = [_dot_nt(q_ref[:, sl], k_ref[pl.ds(start, BLOCK), sl]) * SCALE for sl, _, start, _ in work]
            das = [_dot_nt(dob_ref[:, sl], v_ref[pl.ds(start, BLOCK), sl]) for sl, _, start, _ in work]
            sigs, betweens = [], []
            for z, (_, _, _, mask) in zip(zs, work):
                sp = _softplus(z)
                sigs.append(jnp.exp(z - sp))
                betweens.append(_split_dot(jnp.where(mask, -sp, 0.0) if masked else -sp, wa))
            es, avs, sums = [], [], []
            for (sl, j, _, mask), sig, bt, da in zip(work, sigs, betweens, das):
                run = jnp.sum(jnp.where(lane == j, rc_ref[:, sl], 0.0), axis=-1, keepdims=True)
                a = sig * jnp.exp(run + bt)
                if masked:
                    a = jnp.where(mask, a, 0.0)
                e = a * da
                es.append(e)
                avs.append(a.astype(BF16))
                sums.append(_split_dot(e, w))
            for (sl, _, start, mask), sig, e, ab, sm in zip(work, sigs, es, avs, sums):
                pre = pre_ref[:, sl]
                dz = (e * (1.0 - sig) - sig * (pre + sm[:, :BLOCK])) * SCALE
                if masked:
                    dz = jnp.where(mask, dz, 0.0)
                dz = dz.astype(BF16)
                pre_ref[:, sl] = pre + sm[:, BLOCK:]
                acc_ref[:, sl] += _dot(dz, k_ref[pl.ds(start, BLOCK), sl])
                dk_ref[pl.ds(start, BLOCK), sl] += _dot_tn(dz, q_ref[:, sl])
                dv_ref[pl.ds(start, BLOCK), sl] += _dot_tn(ab, dob_ref[:, sl])

        def left(t, carry):
            chunks([SB_GROUP * t + g for g in range(SB_GROUP)], False)
            return carry

        def diagonal(t, carry):
            chunks([n_chunks - per + SB_GROUP * t + g for g in range(SB_GROUP)], True)
            return carry

        n_left = (n_chunks - per) // SB_GROUP
        ran = trips_ref[pl.program_id(0) // (SB_NH_FWD // nh), qi]
        lax.fori_loop(n_left - ran, n_left, left, 0)
        lax.fori_loop(0, per // SB_GROUP, diagonal, 0)
        dq_ref[...] = acc_ref[...]

    head = lambda off: (lambda h, i, t: (0, (base + off) // nh + h))
    blk = pl.BlockSpec((tq, wide), lambda h, i, t: (i, h))
    whole = pl.BlockSpec((s, wide), lambda h, i, t: (0, h))
    once = pl.Buffered(1)
    grid_spec = pltpu.PrefetchScalarGridSpec(
        num_scalar_prefetch=1, grid=(SB_HEADS // nh, nq),
        in_specs=[pl.BlockSpec((tq, wide), lambda h, i, t: (i, base // nh + h)),
                  pl.BlockSpec((s, wide), head(SB_HEADS), pipeline_mode=once),
                  pl.BlockSpec((s, wide), head(2 * SB_HEADS), pipeline_mode=once),
                  pl.BlockSpec((BLOCK, 2 * BLOCK), lambda h, i, t: (0, 0)),
                  blk, blk],
        out_specs=[blk, whole, whole],
        scratch_shapes=[pltpu.VMEM((tq, wide), F32), pltpu.VMEM((tq, wide), F32), pltpu.VMEM((tq, wide), BF16)],
    )
    return pl.pallas_call(
        body, name=name, grid_spec=grid_spec,
        out_shape=[jax.ShapeDtypeStruct((s, SB_WIDTH), F32)] * 3,
        compiler_params=_params(("parallel", "arbitrary")),
    )(trips, qkv, qkv, qkv, before, rc, do)


GATE_TC = 512


def _gate_fwd(proj, gate_b, up_dil, up_sb, *, name):
    s, d = up_dil.shape
    tr = _pick(s, 512)
    g0 = QKV_WIDTH // GATE_TC
    nc = d // GATE_TC

    def body(ga_ref, gb_ref, b_ref, ud_ref, us_ref, o_ref):
        ga = jax.nn.sigmoid(ga_ref[...] + b_ref[0:1, :])
        gb = jax.nn.sigmoid(gb_ref[...] + b_ref[1:2, :])
        o_ref[...] = (ga * ud_ref[...] + gb * us_ref[...]).astype(o_ref.dtype)

    blk = pl.BlockSpec((tr, GATE_TC), lambda i, j: (i, j))
    return pl.pallas_call(
        body, name=name, grid=(s // tr, nc),
        in_specs=[pl.BlockSpec((tr, GATE_TC), lambda i, j: (i, g0 + j)),
                  pl.BlockSpec((tr, GATE_TC), lambda i, j: (i, g0 + nc + j)),
                  pl.BlockSpec((2, GATE_TC), lambda i, j: (0, j)), blk, blk],
        out_specs=blk,
        out_shape=jax.ShapeDtypeStruct((s, d), BF16),
        compiler_params=_params(("parallel", "parallel")),
    )(proj, proj, gate_b, up_dil, up_sb)


def _gate_bwd(proj, gate_b, up_dil, up_sb, dmixed, *, name):
    s, d = up_dil.shape
    tr = _pick(s, 512)
    g0 = QKV_WIDTH // GATE_TC
    nc = d // GATE_TC

    def body(ga_ref, gb_ref, b_ref, ud_ref, us_ref, dm_ref, dud_ref, dus_ref, dga_ref, dgb_ref, db_ref):
        ga = jax.nn.sigmoid(ga_ref[...] + b_ref[0:1, :])
        gb = jax.nn.sigmoid(gb_ref[...] + b_ref[1:2, :])
        dm = dm_ref[...]
        dud_ref[...] = (dm * ga).astype(BF16)
        dus_ref[...] = (dm * gb).astype(BF16)
        dla = dm * ud_ref[...] * ga * (1.0 - ga)
        dlb = dm * us_ref[...] * gb * (1.0 - gb)
        dga_ref[...] = dla.astype(BF16)
        dgb_ref[...] = dlb.astype(BF16)

        @pl.when(pl.program_id(1) == 0)
        def _():
            db_ref[...] = jnp.zeros_like(db_ref)

        db_ref[0:1, :] += jnp.sum(dla, axis=0, keepdims=True)
        db_ref[1:2, :] += jnp.sum(dlb, axis=0, keepdims=True)

    blk = pl.BlockSpec((tr, GATE_TC), lambda j, i: (i, j))
    dud, dus, dga, dgb, db = pl.pallas_call(
        body, name=name, grid=(nc, s // tr),
        in_specs=[pl.BlockSpec((tr, GATE_TC), lambda j, i: (i, g0 + j)),
                  pl.BlockSpec((tr, GATE_TC), lambda j, i: (i, g0 + nc + j)),
                  pl.BlockSpec((2, GATE_TC), lambda j, i: (0, j)), blk, blk, blk],
        out_specs=[blk, blk, blk, blk, pl.BlockSpec((2, GATE_TC), lambda j, i: (0, j))],
        out_shape=[jax.ShapeDtypeStruct((s, d), BF16)] * 4 + [jax.ShapeDtypeStruct((2, d), F32)],
        compiler_params=_params(("parallel", "arbitrary")),
    )(proj, proj, gate_b, up_dil, up_sb, dmixed)
    return dud, dus, jnp.concatenate([dga, dgb], axis=1), db


def _adamw(w, g, m, v, *, name):
    shape = w.shape
    cols = shape[-1]
    rows = w.size // cols
    tr = _pick(rows, max(8, (512 * 1024) // cols))
    if rows % tr or (tr % 8 and tr != rows):
        tr = rows
    c1 = 1.0 - ADAM_B1 ** ADAM_STEP
    c2 = 1.0 - ADAM_B2 ** ADAM_STEP

    def body(w_ref, g_ref, m_ref, v_ref, d_ref, nm_ref, nv_ref):
        gv = g_ref[...]
        nm = ADAM_B1 * m_ref[...] + (1.0 - ADAM_B1) * gv
        nv = ADAM_B2 * v_ref[...] + (1.0 - ADAM_B2) * (gv * gv)
        d_ref[...] = -ADAM_LR * ((nm / c1) / (jnp.sqrt(nv / c2) + ADAM_EPS) + ADAM_WD * w_ref[...])
        nm_ref[...] = nm
        nv_ref[...] = nv

    blk = pl.BlockSpec((tr, cols), lambda i: (i, 0))
    outs = pl.pallas_call(
        body, name=name, grid=(rows // tr,),
        in_specs=[blk] * 4, out_specs=[blk] * 3,
        out_shape=[jax.ShapeDtypeStruct((rows, cols), F32)] * 3,
        compiler_params=_params(("parallel",)),
    )(*(t.reshape(rows, cols) for t in (w, g, m, v)))
    return tuple(t.reshape(shape) for t in outs)


def _add_half(full, other, which, *, name):
    n, rh, c = other.shape
    tr = _pick(rh, max(8, (1024 * 1024) // c))
    nb = rh // tr

    def body(which_ref, a_ref, b_ref, o_ref):
        o_ref[...] = (a_ref[...] + b_ref[...]).astype(o_ref.dtype)

    grid_spec = pltpu.PrefetchScalarGridSpec(
        num_scalar_prefetch=1, grid=(n, nb),
        in_specs=[pl.BlockSpec((None, tr, c), lambda s, i, w: (s, w[0] * nb + i, 0)),
                  pl.BlockSpec((None, tr, c), lambda s, i, w: (s, i, 0))],
        out_specs=pl.BlockSpec((None, tr, c), lambda s, i, w: (s, i, 0)),
    )
    return pl.pallas_call(
        body, name=name, grid_spec=grid_spec,
        out_shape=jax.ShapeDtypeStruct((n, rh, c), BF16),
        compiler_params=_params(("parallel", "parallel")),
    )(which, full, other)


def _sum_chips(own, parts, chip, *, name):
    n, r, c = parts.shape
    tr = _pick(r, max(16, (1024 * 1024) // c))

    def body(chip_ref, own_ref, p1_ref, p2_ref, p3_ref, o_ref):
        acc = own_ref[...].astype(F32)
        for p in (p1_ref, p2_ref, p3_ref):
            acc = acc + p[...].astype(F32)
        o_ref[...] = acc

    def slot(k):
        return pl.BlockSpec((None, tr, c), lambda i, o: (o[0] ^ k, i, 0))

    grid_spec = pltpu.PrefetchScalarGridSpec(
        num_scalar_prefetch=1, grid=(r // tr,),
        in_specs=[slot(0), slot(1), slot(2), slot(3)],
        out_specs=pl.BlockSpec((tr, c), lambda i, o: (i, 0)),
    )
    return pl.pallas_call(
        body, name=name, grid_spec=grid_spec,
        out_shape=jax.ShapeDtypeStruct((r, c), F32),
        compiler_params=_params(("parallel",)),
    )(chip, own, parts, parts, parts)


def _place_piece(full, piece, index, *, name):
    a, n, r, c = full.shape
    tr = _pick(r, max(16, (1024 * 1024) // c))

    def body(index_ref, piece_ref, full_ref, o_ref):
        o_ref[...] = piece_ref[...]

    grid_spec = pltpu.PrefetchScalarGridSpec(
        num_scalar_prefetch=1, grid=(a, r // tr),
        in_specs=[pl.BlockSpec((None, tr, c), lambda l, i, k: (l, i, 0)), ANY],
        out_specs=pl.BlockSpec((None, None, tr, c), lambda l, i, k: (l, k[0], i, 0)),
    )
    return pl.pallas_call(
        body, name=name, grid_spec=grid_spec,
        out_shape=jax.ShapeDtypeStruct(full.shape, full.dtype),
        input_output_aliases={2: 0},
        compiler_params=_params(("parallel", "parallel")),
    )(index, piece, full)


ANY = pl.BlockSpec(memory_space=pl.ANY)


def _place():
    x, y, c = lax.axis_index("x"), lax.axis_index("y"), lax.axis_index("c")
    chips = [(1 - x, y), (x, 1 - y), (1 - x, 1 - y)]
    return x, y, c, chips


def _remote(src, dst, send_sem, recv_sem, to):
    return pltpu.make_async_remote_copy(src_ref=src, dst_ref=dst, send_sem=send_sem, recv_sem=recv_sem,
                                        device_id=to, device_id_type=MESH)


def _half(ref, chip, which):
    rh = ref.shape[1] // 2
    return ref.at[chip, pl.ds(which * rh, rh), :]


def _gather_ici_copies(layer):
    def copies(ins, outs, send_sems, recv_sems):
        x, y, c, chips = _place()
        me = 2 * x + y
        sends, recvs = [], []
        for t in range(len(ins)):
            rh = ins[t].shape[1] // 2
            for k, (px, py) in enumerate(chips):
                sems = (send_sems.at[3 * t + k], recv_sems.at[3 * t + k])
                sends.append(_remote(ins[t].at[layer, pl.ds(c * rh, rh), :], _half(outs[t], me, c), *sems, (px, py, c)))
                landed = _half(outs[t], 2 * px + py, c)
                recvs.append(_remote(landed, landed, *sems, (px, py, c)))
        return sends, recvs
    return copies


def _gather_pass_copies(ins, outs, send_sems, recv_sems):
    x, y, c, chips = _place()
    sibling = (x, y, 1 - c)
    sends, recvs = [], []
    for t in range(len(ins)):
        for k, (px, py) in enumerate(chips):
            sems = (send_sems.at[3 * t + k], recv_sems.at[3 * t + k])
            sends.append(_remote(_half(ins[t], 2 * px + py, c), _half(outs[t], 2 * px + py, c), *sems, sibling))
            other = _half(outs[t], 2 * px + py, 1 - c)
            recvs.append(_remote(other, other, *sems, sibling))
    return sends, recvs


def _gathered_shapes(shards):
    return [jax.ShapeDtypeStruct((N_CHIPS,) + w.shape[1:], w.dtype) for w in shards]


def _gather_ici_side(shards, layer):
    return (tuple(shards), _gathered_shapes(shards), _gather_ici_copies(layer), 3 * len(shards))


def _gather_layer(shards, layer, *, name):
    n = len(shards)
    ici = _gather_ici_copies(layer)

    def body(*refs):
        ins, outs = refs[:n], refs[n:2 * n]
        send_sems, recv_sems, pass_send, pass_recv = refs[2 * n:]
        sends, recvs = ici(ins, outs, send_sems, recv_sems)
        passes, arrivals = _gather_pass_copies(outs, outs, pass_send, pass_recv)
        for cp in sends:
            cp.start()
        for landed, onward in zip(recvs, passes):
            landed.wait_recv()
            onward.start()
        for cp in arrivals:
            cp.wait_recv()
        for cp in sends + passes:
            cp.wait_send()

    return pl.pallas_call(
        body, name=name,
        in_specs=[ANY] * n, out_specs=[ANY] * n,
        out_shape=_gathered_shapes(shards),
        scratch_shapes=[pltpu.SemaphoreType.DMA((3 * n,)) for _ in range(4)],
        compiler_params=pltpu.CompilerParams(has_side_effects=True),
    )(*shards)


def _gather_pass(landed, *, name):
    n = len(landed)

    def body(*refs):
        ins, outs = refs[:n], refs[n:2 * n]
        send_sems, recv_sems = refs[2 * n:]
        passes, arrivals = _gather_pass_copies(ins, outs, send_sems, recv_sems)
        for cp in passes:
            cp.start()
        for cp in arrivals:
            cp.wait_recv()
        for cp in passes:
            cp.wait_send()

    return pl.pallas_call(
        body, name=name,
        in_specs=[ANY] * n, out_specs=[ANY] * n,
        out_shape=[jax.ShapeDtypeStruct(w.shape, w.dtype) for w in landed],
        input_output_aliases={t: t for t in range(n)},
        scratch_shapes=[pltpu.SemaphoreType.DMA((3 * n,)), pltpu.SemaphoreType.DMA((3 * n,))],
        compiler_params=pltpu.CompilerParams(has_side_effects=True),
    )(*landed)


def _pair_swap_halves(grads, *, name):
    n = len(grads)

    def body(*refs):
        ins, outs = refs[:n], refs[n:2 * n]
        send_sems, recv_sems = refs[2 * n:]
        x, y, c, _ = _place()
        sibling = (x, y, 1 - c)
        cps = []
        for t in range(n):
            rh = ins[t].shape[1] // 2
            cp = _remote(ins[t].at[:, pl.ds((1 - c) * rh, rh), :], outs[t], send_sems.at[t], recv_sems.at[t], sibling)
            cp.start()
            cps.append(cp)
        for cp in cps:
            cp.wait()

    return pl.pallas_call(
        body, name=name,
        in_specs=[ANY] * n, out_specs=[ANY] * n,
        out_shape=[jax.ShapeDtypeStruct((g.shape[0], g.shape[1] // 2, g.shape[2]), g.dtype) for g in grads],
        scratch_shapes=[pltpu.SemaphoreType.DMA((n,)), pltpu.SemaphoreType.DMA((n,))],
        compiler_params=pltpu.CompilerParams(has_side_effects=True),
    )(*grads)


def _chip_copies(ins, outs, send_sems, recv_sems):
    x, y, c, chips = _place()
    me = 2 * x + y
    sends, recvs = [], []
    for t in range(len(ins)):
        for k, (px, py) in enumerate(chips):
            sems = (send_sems.at[3 * t + k], recv_sems.at[3 * t + k])
            sends.append(_remote(ins[t].at[2 * px + py], outs[t].at[me], *sems, (px, py, c)))
            theirs = outs[t].at[2 * px + py]
            recvs.append(_remote(theirs, theirs, *sems, (px, py, c)))
    return sends, recvs


def _chip_side(parts):
    return (tuple(parts), [jax.ShapeDtypeStruct(p.shape, p.dtype) for p in parts], _chip_copies, 3 * len(parts))


def _chip_exchange(parts, *, name):
    n = len(parts)

    def body(*refs):
        sends, recvs = _chip_copies(refs[:n], refs[n:2 * n], *refs[2 * n:])
        for cp in sends:
            cp.start()
        for cp in recvs:
            cp.wait_recv()
        for cp in sends:
            cp.wait_send()

    return pl.pallas_call(
        body, name=name,
        in_specs=[ANY] * n, out_specs=[ANY] * n,
        out_shape=[jax.ShapeDtypeStruct(p.shape, p.dtype) for p in parts],
        scratch_shapes=[pltpu.SemaphoreType.DMA((3 * n,)), pltpu.SemaphoreType.DMA((3 * n,))],
        compiler_params=pltpu.CompilerParams(has_side_effects=True),
    )(*parts)


def _pair_join_halves(halves, *, name):
    n = len(halves)

    def body(*refs):
        ins, outs = refs[:n], refs[n:2 * n]
        send_sems, recv_sems = refs[2 * n:]
        x, y, c, _ = _place()
        sibling = (x, y, 1 - c)
        cps = []
        for t in range(n):
            rh = ins[t].shape[0]
            cp = _remote(ins[t], outs[t].at[pl.ds(c * rh, rh), :], send_sems.at[t], recv_sems.at[t], sibling)
            cp.start()
            cps.append(cp)
        for t, cp in enumerate(cps):
            rh = ins[t].shape[0]
            theirs = outs[t].at[pl.ds((1 - c) * rh, rh), :]
            _remote(theirs, theirs, send_sems.at[t], recv_sems.at[t], sibling).wait_recv()
            cp.wait_send()

    return pl.pallas_call(
        body, name=name,
        in_specs=[ANY] * n, out_specs=[ANY] * n,
        out_shape=[jax.ShapeDtypeStruct((2 * h.shape[0], h.shape[1]), h.dtype) for h in halves],
        scratch_shapes=[pltpu.SemaphoreType.DMA((n,)), pltpu.SemaphoreType.DMA((n,))],
        compiler_params=pltpu.CompilerParams(has_side_effects=True),
    )(*halves)


def _exchange_small(buf, *, name):
    rows, lanes = buf.shape

    def body(in_ref, out_ref, send_sems, recv_sems, local_sem):
        x, y, c = lax.axis_index("x"), lax.axis_index("y"), lax.axis_index("c")
        me = 4 * x + 2 * y + c
        lc = pltpu.make_async_copy(in_ref, out_ref.at[me], local_sem)
        lc.start()
        cps = []
        for k in range(1, 8):
            fx, fy, fc = (k >> 2) & 1, (k >> 1) & 1, k & 1
            to = (x ^ fx, y ^ fy, c ^ fc)
            cp = _remote(in_ref, out_ref.at[me], send_sems.at[k - 1], recv_sems.at[k - 1], to)
            cp.start()
            cps.append(cp)
        for k in range(1, 8):
            fx, fy, fc = (k >> 2) & 1, (k >> 1) & 1, k & 1
            theirs = out_ref.at[4 * (x ^ fx) + 2 * (y ^ fy) + (c ^ fc)]
            _remote(theirs, theirs, send_sems.at[k - 1], recv_sems.at[k - 1], (x ^ fx, y ^ fy, c ^ fc)).wait_recv()
        for cp in cps:
            cp.wait_send()
        lc.wait()

    return pl.pallas_call(
        body, name=name,
        in_specs=[ANY], out_specs=ANY,
        out_shape=jax.ShapeDtypeStruct((8, rows, lanes), buf.dtype),
        scratch_shapes=[pltpu.SemaphoreType.DMA((7,)), pltpu.SemaphoreType.DMA((7,)), pltpu.SemaphoreType.DMA],
        compiler_params=pltpu.CompilerParams(has_side_effects=True),
    )(buf)


def _sum_slots(slots, *, name):
    n, rows, lanes = slots.shape

    def body(s_ref, o_ref):
        acc = s_ref[0]
        for t in range(1, n):
            acc = acc + s_ref[t]
        o_ref[...] = acc

    return pl.pallas_call(
        body, name=name,
        out_shape=jax.ShapeDtypeStruct((rows, lanes), slots.dtype),
    )(slots)


def _relu2(acc):
    r = jnp.maximum(acc, 0.0)
    return acc, r * r


def _with_side(result, side):
    return result if side else (result, ())


def _layer_fwd(x, w, cos, sin, tag, sides=None, rest=None):
    sides = sides or {}
    got = {}
    h = _rmsnorm_fwd(x, w["norm1_g"], name=f"norm1_{tag}")
    proj, got["proj"] = _with_side(_matmul(h, w["w_in"], mode="nn", b_sharded=True, tm=512, side=sides.get("proj"),
                                           name=f"proj_{tag}"), sides.get("proj"))
    if rest:
        w = {**w, **rest(got["proj"])}
    qkv, res1, res2 = _qkv_prep(proj, w["q_norm_g"], w["k_norm_g"], cos, sin, name=f"qkv_prep_{tag}")
    dil_src = [(qkv, (0, 3, 6))] + [(t.reshape(-1, 3 * GROUP_WIDTH), (0, 1, 2)) for t in (res1, res2)]
    outs, lds = [], []
    for g, (src, cols) in enumerate(dil_src):
        o_g, ld_g = _dil_attn_fwd(src, g, cols, name=f"dil{g}_fwd_{tag}")
        outs.append(o_g)
        lds.append(ld_g)
    yb, y, lse = _dil_merge(outs, lds, name=f"dil_merge_{tag}")
    o_sb, rc, trips = _sb_attn_fwd(qkv, name=f"sb_fwd_{tag}")
    up_dil = _matmul(yb, w["w_up_dil"], mode="nn", b_sharded=True, name=f"up_dil_{tag}")
    up_sb = _matmul(o_sb, w["w_up_sb"], mode="nn", b_sharded=True, name=f"up_sb_{tag}")
    mixed = _gate_fwd(proj, w["gate_b"], up_dil, up_sb, name=f"gate_{tag}")
    x1, got["out"] = _with_side(_matmul(mixed, w["w_out"], mode="nn", extras=(x,), epilogue=lambda acc, res: (acc + res,),
                                        side=sides.get("out"), name=f"out_{tag}"), sides.get("out"))
    h2 = _rmsnorm_fwd(x1, w["norm2_g"], name=f"norm2_{tag}")
    (u, a), got["ff1"] = _with_side(_matmul(h2, w["w_ff1"], mode="nn", b_sharded=True, out_dtypes=(F32, BF16),
                                            epilogue=_relu2, side=sides.get("ff1"), name=f"ff1_{tag}"), sides.get("ff1"))
    x2, got["ff2"] = _with_side(_matmul(a, w["w_ff2"], mode="nn", extras=(x1,), epilogue=lambda acc, res: (acc + res,),
                                        side=sides.get("ff2"), name=f"ff2_{tag}"), sides.get("ff2"))
    saved = dict(x=x, h=h, proj=proj, qkv=qkv, dil_src=dil_src, y=y, yb=yb, lse=lse, o_sb=o_sb, rc=rc, trips=trips, up_dil=up_dil, up_sb=up_sb,
                 mixed=mixed, x1=x1, h2=h2, u=u, a=a, w=w)
    return x2, saved, got


def _layer_bwd(dx, dxb, sv, cos, sin, tag, sides=None, late=None):
    sides = dict(sides or {})
    w = sv["w"]
    grads, got = {}, {}
    du, got["d_a"] = _with_side(_matmul(dxb, w["w_ff2"], mode="nt", extras=(sv["u"],), out_dtypes=(BF16,),
                                        epilogue=lambda acc, u: (acc * (2.0 * jnp.maximum(u, 0.0)),),
                                        side=sides.get("d_a"), name=f"d_a_{tag}"), sides.get("d_a"))
    grads["w_ff2"], got["dw_ff2"] = _with_side(_matmul(sv["a"], dxb, mode="tn", side=sides.get("dw_ff2"),
                                                       name=f"dw_ff2_{tag}"), sides.get("dw_ff2"))
    dh2, got["d_h2"] = _with_side(_matmul(du, w["w_ff1"], mode="nt", b_sharded=True, side=sides.get("d_h2"),
                                          name=f"d_h2_{tag}"), sides.get("d_h2"))
    grads["w_ff1"] = _matmul(sv["h2"], du, mode="tn", out_sharded=True, name=f"dw_ff1_{tag}")
    dx1, dx1b, grads["norm2_g"] = _rmsnorm_bwd(sv["x1"], w["norm2_g"], dh2, dx, name=f"norm2_bwd_{tag}")
    dmixed = _matmul(dx1b, w["w_out"], mode="nt", name=f"d_mixed_{tag}")
    grads["w_out"] = _matmul(sv["mixed"], dx1b, mode="tn", name=f"dw_out_{tag}")
    dud, dus, dgate, grads["gate_b"] = _gate_bwd(sv["proj"], w["gate_b"], sv["up_dil"], sv["up_sb"], dmixed,
                                                 name=f"gate_bwd_{tag}")
    dy_dil = _matmul(dud, w["w_up_dil"], mode="nt", b_sharded=True, name=f"d_ydil_{tag}")
    grads["w_up_dil"] = _matmul(sv["yb"], dud, mode="tn", out_sharded=True, name=f"dw_up_dil_{tag}")
    dy_sb = _matmul(dus, w["w_up_sb"], mode="nt", b_sharded=True, name=f"d_ysb_{tag}")
    grads["w_up_sb"] = _matmul(sv["o_sb"], dus, mode="tn", out_sharded=True, name=f"dw_up_sb_{tag}")
    if late:
        sides.update(late(grads))
    dys = [dy_dil] + _dil_to_residues(dy_dil, name=f"dil_dy_{tag}")
    d_dil = [_dil_attn_bwd(src, g, cols, sv["y"][g], sv["lse"][g], dys[g], name=f"dil{g}_bwd_{tag}")
             for g, (src, cols) in enumerate(sv["dil_src"])]
    for g in (1, 2):
        d_dil[g] = [t.reshape(DIL_GROUPS[g][1], -1, GROUP_WIDTH) for t in d_dil[g]]
    d_sb = _sb_attn_bwd(sv["qkv"], sv["rc"], sv["trips"], dy_sb, name=f"sb_bwd_{tag}")
    dproj, grads["q_norm_g"], grads["k_norm_g"] = _qkv_prep_bwd(
        sv["proj"], w["q_norm_g"], w["k_norm_g"], cos, sin, d_dil, d_sb, dgate, name=f"qkv_prep_bwd_{tag}")
    dh, got["d_h"] = _with_side(_matmul(dproj, w["w_in"], mode="nt", b_sharded=True, tm=512, tn=2048, side=sides.get("d_h"),
                                        name=f"d_h_{tag}"), sides.get("d_h"))
    grads["w_in"], got["dw_in"] = _with_side(_matmul(sv["h"], dproj, mode="tn", out_sharded=True, tm=512, tk=1024,
                                                     side=sides.get("dw_in"), name=f"dw_in_{tag}"), sides.get("dw_in"))
    dx0, dx0b, grads["norm1_g"] = _rmsnorm_bwd(sv["x"], w["norm1_g"], dh, dx1, name=f"norm1_bwd_{tag}")
    return dx0, dx0b, grads, got


BIG = ("w_in", "w_up_dil", "w_up_sb", "w_out", "w_ff1", "w_ff2")
ROW_SHARDED = ("w_out", "w_ff2")
WEIGHTS = ("norm1_g", "w_in", "q_norm_g", "k_norm_g", "w_up_dil", "w_up_sb", "gate_b", "w_out", "norm2_g", "w_ff1", "w_ff2")


def _reduce_begin(grads, which, tag):
    names = list(grads)
    full = [grads[n] for n in names]
    theirs = _pair_swap_halves(full, name=f"rs_pair_{tag}")
    return {n: _add_half(f, t, which, name=f"rs_pair_sum_{n}_{tag}") for n, f, t in zip(names, full, theirs)}


def _reduce_end(pair, parts, which, chip, tag):
    names = list(pair)
    halves = [_sum_chips(pair[n], parts[n], chip, name=f"rs_chip_sum_{n}_{tag}") for n in names]
    joined = _pair_join_halves(halves, name=f"rs_join_{tag}")
    out = {}
    for n, j, h in zip(names, joined, halves):
        r, c = j.shape
        out[n] = _place_piece(j.reshape(1, 2, r // 2, c), h[None], which, name=f"rs_place_{n}_{tag}").reshape(r, c)
    return out


def _pack_rows(vecs):
    rows, spans, at = [], [], 0
    for v in vecs:
        r = v.size // 128
        padded = -(-r // 8) * 8
        rows.append(jnp.pad(v.reshape(r, 128), ((0, padded - r), (0, 0))))
        spans.append((at, r))
        at += padded
    return jnp.concatenate(rows, axis=0), spans


def kernel(x, norm1_g, w_in, q_norm_g, k_norm_g, w_up_dil, w_up_sb, gate_b, w_out, norm2_g, w_ff1, w_ff2, loss_target, m_norm1_g, m_w_in, m_q_norm_g, m_k_norm_g, m_w_up_dil, m_w_up_sb, m_gate_b, m_w_out, m_norm2_g, m_w_ff1, m_w_ff2, v_norm1_g, v_w_in, v_q_norm_g, v_k_norm_g, v_w_up_dil, v_w_up_sb, v_gate_b, v_w_out, v_norm2_g, v_w_ff1, v_w_ff2):
    weights = dict(norm1_g=norm1_g, w_in=w_in, q_norm_g=q_norm_g, k_norm_g=k_norm_g, w_up_dil=w_up_dil,
                   w_up_sb=w_up_sb, gate_b=gate_b, w_out=w_out, norm2_g=norm2_g, w_ff1=w_ff1, w_ff2=w_ff2)
    moments_m = dict(norm1_g=m_norm1_g, w_in=m_w_in, q_norm_g=m_q_norm_g, k_norm_g=m_k_norm_g, w_up_dil=m_w_up_dil,
                     w_up_sb=m_w_up_sb, gate_b=m_gate_b, w_out=m_w_out, norm2_g=m_norm2_g, w_ff1=m_w_ff1, w_ff2=m_w_ff2)
    moments_v = dict(norm1_g=v_norm1_g, w_in=v_w_in, q_norm_g=v_q_norm_g, k_norm_g=v_k_norm_g, w_up_dil=v_w_up_dil,
                     w_up_sb=v_w_up_sb, gate_b=v_gate_b, w_out=v_w_out, norm2_g=v_norm2_g, w_ff1=v_w_ff1, w_ff2=v_w_ff2)
    depth = w_in.shape[0]
    seq, d_model = x.shape[1], x.shape[2]
    chip = 2 * lax.axis_index("x") + lax.axis_index("y")
    core = lax.axis_index("c")
    which = jnp.reshape(core, (1,)).astype(jnp.int32)

    chip_index = jnp.reshape(chip, (1,)).astype(jnp.int32)
    shards = {n: weights[n].astype(BF16) for n in BIG}
    bias_buf, ((_, bias_rows),) = _pack_rows([gate_b])
    bias_slots = _exchange_small(bias_buf, name="gather_gate_b")
    bias = bias_slots[0::2, :bias_rows].reshape(N_CHIPS, depth, 2, d_model // N_CHIPS)
    bias = jnp.transpose(bias, (1, 2, 0, 3)).reshape(depth, 2, d_model)

    def placed(names, l, landed):
        w = {}
        for n in names:
            w[n] = _place_piece(landed[n][None], shards[n][l:l + 1], chip_index, name=f"gather_place_{n}_l{l}")[0]
            if n in ROW_SHARDED:
                w[n] = w[n].reshape(-1, w[n].shape[-1])
        return w

    def ici_side(names, l):
        return _gather_ici_side([shards[n] for n in names], l)

    def by_chip(n, g):
        return g.reshape(N_CHIPS, -1, g.shape[-1]) if n in ROW_SHARDED else g

    ahead = {"ff1": ("w_in",), "out": ("w_up_dil", "w_up_sb", "w_out"), "ff2": ("w_ff1",)}
    cos, sin = _rope_tables(seq)
    first = ("w_in",)
    ready = placed(first, 0, dict(zip(first, _gather_layer([shards[n] for n in first], 0, name="gather_l0_w_in"))))
    pending = tuple(n for n in BIG if n not in first)
    act = x[0]
    saved = []
    for l in range(depth):
        def rest(landed, l=l, pending=pending):
            passed = _gather_pass(list(landed), name=f"gather_pass_l{l}_rest")
            return placed(pending, l, dict(zip(pending, passed)))

        sides = {"proj": ici_side(pending, l)}
        if l + 1 < depth:
            sides.update({k: ici_side(names, l + 1) for k, names in ahead.items()})
        w = dict(ready, norm1_g=norm1_g[l], norm2_g=norm2_g[l], q_norm_g=q_norm_g[l], k_norm_g=k_norm_g[l], gate_b=bias[l])
        act, sv, got = _layer_fwd(act, w, cos, sin, f"l{l}", sides, rest)
        saved.append(sv)
        if l + 1 < depth:
            names = [n for k in ahead for n in ahead[k]]
            passed = _gather_pass([t for k in ahead for t in got[k]], name=f"gather_pass_l{l + 1}")
            ready = placed(names, l + 1, dict(zip(names, passed)))
            pending = ("w_ff2",)
    loss, dx, dxb = _loss_head(act, loss_target[0])
    loss = lax.psum(loss, ("x", "y", "c"))

    below = {"d_a": ("w_in",), "dw_ff2": ("w_ff1", "w_out", "w_up_dil", "w_up_sb"), "d_h2": ("w_ff2",)}
    own = {"d_h": ("w_ff2", "w_ff1"), "dw_in": ("w_out", "w_up_dil", "w_up_sb")}
    grads, reduced = [None] * depth, [None] * depth
    pair = None
    for l in reversed(range(depth)):
        sides = {k: _chip_side([pair[n] for n in names]) for k, names in below.items()} if pair else None
        early = {}

        def late(g, early=early):
            for k, names in own.items():
                early.update(_reduce_begin({n: by_chip(n, g[n]) for n in names}, which, f"l0_{k}"))
            return {k: _chip_side([early[n] for n in names]) for k, names in own.items()}

        dx, dxb, grads[l], got = _layer_bwd(dx, dxb, saved[l], cos, sin, f"l{l}", sides, late if l == 0 else None)
        if pair:
            parts = {n: t for k, names in below.items() for n, t in zip(names, got[k])}
            reduced[l + 1] = _reduce_end(pair, parts, which, chip_index, f"l{l + 1}")
        if l > 0:
            pair = _reduce_begin({n: by_chip(n, grads[l][n]) for n in BIG}, which, f"l{l}")
    last = _reduce_begin({"w_in": grads[0]["w_in"]}, which, "l0_w_in")
    parts = {n: t for k, names in own.items() for n, t in zip(names, got[k])}
    parts["w_in"] = _chip_exchange([last["w_in"]], name="rs_chips_l0_w_in")[0]
    pair = dict(early, **last)
    reduced[0] = _reduce_end({n: pair[n] for n in BIG}, parts, which, chip_index, "l0")
    final = {n: jnp.stack([reduced[l][n] for l in range(depth)]) for n in BIG}

    small_names = ("norm1_g", "norm2_g", "q_norm_g", "k_norm_g", "gate_b")
    packed, spans = _pack_rows([grads[l][n] for l in range(depth) for n in small_names])
    total = _sum_slots(_exchange_small(packed, name="exchange_small_grads"), name="sum_small_grads")
    pieces = iter(total[at:at + r] for at, r in spans)
    small = {n: [] for n in small_names}
    for l in range(depth):
        for n in small_names:
            small[n].append(next(pieces))
    for n in ("norm1_g", "norm2_g"):
        final[n] = jnp.stack([p.reshape(d_model) for p in small[n]])
    for n in ("q_norm_g", "k_norm_g"):
        final[n] = jnp.stack([p.reshape(N_GROUPS, HEAD_DIM) for p in small[n]])
    shard_cols = d_model // N_CHIPS
    final["gate_b"] = jnp.stack([lax.dynamic_slice_in_dim(p.reshape(2, d_model), chip * shard_cols, shard_cols, axis=1)
                                 for p in small["gate_b"]])

    deltas, new_m, new_v = {}, {}, {}
    for n in WEIGHTS:
        deltas[n], new_m[n], new_v[n] = _adamw(weights[n], final[n], moments_m[n], moments_v[n], name=f"adamw_{n}")
    return (loss, dx[None], *[final[n] for n in WEIGHTS], *[deltas[n] for n in WEIGHTS],
            *[new_m[n] for n in WEIGHTS], *[new_v[n] for n in WEIGHTS])
```

```python
import functools
import math

import jax
import jax.numpy as jnp
from jax import lax
from jax.experimental import pallas as pl
from jax.experimental.pallas import tpu as pltpu

F32 = jnp.float32
BF16 = jnp.bfloat16

HEAD_DIM = 128
DIL_GROUPS = ((128, 1), (512, 4), (2048, 16))
N_GROUPS = 3
HEADS_PER_GROUP = 4
GROUP_WIDTH = HEADS_PER_GROUP * HEAD_DIM
DIL_WIDTH = N_GROUPS * GROUP_WIDTH
SB_HEADS = 8
SB_WIDTH = SB_HEADS * HEAD_DIM
QKV_WIDTH = 3 * DIL_WIDTH + 3 * SB_WIDTH
BLOCK = 128
ROPE_THETA = 10000.0
EPS = 1e-6
SCALE = 1.0 / math.sqrt(HEAD_DIM)
NEG_INF = float("-inf")

ADAM_LR = 0.001
ADAM_B1 = 0.9
ADAM_B2 = 0.999
ADAM_EPS = 1e-08
ADAM_WD = 0.01
ADAM_STEP = 10

N_CHIPS = 4
MESH = pl.DeviceIdType.MESH
MIB = 1024 * 1024
VMEM_LIMIT = 56 * MIB


def _params(semantics=None, vmem=VMEM_LIMIT):
    return pltpu.CompilerParams(dimension_semantics=semantics, vmem_limit_bytes=vmem)


def _pick(n, pref):
    if n <= pref:
        return n
    t = 1 << (pref.bit_length() - 1)
    while n % t:
        t //= 2
    return t


def _tile(n, pref):
    best = 0
    for t in range(256, min(n, pref) + 1, 256):
        if n % t == 0:
            best = t
    return best if best else n


def _matmul(a, b, *, mode, name, out_dtypes=(F32,), extras=(), epilogue=None, b_sharded=False, out_sharded=False,
            tm=1024, tn=1024, tk=2048, side=None):
    if mode == "nn":
        m, kdim = a.shape
        n = b.shape[-1] * (N_CHIPS if b_sharded else 1)
        assert (b.shape[-2] if b_sharded else b.shape[0]) == kdim
    elif mode == "nt":
        m, kdim = a.shape
        n = b.shape[-2]
        assert b.shape[-1] * (N_CHIPS if b_sharded else 1) == kdim
    else:
        kdim, m = a.shape
        n = b.shape[1]
        assert b.shape[0] == kdim and not b_sharded
    shard_n = n // N_CHIPS if (out_sharded or (b_sharded and mode == "nn")) else n
    shard_k = kdim // N_CHIPS if (b_sharded and mode == "nt") else kdim
    tm = _tile(m, tm)
    tn = _tile(shard_n, tn)
    tk = _tile(shard_k, tk)
    nk = kdim // tk
    nj_n = shard_n // tn
    nj_k = shard_k // tk
    j_outer = nk == 1 and (kdim * n + m * kdim * (n // tn)) < (m * kdim + kdim * n * (m // tm))
    grid = (n // tn, m // tm, nk) if j_outer else (m // tm, n // tn, nk)

    def at(index):
        return (lambda g0, g1, k: index(g1, g0, k)) if j_outer else index

    if mode == "nn":
        a_spec = pl.BlockSpec((tm, tk), at(lambda i, j, k: (i, k)))
        if b_sharded:
            b_spec = pl.BlockSpec((None, tk, tn), at(lambda i, j, k: (j // nj_n, k, j % nj_n)))
        else:
            b_spec = pl.BlockSpec((tk, tn), at(lambda i, j, k: (k, j)))
        dims = (((1,), (0,)), ((), ()))
    elif mode == "nt":
        a_spec = pl.BlockSpec((tm, tk), at(lambda i, j, k: (i, k)))
        if b_sharded:
            b_spec = pl.BlockSpec((None, tn, tk), at(lambda i, j, k: (k // nj_k, j, k % nj_k)))
        else:
            b_spec = pl.BlockSpec((tn, tk), at(lambda i, j, k: (j, k)))
        dims = (((1,), (1,)), ((), ()))
    else:
        a_spec = pl.BlockSpec((tk, tm), at(lambda i, j, k: (k, i)))
        b_spec = pl.BlockSpec((tk, tn), at(lambda i, j, k: (k, j)))
        dims = (((0,), (0,)), ((), ()))
    if out_sharded:
        o_spec = pl.BlockSpec((None, tm, tn), at(lambda i, j, k: (j // nj_n, i, j % nj_n)))
        o_shape = (N_CHIPS, m, shard_n)
    else:
        o_spec = pl.BlockSpec((tm, tn), at(lambda i, j, k: (i, j)))
        o_shape = (m, n)
    x_spec = pl.BlockSpec((tm, tn), at(lambda i, j, k: (i, j)))
    n_extra = len(extras)
    n_out = len(out_dtypes)

    side_ins, side_shapes, side_copies, n_copies = side if side else ((), (), None, 0)
    n_side_in, n_side_out = len(side_ins), len(side_shapes)

    def finish(acc, extra_refs, out_refs):
        res = (acc,) if epilogue is None else epilogue(acc, *[e[...] for e in extra_refs])
        for o, r in zip(out_refs, res):
            o[...] = r.astype(o.dtype)

    def body(a_ref, b_ref, *rest):
        extra_refs = rest[:n_extra]
        side_in_refs = rest[n_extra:n_extra + n_side_in]
        rest = rest[n_extra + n_side_in:]
        out_refs, side_out_refs, scratch = rest[:n_out], rest[n_out:n_out + n_side_out], rest[n_out + n_side_out:]
        if side:
            step = (pl.program_id(0) * grid[1] + pl.program_id(1)) * grid[2] + pl.program_id(2)
            send_sems, recv_sems = scratch[-2:]

            @pl.when(step == 0)
            def _():
                sends, _ = side_copies(side_in_refs, side_out_refs, send_sems, recv_sems)
                for cp in sends:
                    cp.start()

        if nk == 1:
            finish(lax.dot_general(a_ref[...], b_ref[...], dims, preferred_element_type=F32), extra_refs, out_refs)
        else:
            acc = scratch[0]
            k = pl.program_id(2)

            @pl.when(k == 0)
            def _():
                acc[...] = jnp.zeros_like(acc)

            acc[...] += lax.dot_general(a_ref[...], b_ref[...], dims, preferred_element_type=F32)

            @pl.when(k == nk - 1)
            def _():
                finish(acc[...], extra_refs, out_refs)

        if side:
            @pl.when(step == grid[0] * grid[1] * grid[2] - 1)
            def _():
                sends, recvs = side_copies(side_in_refs, side_out_refs, send_sems, recv_sems)
                for cp in recvs:
                    cp.wait_recv()
                for cp in sends:
                    cp.wait_send()

    scratch_shapes = [] if nk == 1 else [pltpu.VMEM((tm, tn), F32)]
    if side:
        scratch_shapes += [pltpu.SemaphoreType.DMA((n_copies,)), pltpu.SemaphoreType.DMA((n_copies,))]
    outs = pl.pallas_call(
        body,
        name=name,
        grid=grid,
        in_specs=[a_spec, b_spec] + [x_spec] * n_extra + [ANY] * n_side_in,
        out_specs=[o_spec] * n_out + [ANY] * n_side_out,
        out_shape=[jax.ShapeDtypeStruct(o_shape, dt) for dt in out_dtypes] + list(side_shapes),
        scratch_shapes=scratch_shapes,
        compiler_params=_params(("arbitrary",) * 3 if side else ("parallel", "parallel", "arbitrary")),
    )(a, b, *extras, *side_ins)
    if side:
        return (outs[0] if n_out == 1 else outs[:n_out]), outs[n_out:]
    return outs[0] if n_out == 1 else outs


def _rmsnorm_fwd(x, g, *, name):
    s, d = x.shape
    tr = _pick(s, 256)

    def body(x_ref, g_ref, o_ref):
        xv = x_ref[...]
        r = lax.rsqrt(jnp.mean(xv * xv, axis=-1, keepdims=True) + EPS)
        o_ref[...] = (xv * r * g_ref[...]).astype(o_ref.dtype)

    return pl.pallas_call(
        body, name=name, grid=(s // tr,),
        in_specs=[pl.BlockSpec((tr, d), lambda i: (i, 0)), pl.BlockSpec((1, d), lambda i: (0, 0))],
        out_specs=pl.BlockSpec((tr, d), lambda i: (i, 0)),
        out_shape=jax.ShapeDtypeStruct((s, d), BF16),
        compiler_params=_params(("parallel",)),
    )(x, g.reshape(1, d))


def _rmsnorm_bwd(x, g, dh, dres, *, name):
    s, d = x.shape
    tr = _pick(s, 256)

    def body(x_ref, g_ref, dh_ref, dres_ref, dx_ref, dxb_ref, dg_ref):
        xv = x_ref[...]
        r = lax.rsqrt(jnp.mean(xv * xv, axis=-1, keepdims=True) + EPS)
        y = xv * r
        dhv = dh_ref[...]
        dy = dhv * g_ref[...]
        dx = dres_ref[...] + r * (dy - y * jnp.mean(dy * y, axis=-1, keepdims=True))
        dx_ref[...] = dx
        dxb_ref[...] = dx.astype(BF16)

        @pl.when(pl.program_id(0) == 0)
        def _():
            dg_ref[...] = jnp.zeros_like(dg_ref)

        dg_ref[...] += jnp.sum(dhv * y, axis=0, keepdims=True)

    row = pl.BlockSpec((tr, d), lambda i: (i, 0))
    vec = pl.BlockSpec((1, d), lambda i: (0, 0))
    dx, dxb, dg = pl.pallas_call(
        body, name=name, grid=(s // tr,),
        in_specs=[row, vec, row, row],
        out_specs=[row, row, vec],
        out_shape=[jax.ShapeDtypeStruct((s, d), F32), jax.ShapeDtypeStruct((s, d), BF16),
                   jax.ShapeDtypeStruct((1, d), F32)],
        compiler_params=_params(("arbitrary",)),
    )(x, g.reshape(1, d), dh, dres)
    return dx, dxb, dg.reshape(d)


def _loss_head(y, target):
    s, d = y.shape
    tr = _pick(s, 256)

    def body(y_ref, t_ref, dy_ref, dyb_ref, part_ref):
        err = y_ref[...] - t_ref[...]
        dy = err * (1.0 / d)
        dy_ref[...] = dy
        dyb_ref[...] = dy.astype(BF16)

        @pl.when(pl.program_id(0) == 0)
        def _():
            part_ref[...] = jnp.zeros_like(part_ref)

        part_ref[...] += jnp.sum(err * err, axis=0, keepdims=True)

    row = pl.BlockSpec((tr, d), lambda i: (i, 0))
    vec = pl.BlockSpec((1, d), lambda i: (0, 0))
    dy, dyb, part = pl.pallas_call(
        body, name="loss_head", grid=(s // tr,),
        in_specs=[row, row], out_specs=[row, row, vec],
        out_shape=[jax.ShapeDtypeStruct((s, d), F32), jax.ShapeDtypeStruct((s, d), BF16),
                   jax.ShapeDtypeStruct((1, d), F32)],
        compiler_params=_params(("arbitrary",)),
    )(y, target)
    return 0.5 * jnp.sum(part) / d, dy, dyb


def _rope_tables(s):
    half = HEAD_DIM // 2
    inv_freq = ROPE_THETA ** (-jnp.arange(half, dtype=F32) / half)
    ang = jnp.arange(s, dtype=F32)[:, None] * inv_freq[None, :]
    cos, sin = jnp.cos(ang), jnp.sin(ang)
    return jnp.concatenate([cos, cos], axis=-1), jnp.concatenate([-sin, sin], axis=-1)


PREP_ROWS = 256


def _residue_spec(r, width):
    return pl.BlockSpec((r, PREP_ROWS // r, width), lambda i: (0, i, 0))


def _to_residues(scr, tile, r, dst_ref, cols, dtype):
    scr[...] = tile
    for c in range(r):
        dst_ref[c, :, cols] = scr[pl.ds(c, PREP_ROWS // r, stride=r), :].astype(dtype)


def _from_residues(scr, src_ref, cols, r):
    for c in range(r):
        scr[pl.ds(c, PREP_ROWS // r, stride=r), :] = src_ref[c, :, cols].astype(F32)
    return scr[...]


def _qkv_prep(proj, qg, kg, cos, sin, *, name):
    s = proj.shape[0]
    tr = PREP_ROWS
    assert s % tr == 0

    def body(p_ref, qg_ref, kg_ref, cos_ref, sin_ref, o_ref, res1_ref, res2_ref, scr):
        res_refs = (None, res1_ref, res2_ref)
        c = cos_ref[...]
        sn = sin_ref[...]
        for which, g_ref in ((0, qg_ref), (1, kg_ref)):
            for g in range(N_GROUPS):
                gain = g_ref[g:g + 1, :]
                for h in range(HEADS_PER_GROUP):
                    off = which * DIL_WIDTH + g * GROUP_WIDTH + h * HEAD_DIM
                    xv = p_ref[:, off:off + HEAD_DIM]
                    r = lax.rsqrt(jnp.mean(xv * xv, axis=-1, keepdims=True) + EPS)
                    y = xv * r * gain
                    rot = y * c + pltpu.roll(y, HEAD_DIM // 2, 1) * sn
                    if g == 0:
                        o_ref[:, off:off + HEAD_DIM] = rot.astype(BF16)
                    else:
                        at = which * GROUP_WIDTH + h * HEAD_DIM
                        _to_residues(scr, rot, DIL_GROUPS[g][1], res_refs[g], slice(at, at + HEAD_DIM), BF16)
        o_ref[:, 2 * DIL_WIDTH:2 * DIL_WIDTH + GROUP_WIDTH] = p_ref[:, 2 * DIL_WIDTH:2 * DIL_WIDTH + GROUP_WIDTH].astype(BF16)
        for g in range(1, N_GROUPS):
            for h in range(HEADS_PER_GROUP):
                off = 2 * DIL_WIDTH + g * GROUP_WIDTH + h * HEAD_DIM
                at = 2 * GROUP_WIDTH + h * HEAD_DIM
                _to_residues(scr, p_ref[:, off:off + HEAD_DIM], DIL_GROUPS[g][1], res_refs[g], slice(at, at + HEAD_DIM), BF16)
        o_ref[:, 3 * DIL_WIDTH:] = p_ref[:, 3 * DIL_WIDTH:].astype(BF16)

    r1, r2 = DIL_GROUPS[1][1], DIL_GROUPS[2][1]
    return pl.pallas_call(
        body, name=name, grid=(s // tr,),
        in_specs=[pl.BlockSpec((tr, QKV_WIDTH), lambda i: (i, 0)),
                  pl.BlockSpec((N_GROUPS, HEAD_DIM), lambda i: (0, 0)),
                  pl.BlockSpec((N_GROUPS, HEAD_DIM), lambda i: (0, 0)),
                  pl.BlockSpec((tr, HEAD_DIM), lambda i: (i, 0)),
                  pl.BlockSpec((tr, HEAD_DIM), lambda i: (i, 0))],
        out_specs=[pl.BlockSpec((tr, QKV_WIDTH), lambda i: (i, 0)),
                   _residue_spec(r1, 3 * GROUP_WIDTH), _residue_spec(r2, 3 * GROUP_WIDTH)],
        out_shape=[jax.ShapeDtypeStruct((s, QKV_WIDTH), BF16),
                   jax.ShapeDtypeStruct((r1, s // r1, 3 * GROUP_WIDTH), BF16),
                   jax.ShapeDtypeStruct((r2, s // r2, 3 * GROUP_WIDTH), BF16)],
        scratch_shapes=[pltpu.VMEM((tr, HEAD_DIM), F32)],
        compiler_params=_params(("parallel",)),
    )(proj, qg, kg, cos, sin)


def _qkv_prep_bwd(proj, qg, kg, cos, sin, dqkv_dil, dqkv_sb, dgate, *, name):
    s, n_in = proj.shape
    n_gate = dgate[0].shape[1]
    tr = PREP_ROWS

    def body(p_ref, qg_ref, kg_ref, cos_ref, sin_ref, *rest):
        dil_refs = rest[:9]
        sb_refs = rest[9:12]
        dgate_refs = rest[12:14]
        o_ref, dqg_ref, dkg_ref, scr = rest[14:18]
        c = cos_ref[...]
        sn = sin_ref[...]

        def incoming(g, which, h):
            cols = slice(h * HEAD_DIM, (h + 1) * HEAD_DIM)
            if g == 0:
                return dil_refs[which][:, cols]
            return _from_residues(scr, dil_refs[3 * g + which], cols, DIL_GROUPS[g][1])

        @pl.when(pl.program_id(0) == 0)
        def _():
            dqg_ref[...] = jnp.zeros_like(dqg_ref)
            dkg_ref[...] = jnp.zeros_like(dkg_ref)

        for which, g_ref, dg_ref in ((0, qg_ref, dqg_ref), (1, kg_ref, dkg_ref)):
            for g in range(N_GROUPS):
                gain = g_ref[g:g + 1, :]
                dgain = jnp.zeros((1, HEAD_DIM), F32)
                for h in range(HEADS_PER_GROUP):
                    off = which * DIL_WIDTH + g * GROUP_WIDTH + h * HEAD_DIM
                    xv = p_ref[:, off:off + HEAD_DIM]
                    r = lax.rsqrt(jnp.mean(xv * xv, axis=-1, keepdims=True) + EPS)
                    nx = xv * r
                    dout = incoming(g, which, h)
                    dy = dout * c + pltpu.roll(dout * sn, HEAD_DIM // 2, 1)
                    dgain = dgain + jnp.sum(dy * nx, axis=0, keepdims=True)
                    dn = dy * gain
                    dxv = r * (dn - nx * jnp.mean(dn * nx, axis=-1, keepdims=True))
                    o_ref[:, off:off + HEAD_DIM] = dxv.astype(BF16)
                dg_ref[g:g + 1, :] += dgain
        for g in range(N_GROUPS):
            for h in range(HEADS_PER_GROUP):
                off = 2 * DIL_WIDTH + g * GROUP_WIDTH + h * HEAD_DIM
                o_ref[:, off:off + HEAD_DIM] = incoming(g, 2, h).astype(BF16)
        for t in range(3):
            off = 3 * DIL_WIDTH + t * SB_WIDTH
            o_ref[:, off:off + SB_WIDTH] = sb_refs[t][...].astype(BF16)
        o_ref[:, QKV_WIDTH:QKV_WIDTH + n_gate] = dgate_refs[0][...]
        o_ref[:, QKV_WIDTH + n_gate:] = dgate_refs[1][...]

    grp = pl.BlockSpec((tr, GROUP_WIDTH), lambda i: (i, 0))
    sbs = pl.BlockSpec((tr, SB_WIDTH), lambda i: (i, 0))
    gain_spec = pl.BlockSpec((N_GROUPS, HEAD_DIM), lambda i: (0, 0))
    tab = pl.BlockSpec((tr, HEAD_DIM), lambda i: (i, 0))
    flat_dil = [t for grp3 in dqkv_dil for t in grp3]
    dil_specs = [grp] * 3 + [_residue_spec(DIL_GROUPS[g][1], GROUP_WIDTH) for g in (1, 2) for _ in range(3)]
    return pl.pallas_call(
        body, name=name, grid=(s // tr,),
        in_specs=[pl.BlockSpec((tr, 2 * DIL_WIDTH), lambda i: (i, 0)), gain_spec, gain_spec, tab, tab]
                 + dil_specs + [sbs] * 3 + [pl.BlockSpec((tr, n_gate), lambda i: (i, 0))] * 2,
        out_specs=[pl.BlockSpec((tr, n_in), lambda i: (i, 0)), gain_spec, gain_spec],
        out_shape=[jax.ShapeDtypeStruct((s, n_in), BF16),
                   jax.ShapeDtypeStruct((N_GROUPS, HEAD_DIM), F32),
                   jax.ShapeDtypeStruct((N_GROUPS, HEAD_DIM), F32)],
        scratch_shapes=[pltpu.VMEM((tr, HEAD_DIM), F32)],
        compiler_params=_params(("arbitrary",)),
    )(proj, qg, kg, cos, sin, *flat_dil, *dqkv_sb, *dgate)


def _dil_masks(i):
    row = lax.broadcasted_iota(jnp.int32, (BLOCK, BLOCK), 0)
    col = lax.broadcasted_iota(jnp.int32, (BLOCK, BLOCK), 1)
    return col <= row, (col >= row) & (i > 0)


def _dil_sub(nb):
    return 2 if nb % 2 == 0 else 1


def _dot_nt(a, b):
    return lax.dot_general(a, b, (((1,), (1,)), ((), ())), preferred_element_type=F32)


def _dot_tn(a, b):
    return lax.dot_general(a, b, (((0,), (0,)), ((), ())), preferred_element_type=F32)


def _dot(a, b):
    return jnp.dot(a, b, preferred_element_type=F32)


def _dil_attn_fwd(src, g, cols, *, name):
    s = src.shape[0]
    r = DIL_GROUPS[g][1]
    nb = s // r // BLOCK
    sub = _dil_sub(nb)
    steps = nb // sub

    def body(q_ref, kp_ref, kc_ref, vp_ref, vc_ref, o_ref, ld_ref):
        i = pl.program_id(1)
        work = []
        for u in range(sub):
            rows = slice(u * BLOCK, (u + 1) * BLOCK)
            before = slice((u - 1) * BLOCK, u * BLOCK)
            mask_c, mask_p = _dil_masks(i if u == 0 else 1)
            for h in range(HEADS_PER_GROUP):
                sl = slice(h * HEAD_DIM, (h + 1) * HEAD_DIM)
                k_prev = kp_ref[:, sl] if u == 0 else kc_ref[before, sl]
                v_prev = vp_ref[:, sl] if u == 0 else vc_ref[before, sl]
                work.append((rows, sl, mask_c, mask_p, v_prev,
                             _dot_nt(q_ref[rows, sl], kc_ref[rows, sl]), _dot_nt(q_ref[rows, sl], k_prev)))
        probs = []
        for rows, sl, mask_c, mask_p, _, raw_c, raw_p in work:
            s_c = jnp.where(mask_c, raw_c * SCALE, NEG_INF)
            s_p = jnp.where(mask_p, raw_p * SCALE, NEG_INF)
            m = jnp.maximum(jnp.max(s_c, axis=-1, keepdims=True), jnp.max(s_p, axis=-1, keepdims=True))
            p_c = jnp.exp(s_c - m)
            p_p = jnp.exp(s_p - m)
            l = jnp.sum(p_c, axis=-1, keepdims=True) + jnp.sum(p_p, axis=-1, keepdims=True)
            inv = 1.0 / l
            probs.append(((p_c * inv).astype(BF16), (p_p * inv).astype(BF16)))
            ld_ref[rows, sl] = jnp.broadcast_to(m + jnp.log(l), (BLOCK, HEAD_DIM))
        for (rows, sl, _, _, v_prev, _, _), (pn_c, pn_p) in zip(work, probs):
            o_ref[rows, sl] = _dot(pn_c, vc_ref[rows, sl]) + _dot(pn_p, v_prev)

    def col(which):
        return lambda c, i: (c * steps + i, cols[which])

    def col_prev(which):
        return lambda c, i: (c * nb + jnp.maximum(sub * i - 1, 0), cols[which])

    own = (sub * BLOCK, GROUP_WIDTH)
    one = (BLOCK, GROUP_WIDTH)
    return pl.pallas_call(
        body, name=name, grid=(r, steps),
        in_specs=[pl.BlockSpec(own, col(0)), pl.BlockSpec(one, col_prev(1)), pl.BlockSpec(own, col(1)),
                  pl.BlockSpec(one, col_prev(2)), pl.BlockSpec(own, col(2))],
        out_specs=[pl.BlockSpec(own, lambda c, i: (c * steps + i, 0))] * 2,
        out_shape=[jax.ShapeDtypeStruct((s, GROUP_WIDTH), F32)] * 2,
        compiler_params=_params(("parallel", "parallel")),
    )(src, src, src, src, src)


def _dil_merge(outs, lds, *, name):
    s = outs[0].shape[0]
    tr = PREP_ROWS
    rs = [DIL_GROUPS[g][1] for g in range(N_GROUPS)]

    def body(o0, o1, o2, l0, l1, l2, yb_ref, y0_ref, y1_ref, y2_ref, e0_ref, e1_ref, e2_ref, scr_a, scr_b):
        o_refs, l_refs = (o0, o1, o2), (l0, l1, l2)
        y_refs, e_refs = (y0_ref, y1_ref, y2_ref), (e0_ref, e1_ref, e2_ref)
        for h in range(HEADS_PER_GROUP):
            cols = slice(h * HEAD_DIM, (h + 1) * HEAD_DIM)
            o = [o_refs[0][:, cols]] + [_from_residues(scr_a, o_refs[g], cols, rs[g]) for g in (1, 2)]
            a = [l_refs[0][:, cols]] + [_from_residues(scr_b, l_refs[g], cols, rs[g]) for g in (1, 2)]
            m = jnp.maximum(jnp.maximum(a[0], a[1]), a[2])
            e = [jnp.exp(t - m) for t in a]
            tot = e[0] + e[1] + e[2]
            inv = 1.0 / tot
            y = (e[0] * inv) * o[0] + (e[1] * inv) * o[1] + (e[2] * inv) * o[2]
            lse = m + jnp.log(tot)
            yb_ref[:, cols] = y.astype(BF16)
            y_refs[0][:, cols] = y
            e_refs[0][:, cols] = lse
            for g in (1, 2):
                _to_residues(scr_a, y, rs[g], y_refs[g], cols, F32)
                _to_residues(scr_b, lse, rs[g], e_refs[g], cols, F32)

    blk = pl.BlockSpec((tr, GROUP_WIDTH), lambda i: (i, 0))
    specs = [blk] + [_residue_spec(rs[g], GROUP_WIDTH) for g in (1, 2)]
    shapes = [jax.ShapeDtypeStruct((s, GROUP_WIDTH), F32)] + [
        jax.ShapeDtypeStruct((rs[g], s // rs[g], GROUP_WIDTH), F32) for g in (1, 2)]
    res = pl.pallas_call(
        body, name=name, grid=(s // tr,),
        in_specs=specs * 2, out_specs=[blk] + specs * 2,
        out_shape=[jax.ShapeDtypeStruct((s, GROUP_WIDTH), BF16)] + shapes * 2,
        scratch_shapes=[pltpu.VMEM((tr, HEAD_DIM), F32), pltpu.VMEM((tr, HEAD_DIM), F32)],
        compiler_params=_params(("parallel",)),
    )(*[t.reshape(sh.shape) for t, sh in zip(outs, shapes)], *[t.reshape(sh.shape) for t, sh in zip(lds, shapes)])
    yb, ys, lses = res[0], res[1:4], res[4:7]
    return yb, [t.reshape(s, GROUP_WIDTH) for t in ys], [t.reshape(s, GROUP_WIDTH) for t in lses]


def _dil_to_residues(x, *, name):
    s = x.shape[0]
    tr = PREP_ROWS
    rs = [DIL_GROUPS[g][1] for g in (1, 2)]

    def body(x_ref, a_ref, b_ref, scr):
        for h in range(HEADS_PER_GROUP):
            cols = slice(h * HEAD_DIM, (h + 1) * HEAD_DIM)
            tile = x_ref[:, cols]
            _to_residues(scr, tile, rs[0], a_ref, cols, F32)
            _to_residues(scr, tile, rs[1], b_ref, cols, F32)

    outs = pl.pallas_call(
        body, name=name, grid=(s // tr,),
        in_specs=[pl.BlockSpec((tr, GROUP_WIDTH), lambda i: (i, 0))],
        out_specs=[_residue_spec(r, GROUP_WIDTH) for r in rs],
        out_shape=[jax.ShapeDtypeStruct((r, s // r, GROUP_WIDTH), F32) for r in rs],
        scratch_shapes=[pltpu.VMEM((tr, HEAD_DIM), F32)],
        compiler_params=_params(("parallel",)),
    )(x)
    return [t.reshape(s, GROUP_WIDTH) for t in outs]


def _dil_attn_bwd(src, g, cols, y, lse, dy, *, name):
    s = src.shape[0]
    r = DIL_GROUPS[g][1]
    nb = s // r // BLOCK
    sub = _dil_sub(nb)
    steps = nb // sub

    def body(q_ref, qn_ref, kp_ref, kc_ref, vp_ref, vc_ref, y_ref, yn_ref, l_ref, ln_ref, d_ref, dn_ref,
             dq_ref, dk_ref, dv_ref):
        i = pl.program_id(1)
        work = []
        for u in range(sub):
            rows = slice(u * BLOCK, (u + 1) * BLOCK)
            before = slice((u - 1) * BLOCK, u * BLOCK)
            after = slice((u + 1) * BLOCK, (u + 2) * BLOCK)
            last = u == sub - 1
            mask_c, mask_p = _dil_masks(i if u == 0 else 1)
            _, mask_n = _dil_masks(jnp.where(sub * i + u + 1 < nb, 1, 0) if last else 1)
            for h in range(HEADS_PER_GROUP):
                sl = slice(h * HEAD_DIM, (h + 1) * HEAD_DIM)
                nxt = lambda own_ref, next_ref, sl=sl: next_ref[:, sl] if last else own_ref[after, sl]
                work.append(dict(
                    rows=rows, sl=sl, masks=(mask_c, mask_p, mask_n),
                    q=q_ref[rows, sl], qn=nxt(q_ref, qn_ref), kc=kc_ref[rows, sl], vc=vc_ref[rows, sl],
                    kp=kp_ref[:, sl] if u == 0 else kc_ref[before, sl],
                    vp=vp_ref[:, sl] if u == 0 else vc_ref[before, sl],
                    dy=d_ref[rows, sl], dyn=nxt(d_ref, dn_ref), y=y_ref[rows, sl], yn=nxt(y_ref, yn_ref),
                    lse=l_ref[rows, sl], lsen=nxt(l_ref, ln_ref)))
        for t in work:
            t["dyb"], t["dynb"] = t["dy"].astype(BF16), t["dyn"].astype(BF16)
            t["first"] = (_dot_nt(t["q"], t["kc"]), _dot_nt(t["q"], t["kp"]), _dot_nt(t["qn"], t["kc"]),
                          _dot_nt(t["dyb"], t["vc"]), _dot_nt(t["dyb"], t["vp"]), _dot_nt(t["dynb"], t["vc"]))
        for t in work:
            s_c, s_p, s_n, dp_c, dp_p, dp_n = t["first"]
            mask_c, mask_p, mask_n = t["masks"]
            delta = jnp.sum(t["dy"] * t["y"], axis=-1, keepdims=True)
            delta_n = jnp.sum(t["dyn"] * t["yn"], axis=-1, keepdims=True)
            p_c = jnp.where(mask_c, jnp.exp(s_c * SCALE - t["lse"]), 0.0)
            p_p = jnp.where(mask_p, jnp.exp(s_p * SCALE - t["lse"]), 0.0)
            p_n = jnp.where(mask_n, jnp.exp(s_n * SCALE - t["lsen"]), 0.0)
            t["second"] = ((p_c * (dp_c - delta) * SCALE).astype(BF16), (p_p * (dp_p - delta) * SCALE).astype(BF16),
                           (p_n * (dp_n - delta_n) * SCALE).astype(BF16), p_c.astype(BF16), p_n.astype(BF16))
        for t in work:
            ds_c, ds_p, ds_n, pb_c, pb_n = t["second"]
            rows, sl = t["rows"], t["sl"]
            dq_ref[rows, sl] = _dot(ds_c, t["kc"]) + _dot(ds_p, t["kp"])
            dk_ref[rows, sl] = _dot_tn(ds_c, t["q"]) + _dot_tn(ds_n, t["qn"])
            dv_ref[rows, sl] = _dot_tn(pb_c, t["dyb"]) + _dot_tn(pb_n, t["dynb"])

    def col(which):
        return lambda c, i: (c * steps + i, cols[which])

    def col_prev(which):
        return lambda c, i: (c * nb + jnp.maximum(sub * i - 1, 0), cols[which])

    def after_rows(c, i):
        return c * nb + jnp.minimum(sub * i + sub, nb - 1)

    rows_own = (sub * BLOCK, GROUP_WIDTH)
    one = (BLOCK, GROUP_WIDTH)
    own = pl.BlockSpec(rows_own, lambda c, i: (c * steps + i, 0))
    nxt = pl.BlockSpec(one, lambda c, i: (after_rows(c, i), 0))
    return pl.pallas_call(
        body, name=name, grid=(r, steps),
        in_specs=[pl.BlockSpec(rows_own, col(0)), pl.BlockSpec(one, lambda c, i: (after_rows(c, i), cols[0])),
                  pl.BlockSpec(one, col_prev(1)), pl.BlockSpec(rows_own, col(1)),
                  pl.BlockSpec(one, col_prev(2)), pl.BlockSpec(rows_own, col(2)),
                  own, nxt, own, nxt, own, nxt],
        out_specs=[own] * 3,
        out_shape=[jax.ShapeDtypeStruct((s, GROUP_WIDTH), F32)] * 3,
        compiler_params=_params(("parallel", "parallel")),
    )(src, src, src, src, src, src, y, y, lse, lse, dy, dy)


SB_TQ = 256
SB_GROUP = 2
SB_DEAD = -104.0
SB_NH_FWD = 4
SB_NH_BWD = 2


def _split_dot(x, w):
    hi = x.astype(BF16)
    lo = (x - hi.astype(F32)).astype(BF16)
    return _dot(hi, w) + _dot(lo, w)


def _sb_consts():
    j = lax.broadcasted_iota(jnp.int32, (BLOCK, BLOCK), 0)
    k = lax.broadcasted_iota(jnp.int32, (BLOCK, BLOCK), 1)
    ones = jnp.ones((BLOCK, BLOCK), F32)
    after = jnp.concatenate([(j > k).astype(F32), ones], axis=1).astype(BF16)
    before = jnp.concatenate([(j < k).astype(F32), ones], axis=1).astype(BF16)
    return after, before


def _sb_mask(row0, col0, tq):
    row = row0 + lax.broadcasted_iota(jnp.int32, (tq, BLOCK), 0)
    col = col0 + lax.broadcasted_iota(jnp.int32, (tq, BLOCK), 1)
    return col < row


def _softplus(z):
    return jnp.maximum(z, 0.0) + jnp.log1p(jnp.exp(-jnp.abs(z)))


def _sb_attn_fwd(qkv, *, name):
    s = qkv.shape[0]
    tq = _pick(s, SB_TQ)
    nq = s // tq
    per = tq // BLOCK
    base = 3 * DIL_WIDTH // HEAD_DIM
    after, _ = _sb_consts()

    nh = SB_NH_FWD
    wide = nh * HEAD_DIM

    def body(q_ref, k_ref, v_ref, w_ref, o_ref, rc_ref, trips_ref, acc_ref, run_ref):
        qi = pl.program_id(1)
        w = w_ref[...]
        acc_ref[...] = jnp.zeros_like(acc_ref)
        run_ref[...] = jnp.zeros_like(run_ref)
        rc_ref[...] = jnp.zeros_like(rc_ref)
        lane = lax.broadcasted_iota(jnp.int32, (tq, BLOCK), 1)
        n_chunks = (qi + 1) * per

        def chunks(js, masked):
            sls = [slice(h * HEAD_DIM, (h + 1) * HEAD_DIM) for h in range(nh)]
            work = [(sl, j, pl.multiple_of(j * BLOCK, BLOCK), _sb_mask(qi * tq, j * BLOCK, tq) if masked else None)
                    for j in js for sl in sls]
            zs = [_dot_nt(q_ref[:, sl], k_ref[pl.ds(start, BLOCK), sl]) * SCALE for sl, _, start, _ in work]
            logits, sums = [], []
            for z, (_, _, _, mask) in zip(zs, work):
                sp = _softplus(z)
                logits.append(z - sp)
                sums.append(_split_dot(jnp.where(mask, -sp, 0.0) if masked else -sp, w))
            for (sl, j, start, mask), lg, sm in zip(work, logits, sums):
                run = run_ref[:, sl]
                a = jnp.exp(lg + run + sm[:, :BLOCK])
                if masked:
                    a = jnp.where(mask, a, 0.0)
                acc_ref[:, sl] += _dot(a.astype(BF16), v_ref[pl.ds(start, BLOCK), sl])
                rc_ref[:, sl] = jnp.where(lane == j, run, rc_ref[:, sl])
                run_ref[:, sl] = run + sm[:, BLOCK:]

        def diagonal(t, carry):
            last = n_chunks - 1 - SB_GROUP * t
            chunks([last - g for g in range(SB_GROUP)], True)
            return carry

        def alive():
            return (jnp.max(run_ref[...]) > SB_DEAD).astype(jnp.int32)

        def more(carry):
            t, live = carry
            return jnp.logical_and(t < n_left, live > 0)

        def left(carry):
            t, _ = carry
            last = n_chunks - 1 - per - SB_GROUP * t
            chunks([last - g for g in range(SB_GROUP)], False)
            return t + 1, alive()

        lax.fori_loop(0, per // SB_GROUP, diagonal, 0)
        n_left = (n_chunks - per) // SB_GROUP
        trips, _ = lax.while_loop(more, left, (jnp.int32(0), alive()))
        trips_ref[pl.program_id(0), qi] = trips
        o_ref[...] = acc_ref[...].astype(o_ref.dtype)

    head = lambda off: (lambda h, i: (0, (base + off) // nh + h))
    once = pl.Buffered(1)
    return pl.pallas_call(
        body, name=name, grid=(SB_HEADS // nh, nq),
        in_specs=[pl.BlockSpec((tq, wide), lambda h, i: (i, base // nh + h)),
                  pl.BlockSpec((s, wide), head(SB_HEADS), pipeline_mode=once),
                  pl.BlockSpec((s, wide), head(2 * SB_HEADS), pipeline_mode=once),
                  pl.BlockSpec((BLOCK, 2 * BLOCK), lambda h, i: (0, 0))],
        out_specs=[pl.BlockSpec((tq, wide), lambda h, i: (i, h))] * 2 + [pl.BlockSpec(memory_space=pltpu.SMEM)],
        out_shape=[jax.ShapeDtypeStruct((s, SB_WIDTH), BF16), jax.ShapeDtypeStruct((s, SB_WIDTH), F32),
                   jax.ShapeDtypeStruct((SB_HEADS // nh, nq), jnp.int32)],
        scratch_shapes=[pltpu.VMEM((tq, wide), F32), pltpu.VMEM((tq, wide), F32)],
        compiler_params=_params(("arbitrary", "arbitrary")),
    )(qkv, qkv, qkv, after)


def _sb_attn_bwd(qkv, rc, trips, do, *, name):
    s = qkv.shape[0]
    tq = _pick(s, SB_TQ)
    nq = s // tq
    per = tq // BLOCK
    base = 3 * DIL_WIDTH // HEAD_DIM
    _, before = _sb_consts()

    nh = SB_NH_BWD
    wide = nh * HEAD_DIM

    def body(trips_ref, q_ref, k_ref, v_ref, w_ref, rc_ref, do_ref, dq_ref, dk_ref, dv_ref, acc_ref, pre_ref, dob_ref):
        qi = pl.program_id(1)

        @pl.when(qi == 0)
        def _():
            dk_ref[...] = jnp.zeros_like(dk_ref)
            dv_ref[...] = jnp.zeros_like(dv_ref)

        w = w_ref[...]
        wa = (lax.broadcasted_iota(jnp.int32, (BLOCK, BLOCK), 0)
              > lax.broadcasted_iota(jnp.int32, (BLOCK, BLOCK), 1)).astype(BF16)
        dob_ref[...] = do_ref[...].astype(BF16)
        acc_ref[...] = jnp.zeros_like(acc_ref)
        pre_ref[...] = jnp.zeros_like(pre_ref)
        lane = lax.broadcasted_iota(jnp.int32, (tq, BLOCK), 1)
        n_chunks = (qi + 1) * per

        def chunks(js, masked):
            sls = [slice(h * HEAD_DIM, (h + 1) * HEAD_DIM) for h in range(nh)]
            work = [(sl, j, pl.multiple_of(j * BLOCK, BLOCK), _sb_mask(qi * tq, j * BLOCK, tq) if masked else None)
                    for j in js for sl in sls]
            zs = [_dot_nt(q_ref[:, sl], k_ref[pl.ds(start, BLOCK), sl]) * SCALE for sl, _, start, _ in work]
            das = [_dot_nt(dob_ref[:, sl], v_ref[pl.ds(start, BLOCK), sl]) for sl, _, start, _ in work]
            sigs, betweens = [], []
            for z, (_, _, _, mask) in zip(zs, work):
                sp = _softplus(z)
                sigs.append(jnp.exp(z - sp))
                betweens.append(_split_dot(jnp.where(mask, -sp, 0.0) if masked else -sp, wa))
            es, avs, sums = [], [], []
            for (sl, j, _, mask), sig, bt, da in zip(work, sigs, betweens, das):
                run = jnp.sum(jnp.where(lane == j, rc_ref[:, sl], 0.0), axis=-1, keepdims=True)
                a = sig * jnp.exp(run + bt)
                if masked:
                    a = jnp.where(mask, a, 0.0)
                e = a * da
                es.append(e)
                avs.append(a.astype(BF16))
                sums.append(_split_dot(e, w))
            for (sl, _, start, mask), sig, e, ab, sm in zip(work, sigs, es, avs, sums):
                pre = pre_ref[:, sl]
                dz = (e * (1.0 - sig) - sig * (pre + sm[:, :BLOCK])) * SCALE
                if masked:
                    dz = jnp.where(mask, dz, 0.0)
                dz = dz.astype(BF16)
                pre_ref[:, sl] = pre + sm[:, BLOCK:]
                acc_ref[:, sl] += _dot(dz, k_ref[pl.ds(start, BLOCK), sl])
                dk_ref[pl.ds(start, BLOCK), sl] += _dot_tn(dz, q_ref[:, sl])
                dv_ref[pl.ds(start, BLOCK), sl] += _dot_tn(ab, dob_ref[:, sl])

        def left(t, carry):
            chunks([SB_GROUP * t + g for g in range(SB_GROUP)], False)
            return carry

        def diagonal(t, carry):
            chunks([n_chunks - per + SB_GROUP * t + g for g in range(SB_GROUP)], True)
            return carry

        n_left = (n_chunks - per) // SB_GROUP
        ran = trips_ref[pl.program_id(0) // (SB_NH_FWD // nh), qi]
        lax.fori_loop(n_left - ran, n_left, left, 0)
        lax.fori_loop(0, per // SB_GROUP, diagonal, 0)
        dq_ref[...] = acc_ref[...]

    head = lambda off: (lambda h, i, t: (0, (base + off) // nh + h))
    blk = pl.BlockSpec((tq, wide), lambda h, i, t: (i, h))
    whole = pl.BlockSpec((s, wide), lambda h, i, t: (0, h))
    once = pl.Buffered(1)
    grid_spec = pltpu.PrefetchScalarGridSpec(
        num_scalar_prefetch=1, grid=(SB_HEADS // nh, nq),
        in_specs=[pl.BlockSpec((tq, wide), lambda h, i, t: (i, base // nh + h)),
                  pl.BlockSpec((s, wide), head(SB_HEADS), pipeline_mode=once),
                  pl.BlockSpec((s, wide), head(2 * SB_HEADS), pipeline_mode=once),
                  pl.BlockSpec((BLOCK, 2 * BLOCK), lambda h, i, t: (0, 0)),
                  blk, blk],
        out_specs=[blk, whole, whole],
        scratch_shapes=[pltpu.VMEM((tq, wide), F32), pltpu.VMEM((tq, wide), F32), pltpu.VMEM((tq, wide), BF16)],
    )
    return pl.pallas_call(
        body, name=name, grid_spec=grid_spec,
        out_shape=[jax.ShapeDtypeStruct((s, SB_WIDTH), F32)] * 3,
        compiler_params=_params(("parallel", "arbitrary")),
    )(trips, qkv, qkv, qkv, before, rc, do)


GATE_TC = 512


def _gate_fwd(proj, gate_b, up_dil, up_sb, *, name):
    s, d = up_dil.shape
    tr = _pick(s, 512)
    g0 = QKV_WIDTH // GATE_TC
    nc = d // GATE_TC

    def body(ga_ref, gb_ref, b_ref, ud_ref, us_ref, o_ref):
        ga = jax.nn.sigmoid(ga_ref[...] + b_ref[0:1, :])
        gb = jax.nn.sigmoid(gb_ref[...] + b_ref[1:2, :])
        o_ref[...] = (ga * ud_ref[...] + gb * us_ref[...]).astype(o_ref.dtype)

    blk = pl.BlockSpec((tr, GATE_TC), lambda i, j: (i, j))
    return pl.pallas_call(
        body, name=name, grid=(s // tr, nc),
        in_specs=[pl.BlockSpec((tr, GATE_TC), lambda i, j: (i, g0 + j)),
                  pl.BlockSpec((tr, GATE_TC), lambda i, j: (i, g0 + nc + j)),
                  pl.BlockSpec((2, GATE_TC), lambda i, j: (0, j)), blk, blk],
        out_specs=blk,
        out_shape=jax.ShapeDtypeStruct((s, d), BF16),
        compiler_params=_params(("parallel", "parallel")),
    )(proj, proj, gate_b, up_dil, up_sb)


def _gate_bwd(proj, gate_b, up_dil, up_sb, dmixed, *, name):
    s, d = up_dil.shape
    tr = _pick(s, 512)
    g0 = QKV_WIDTH // GATE_TC
    nc = d // GATE_TC

    def body(ga_ref, gb_ref, b_ref, ud_ref, us_ref, dm_ref, dud_ref, dus_ref, dga_ref, dgb_ref, db_ref):
        ga = jax.nn.sigmoid(ga_ref[...] + b_ref[0:1, :])
        gb = jax.nn.sigmoid(gb_ref[...] + b_ref[1:2, :])
        dm = dm_ref[...]
        dud_ref[...] = (dm * ga).astype(BF16)
        dus_ref[...] = (dm * gb).astype(BF16)
        dla = dm * ud_ref[...] * ga * (1.0 - ga)
        dlb = dm * us_ref[...] * gb * (1.0 - gb)
        dga_ref[...] = dla.astype(BF16)
        dgb_ref[...] = dlb.astype(BF16)

        @pl.when(pl.program_id(1) == 0)
        def _():
            db_ref[...] = jnp.zeros_like(db_ref)

        db_ref[0:1, :] += jnp.sum(dla, axis=0, keepdims=True)
        db_ref[1:2, :] += jnp.sum(dlb, axis=0, keepdims=True)

    blk = pl.BlockSpec((tr, GATE_TC), lambda j, i: (i, j))
    dud, dus, dga, dgb, db = pl.pallas_call(
        body, name=name, grid=(nc, s // tr),
        in_specs=[pl.BlockSpec((tr, GATE_TC), lambda j, i: (i, g0 + j)),
                  pl.BlockSpec((tr, GATE_TC), lambda j, i: (i, g0 + nc + j)),
                  pl.BlockSpec((2, GATE_TC), lambda j, i: (0, j)), blk, blk, blk],
        out_specs=[blk, blk, blk, blk, pl.BlockSpec((2, GATE_TC), lambda j, i: (0, j))],
        out_shape=[jax.ShapeDtypeStruct((s, d), BF16)] * 4 + [jax.ShapeDtypeStruct((2, d), F32)],
        compiler_params=_params(("parallel", "arbitrary")),
    )(proj, proj, gate_b, up_dil, up_sb, dmixed)
    return dud, dus, (dga, dgb), db


def _adamw(w, g, m, v, *, name):
    shape = w.shape
    cols = shape[-1]
    rows = w.size // cols
    tr = _pick(rows, max(8, (512 * 1024) // cols))
    if rows % tr or (tr % 8 and tr != rows):
        tr = rows
    c1 = 1.0 - ADAM_B1 ** ADAM_STEP
    c2 = 1.0 - ADAM_B2 ** ADAM_STEP

    def body(w_ref, g_ref, m_ref, v_ref, d_ref, nm_ref, nv_ref):
        gv = g_ref[...]
        nm = ADAM_B1 * m_ref[...] + (1.0 - ADAM_B1) * gv
        nv = ADAM_B2 * v_ref[...] + (1.0 - ADAM_B2) * (gv * gv)
        d_ref[...] = -ADAM_LR * ((nm / c1) / (jnp.sqrt(nv / c2) + ADAM_EPS) + ADAM_WD * w_ref[...])
        nm_ref[...] = nm
        nv_ref[...] = nv

    blk = pl.BlockSpec((tr, cols), lambda i: (i, 0))
    outs = pl.pallas_call(
        body, name=name, grid=(rows // tr,),
        in_specs=[blk] * 4, out_specs=[blk] * 3,
        out_shape=[jax.ShapeDtypeStruct((rows, cols), F32)] * 3,
        compiler_params=_params(("parallel",)),
    )(*(t.reshape(rows, cols) for t in (w, g, m, v)))
    return tuple(t.reshape(shape) for t in outs)


def _add_half(full, other, which, *, name):
    n, rh, c = other.shape
    tr = _pick(rh, max(8, (1024 * 1024) // c))
    nb = rh // tr

    def body(which_ref, a_ref, b_ref, o_ref):
        o_ref[...] = (a_ref[...] + b_ref[...]).astype(o_ref.dtype)

    grid_spec = pltpu.PrefetchScalarGridSpec(
        num_scalar_prefetch=1, grid=(n, nb),
        in_specs=[pl.BlockSpec((None, tr, c), lambda s, i, w: (s, w[0] * nb + i, 0)),
                  pl.BlockSpec((None, tr, c), lambda s, i, w: (s, i, 0))],
        out_specs=pl.BlockSpec((None, tr, c), lambda s, i, w: (s, i, 0)),
    )
    return pl.pallas_call(
        body, name=name, grid_spec=grid_spec,
        out_shape=jax.ShapeDtypeStruct((n, rh, c), BF16),
        compiler_params=_params(("parallel", "parallel")),
    )(which, full, other)


def _sum_chips(own, parts, chip, *, name):
    n, r, c = parts.shape
    tr = _pick(r, max(16, (1024 * 1024) // c))

    def body(chip_ref, own_ref, p1_ref, p2_ref, p3_ref, o_ref):
        acc = own_ref[...].astype(F32)
        for p in (p1_ref, p2_ref, p3_ref):
            acc = acc + p[...].astype(F32)
        o_ref[...] = acc

    def slot(k):
        return pl.BlockSpec((None, tr, c), lambda i, o: (o[0] ^ k, i, 0))

    grid_spec = pltpu.PrefetchScalarGridSpec(
        num_scalar_prefetch=1, grid=(r // tr,),
        in_specs=[slot(0), slot(1), slot(2), slot(3)],
        out_specs=pl.BlockSpec((tr, c), lambda i, o: (i, 0)),
    )
    return pl.pallas_call(
        body, name=name, grid_spec=grid_spec,
        out_shape=jax.ShapeDtypeStruct((r, c), F32),
        compiler_params=_params(("parallel",)),
    )(chip, own, parts, parts, parts)


def _place_piece(full, piece, index, *, name):
    a, n, r, c = full.shape
    tr = _pick(r, max(16, (1024 * 1024) // c))

    def body(index_ref, piece_ref, full_ref, o_ref):
        o_ref[...] = piece_ref[...]

    grid_spec = pltpu.PrefetchScalarGridSpec(
        num_scalar_prefetch=1, grid=(a, r // tr),
        in_specs=[pl.BlockSpec((None, tr, c), lambda l, i, k: (l, i, 0)), ANY],
        out_specs=pl.BlockSpec((None, None, tr, c), lambda l, i, k: (l, k[0], i, 0)),
    )
    return pl.pallas_call(
        body, name=name, grid_spec=grid_spec,
        out_shape=jax.ShapeDtypeStruct(full.shape, full.dtype),
        input_output_aliases={2: 0},
        compiler_params=_params(("parallel", "parallel")),
    )(index, piece, full)


ANY = pl.BlockSpec(memory_space=pl.ANY)


def _place():
    x, y, c = lax.axis_index("x"), lax.axis_index("y"), lax.axis_index("c")
    chips = [(1 - x, y), (x, 1 - y), (1 - x, 1 - y)]
    return x, y, c, chips


def _remote(src, dst, send_sem, recv_sem, to):
    return pltpu.make_async_remote_copy(src_ref=src, dst_ref=dst, send_sem=send_sem, recv_sem=recv_sem,
                                        device_id=to, device_id_type=MESH)


def _half(ref, chip, which):
    rh = ref.shape[1] // 2
    return ref.at[chip, pl.ds(which * rh, rh), :]


def _gather_ici_copies(layer):
    def copies(ins, outs, send_sems, recv_sems):
        x, y, c, chips = _place()
        me = 2 * x + y
        sends, recvs = [], []
        for t in range(len(ins)):
            rh = ins[t].shape[1] // 2
            for k, (px, py) in enumerate(chips):
                sems = (send_sems.at[3 * t + k], recv_sems.at[3 * t + k])
                sends.append(_remote(ins[t].at[layer, pl.ds(c * rh, rh), :], _half(outs[t], me, c), *sems, (px, py, c)))
                landed = _half(outs[t], 2 * px + py, c)
                recvs.append(_remote(landed, landed, *sems, (px, py, c)))
        return sends, recvs
    return copies


def _gather_pass_copies(ins, outs, send_sems, recv_sems):
    x, y, c, chips = _place()
    sibling = (x, y, 1 - c)
    sends, recvs = [], []
    for t in range(len(ins)):
        for k, (px, py) in enumerate(chips):
            sems = (send_sems.at[3 * t + k], recv_sems.at[3 * t + k])
            sends.append(_remote(_half(ins[t], 2 * px + py, c), _half(outs[t], 2 * px + py, c), *sems, sibling))
            other = _half(outs[t], 2 * px + py, 1 - c)
            recvs.append(_remote(other, other, *sems, sibling))
    return sends, recvs


def _gathered_shapes(shards):
    return [jax.ShapeDtypeStruct((N_CHIPS,) + w.shape[1:], w.dtype) for w in shards]


def _gather_ici_side(shards, layer):
    return (tuple(shards), _gathered_shapes(shards), _gather_ici_copies(layer), 3 * len(shards))


def _gather_layer(shards, layer, *, name):
    n = len(shards)
    ici = _gather_ici_copies(layer)

    def body(*refs):
        ins, outs = refs[:n], refs[n:2 * n]
        send_sems, recv_sems, pass_send, pass_recv = refs[2 * n:]
        sends, recvs = ici(ins, outs, send_sems, recv_sems)
        passes, arrivals = _gather_pass_copies(outs, outs, pass_send, pass_recv)
        for cp in sends:
            cp.start()
        for landed, onward in zip(recvs, passes):
            landed.wait_recv()
            onward.start()
        for cp in arrivals:
            cp.wait_recv()
        for cp in sends + passes:
            cp.wait_send()

    return pl.pallas_call(
        body, name=name,
        in_specs=[ANY] * n, out_specs=[ANY] * n,
        out_shape=_gathered_shapes(shards),
        scratch_shapes=[pltpu.SemaphoreType.DMA((3 * n,)) for _ in range(4)],
        compiler_params=pltpu.CompilerParams(has_side_effects=True),
    )(*shards)


def _gather_pass(landed, *, name):
    n = len(landed)

    def body(*refs):
        ins, outs = refs[:n], refs[n:2 * n]
        send_sems, recv_sems = refs[2 * n:]
        passes, arrivals = _gather_pass_copies(ins, outs, send_sems, recv_sems)
        for cp in passes:
            cp.start()
        for cp in arrivals:
            cp.wait_recv()
        for cp in passes:
            cp.wait_send()

    return pl.pallas_call(
        body, name=name,
        in_specs=[ANY] * n, out_specs=[ANY] * n,
        out_shape=[jax.ShapeDtypeStruct(w.shape, w.dtype) for w in landed],
        input_output_aliases={t: t for t in range(n)},
        scratch_shapes=[pltpu.SemaphoreType.DMA((3 * n,)), pltpu.SemaphoreType.DMA((3 * n,))],
        compiler_params=pltpu.CompilerParams(has_side_effects=True),
    )(*landed)


def _pair_swap_halves(grads, *, name):
    n = len(grads)

    def body(*refs):
        ins, outs = refs[:n], refs[n:2 * n]
        send_sems, recv_sems = refs[2 * n:]
        x, y, c, _ = _place()
        sibling = (x, y, 1 - c)
        cps = []
        for t in range(n):
            rh = ins[t].shape[1] // 2
            cp = _remote(ins[t].at[:, pl.ds((1 - c) * rh, rh), :], outs[t], send_sems.at[t], recv_sems.at[t], sibling)
            cp.start()
            cps.append(cp)
        for cp in cps:
            cp.wait()

    return pl.pallas_call(
        body, name=name,
        in_specs=[ANY] * n, out_specs=[ANY] * n,
        out_shape=[jax.ShapeDtypeStruct((g.shape[0], g.shape[1] // 2, g.shape[2]), g.dtype) for g in grads],
        scratch_shapes=[pltpu.SemaphoreType.DMA((n,)), pltpu.SemaphoreType.DMA((n,))],
        compiler_params=pltpu.CompilerParams(has_side_effects=True),
    )(*grads)


def _chip_copies(ins, outs, send_sems, recv_sems):
    x, y, c, chips = _place()
    me = 2 * x + y
    sends, recvs = [], []
    for t in range(len(ins)):
        for k, (px, py) in enumerate(chips):
            sems = (send_sems.at[3 * t + k], recv_sems.at[3 * t + k])
            sends.append(_remote(ins[t].at[2 * px + py], outs[t].at[me], *sems, (px, py, c)))
            theirs = outs[t].at[2 * px + py]
            recvs.append(_remote(theirs, theirs, *sems, (px, py, c)))
    return sends, recvs


def _chip_side(parts):
    return (tuple(parts), [jax.ShapeDtypeStruct(p.shape, p.dtype) for p in parts], _chip_copies, 3 * len(parts))


def _chip_exchange(parts, *, name):
    n = len(parts)

    def body(*refs):
        sends, recvs = _chip_copies(refs[:n], refs[n:2 * n], *refs[2 * n:])
        for cp in sends:
            cp.start()
        for cp in recvs:
            cp.wait_recv()
        for cp in sends:
            cp.wait_send()

    return pl.pallas_call(
        body, name=name,
        in_specs=[ANY] * n, out_specs=[ANY] * n,
        out_shape=[jax.ShapeDtypeStruct(p.shape, p.dtype) for p in parts],
        scratch_shapes=[pltpu.SemaphoreType.DMA((3 * n,)), pltpu.SemaphoreType.DMA((3 * n,))],
        compiler_params=pltpu.CompilerParams(has_side_effects=True),
    )(*parts)


def _pair_join_halves(halves, *, name):
    n = len(halves)

    def body(*refs):
        ins, outs = refs[:n], refs[n:2 * n]
        send_sems, recv_sems = refs[2 * n:]
        x, y, c, _ = _place()
        sibling = (x, y, 1 - c)
        cps = []
        for t in range(n):
            rh = ins[t].shape[0]
            cp = _remote(ins[t], outs[t].at[pl.ds(c * rh, rh), :], send_sems.at[t], recv_sems.at[t], sibling)
            cp.start()
            cps.append(cp)
        for t, cp in enumerate(cps):
            rh = ins[t].shape[0]
            theirs = outs[t].at[pl.ds((1 - c) * rh, rh), :]
            _remote(theirs, theirs, send_sems.at[t], recv_sems.at[t], sibling).wait_recv()
            cp.wait_send()

    return pl.pallas_call(
        body, name=name,
        in_specs=[ANY] * n, out_specs=[ANY] * n,
        out_shape=[jax.ShapeDtypeStruct((2 * h.shape[0], h.shape[1]), h.dtype) for h in halves],
        scratch_shapes=[pltpu.SemaphoreType.DMA((n,)), pltpu.SemaphoreType.DMA((n,))],
        compiler_params=pltpu.CompilerParams(has_side_effects=True),
    )(*halves)


def _exchange_small(buf, *, name):
    rows, lanes = buf.shape

    def body(in_ref, out_ref, send_sems, recv_sems, local_sem):
        x, y, c = lax.axis_index("x"), lax.axis_index("y"), lax.axis_index("c")
        me = 4 * x + 2 * y + c
        lc = pltpu.make_async_copy(in_ref, out_ref.at[me], local_sem)
        lc.start()
        cps = []
        for k in range(1, 8):
            fx, fy, fc = (k >> 2) & 1, (k >> 1) & 1, k & 1
            to = (x ^ fx, y ^ fy, c ^ fc)
            cp = _remote(in_ref, out_ref.at[me], send_sems.at[k - 1], recv_sems.at[k - 1], to)
            cp.start()
            cps.append(cp)
        for k in range(1, 8):
            fx, fy, fc = (k >> 2) & 1, (k >> 1) & 1, k & 1
            theirs = out_ref.at[4 * (x ^ fx) + 2 * (y ^ fy) + (c ^ fc)]
            _remote(theirs, theirs, send_sems.at[k - 1], recv_sems.at[k - 1], (x ^ fx, y ^ fy, c ^ fc)).wait_recv()
        for cp in cps:
            cp.wait_send()
        lc.wait()

    return pl.pallas_call(
        body, name=name,
        in_specs=[ANY], out_specs=ANY,
        out_shape=jax.ShapeDtypeStruct((8, rows, lanes), buf.dtype),
        scratch_shapes=[pltpu.SemaphoreType.DMA((7,)), pltpu.SemaphoreType.DMA((7,)), pltpu.SemaphoreType.DMA],
        compiler_params=pltpu.CompilerParams(has_side_effects=True),
    )(buf)


def _sum_slots(slots, *, name):
    n, rows, lanes = slots.shape

    def body(s_ref, o_ref):
        acc = s_ref[0]
        for t in range(1, n):
            acc = acc + s_ref[t]
        o_ref[...] = acc

    return pl.pallas_call(
        body, name=name,
        out_shape=jax.ShapeDtypeStruct((rows, lanes), slots.dtype),
    )(slots)


def _relu2(acc):
    r = jnp.maximum(acc, 0.0)
    return acc, r * r


def _with_side(result, side):
    return result if side else (result, ())


def _layer_fwd(x, w, cos, sin, tag, sides=None, rest=None):
    sides = sides or {}
    got = {}
    h = _rmsnorm_fwd(x, w["norm1_g"], name=f"norm1_{tag}")
    proj, got["proj"] = _with_side(_matmul(h, w["w_in"], mode="nn", b_sharded=True, tm=512, side=sides.get("proj"),
                                           name=f"proj_{tag}"), sides.get("proj"))
    if rest:
        w = {**w, **rest(got["proj"])}
    qkv, res1, res2 = _qkv_prep(proj, w["q_norm_g"], w["k_norm_g"], cos, sin, name=f"qkv_prep_{tag}")
    dil_src = [(qkv, (0, 3, 6))] + [(t.reshape(-1, 3 * GROUP_WIDTH), (0, 1, 2)) for t in (res1, res2)]
    outs, lds = [], []
    for g, (src, cols) in enumerate(dil_src):
        o_g, ld_g = _dil_attn_fwd(src, g, cols, name=f"dil{g}_fwd_{tag}")
        outs.append(o_g)
        lds.append(ld_g)
    yb, y, lse = _dil_merge(outs, lds, name=f"dil_merge_{tag}")
    o_sb, rc, trips = _sb_attn_fwd(qkv, name=f"sb_fwd_{tag}")
    up_dil = _matmul(yb, w["w_up_dil"], mode="nn", b_sharded=True, name=f"up_dil_{tag}")
    up_sb = _matmul(o_sb, w["w_up_sb"], mode="nn", b_sharded=True, name=f"up_sb_{tag}")
    mixed = _gate_fwd(proj, w["gate_b"], up_dil, up_sb, name=f"gate_{tag}")
    x1, got["out"] = _with_side(_matmul(mixed, w["w_out"], mode="nn", extras=(x,), epilogue=lambda acc, res: (acc + res,),
                                        side=sides.get("out"), name=f"out_{tag}"), sides.get("out"))
    h2 = _rmsnorm_fwd(x1, w["norm2_g"], name=f"norm2_{tag}")
    (u, a), got["ff1"] = _with_side(_matmul(h2, w["w_ff1"], mode="nn", b_sharded=True, out_dtypes=(F32, BF16),
                                            epilogue=_relu2, side=sides.get("ff1"), name=f"ff1_{tag}"), sides.get("ff1"))
    x2, got["ff2"] = _with_side(_matmul(a, w["w_ff2"], mode="nn", extras=(x1,), epilogue=lambda acc, res: (acc + res,),
                                        side=sides.get("ff2"), name=f"ff2_{tag}"), sides.get("ff2"))
    saved = dict(x=x, h=h, proj=proj, qkv=qkv, dil_src=dil_src, y=y, yb=yb, lse=lse, o_sb=o_sb, rc=rc, trips=trips, up_dil=up_dil, up_sb=up_sb,
                 mixed=mixed, x1=x1, h2=h2, u=u, a=a, w=w)
    return x2, saved, got


def _layer_bwd(dx, dxb, sv, cos, sin, tag, sides=None, late=None):
    sides = dict(sides or {})
    w = sv["w"]
    grads, got = {}, {}
    du, got["d_a"] = _with_side(_matmul(dxb, w["w_ff2"], mode="nt", extras=(sv["u"],), out_dtypes=(BF16,),
                                        epilogue=lambda acc, u: (acc * (2.0 * jnp.maximum(u, 0.0)),),
                                        side=sides.get("d_a"), name=f"d_a_{tag}"), sides.get("d_a"))
    grads["w_ff2"], got["dw_ff2"] = _with_side(_matmul(sv["a"], dxb, mode="tn", side=sides.get("dw_ff2"),
                                                       name=f"dw_ff2_{tag}"), sides.get("dw_ff2"))
    dh2, got["d_h2"] = _with_side(_matmul(du, w["w_ff1"], mode="nt", b_sharded=True, side=sides.get("d_h2"),
                                          name=f"d_h2_{tag}"), sides.get("d_h2"))
    grads["w_ff1"] = _matmul(sv["h2"], du, mode="tn", out_sharded=True, name=f"dw_ff1_{tag}")
    dx1, dx1b, grads["norm2_g"] = _rmsnorm_bwd(sv["x1"], w["norm2_g"], dh2, dx, name=f"norm2_bwd_{tag}")
    dmixed = _matmul(dx1b, w["w_out"], mode="nt", name=f"d_mixed_{tag}")
    grads["w_out"] = _matmul(sv["mixed"], dx1b, mode="tn", name=f"dw_out_{tag}")
    dud, dus, dgate, grads["gate_b"] = _gate_bwd(sv["proj"], w["gate_b"], sv["up_dil"], sv["up_sb"], dmixed,
                                                 name=f"gate_bwd_{tag}")
    dy_dil = _matmul(dud, w["w_up_dil"], mode="nt", b_sharded=True, name=f"d_ydil_{tag}")
    grads["w_up_dil"] = _matmul(sv["yb"], dud, mode="tn", out_sharded=True, name=f"dw_up_dil_{tag}")
    dy_sb = _matmul(dus, w["w_up_sb"], mode="nt", b_sharded=True, name=f"d_ysb_{tag}")
    grads["w_up_sb"] = _matmul(sv["o_sb"], dus, mode="tn", out_sharded=True, name=f"dw_up_sb_{tag}")
    if late:
        sides.update(late(grads))
    dys = [dy_dil] + _dil_to_residues(dy_dil, name=f"dil_dy_{tag}")
    d_dil = [_dil_attn_bwd(src, g, cols, sv["y"][g], sv["lse"][g], dys[g], name=f"dil{g}_bwd_{tag}")
             for g, (src, cols) in enumerate(sv["dil_src"])]
    for g in (1, 2):
        d_dil[g] = [t.reshape(DIL_GROUPS[g][1], -1, GROUP_WIDTH) for t in d_dil[g]]
    d_sb = _sb_attn_bwd(sv["qkv"], sv["rc"], sv["trips"], dy_sb, name=f"sb_bwd_{tag}")
    dproj, grads["q_norm_g"], grads["k_norm_g"] = _qkv_prep_bwd(
        sv["proj"], w["q_norm_g"], w["k_norm_g"], cos, sin, d_dil, d_sb, dgate, name=f"qkv_prep_bwd_{tag}")
    dh, got["d_h"] = _with_side(_matmul(dproj, w["w_in"], mode="nt", b_sharded=True, tm=512, tn=2048, side=sides.get("d_h"),
                                        name=f"d_h_{tag}"), sides.get("d_h"))
    grads["w_in"], got["dw_in"] = _with_side(_matmul(sv["h"], dproj, mode="tn", out_sharded=True, tm=512, tk=1024,
                                                     side=sides.get("dw_in"), name=f"dw_in_{tag}"), sides.get("dw_in"))
    dx0, dx0b, grads["norm1_g"] = _rmsnorm_bwd(sv["x"], w["norm1_g"], dh, dx1, name=f"norm1_bwd_{tag}")
    return dx0, dx0b, grads, got


BIG = ("w_in", "w_up_dil", "w_up_sb", "w_out", "w_ff1", "w_ff2")
ROW_SHARDED = ("w_out", "w_ff2")
WEIGHTS = ("norm1_g", "w_in", "q_norm_g", "k_norm_g", "w_up_dil", "w_up_sb", "gate_b", "w_out", "norm2_g", "w_ff1", "w_ff2")


def _reduce_begin(grads, which, tag):
    names = list(grads)
    full = [grads[n] for n in names]
    theirs = _pair_swap_halves(full, name=f"rs_pair_{tag}")
    return {n: _add_half(f, t, which, name=f"rs_pair_sum_{n}_{tag}") for n, f, t in zip(names, full, theirs)}


def _reduce_end(pair, parts, which, chip, tag):
    names = list(pair)
    halves = [_sum_chips(pair[n], parts[n], chip, name=f"rs_chip_sum_{n}_{tag}") for n in names]
    joined = _pair_join_halves(halves, name=f"rs_join_{tag}")
    out = {}
    for n, j, h in zip(names, joined, halves):
        r, c = j.shape
        out[n] = _place_piece(j.reshape(1, 2, r // 2, c), h[None], which, name=f"rs_place_{n}_{tag}").reshape(r, c)
    return out


def _pack_rows(vecs):
    rows, spans, at = [], [], 0
    for v in vecs:
        r = v.size // 128
        padded = -(-r // 8) * 8
        rows.append(jnp.pad(v.reshape(r, 128), ((0, padded - r), (0, 0))))
        spans.append((at, r))
        at += padded
    return jnp.concatenate(rows, axis=0), spans


def kernel(x, norm1_g, w_in, q_norm_g, k_norm_g, w_up_dil, w_up_sb, gate_b, w_out, norm2_g, w_ff1, w_ff2, loss_target, m_norm1_g, m_w_in, m_q_norm_g, m_k_norm_g, m_w_up_dil, m_w_up_sb, m_gate_b, m_w_out, m_norm2_g, m_w_ff1, m_w_ff2, v_norm1_g, v_w_in, v_q_norm_g, v_k_norm_g, v_w_up_dil, v_w_up_sb, v_gate_b, v_w_out, v_norm2_g, v_w_ff1, v_w_ff2):
    weights = dict(norm1_g=norm1_g, w_in=w_in, q_norm_g=q_norm_g, k_norm_g=k_norm_g, w_up_dil=w_up_dil,
                   w_up_sb=w_up_sb, gate_b=gate_b, w_out=w_out, norm2_g=norm2_g, w_ff1=w_ff1, w_ff2=w_ff2)
    moments_m = dict(norm1_g=m_norm1_g, w_in=m_w_in, q_norm_g=m_q_norm_g, k_norm_g=m_k_norm_g, w_up_dil=m_w_up_dil,
                     w_up_sb=m_w_up_sb, gate_b=m_gate_b, w_out=m_w_out, norm2_g=m_norm2_g, w_ff1=m_w_ff1, w_ff2=m_w_ff2)
    moments_v = dict(norm1_g=v_norm1_g, w_in=v_w_in, q_norm_g=v_q_norm_g, k_norm_g=v_k_norm_g, w_up_dil=v_w_up_dil,
                     w_up_sb=v_w_up_sb, gate_b=v_gate_b, w_out=v_w_out, norm2_g=v_norm2_g, w_ff1=v_w_ff1, w_ff2=v_w_ff2)
    depth = w_in.shape[0]
    seq, d_model = x.shape[1], x.shape[2]
    chip = 2 * lax.axis_index("x") + lax.axis_index("y")
    core = lax.axis_index("c")
    which = jnp.reshape(core, (1,)).astype(jnp.int32)

    chip_index = jnp.reshape(chip, (1,)).astype(jnp.int32)
    shards = {n: weights[n].astype(BF16) for n in BIG}
    bias_buf, ((_, bias_rows),) = _pack_rows([gate_b])
    bias_slots = _exchange_small(bias_buf, name="gather_gate_b")
    bias = bias_slots[0::2, :bias_rows].reshape(N_CHIPS, depth, 2, d_model // N_CHIPS)
    bias = jnp.transpose(bias, (1, 2, 0, 3)).reshape(depth, 2, d_model)

    def placed(names, l, landed):
        w = {}
        for n in names:
            w[n] = _place_piece(landed[n][None], shards[n][l:l + 1], chip_index, name=f"gather_place_{n}_l{l}")[0]
            if n in ROW_SHARDED:
                w[n] = w[n].reshape(-1, w[n].shape[-1])
        return w

    def ici_side(names, l):
        return _gather_ici_side([shards[n] for n in names], l)

    def by_chip(n, g):
        return g.reshape(N_CHIPS, -1, g.shape[-1]) if n in ROW_SHARDED else g

    ahead = {"ff1": ("w_in",), "out": ("w_up_dil", "w_up_sb", "w_out"), "ff2": ("w_ff1",)}
    cos, sin = _rope_tables(seq)
    first = ("w_in",)
    ready = placed(first, 0, dict(zip(first, _gather_layer([shards[n] for n in first], 0, name="gather_l0_w_in"))))
    pending = tuple(n for n in BIG if n not in first)
    act = x[0]
    saved = []
    for l in range(depth):
        def rest(landed, l=l, pending=pending):
            passed = _gather_pass(list(landed), name=f"gather_pass_l{l}_rest")
            return placed(pending, l, dict(zip(pending, passed)))

        sides = {"proj": ici_side(pending, l)}
        if l + 1 < depth:
            sides.update({k: ici_side(names, l + 1) for k, names in ahead.items()})
        w = dict(ready, norm1_g=norm1_g[l], norm2_g=norm2_g[l], q_norm_g=q_norm_g[l], k_norm_g=k_norm_g[l], gate_b=bias[l])
        act, sv, got = _layer_fwd(act, w, cos, sin, f"l{l}", sides, rest)
        saved.append(sv)
        if l + 1 < depth:
            names = [n for k in ahead for n in ahead[k]]
            passed = _gather_pass([t for k in ahead for t in got[k]], name=f"gather_pass_l{l + 1}")
            ready = placed(names, l + 1, dict(zip(names, passed)))
            pending = ("w_ff2",)
    loss, dx, dxb = _loss_head(act, loss_target[0])
    loss = lax.psum(loss, ("x", "y", "c"))

    below = {"d_a": ("w_in",), "dw_ff2": ("w_ff1", "w_out", "w_up_dil", "w_up_sb"), "d_h2": ("w_ff2",)}
    own = {"d_h": ("w_ff2", "w_ff1"), "dw_in": ("w_out", "w_up_dil", "w_up_sb")}
    grads, reduced = [None] * depth, [None] * depth
    pair = None
    for l in reversed(range(depth)):
        sides = {k: _chip_side([pair[n] for n in names]) for k, names in below.items()} if pair else None
        early = {}

        def late(g, early=early):
            for k, names in own.items():
                early.update(_reduce_begin({n: by_chip(n, g[n]) for n in names}, which, f"l0_{k}"))
            return {k: _chip_side([early[n] for n in names]) for k, names in own.items()}

        dx, dxb, grads[l], got = _layer_bwd(dx, dxb, saved[l], cos, sin, f"l{l}", sides, late if l == 0 else None)
        if pair:
            parts = {n: t for k, names in below.items() for n, t in zip(names, got[k])}
            reduced[l + 1] = _reduce_end(pair, parts, which, chip_index, f"l{l + 1}")
        if l > 0:
            pair = _reduce_begin({n: by_chip(n, grads[l][n]) for n in BIG}, which, f"l{l}")
    last = _reduce_begin({"w_in": grads[0]["w_in"]}, which, "l0_w_in")
    parts = {n: t for k, names in own.items() for n, t in zip(names, got[k])}
    parts["w_in"] = _chip_exchange([last["w_in"]], name="rs_chips_l0_w_in")[0]
    pair = dict(early, **last)
    reduced[0] = _reduce_end({n: pair[n] for n in BIG}, parts, which, chip_index, "l0")
    final = {n: jnp.stack([reduced[l][n] for l in range(depth)]) for n in BIG}

    small_names = ("norm1_g", "norm2_g", "q_norm_g", "k_norm_g", "gate_b")
    packed, spans = _pack_rows([grads[l][n] for l in range(depth) for n in small_names])
    total = _sum_slots(_exchange_small(packed, name="exchange_small_grads"), name="sum_small_grads")
    pieces = iter(total[at:at + r] for at, r in spans)
    small = {n: [] for n in small_names}
    for l in range(depth):
        for n in small_names:
            small[n].append(next(pieces))
    for n in ("norm1_g", "norm2_g"):
        final[n] = jnp.stack([p.reshape(d_model) for p in small[n]])
    for n in ("q_norm_g", "k_norm_g"):
        final[n] = jnp.stack([p.reshape(N_GROUPS, HEAD_DIM) for p in small[n]])
    shard_cols = d_model // N_CHIPS
    final["gate_b"] = jnp.stack([lax.dynamic_slice_in_dim(p.reshape(2, d_model), chip * shard_cols, shard_cols, axis=1)
                                 for p in small["gate_b"]])

    deltas, new_m, new_v = {}, {}, {}
    for n in WEIGHTS:
        deltas[n], new_m[n], new_v[n] = _adamw(weights[n], final[n], moments_m[n], moments_v[n], name=f"adamw_{n}")
    return (loss, dx[None], *[final[n] for n in WEIGHTS], *[deltas[n] for n in WEIGHTS],
            *[new_m[n] for n in WEIGHTS], *[new_v[n] for n in WEIGHTS])
```

```python
import functools
import math

import jax
import jax.numpy as jnp
from jax import lax
from jax.experimental import pallas as pl
from jax.experimental.pallas import tpu as pltpu

F32 = jnp.float32
BF16 = jnp.bfloat16

HEAD_DIM = 128
DIL_GROUPS = ((128, 1), (512, 4), (2048, 16))
N_GROUPS = 3
HEADS_PER_GROUP = 4
GROUP_WIDTH = HEADS_PER_GROUP * HEAD_DIM
DIL_WIDTH = N_GROUPS * GROUP_WIDTH
SB_HEADS = 8
SB_WIDTH = SB_HEADS * HEAD_DIM
QKV_WIDTH = 3 * DIL_WIDTH + 3 * SB_WIDTH
BLOCK = 128
ROPE_THETA = 10000.0
EPS = 1e-6
SCALE = 1.0 / math.sqrt(HEAD_DIM)
NEG_INF = float("-inf")

ADAM_LR = 0.001
ADAM_B1 = 0.9
ADAM_B2 = 0.999
ADAM_EPS = 1e-08
ADAM_WD = 0.01
ADAM_STEP = 10

N_CHIPS = 4
MESH = pl.DeviceIdType.MESH
MIB = 1024 * 1024
VMEM_LIMIT = 56 * MIB


def _params(semantics=None, vmem=VMEM_LIMIT):
    return pltpu.CompilerParams(dimension_semantics=semantics, vmem_limit_bytes=vmem)


def _pick(n, pref):
    if n <= pref:
        return n
    t = 1 << (pref.bit_length() - 1)
    while n % t:
        t //= 2
    return t


def _tile(n, pref):
    best = 0
    for t in range(256, min(n, pref) + 1, 256):
        if n % t == 0:
            best = t
    return best if best else n


def _matmul(a, b, *, mode, name, out_dtypes=(F32,), extras=(), epilogue=None, b_sharded=False, out_sharded=False,
            tm=1024, tn=1024, tk=2048, side=None):
    if mode == "nn":
        m, kdim = a.shape
        n = b.shape[-1] * (N_CHIPS if b_sharded else 1)
        assert (b.shape[-2] if b_sharded else b.shape[0]) == kdim
    elif mode == "nt":
        m, kdim = a.shape
        n = b.shape[-2]
        assert b.shape[-1] * (N_CHIPS if b_sharded else 1) == kdim
    else:
        kdim, m = a.shape
        n = b.shape[1]
        assert b.shape[0] == kdim and not b_sharded
    shard_n = n // N_CHIPS if (out_sharded or (b_sharded and mode == "nn")) else n
    shard_k = kdim // N_CHIPS if (b_sharded and mode == "nt") else kdim
    tm = _tile(m, tm)
    tn = _tile(shard_n, tn)
    tk = _tile(shard_k, tk)
    nk = kdim // tk
    nj_n = shard_n // tn
    nj_k = shard_k // tk
    j_outer = nk == 1 and (kdim * n + m * kdim * (n // tn)) < (m * kdim + kdim * n * (m // tm))
    grid = (n // tn, m // tm, nk) if j_outer else (m // tm, n // tn, nk)

    def at(index):
        return (lambda g0, g1, k: index(g1, g0, k)) if j_outer else index

    if mode == "nn":
        a_spec = pl.BlockSpec((tm, tk), at(lambda i, j, k: (i, k)))
        if b_sharded:
            b_spec = pl.BlockSpec((None, tk, tn), at(lambda i, j, k: (j // nj_n, k, j % nj_n)))
        else:
            b_spec = pl.BlockSpec((tk, tn), at(lambda i, j, k: (k, j)))
        dims = (((1,), (0,)), ((), ()))
    elif mode == "nt":
        a_spec = pl.BlockSpec((tm, tk), at(lambda i, j, k: (i, k)))
        if b_sharded:
            b_spec = pl.BlockSpec((None, tn, tk), at(lambda i, j, k: (k // nj_k, j, k % nj_k)))
        else:
            b_spec = pl.BlockSpec((tn, tk), at(lambda i, j, k: (j, k)))
        dims = (((1,), (1,)), ((), ()))
    else:
        a_spec = pl.BlockSpec((tk, tm), at(lambda i, j, k: (k, i)))
        b_spec = pl.BlockSpec((tk, tn), at(lambda i, j, k: (k, j)))
        dims = (((0,), (0,)), ((), ()))
    if out_sharded:
        o_spec = pl.BlockSpec((None, tm, tn), at(lambda i, j, k: (j // nj_n, i, j % nj_n)))
        o_shape = (N_CHIPS, m, shard_n)
    else:
        o_spec = pl.BlockSpec((tm, tn), at(lambda i, j, k: (i, j)))
        o_shape = (m, n)
    x_spec = pl.BlockSpec((tm, tn), at(lambda i, j, k: (i, j)))
    n_extra = len(extras)
    n_out = len(out_dtypes)

    side_ins, side_shapes, side_copies, n_copies = side if side else ((), (), None, 0)
    n_side_in, n_side_out = len(side_ins), len(side_shapes)

    def finish(acc, extra_refs, out_refs):
        res = (acc,) if epilogue is None else epilogue(acc, *[e[...] for e in extra_refs])
        for o, r in zip(out_refs, res):
            o[...] = r.astype(o.dtype)

    def body(a_ref, b_ref, *rest):
        extra_refs = rest[:n_extra]
        side_in_refs = rest[n_extra:n_extra + n_side_in]
        rest = rest[n_extra + n_side_in:]
        out_refs, side_out_refs, scratch = rest[:n_out], rest[n_out:n_out + n_side_out], rest[n_out + n_side_out:]
        if side:
            step = (pl.program_id(0) * grid[1] + pl.program_id(1)) * grid[2] + pl.program_id(2)
            send_sems, recv_sems = scratch[-2:]

            @pl.when(step == 0)
            def _():
                sends, _ = side_copies(side_in_refs, side_out_refs, send_sems, recv_sems)
                for cp in sends:
                    cp.start()

        if nk == 1:
            finish(lax.dot_general(a_ref[...], b_ref[...], dims, preferred_element_type=F32), extra_refs, out_refs)
        else:
            acc = scratch[0]
            k = pl.program_id(2)

            @pl.when(k == 0)
            def _():
                acc[...] = jnp.zeros_like(acc)

            acc[...] += lax.dot_general(a_ref[...], b_ref[...], dims, preferred_element_type=F32)

            @pl.when(k == nk - 1)
            def _():
                finish(acc[...], extra_refs, out_refs)

        if side:
            @pl.when(step == grid[0] * grid[1] * grid[2] - 1)
            def _():
                sends, recvs = side_copies(side_in_refs, side_out_refs, send_sems, recv_sems)
                for cp in recvs:
                    cp.wait_recv()
                for cp in sends:
                    cp.wait_send()

    scratch_shapes = [] if nk == 1 else [pltpu.VMEM((tm, tn), F32)]
    if side:
        scratch_shapes += [pltpu.SemaphoreType.DMA((n_copies,)), pltpu.SemaphoreType.DMA((n_copies,))]
    outs = pl.pallas_call(
        body,
        name=name,
        grid=grid,
        in_specs=[a_spec, b_spec] + [x_spec] * n_extra + [ANY] * n_side_in,
        out_specs=[o_spec] * n_out + [ANY] * n_side_out,
        out_shape=[jax.ShapeDtypeStruct(o_shape, dt) for dt in out_dtypes] + list(side_shapes),
        scratch_shapes=scratch_shapes,
        compiler_params=_params(("arbitrary",) * 3 if side else ("parallel", "parallel", "arbitrary")),
    )(a, b, *extras, *side_ins)
    if side:
        return (outs[0] if n_out == 1 else outs[:n_out]), outs[n_out:]
    return outs[0] if n_out == 1 else outs


def _rmsnorm_fwd(x, g, *, name):
    s, d = x.shape
    tr = _pick(s, 256)

    def body(x_ref, g_ref, o_ref):
        xv = x_ref[...]
        r = lax.rsqrt(jnp.mean(xv * xv, axis=-1, keepdims=True) + EPS)
        o_ref[...] = (xv * r * g_ref[...]).astype(o_ref.dtype)

    return pl.pallas_call(
        body, name=name, grid=(s // tr,),
        in_specs=[pl.BlockSpec((tr, d), lambda i: (i, 0)), pl.BlockSpec((1, d), lambda i: (0, 0))],
        out_specs=pl.BlockSpec((tr, d), lambda i: (i, 0)),
        out_shape=jax.ShapeDtypeStruct((s, d), BF16),
        compiler_params=_params(("parallel",)),
    )(x, g.reshape(1, d))


def _rmsnorm_bwd(x, g, dh, dres, *, name):
    s, d = x.shape
    tr = _pick(s, 256)

    def body(x_ref, g_ref, dh_ref, dres_ref, dx_ref, dxb_ref, dg_ref):
        xv = x_ref[...]
        r = lax.rsqrt(jnp.mean(xv * xv, axis=-1, keepdims=True) + EPS)
        y = xv * r
        dhv = dh_ref[...]
        dy = dhv * g_ref[...]
        dx = dres_ref[...] + r * (dy - y * jnp.mean(dy * y, axis=-1, keepdims=True))
        dx_ref[...] = dx
        dxb_ref[...] = dx.astype(BF16)

        @pl.when(pl.program_id(0) == 0)
        def _():
            dg_ref[...] = jnp.zeros_like(dg_ref)

        dg_ref[...] += jnp.sum(dhv * y, axis=0, keepdims=True)

    row = pl.BlockSpec((tr, d), lambda i: (i, 0))
    vec = pl.BlockSpec((1, d), lambda i: (0, 0))
    dx, dxb, dg = pl.pallas_call(
        body, name=name, grid=(s // tr,),
        in_specs=[row, vec, row, row],
        out_specs=[row, row, vec],
        out_shape=[jax.ShapeDtypeStruct((s, d), F32), jax.ShapeDtypeStruct((s, d), BF16),
                   jax.ShapeDtypeStruct((1, d), F32)],
        compiler_params=_params(("arbitrary",)),
    )(x, g.reshape(1, d), dh, dres)
    return dx, dxb, dg.reshape(d)


def _loss_head(y, target):
    s, d = y.shape
    tr = _pick(s, 256)

    def body(y_ref, t_ref, dy_ref, dyb_ref, part_ref):
        err = y_ref[...] - t_ref[...]
        dy = err * (1.0 / d)
        dy_ref[...] = dy
        dyb_ref[...] = dy.astype(BF16)

        @pl.when(pl.program_id(0) == 0)
        def _():
            part_ref[...] = jnp.zeros_like(part_ref)

        part_ref[...] += jnp.sum(err * err, axis=0, keepdims=True)

    row = pl.BlockSpec((tr, d), lambda i: (i, 0))
    vec = pl.BlockSpec((1, d), lambda i: (0, 0))
    dy, dyb, part = pl.pallas_call(
        body, name="loss_head", grid=(s // tr,),
        in_specs=[row, row], out_specs=[row, row, vec],
        out_shape=[jax.ShapeDtypeStruct((s, d), F32), jax.ShapeDtypeStruct((s, d), BF16),
                   jax.ShapeDtypeStruct((1, d), F32)],
        compiler_params=_params(("arbitrary",)),
    )(y, target)
    return 0.5 * jnp.sum(part) / d, dy, dyb


def _rope_tables(s):
    half = HEAD_DIM // 2
    inv_freq = ROPE_THETA ** (-jnp.arange(half, dtype=F32) / half)
    ang = jnp.arange(s, dtype=F32)[:, None] * inv_freq[None, :]
    cos, sin = jnp.cos(ang), jnp.sin(ang)
    return jnp.concatenate([cos, cos], axis=-1), jnp.concatenate([-sin, sin], axis=-1)


PREP_ROWS = 256


def _residue_spec(r, width):
    return pl.BlockSpec((r, PREP_ROWS // r, width), lambda i: (0, i, 0))


def _to_residues(scr, tile, r, dst_ref, cols, dtype):
    scr[...] = tile
    for c in range(r):
        dst_ref[c, :, cols] = scr[pl.ds(c, PREP_ROWS // r, stride=r), :].astype(dtype)


def _from_residues(scr, src_ref, cols, r):
    for c in range(r):
        scr[pl.ds(c, PREP_ROWS // r, stride=r), :] = src_ref[c, :, cols].astype(F32)
    return scr[...]


def _qkv_prep(proj, qg, kg, cos, sin, *, name):
    s = proj.shape[0]
    tr = PREP_ROWS
    assert s % tr == 0

    def body(p_ref, qg_ref, kg_ref, cos_ref, sin_ref, o_ref, res1_ref, res2_ref, scr):
        res_refs = (None, res1_ref, res2_ref)
        c = cos_ref[...]
        sn = sin_ref[...]
        for which, g_ref in ((0, qg_ref), (1, kg_ref)):
            for g in range(N_GROUPS):
                gain = g_ref[g:g + 1, :]
                for h in range(HEADS_PER_GROUP):
                    off = which * DIL_WIDTH + g * GROUP_WIDTH + h * HEAD_DIM
                    xv = p_ref[:, off:off + HEAD_DIM]
                    r = lax.rsqrt(jnp.mean(xv * xv, axis=-1, keepdims=True) + EPS)
                    y = xv * r * gain
                    rot = y * c + pltpu.roll(y, HEAD_DIM // 2, 1) * sn
                    if g == 0:
                        o_ref[:, off:off + HEAD_DIM] = rot.astype(BF16)
                    else:
                        at = which * GROUP_WIDTH + h * HEAD_DIM
                        _to_residues(scr, rot, DIL_GROUPS[g][1], res_refs[g], slice(at, at + HEAD_DIM), BF16)
        o_ref[:, 2 * DIL_WIDTH:2 * DIL_WIDTH + GROUP_WIDTH] = p_ref[:, 2 * DIL_WIDTH:2 * DIL_WIDTH + GROUP_WIDTH].astype(BF16)
        for g in range(1, N_GROUPS):
            for h in range(HEADS_PER_GROUP):
                off = 2 * DIL_WIDTH + g * GROUP_WIDTH + h * HEAD_DIM
                at = 2 * GROUP_WIDTH + h * HEAD_DIM
                _to_residues(scr, p_ref[:, off:off + HEAD_DIM], DIL_GROUPS[g][1], res_refs[g], slice(at, at + HEAD_DIM), BF16)
        o_ref[:, 3 * DIL_WIDTH:] = p_ref[:, 3 * DIL_WIDTH:].astype(BF16)

    r1, r2 = DIL_GROUPS[1][1], DIL_GROUPS[2][1]
    return pl.pallas_call(
        body, name=name, grid=(s // tr,),
        in_specs=[pl.BlockSpec((tr, QKV_WIDTH), lambda i: (i, 0)),
                  pl.BlockSpec((N_GROUPS, HEAD_DIM), lambda i: (0, 0)),
                  pl.BlockSpec((N_GROUPS, HEAD_DIM), lambda i: (0, 0)),
                  pl.BlockSpec((tr, HEAD_DIM), lambda i: (i, 0)),
                  pl.BlockSpec((tr, HEAD_DIM), lambda i: (i, 0))],
        out_specs=[pl.BlockSpec((tr, QKV_WIDTH), lambda i: (i, 0)),
                   _residue_spec(r1, 3 * GROUP_WIDTH), _residue_spec(r2, 3 * GROUP_WIDTH)],
        out_shape=[jax.ShapeDtypeStruct((s, QKV_WIDTH), BF16),
                   jax.ShapeDtypeStruct((r1, s // r1, 3 * GROUP_WIDTH), BF16),
                   jax.ShapeDtypeStruct((r2, s // r2, 3 * GROUP_WIDTH), BF16)],
        scratch_shapes=[pltpu.VMEM((tr, HEAD_DIM), F32)],
        compiler_params=_params(("parallel",)),
    )(proj, qg, kg, cos, sin)


def _qkv_prep_bwd(proj, qg, kg, cos, sin, dqkv_dil, dqkv_sb, dgate, *, name):
    s, n_in = proj.shape
    n_gate = dgate[0].shape[1]
    tr = PREP_ROWS

    def body(p_ref, qg_ref, kg_ref, cos_ref, sin_ref, *rest):
        dil_refs = rest[:9]
        sb_refs = rest[9:12]
        dgate_refs = rest[12:14]
        o_ref, dqg_ref, dkg_ref, scr = rest[14:18]
        c = cos_ref[...]
        sn = sin_ref[...]

        def incoming(g, which, h):
            cols = slice(h * HEAD_DIM, (h + 1) * HEAD_DIM)
            if g == 0:
                return dil_refs[which][:, cols]
            return _from_residues(scr, dil_refs[3 * g + which], cols, DIL_GROUPS[g][1])

        @pl.when(pl.program_id(0) == 0)
        def _():
            dqg_ref[...] = jnp.zeros_like(dqg_ref)
            dkg_ref[...] = jnp.zeros_like(dkg_ref)

        for which, g_ref, dg_ref in ((0, qg_ref, dqg_ref), (1, kg_ref, dkg_ref)):
            for g in range(N_GROUPS):
                gain = g_ref[g:g + 1, :]
                dgain = jnp.zeros((1, HEAD_DIM), F32)
                for h in range(HEADS_PER_GROUP):
                    off = which * DIL_WIDTH + g * GROUP_WIDTH + h * HEAD_DIM
                    xv = p_ref[:, off:off + HEAD_DIM]
                    r = lax.rsqrt(jnp.mean(xv * xv, axis=-1, keepdims=True) + EPS)
                    nx = xv * r
                    dout = incoming(g, which, h)
                    dy = dout * c + pltpu.roll(dout * sn, HEAD_DIM // 2, 1)
                    dgain = dgain + jnp.sum(dy * nx, axis=0, keepdims=True)
                    dn = dy * gain
                    dxv = r * (dn - nx * jnp.mean(dn * nx, axis=-1, keepdims=True))
                    o_ref[:, off:off + HEAD_DIM] = dxv.astype(BF16)
                dg_ref[g:g + 1, :] += dgain
        for g in range(N_GROUPS):
            for h in range(HEADS_PER_GROUP):
                off = 2 * DIL_WIDTH + g * GROUP_WIDTH + h * HEAD_DIM
                o_ref[:, off:off + HEAD_DIM] = incoming(g, 2, h).astype(BF16)
        for t in range(3):
            off = 3 * DIL_WIDTH + t * SB_WIDTH
            o_ref[:, off:off + SB_WIDTH] = sb_refs[t][...].astype(BF16)
        o_ref[:, QKV_WIDTH:QKV_WIDTH + n_gate] = dgate_refs[0][...]
        o_ref[:, QKV_WIDTH + n_gate:] = dgate_refs[1][...]

    grp = pl.BlockSpec((tr, GROUP_WIDTH), lambda i: (i, 0))
    sbs = pl.BlockSpec((tr, SB_WIDTH), lambda i: (i, 0))
    gain_spec = pl.BlockSpec((N_GROUPS, HEAD_DIM), lambda i: (0, 0))
    tab = pl.BlockSpec((tr, HEAD_DIM), lambda i: (i, 0))
    flat_dil = [t for grp3 in dqkv_dil for t in grp3]
    dil_specs = [grp] * 3 + [_residue_spec(DIL_GROUPS[g][1], GROUP_WIDTH) for g in (1, 2) for _ in range(3)]
    return pl.pallas_call(
        body, name=name, grid=(s // tr,),
        in_specs=[pl.BlockSpec((tr, 2 * DIL_WIDTH), lambda i: (i, 0)), gain_spec, gain_spec, tab, tab]
                 + dil_specs + [sbs] * 3 + [pl.BlockSpec((tr, n_gate), lambda i: (i, 0))] * 2,
        out_specs=[pl.BlockSpec((tr, n_in), lambda i: (i, 0)), gain_spec, gain_spec],
        out_shape=[jax.ShapeDtypeStruct((s, n_in), BF16),
                   jax.ShapeDtypeStruct((N_GROUPS, HEAD_DIM), F32),
                   jax.ShapeDtypeStruct((N_GROUPS, HEAD_DIM), F32)],
        scratch_shapes=[pltpu.VMEM((tr, HEAD_DIM), F32)],
        compiler_params=_params(("arbitrary",)),
    )(proj, qg, kg, cos, sin, *flat_dil, *dqkv_sb, *dgate)


def _dil_masks(i):
    row = lax.broadcasted_iota(jnp.int32, (BLOCK, BLOCK), 0)
    col = lax.broadcasted_iota(jnp.int32, (BLOCK, BLOCK), 1)
    return col <= row, (col >= row) & (i > 0)


def _dil_sub(nb):
    return 2 if nb % 2 == 0 else 1


def _dot_nt(a, b):
    return lax.dot_general(a, b, (((1,), (1,)), ((), ())), preferred_element_type=F32)


def _dot_tn(a, b):
    return lax.dot_general(a, b, (((0,), (0,)), ((), ())), preferred_element_type=F32)


def _dot(a, b):
    return jnp.dot(a, b, preferred_element_type=F32)


def _dil_attn_fwd(src, g, cols, *, name):
    s = src.shape[0]
    r = DIL_GROUPS[g][1]
    nb = s // r // BLOCK
    sub = _dil_sub(nb)
    steps = nb // sub

    def body(q_ref, kp_ref, kc_ref, vp_ref, vc_ref, o_ref, ld_ref):
        i = pl.program_id(1)
        work = []
        for u in range(sub):
            rows = slice(u * BLOCK, (u + 1) * BLOCK)
            before = slice((u - 1) * BLOCK, u * BLOCK)
            mask_c, mask_p = _dil_masks(i if u == 0 else 1)
            for h in range(HEADS_PER_GROUP):
                sl = slice(h * HEAD_DIM, (h + 1) * HEAD_DIM)
                k_prev = kp_ref[:, sl] if u == 0 else kc_ref[before, sl]
                v_prev = vp_ref[:, sl] if u == 0 else vc_ref[before, sl]
                work.append((rows, sl, mask_c, mask_p, v_prev,
                             _dot_nt(q_ref[rows, sl], kc_ref[rows, sl]), _dot_nt(q_ref[rows, sl], k_prev)))
        probs = []
        for rows, sl, mask_c, mask_p, _, raw_c, raw_p in work:
            s_c = jnp.where(mask_c, raw_c * SCALE, NEG_INF)
            s_p = jnp.where(mask_p, raw_p * SCALE, NEG_INF)
            m = jnp.maximum(jnp.max(s_c, axis=-1, keepdims=True), jnp.max(s_p, axis=-1, keepdims=True))
            p_c = jnp.exp(s_c - m)
            p_p = jnp.exp(s_p - m)
            l = jnp.sum(p_c, axis=-1, keepdims=True) + jnp.sum(p_p, axis=-1, keepdims=True)
            inv = 1.0 / l
            probs.append(((p_c * inv).astype(BF16), (p_p * inv).astype(BF16)))
            ld_ref[rows, sl] = jnp.broadcast_to(m + jnp.log(l), (BLOCK, HEAD_DIM))
        for (rows, sl, _, _, v_prev, _, _), (pn_c, pn_p) in zip(work, probs):
            o_ref[rows, sl] = _dot(pn_c, vc_ref[rows, sl]) + _dot(pn_p, v_prev)

    def col(which):
        return lambda c, i: (c * steps + i, cols[which])

    def col_prev(which):
        return lambda c, i: (c * nb + jnp.maximum(sub * i - 1, 0), cols[which])

    own = (sub * BLOCK, GROUP_WIDTH)
    one = (BLOCK, GROUP_WIDTH)
    return pl.pallas_call(
        body, name=name, grid=(r, steps),
        in_specs=[pl.BlockSpec(own, col(0)), pl.BlockSpec(one, col_prev(1)), pl.BlockSpec(own, col(1)),
                  pl.BlockSpec(one, col_prev(2)), pl.BlockSpec(own, col(2))],
        out_specs=[pl.BlockSpec(own, lambda c, i: (c * steps + i, 0))] * 2,
        out_shape=[jax.ShapeDtypeStruct((s, GROUP_WIDTH), F32)] * 2,
        compiler_params=_params(("parallel", "parallel")),
    )(src, src, src, src, src)


def _dil_merge(outs, lds, *, name):
    s = outs[0].shape[0]
    tr = PREP_ROWS
    rs = [DIL_GROUPS[g][1] for g in range(N_GROUPS)]

    def body(o0, o1, o2, l0, l1, l2, yb_ref, y0_ref, y1_ref, y2_ref, e0_ref, e1_ref, e2_ref, scr_a, scr_b):
        o_refs, l_refs = (o0, o1, o2), (l0, l1, l2)
        y_refs, e_refs = (y0_ref, y1_ref, y2_ref), (e0_ref, e1_ref, e2_ref)
        for h in range(HEADS_PER_GROUP):
            cols = slice(h * HEAD_DIM, (h + 1) * HEAD_DIM)
            o = [o_refs[0][:, cols]] + [_from_residues(scr_a, o_refs[g], cols, rs[g]) for g in (1, 2)]
            a = [l_refs[0][:, cols]] + [_from_residues(scr_b, l_refs[g], cols, rs[g]) for g in (1, 2)]
            m = jnp.maximum(jnp.maximum(a[0], a[1]), a[2])
            e = [jnp.exp(t - m) for t in a]
            tot = e[0] + e[1] + e[2]
            inv = 1.0 / tot
            y = (e[0] * inv) * o[0] + (e[1] * inv) * o[1] + (e[2] * inv) * o[2]
            lse = m + jnp.log(tot)
            yb_ref[:, cols] = y.astype(BF16)
            y_refs[0][:, cols] = y
            e_refs[0][:, cols] = lse
            for g in (1, 2):
                _to_residues(scr_a, y, rs[g], y_refs[g], cols, F32)
                _to_residues(scr_b, lse, rs[g], e_refs[g], cols, F32)

    blk = pl.BlockSpec((tr, GROUP_WIDTH), lambda i: (i, 0))
    specs = [blk] + [_residue_spec(rs[g], GROUP_WIDTH) for g in (1, 2)]
    shapes = [jax.ShapeDtypeStruct((s, GROUP_WIDTH), F32)] + [
        jax.ShapeDtypeStruct((rs[g], s // rs[g], GROUP_WIDTH), F32) for g in (1, 2)]
    res = pl.pallas_call(
        body, name=name, grid=(s // tr,),
        in_specs=specs * 2, out_specs=[blk] + specs * 2,
        out_shape=[jax.ShapeDtypeStruct((s, GROUP_WIDTH), BF16)] + shapes * 2,
        scratch_shapes=[pltpu.VMEM((tr, HEAD_DIM), F32), pltpu.VMEM((tr, HEAD_DIM), F32)],
        compiler_params=_params(("parallel",)),
    )(*[t.reshape(sh.shape) for t, sh in zip(outs, shapes)], *[t.reshape(sh.shape) for t, sh in zip(lds, shapes)])
    yb, ys, lses = res[0], res[1:4], res[4:7]
    return yb, [t.reshape(s, GROUP_WIDTH) for t in ys], [t.reshape(s, GROUP_WIDTH) for t in lses]


def _dil_to_residues(x, *, name):
    s = x.shape[0]
    tr = PREP_ROWS
    rs = [DIL_GROUPS[g][1] for g in (1, 2)]

    def body(x_ref, a_ref, b_ref, scr):
        for h in range(HEADS_PER_GROUP):
            cols = slice(h * HEAD_DIM, (h + 1) * HEAD_DIM)
            tile = x_ref[:, cols]
            _to_residues(scr, tile, rs[0], a_ref, cols, F32)
            _to_residues(scr, tile, rs[1], b_ref, cols, F32)

    outs = pl.pallas_call(
        body, name=name, grid=(s // tr,),
        in_specs=[pl.BlockSpec((tr, GROUP_WIDTH), lambda i: (i, 0))],
        out_specs=[_residue_spec(r, GROUP_WIDTH) for r in rs],
        out_shape=[jax.ShapeDtypeStruct((r, s // r, GROUP_WIDTH), F32) for r in rs],
        scratch_shapes=[pltpu.VMEM((tr, HEAD_DIM), F32)],
        compiler_params=_params(("parallel",)),
    )(x)
    return [t.reshape(s, GROUP_WIDTH) for t in outs]


def _dil_attn_bwd(src, g, cols, y, lse, dy, *, name):
    s = src.shape[0]
    r = DIL_GROUPS[g][1]
    nb = s // r // BLOCK
    sub = _dil_sub(nb)
    steps = nb // sub

    def body(q_ref, qn_ref, kp_ref, kc_ref, vp_ref, vc_ref, y_ref, yn_ref, l_ref, ln_ref, d_ref, dn_ref,
             dq_ref, dk_ref, dv_ref):
        i = pl.program_id(1)
        work = []
        for u in range(sub):
            rows = slice(u * BLOCK, (u + 1) * BLOCK)
            before = slice((u - 1) * BLOCK, u * BLOCK)
            after = slice((u + 1) * BLOCK, (u + 2) * BLOCK)
            last = u == sub - 1
            mask_c, mask_p = _dil_masks(i if u == 0 else 1)
            _, mask_n = _dil_masks(jnp.where(sub * i + u + 1 < nb, 1, 0) if last else 1)
            for h in range(HEADS_PER_GROUP):
                sl = slice(h * HEAD_DIM, (h + 1) * HEAD_DIM)
                nxt = lambda own_ref, next_ref, sl=sl: next_ref[:, sl] if last else own_ref[after, sl]
                work.append(dict(
                    rows=rows, sl=sl, masks=(mask_c, mask_p, mask_n),
                    q=q_ref[rows, sl], qn=nxt(q_ref, qn_ref), kc=kc_ref[rows, sl], vc=vc_ref[rows, sl],
                    kp=kp_ref[:, sl] if u == 0 else kc_ref[before, sl],
                    vp=vp_ref[:, sl] if u == 0 else vc_ref[before, sl],
                    dy=d_ref[rows, sl], dyn=nxt(d_ref, dn_ref), y=y_ref[rows, sl], yn=nxt(y_ref, yn_ref),
                    lse=l_ref[rows, sl], lsen=nxt(l_ref, ln_ref)))
        for t in work:
            t["dyb"], t["dynb"] = t["dy"].astype(BF16), t["dyn"].astype(BF16)
            t["first"] = (_dot_nt(t["q"], t["kc"]), _dot_nt(t["q"], t["kp"]), _dot_nt(t["qn"], t["kc"]),
                          _dot_nt(t["dyb"], t["vc"]), _dot_nt(t["dyb"], t["vp"]), _dot_nt(t["dynb"], t["vc"]))
        for t in work:
            s_c, s_p, s_n, dp_c, dp_p, dp_n = t["first"]
            mask_c, mask_p, mask_n = t["masks"]
            delta = jnp.sum(t["dy"] * t["y"], axis=-1, keepdims=True)
            delta_n = jnp.sum(t["dyn"] * t["yn"], axis=-1, keepdims=True)
            p_c = jnp.where(mask_c, jnp.exp(s_c * SCALE - t["lse"]), 0.0)
            p_p = jnp.where(mask_p, jnp.exp(s_p * SCALE - t["lse"]), 0.0)
            p_n = jnp.where(mask_n, jnp.exp(s_n * SCALE - t["lsen"]), 0.0)
            t["second"] = ((p_c * (dp_c - delta) * SCALE).astype(BF16), (p_p * (dp_p - delta) * SCALE).astype(BF16),
                           (p_n * (dp_n - delta_n) * SCALE).astype(BF16), p_c.astype(BF16), p_n.astype(BF16))
        for t in work:
            ds_c, ds_p, ds_n, pb_c, pb_n = t["second"]
            rows, sl = t["rows"], t["sl"]
            dq_ref[rows, sl] = _dot(ds_c, t["kc"]) + _dot(ds_p, t["kp"])
            dk_ref[rows, sl] = _dot_tn(ds_c, t["q"]) + _dot_tn(ds_n, t["qn"])
            dv_ref[rows, sl] = _dot_tn(pb_c, t["dyb"]) + _dot_tn(pb_n, t["dynb"])

    def col(which):
        return lambda c, i: (c * steps + i, cols[which])

    def col_prev(which):
        return lambda c, i: (c * nb + jnp.maximum(sub * i - 1, 0), cols[which])

    def after_rows(c, i):
        return c * nb + jnp.minimum(sub * i + sub, nb - 1)

    rows_own = (sub * BLOCK, GROUP_WIDTH)
    one = (BLOCK, GROUP_WIDTH)
    own = pl.BlockSpec(rows_own, lambda c, i: (c * steps + i, 0))
    nxt = pl.BlockSpec(one, lambda c, i: (after_rows(c, i), 0))
    return pl.pallas_call(
        body, name=name, grid=(r, steps),
        in_specs=[pl.BlockSpec(rows_own, col(0)), pl.BlockSpec(one, lambda c, i: (after_rows(c, i), cols[0])),
                  pl.BlockSpec(one, col_prev(1)), pl.BlockSpec(rows_own, col(1)),
                  pl.BlockSpec(one, col_prev(2)), pl.BlockSpec(rows_own, col(2)),
                  own, nxt, own, nxt, own, nxt],
        out_specs=[own] * 3,
        out_shape=[jax.ShapeDtypeStruct((s, GROUP_WIDTH), F32)] * 3,
        compiler_params=_params(("parallel", "parallel")),
    )(src, src, src, src, src, src, y, y, lse, lse, dy, dy)


SB_TQ = 256
SB_GROUP = 2
SB_DEAD = -104.0
SB_NH_FWD = 4
SB_NH_BWD = 2


def _split_dot(x, w):
    hi = x.astype(BF16)
    lo = (x - hi.astype(F32)).astype(BF16)
    return _dot(hi, w) + _dot(lo, w)


def _sb_consts():
    j = lax.broadcasted_iota(jnp.int32, (BLOCK, BLOCK), 0)
    k = lax.broadcasted_iota(jnp.int32, (BLOCK, BLOCK), 1)
    ones = jnp.ones((BLOCK, BLOCK), F32)
    after = jnp.concatenate([(j > k).astype(F32), ones], axis=1).astype(BF16)
    before = jnp.concatenate([(j < k).astype(F32), ones], axis=1).astype(BF16)
    return after, before


def _sb_mask(row0, col0, tq):
    row = row0 + lax.broadcasted_iota(jnp.int32, (tq, BLOCK), 0)
    col = col0 + lax.broadcasted_iota(jnp.int32, (tq, BLOCK), 1)
    return col < row


def _softplus(z):
    return jnp.maximum(z, 0.0) + jnp.log1p(jnp.exp(-jnp.abs(z)))


def _sb_attn_fwd(qkv, *, name):
    s = qkv.shape[0]
    tq = _pick(s, SB_TQ)
    nq = s // tq
    per = tq // BLOCK
    base = 3 * DIL_WIDTH // HEAD_DIM
    after, _ = _sb_consts()

    nh = SB_NH_FWD
    wide = nh * HEAD_DIM

    def body(q_ref, k_ref, v_ref, w_ref, o_ref, rc_ref, trips_ref, acc_ref, run_ref):
        qi = pl.program_id(1)
        w = w_ref[...]
        acc_ref[...] = jnp.zeros_like(acc_ref)
        run_ref[...] = jnp.zeros_like(run_ref)
        rc_ref[...] = jnp.zeros_like(rc_ref)
        lane = lax.broadcasted_iota(jnp.int32, (tq, BLOCK), 1)
        n_chunks = (qi + 1) * per

        def chunks(js, masked):
            sls = [slice(h * HEAD_DIM, (h + 1) * HEAD_DIM) for h in range(nh)]
            work = [(sl, j, pl.multiple_of(j * BLOCK, BLOCK), _sb_mask(qi * tq, j * BLOCK, tq) if masked else None)
                    for j in js for sl in sls]
            zs = [_dot_nt(q_ref[:, sl], k_ref[pl.ds(start, BLOCK), sl]) * SCALE for sl, _, start, _ in work]
            logits, sums = [], []
            for z, (_, _, _, mask) in zip(zs, work):
                sp = _softplus(z)
                logits.append(z - sp)
                sums.append(_split_dot(jnp.where(mask, -sp, 0.0) if masked else -sp, w))
            for (sl, j, start, mask), lg, sm in zip(work, logits, sums):
                run = run_ref[:, sl]
                a = jnp.exp(lg + run + sm[:, :BLOCK])
                if masked:
                    a = jnp.where(mask, a, 0.0)
                acc_ref[:, sl] += _dot(a.astype(BF16), v_ref[pl.ds(start, BLOCK), sl])
                rc_ref[:, sl] = jnp.where(lane == j, run, rc_ref[:, sl])
                run_ref[:, sl] = run + sm[:, BLOCK:]

        def diagonal(t, carry):
            last = n_chunks - 1 - SB_GROUP * t
            chunks([last - g for g in range(SB_GROUP)], True)
            return carry

        def alive():
            return (jnp.max(run_ref[...]) > SB_DEAD).astype(jnp.int32)

        def more(carry):
            t, live = carry
            return jnp.logical_and(t < n_left, live > 0)

        def left(carry):
            t, _ = carry
            last = n_chunks - 1 - per - SB_GROUP * t
            chunks([last - g for g in range(SB_GROUP)], False)
            return t + 1, alive()

        lax.fori_loop(0, per // SB_GROUP, diagonal, 0)
        n_left = (n_chunks - per) // SB_GROUP
        trips, _ = lax.while_loop(more, left, (jnp.int32(0), alive()))
        trips_ref[pl.program_id(0), qi] = trips
        o_ref[...] = acc_ref[...].astype(o_ref.dtype)

    head = lambda off: (lambda h, i: (0, (base + off) // nh + h))
    once = pl.Buffered(1)
    return pl.pallas_call(
        body, name=name, grid=(SB_HEADS // nh, nq),
        in_specs=[pl.BlockSpec((tq, wide), lambda h, i: (i, base // nh + h)),
                  pl.BlockSpec((s, wide), head(SB_HEADS), pipeline_mode=once),
                  pl.BlockSpec((s, wide), head(2 * SB_HEADS), pipeline_mode=once),
                  pl.BlockSpec((BLOCK, 2 * BLOCK), lambda h, i: (0, 0))],
        out_specs=[pl.BlockSpec((tq, wide), lambda h, i: (i, h))] * 2 + [pl.BlockSpec(memory_space=pltpu.SMEM)],
        out_shape=[jax.ShapeDtypeStruct((s, SB_WIDTH), BF16), jax.ShapeDtypeStruct((s, SB_WIDTH), F32),
                   jax.ShapeDtypeStruct((SB_HEADS // nh, nq), jnp.int32)],
        scratch_shapes=[pltpu.VMEM((tq, wide), F32), pltpu.VMEM((tq, wide), F32)],
        compiler_params=_params(("arbitrary", "arbitrary")),
    )(qkv, qkv, qkv, after)


def _sb_attn_bwd(qkv, rc, trips, do, *, name):
    s = qkv.shape[0]
    tq = _pick(s, SB_TQ)
    nq = s // tq
    per = tq // BLOCK
    base = 3 * DIL_WIDTH // HEAD_DIM
    _, before = _sb_consts()

    nh = SB_NH_BWD
    wide = nh * HEAD_DIM

    def body(trips_ref, q_ref, k_ref, v_ref, w_ref, rc_ref, do_ref, dq_ref, dk_ref, dv_ref, acc_ref, pre_ref, dob_ref):
        qi = pl.program_id(1)

        @pl.when(qi == 0)
        def _():
            dk_ref[...] = jnp.zeros_like(dk_ref)
            dv_ref[...] = jnp.zeros_like(dv_ref)

        w = w_ref[...]
        wa = (lax.broadcasted_iota(jnp.int32, (BLOCK, BLOCK), 0)
              > lax.broadcasted_iota(jnp.int32, (BLOCK, BLOCK), 1)).astype(BF16)
        dob_ref[...] = do_ref[...].astype(BF16)
        acc_ref[...] = jnp.zeros_like(acc_ref)
        pre_ref[...] = jnp.zeros_like(pre_ref)
        lane = lax.broadcasted_iota(jnp.int32, (tq, BLOCK), 1)
        n_chunks = (qi + 1) * per

        def chunks(js, masked):
            sls = [slice(h * HEAD_DIM, (h + 1) * HEAD_DIM) for h in range(nh)]
            work = [(sl, j, pl.multiple_of(j * BLOCK, BLOCK), _sb_mask(qi * tq, j * BLOCK, tq) if masked else None)
                    for j in js for sl in sls]
            zs = [_dot_nt(q_ref[:, sl], k_ref[pl.ds(start, BLOCK), sl]) * SCALE for sl, _, start, _ in work]
            das = [_dot_nt(dob_ref[:, sl], v_ref[pl.ds(start, BLOCK), sl]) for sl, _, start, _ in work]
            sigs, betweens = [], []
            for z, (_, _, _, mask) in zip(zs, work):
                sp = _softplus(z)
                sigs.append(jnp.exp(z - sp))
                betweens.append(_split_dot(jnp.where(mask, -sp, 0.0) if masked else -sp, wa))
            es, avs, sums = [], [], []
            for (sl, j, _, mask), sig, bt, da in zip(work, sigs, betweens, das):
                run = jnp.sum(jnp.where(lane == j, rc_ref[:, sl], 0.0), axis=-1, keepdims=True)
                a = sig * jnp.exp(run + bt)
                if masked:
                    a = jnp.where(mask, a, 0.0)
                e = a * da
                es.append(e)
                avs.append(a.astype(BF16))
                sums.append(_split_dot(e, w))
            for (sl, _, start, mask), sig, e, ab, sm in zip(work, sigs, es, avs, sums):
                pre = pre_ref[:, sl]
                dz = (e * (1.0 - sig) - sig * (pre + sm[:, :BLOCK])) * SCALE
                if masked:
                    dz = jnp.where(mask, dz, 0.0)
                dz = dz.astype(BF16)
                pre_ref[:, sl] = pre + sm[:, BLOCK:]
                acc_ref[:, sl] += _dot(dz, k_ref[pl.ds(start, BLOCK), sl])
                dk_ref[pl.ds(start, BLOCK), sl] += _dot_tn(dz, q_ref[:, sl])
                dv_ref[pl.ds(start, BLOCK), sl] += _dot_tn(ab, dob_ref[:, sl])

        def left(t, carry):
            chunks([SB_GROUP * t + g for g in range(SB_GROUP)], False)
            return carry

        def diagonal(t, carry):
            chunks([n_chunks - per + SB_GROUP * t + g for g in range(SB_GROUP)], True)
            return carry

        n_left = (n_chunks - per) // SB_GROUP
        ran = trips_ref[pl.program_id(0) // (SB_NH_FWD // nh), qi]
        lax.fori_loop(n_left - ran, n_left, left, 0)
        lax.fori_loop(0, per // SB_GROUP, diagonal, 0)
        dq_ref[...] = acc_ref[...]

    head = lambda off: (lambda h, i, t: (0, (base + off) // nh + h))
    blk = pl.BlockSpec((tq, wide), lambda h, i, t: (i, h))
    whole = pl.BlockSpec((s, wide), lambda h, i, t: (0, h))
    once = pl.Buffered(1)
    grid_spec = pltpu.PrefetchScalarGridSpec(
        num_scalar_prefetch=1, grid=(SB_HEADS // nh, nq),
        in_specs=[pl.BlockSpec((tq, wide), lambda h, i, t: (i, base // nh + h)),
                  pl.BlockSpec((s, wide), head(SB_HEADS), pipeline_mode=once),
                  pl.BlockSpec((s, wide), head(2 * SB_HEADS), pipeline_mode=once),
                  pl.BlockSpec((BLOCK, 2 * BLOCK), lambda h, i, t: (0, 0)),
                  blk, blk],
        out_specs=[blk, whole, whole],
        scratch_shapes=[pltpu.VMEM((tq, wide), F32), pltpu.VMEM((tq, wide), F32), pltpu.VMEM((tq, wide), BF16)],
    )
    return pl.pallas_call(
        body, name=name, grid_spec=grid_spec,
        out_shape=[jax.ShapeDtypeStruct((s, SB_WIDTH), F32)] * 3,
        compiler_params=_params(("parallel", "arbitrary")),
    )(trips, qkv, qkv, qkv, before, rc, do)


GATE_TC = 512


def _gate_fwd(proj, gate_b, up_dil, up_sb, *, name):
    s, d = up_dil.shape
    tr = _pick(s, 512)
    g0 = QKV_WIDTH // GATE_TC
    nc = d // GATE_TC

    def body(ga_ref, gb_ref, b_ref, ud_ref, us_ref, o_ref):
        ga = jax.nn.sigmoid(ga_ref[...] + b_ref[0:1, :])
        gb = jax.nn.sigmoid(gb_ref[...] + b_ref[1:2, :])
        o_ref[...] = (ga * ud_ref[...] + gb * us_ref[...]).astype(o_ref.dtype)

    blk = pl.BlockSpec((tr, GATE_TC), lambda i, j: (i, j))
    return pl.pallas_call(
        body, name=name, grid=(s // tr, nc),
        in_specs=[pl.BlockSpec((tr, GATE_TC), lambda i, j: (i, g0 + j)),
                  pl.BlockSpec((tr, GATE_TC), lambda i, j: (i, g0 + nc + j)),
                  pl.BlockSpec((2, GATE_TC), lambda i, j: (0, j)), blk, blk],
        out_specs=blk,
        out_shape=jax.ShapeDtypeStruct((s, d), BF16),
        compiler_params=_params(("parallel", "parallel")),
    )(proj, proj, gate_b, up_dil, up_sb)


def _gate_bwd(proj, gate_b, up_dil, up_sb, dmixed, *, name):
    s, d = up_dil.shape
    tr = _pick(s, 512)
    g0 = QKV_WIDTH // GATE_TC
    nc = d // GATE_TC

    def body(ga_ref, gb_ref, b_ref, ud_ref, us_ref, dm_ref, dud_ref, dus_ref, dga_ref, dgb_ref, db_ref):
        ga = jax.nn.sigmoid(ga_ref[...] + b_ref[0:1, :])
        gb = jax.nn.sigmoid(gb_ref[...] + b_ref[1:2, :])
        dm = dm_ref[...]
        dud_ref[...] = (dm * ga).astype(BF16)
        dus_ref[...] = (dm * gb).astype(BF16)
        dla = dm * ud_ref[...] * ga * (1.0 - ga)
        dlb = dm * us_ref[...] * gb * (1.0 - gb)
        dga_ref[...] = dla.astype(BF16)
        dgb_ref[...] = dlb.astype(BF16)

        @pl.when(pl.program_id(1) == 0)
        def _():
            db_ref[...] = jnp.zeros_like(db_ref)

        db_ref[0:1, :] += jnp.sum(dla, axis=0, keepdims=True)
        db_ref[1:2, :] += jnp.sum(dlb, axis=0, keepdims=True)

    blk = pl.BlockSpec((tr, GATE_TC), lambda j, i: (i, j))
    dud, dus, dga, dgb, db = pl.pallas_call(
        body, name=name, grid=(nc, s // tr),
        in_specs=[pl.BlockSpec((tr, GATE_TC), lambda j, i: (i, g0 + j)),
                  pl.BlockSpec((tr, GATE_TC), lambda j, i: (i, g0 + nc + j)),
                  pl.BlockSpec((2, GATE_TC), lambda j, i: (0, j)), blk, blk, blk],
        out_specs=[blk, blk, blk, blk, pl.BlockSpec((2, GATE_TC), lambda j, i: (0, j))],
        out_shape=[jax.ShapeDtypeStruct((s, d), BF16)] * 4 + [jax.ShapeDtypeStruct((2, d), F32)],
        compiler_params=_params(("parallel", "arbitrary")),
    )(proj, proj, gate_b, up_dil, up_sb, dmixed)
    return dud, dus, (dga, dgb), db


def _adamw(w, g, m, v, *, name):
    shape = w.shape
    cols = shape[-1]
    rows = w.size // cols
    tr = _pick(rows, max(8, (512 * 1024) // cols))
    if rows % tr or (tr % 8 and tr != rows):
        tr = rows
    c1 = 1.0 - ADAM_B1 ** ADAM_STEP
    c2 = 1.0 - ADAM_B2 ** ADAM_STEP

    def body(w_ref, g_ref, m_ref, v_ref, d_ref, nm_ref, nv_ref):
        gv = g_ref[...]
        nm = ADAM_B1 * m_ref[...] + (1.0 - ADAM_B1) * gv
        nv = ADAM_B2 * v_ref[...] + (1.0 - ADAM_B2) * (gv * gv)
        d_ref[...] = -ADAM_LR * ((nm / c1) / (jnp.sqrt(nv / c2) + ADAM_EPS) + ADAM_WD * w_ref[...])
        nm_ref[...] = nm
        nv_ref[...] = nv

    blk = pl.BlockSpec((tr, cols), lambda i: (i, 0))
    outs = pl.pallas_call(
        body, name=name, grid=(rows // tr,),
        in_specs=[blk] * 4, out_specs=[blk] * 3,
        out_shape=[jax.ShapeDtypeStruct((rows, cols), F32)] * 3,
        compiler_params=_params(("parallel",)),
    )(*(t.reshape(rows, cols) for t in (w, g, m, v)))
    return tuple(t.reshape(shape) for t in outs)


def _adamw_layers(w, grads, m, v, *, name):
    depth, r, cols = w.shape
    tr = _pick(r, max(8, (512 * 1024) // cols))
    per = r // tr
    c1 = 1.0 - ADAM_B1 ** ADAM_STEP
    c2 = 1.0 - ADAM_B2 ** ADAM_STEP

    def body(w_ref, m_ref, v_ref, *rest):
        g_refs, (g_ref, d_ref, nm_ref, nv_ref) = rest[:depth], rest[depth:]
        layer = pl.program_id(0) // per
        gv = g_refs[0][...]
        for l in range(1, depth):
            gv = jnp.where(layer == l, g_refs[l][...], gv)
        nm = ADAM_B1 * m_ref[...] + (1.0 - ADAM_B1) * gv
        nv = ADAM_B2 * v_ref[...] + (1.0 - ADAM_B2) * (gv * gv)
        g_ref[...] = gv
        d_ref[...] = -ADAM_LR * ((nm / c1) / (jnp.sqrt(nv / c2) + ADAM_EPS) + ADAM_WD * w_ref[...])
        nm_ref[...] = nm
        nv_ref[...] = nv

    def of_layer(l):
        return pl.BlockSpec((tr, cols), lambda i: (jnp.clip(i - l * per, 0, per - 1), 0))

    blk = pl.BlockSpec((tr, cols), lambda i: (i, 0))
    outs = pl.pallas_call(
        body, name=name, grid=(depth * per,),
        in_specs=[blk] * 3 + [of_layer(l) for l in range(depth)], out_specs=[blk] * 4,
        out_shape=[jax.ShapeDtypeStruct((depth * r, cols), F32)] * 4,
        compiler_params=_params(("arbitrary",)),
    )(*(t.reshape(depth * r, cols) for t in (w, m, v)), *grads)
    return tuple(t.reshape(w.shape) for t in outs)


def _add_half(full, other, which, *, name):
    n, rh, c = other.shape
    tr = _pick(rh, max(8, (1024 * 1024) // c))
    nb = rh // tr

    def body(which_ref, a_ref, b_ref, o_ref):
        o_ref[...] = (a_ref[...] + b_ref[...]).astype(o_ref.dtype)

    grid_spec = pltpu.PrefetchScalarGridSpec(
        num_scalar_prefetch=1, grid=(n, nb),
        in_specs=[pl.BlockSpec((None, tr, c), lambda s, i, w: (s, w[0] * nb + i, 0)),
                  pl.BlockSpec((None, tr, c), lambda s, i, w: (s, i, 0))],
        out_specs=pl.BlockSpec((None, tr, c), lambda s, i, w: (s, i, 0)),
    )
    return pl.pallas_call(
        body, name=name, grid_spec=grid_spec,
        out_shape=jax.ShapeDtypeStruct((n, rh, c), BF16),
        compiler_params=_params(("parallel", "parallel")),
    )(which, full, other)


def _sum_chips(own, parts, chip, *, name):
    n, r, c = parts.shape
    tr = _pick(r, max(16, (1024 * 1024) // c))

    def body(chip_ref, own_ref, p1_ref, p2_ref, p3_ref, o_ref):
        acc = own_ref[...].astype(F32)
        for p in (p1_ref, p2_ref, p3_ref):
            acc = acc + p[...].astype(F32)
        o_ref[...] = acc

    def slot(k):
        return pl.BlockSpec((None, tr, c), lambda i, o: (o[0] ^ k, i, 0))

    grid_spec = pltpu.PrefetchScalarGridSpec(
        num_scalar_prefetch=1, grid=(r // tr,),
        in_specs=[slot(0), slot(1), slot(2), slot(3)],
        out_specs=pl.BlockSpec((tr, c), lambda i, o: (i, 0)),
    )
    return pl.pallas_call(
        body, name=name, grid_spec=grid_spec,
        out_shape=jax.ShapeDtypeStruct((r, c), F32),
        compiler_params=_params(("parallel",)),
    )(chip, own, parts, parts, parts)


def _place_piece(full, piece, index, *, name):
    a, n, r, c = full.shape
    tr = _pick(r, max(16, (1024 * 1024) // c))

    def body(index_ref, piece_ref, full_ref, o_ref):
        o_ref[...] = piece_ref[...]

    grid_spec = pltpu.PrefetchScalarGridSpec(
        num_scalar_prefetch=1, grid=(a, r // tr),
        in_specs=[pl.BlockSpec((None, tr, c), lambda l, i, k: (l, i, 0)), ANY],
        out_specs=pl.BlockSpec((None, None, tr, c), lambda l, i, k: (l, k[0], i, 0)),
    )
    return pl.pallas_call(
        body, name=name, grid_spec=grid_spec,
        out_shape=jax.ShapeDtypeStruct(full.shape, full.dtype),
        input_output_aliases={2: 0},
        compiler_params=_params(("parallel", "parallel")),
    )(index, piece, full)


ANY = pl.BlockSpec(memory_space=pl.ANY)


def _place():
    x, y, c = lax.axis_index("x"), lax.axis_index("y"), lax.axis_index("c")
    chips = [(1 - x, y), (x, 1 - y), (1 - x, 1 - y)]
    return x, y, c, chips


def _remote(src, dst, send_sem, recv_sem, to):
    return pltpu.make_async_remote_copy(src_ref=src, dst_ref=dst, send_sem=send_sem, recv_sem=recv_sem,
                                        device_id=to, device_id_type=MESH)


def _half(ref, chip, which):
    rh = ref.shape[1] // 2
    return ref.at[chip, pl.ds(which * rh, rh), :]


def _gather_ici_copies(layer):
    def copies(ins, outs, send_sems, recv_sems):
        x, y, c, chips = _place()
        me = 2 * x + y
        sends, recvs = [], []
        for t in range(len(ins)):
            rh = ins[t].shape[1] // 2
            for k, (px, py) in enumerate(chips):
                sems = (send_sems.at[3 * t + k], recv_sems.at[3 * t + k])
                sends.append(_remote(ins[t].at[layer, pl.ds(c * rh, rh), :], _half(outs[t], me, c), *sems, (px, py, c)))
                landed = _half(outs[t], 2 * px + py, c)
                recvs.append(_remote(landed, landed, *sems, (px, py, c)))
        return sends, recvs
    return copies


def _gather_pass_copies(ins, outs, send_sems, recv_sems):
    x, y, c, chips = _place()
    sibling = (x, y, 1 - c)
    sends, recvs = [], []
    for t in range(len(ins)):
        for k, (px, py) in enumerate(chips):
            sems = (send_sems.at[3 * t + k], recv_sems.at[3 * t + k])
            sends.append(_remote(_half(ins[t], 2 * px + py, c), _half(outs[t], 2 * px + py, c), *sems, sibling))
            other = _half(outs[t], 2 * px + py, 1 - c)
            recvs.append(_remote(other, other, *sems, sibling))
    return sends, recvs


def _gathered_shapes(shards):
    return [jax.ShapeDtypeStruct((N_CHIPS,) + w.shape[1:], w.dtype) for w in shards]


def _gather_ici_side(shards, layer):
    return (tuple(shards), _gathered_shapes(shards), _gather_ici_copies(layer), 3 * len(shards))


def _gather_layer(shards, layer, *, name):
    n = len(shards)
    ici = _gather_ici_copies(layer)

    def body(*refs):
        ins, outs = refs[:n], refs[n:2 * n]
        send_sems, recv_sems, pass_send, pass_recv = refs[2 * n:]
        sends, recvs = ici(ins, outs, send_sems, recv_sems)
        passes, arrivals = _gather_pass_copies(outs, outs, pass_send, pass_recv)
        for cp in sends:
            cp.start()
        for landed, onward in zip(recvs, passes):
            landed.wait_recv()
            onward.start()
        for cp in arrivals:
            cp.wait_recv()
        for cp in sends + passes:
            cp.wait_send()

    return pl.pallas_call(
        body, name=name,
        in_specs=[ANY] * n, out_specs=[ANY] * n,
        out_shape=_gathered_shapes(shards),
        scratch_shapes=[pltpu.SemaphoreType.DMA((3 * n,)) for _ in range(4)],
        compiler_params=pltpu.CompilerParams(has_side_effects=True),
    )(*shards)


def _gather_pass(landed, *, name):
    n = len(landed)

    def body(*refs):
        ins, outs = refs[:n], refs[n:2 * n]
        send_sems, recv_sems = refs[2 * n:]
        passes, arrivals = _gather_pass_copies(ins, outs, send_sems, recv_sems)
        for cp in passes:
            cp.start()
        for cp in arrivals:
            cp.wait_recv()
        for cp in passes:
            cp.wait_send()

    return pl.pallas_call(
        body, name=name,
        in_specs=[ANY] * n, out_specs=[ANY] * n,
        out_shape=[jax.ShapeDtypeStruct(w.shape, w.dtype) for w in landed],
        input_output_aliases={t: t for t in range(n)},
        scratch_shapes=[pltpu.SemaphoreType.DMA((3 * n,)), pltpu.SemaphoreType.DMA((3 * n,))],
        compiler_params=pltpu.CompilerParams(has_side_effects=True),
    )(*landed)


def _pair_copies(ins, outs, send_sems, recv_sems):
    x, y, c, _ = _place()
    sibling = (x, y, 1 - c)
    sends, recvs = [], []
    for t in range(len(ins)):
        rh = ins[t].shape[1] // 2
        sems = (send_sems.at[t], recv_sems.at[t])
        sends.append(_remote(ins[t].at[:, pl.ds((1 - c) * rh, rh), :], outs[t], *sems, sibling))
        recvs.append(_remote(outs[t], outs[t], *sems, sibling))
    return sends, recvs


def _pair_shapes(grads):
    return [jax.ShapeDtypeStruct((g.shape[0], g.shape[1] // 2, g.shape[2]), g.dtype) for g in grads]


def _pair_side(grads):
    return (tuple(grads), _pair_shapes(grads), _pair_copies, len(grads))


def _pair_swap_halves(grads, *, name):
    n = len(grads)

    def body(*refs):
        sends, recvs = _pair_copies(refs[:n], refs[n:2 * n], *refs[2 * n:])
        for cp in sends:
            cp.start()
        for cp in recvs:
            cp.wait_recv()
        for cp in sends:
            cp.wait_send()

    return pl.pallas_call(
        body, name=name,
        in_specs=[ANY] * n, out_specs=[ANY] * n,
        out_shape=_pair_shapes(grads),
        scratch_shapes=[pltpu.SemaphoreType.DMA((n,)), pltpu.SemaphoreType.DMA((n,))],
        compiler_params=pltpu.CompilerParams(has_side_effects=True),
    )(*grads)


def _chip_copies(ins, outs, send_sems, recv_sems):
    x, y, c, chips = _place()
    me = 2 * x + y
    sends, recvs = [], []
    for t in range(len(ins)):
        for k, (px, py) in enumerate(chips):
            sems = (send_sems.at[3 * t + k], recv_sems.at[3 * t + k])
            sends.append(_remote(ins[t].at[2 * px + py], outs[t].at[me], *sems, (px, py, c)))
            theirs = outs[t].at[2 * px + py]
            recvs.append(_remote(theirs, theirs, *sems, (px, py, c)))
    return sends, recvs


def _chip_side(parts):
    return (tuple(parts), [jax.ShapeDtypeStruct(p.shape, p.dtype) for p in parts], _chip_copies, 3 * len(parts))


def _chip_exchange(parts, *, name):
    n = len(parts)

    def body(*refs):
        sends, recvs = _chip_copies(refs[:n], refs[n:2 * n], *refs[2 * n:])
        for cp in sends:
            cp.start()
        for cp in recvs:
            cp.wait_recv()
        for cp in sends:
            cp.wait_send()

    return pl.pallas_call(
        body, name=name,
        in_specs=[ANY] * n, out_specs=[ANY] * n,
        out_shape=[jax.ShapeDtypeStruct(p.shape, p.dtype) for p in parts],
        scratch_shapes=[pltpu.SemaphoreType.DMA((3 * n,)), pltpu.SemaphoreType.DMA((3 * n,))],
        compiler_params=pltpu.CompilerParams(has_side_effects=True),
    )(*parts)


def _pair_join_halves(halves, *, name):
    n = len(halves)

    def body(*refs):
        ins, outs = refs[:n], refs[n:2 * n]
        send_sems, recv_sems = refs[2 * n:]
        x, y, c, _ = _place()
        sibling = (x, y, 1 - c)
        cps = []
        for t in range(n):
            rh = ins[t].shape[0]
            cp = _remote(ins[t], outs[t].at[pl.ds(c * rh, rh), :], send_sems.at[t], recv_sems.at[t], sibling)
            cp.start()
            cps.append(cp)
        for t, cp in enumerate(cps):
            rh = ins[t].shape[0]
            theirs = outs[t].at[pl.ds((1 - c) * rh, rh), :]
            _remote(theirs, theirs, send_sems.at[t], recv_sems.at[t], sibling).wait_recv()
            cp.wait_send()

    return pl.pallas_call(
        body, name=name,
        in_specs=[ANY] * n, out_specs=[ANY] * n,
        out_shape=[jax.ShapeDtypeStruct((2 * h.shape[0], h.shape[1]), h.dtype) for h in halves],
        scratch_shapes=[pltpu.SemaphoreType.DMA((n,)), pltpu.SemaphoreType.DMA((n,))],
        compiler_params=pltpu.CompilerParams(has_side_effects=True),
    )(*halves)


def _exchange_small(buf, *, name):
    rows, lanes = buf.shape

    def body(in_ref, out_ref, send_sems, recv_sems, local_sem):
        x, y, c = lax.axis_index("x"), lax.axis_index("y"), lax.axis_index("c")
        me = 4 * x + 2 * y + c
        lc = pltpu.make_async_copy(in_ref, out_ref.at[me], local_sem)
        lc.start()
        cps = []
        for k in range(1, 8):
            fx, fy, fc = (k >> 2) & 1, (k >> 1) & 1, k & 1
            to = (x ^ fx, y ^ fy, c ^ fc)
            cp = _remote(in_ref, out_ref.at[me], send_sems.at[k - 1], recv_sems.at[k - 1], to)
            cp.start()
            cps.append(cp)
        for k in range(1, 8):
            fx, fy, fc = (k >> 2) & 1, (k >> 1) & 1, k & 1
            theirs = out_ref.at[4 * (x ^ fx) + 2 * (y ^ fy) + (c ^ fc)]
            _remote(theirs, theirs, send_sems.at[k - 1], recv_sems.at[k - 1], (x ^ fx, y ^ fy, c ^ fc)).wait_recv()
        for cp in cps:
            cp.wait_send()
        lc.wait()

    return pl.pallas_call(
        body, name=name,
        in_specs=[ANY], out_specs=ANY,
        out_shape=jax.ShapeDtypeStruct((8, rows, lanes), buf.dtype),
        scratch_shapes=[pltpu.SemaphoreType.DMA((7,)), pltpu.SemaphoreType.DMA((7,)), pltpu.SemaphoreType.DMA],
        compiler_params=pltpu.CompilerParams(has_side_effects=True),
    )(buf)


def _sum_slots(slots, *, name):
    n, rows, lanes = slots.shape

    def body(s_ref, o_ref):
        acc = s_ref[0]
        for t in range(1, n):
            acc = acc + s_ref[t]
        o_ref[...] = acc

    return pl.pallas_call(
        body, name=name,
        out_shape=jax.ShapeDtypeStruct((rows, lanes), slots.dtype),
    )(slots)


def _relu2(acc):
    r = jnp.maximum(acc, 0.0)
    return acc, r * r


def _with_side(result, side):
    return result if side else (result, ())


def _layer_fwd(x, w, cos, sin, tag, sides=None, rest=None):
    sides = sides or {}
    got = {}
    h = _rmsnorm_fwd(x, w["norm1_g"], name=f"norm1_{tag}")
    proj, got["proj"] = _with_side(_matmul(h, w["w_in"], mode="nn", b_sharded=True, tm=512, side=sides.get("proj"),
                                           name=f"proj_{tag}"), sides.get("proj"))
    if rest:
        w = {**w, **rest(got["proj"])}
    qkv, res1, res2 = _qkv_prep(proj, w["q_norm_g"], w["k_norm_g"], cos, sin, name=f"qkv_prep_{tag}")
    dil_src = [(qkv, (0, 3, 6))] + [(t.reshape(-1, 3 * GROUP_WIDTH), (0, 1, 2)) for t in (res1, res2)]
    outs, lds = [], []
    for g, (src, cols) in enumerate(dil_src):
        o_g, ld_g = _dil_attn_fwd(src, g, cols, name=f"dil{g}_fwd_{tag}")
        outs.append(o_g)
        lds.append(ld_g)
    yb, y, lse = _dil_merge(outs, lds, name=f"dil_merge_{tag}")
    o_sb, rc, trips = _sb_attn_fwd(qkv, name=f"sb_fwd_{tag}")
    up_dil = _matmul(yb, w["w_up_dil"], mode="nn", b_sharded=True, name=f"up_dil_{tag}")
    up_sb = _matmul(o_sb, w["w_up_sb"], mode="nn", b_sharded=True, name=f"up_sb_{tag}")
    mixed = _gate_fwd(proj, w["gate_b"], up_dil, up_sb, name=f"gate_{tag}")
    x1, got["out"] = _with_side(_matmul(mixed, w["w_out"], mode="nn", extras=(x,), epilogue=lambda acc, res: (acc + res,),
                                        side=sides.get("out"), name=f"out_{tag}"), sides.get("out"))
    h2 = _rmsnorm_fwd(x1, w["norm2_g"], name=f"norm2_{tag}")
    (u, a), got["ff1"] = _with_side(_matmul(h2, w["w_ff1"], mode="nn", b_sharded=True, out_dtypes=(F32, BF16),
                                            epilogue=_relu2, side=sides.get("ff1"), name=f"ff1_{tag}"), sides.get("ff1"))
    x2, got["ff2"] = _with_side(_matmul(a, w["w_ff2"], mode="nn", extras=(x1,), epilogue=lambda acc, res: (acc + res,),
                                        side=sides.get("ff2"), name=f"ff2_{tag}"), sides.get("ff2"))
    saved = dict(x=x, h=h, proj=proj, qkv=qkv, dil_src=dil_src, y=y, yb=yb, lse=lse, o_sb=o_sb, rc=rc, trips=trips, up_dil=up_dil, up_sb=up_sb,
                 mixed=mixed, x1=x1, h2=h2, u=u, a=a, w=w)
    return x2, saved, got


def _layer_bwd(dx, dxb, sv, cos, sin, tag, sides=None, late=None):
    sides = dict(sides or {})
    w = sv["w"]
    grads, got = {}, {}
    du, got["d_a"] = _with_side(_matmul(dxb, w["w_ff2"], mode="nt", extras=(sv["u"],), out_dtypes=(BF16,),
                                        epilogue=lambda acc, u: (acc * (2.0 * jnp.maximum(u, 0.0)),),
                                        side=sides.get("d_a"), name=f"d_a_{tag}"), sides.get("d_a"))
    grads["w_ff2"], got["dw_ff2"] = _with_side(_matmul(sv["a"], dxb, mode="tn", side=sides.get("dw_ff2"),
                                                       name=f"dw_ff2_{tag}"), sides.get("dw_ff2"))
    dh2, got["d_h2"] = _with_side(_matmul(du, w["w_ff1"], mode="nt", b_sharded=True, side=sides.get("d_h2"),
                                          name=f"d_h2_{tag}"), sides.get("d_h2"))
    grads["w_ff1"] = _matmul(sv["h2"], du, mode="tn", out_sharded=True, name=f"dw_ff1_{tag}")
    dx1, dx1b, grads["norm2_g"] = _rmsnorm_bwd(sv["x1"], w["norm2_g"], dh2, dx, name=f"norm2_bwd_{tag}")
    dmixed = _matmul(dx1b, w["w_out"], mode="nt", name=f"d_mixed_{tag}")
    grads["w_out"] = _matmul(sv["mixed"], dx1b, mode="tn", name=f"dw_out_{tag}")
    dud, dus, dgate, grads["gate_b"] = _gate_bwd(sv["proj"], w["gate_b"], sv["up_dil"], sv["up_sb"], dmixed,
                                                 name=f"gate_bwd_{tag}")
    dy_dil = _matmul(dud, w["w_up_dil"], mode="nt", b_sharded=True, name=f"d_ydil_{tag}")
    grads["w_up_dil"] = _matmul(sv["yb"], dud, mode="tn", out_sharded=True, name=f"dw_up_dil_{tag}")
    dy_sb = _matmul(dus, w["w_up_sb"], mode="nt", b_sharded=True, name=f"d_ysb_{tag}")
    grads["w_up_sb"] = _matmul(sv["o_sb"], dus, mode="tn", out_sharded=True, name=f"dw_up_sb_{tag}")
    if late:
        sides.update(late(grads))
    dys = [dy_dil] + _dil_to_residues(dy_dil, name=f"dil_dy_{tag}")
    d_dil = [_dil_attn_bwd(src, g, cols, sv["y"][g], sv["lse"][g], dys[g], name=f"dil{g}_bwd_{tag}")
             for g, (src, cols) in enumerate(sv["dil_src"])]
    for g in (1, 2):
        d_dil[g] = [t.reshape(DIL_GROUPS[g][1], -1, GROUP_WIDTH) for t in d_dil[g]]
    d_sb = _sb_attn_bwd(sv["qkv"], sv["rc"], sv["trips"], dy_sb, name=f"sb_bwd_{tag}")
    dproj, grads["q_norm_g"], grads["k_norm_g"] = _qkv_prep_bwd(
        sv["proj"], w["q_norm_g"], w["k_norm_g"], cos, sin, d_dil, d_sb, dgate, name=f"qkv_prep_bwd_{tag}")
    dh, got["d_h"] = _with_side(_matmul(dproj, w["w_in"], mode="nt", b_sharded=True, tm=512, tn=2048, side=sides.get("d_h"),
                                        name=f"d_h_{tag}"), sides.get("d_h"))
    grads["w_in"], got["dw_in"] = _with_side(_matmul(sv["h"], dproj, mode="tn", out_sharded=True, tm=512, tk=1024,
                                                     side=sides.get("dw_in"), name=f"dw_in_{tag}"), sides.get("dw_in"))
    dx0, dx0b, grads["norm1_g"] = _rmsnorm_bwd(sv["x"], w["norm1_g"], dh, dx1, name=f"norm1_bwd_{tag}")
    return dx0, dx0b, grads, got


BIG = ("w_in", "w_up_dil", "w_up_sb", "w_out", "w_ff1", "w_ff2")
ROW_SHARDED = ("w_out", "w_ff2")
WEIGHTS = ("norm1_g", "w_in", "q_norm_g", "k_norm_g", "w_up_dil", "w_up_sb", "gate_b", "w_out", "norm2_g", "w_ff1", "w_ff2")


def _reduce_begin(grads, which, tag, swapped=None):
    theirs = dict(swapped or {})
    todo = [n for n in grads if n not in theirs]
    if todo:
        theirs.update(zip(todo, _pair_swap_halves([grads[n] for n in todo], name=f"rs_pair_{tag}")))
    return {n: _add_half(grads[n], theirs[n], which, name=f"rs_pair_sum_{n}_{tag}") for n in grads}


def _reduce_end(pair, parts, which, chip, tag):
    names = list(pair)
    halves = [_sum_chips(pair[n], parts[n], chip, name=f"rs_chip_sum_{n}_{tag}") for n in names]
    joined = _pair_join_halves(halves, name=f"rs_join_{tag}")
    out = {}
    for n, j, h in zip(names, joined, halves):
        r, c = j.shape
        out[n] = _place_piece(j.reshape(1, 2, r // 2, c), h[None], which, name=f"rs_place_{n}_{tag}").reshape(r, c)
    return out


def _pack_rows(vecs):
    rows, spans, at = [], [], 0
    for v in vecs:
        r = v.size // 128
        padded = -(-r // 8) * 8
        rows.append(jnp.pad(v.reshape(r, 128), ((0, padded - r), (0, 0))))
        spans.append((at, r))
        at += padded
    return jnp.concatenate(rows, axis=0), spans


def kernel(x, norm1_g, w_in, q_norm_g, k_norm_g, w_up_dil, w_up_sb, gate_b, w_out, norm2_g, w_ff1, w_ff2, loss_target, m_norm1_g, m_w_in, m_q_norm_g, m_k_norm_g, m_w_up_dil, m_w_up_sb, m_gate_b, m_w_out, m_norm2_g, m_w_ff1, m_w_ff2, v_norm1_g, v_w_in, v_q_norm_g, v_k_norm_g, v_w_up_dil, v_w_up_sb, v_gate_b, v_w_out, v_norm2_g, v_w_ff1, v_w_ff2):
    weights = dict(norm1_g=norm1_g, w_in=w_in, q_norm_g=q_norm_g, k_norm_g=k_norm_g, w_up_dil=w_up_dil,
                   w_up_sb=w_up_sb, gate_b=gate_b, w_out=w_out, norm2_g=norm2_g, w_ff1=w_ff1, w_ff2=w_ff2)
    moments_m = dict(norm1_g=m_norm1_g, w_in=m_w_in, q_norm_g=m_q_norm_g, k_norm_g=m_k_norm_g, w_up_dil=m_w_up_dil,
                     w_up_sb=m_w_up_sb, gate_b=m_gate_b, w_out=m_w_out, norm2_g=m_norm2_g, w_ff1=m_w_ff1, w_ff2=m_w_ff2)
    moments_v = dict(norm1_g=v_norm1_g, w_in=v_w_in, q_norm_g=v_q_norm_g, k_norm_g=v_k_norm_g, w_up_dil=v_w_up_dil,
                     w_up_sb=v_w_up_sb, gate_b=v_gate_b, w_out=v_w_out, norm2_g=v_norm2_g, w_ff1=v_w_ff1, w_ff2=v_w_ff2)
    depth = w_in.shape[0]
    seq, d_model = x.shape[1], x.shape[2]
    chip = 2 * lax.axis_index("x") + lax.axis_index("y")
    core = lax.axis_index("c")
    which = jnp.reshape(core, (1,)).astype(jnp.int32)

    chip_index = jnp.reshape(chip, (1,)).astype(jnp.int32)
    shards = {n: weights[n].astype(BF16) for n in BIG}
    bias_buf, ((_, bias_rows),) = _pack_rows([gate_b])
    bias_slots = _exchange_small(bias_buf, name="gather_gate_b")
    bias = bias_slots[0::2, :bias_rows].reshape(N_CHIPS, depth, 2, d_model // N_CHIPS)
    bias = jnp.transpose(bias, (1, 2, 0, 3)).reshape(depth, 2, d_model)

    def placed(names, l, landed):
        w = {}
        for n in names:
            w[n] = _place_piece(landed[n][None], shards[n][l:l + 1], chip_index, name=f"gather_place_{n}_l{l}")[0]
            if n in ROW_SHARDED:
                w[n] = w[n].reshape(-1, w[n].shape[-1])
        return w

    def ici_side(names, l):
        return _gather_ici_side([shards[n] for n in names], l)

    def by_chip(n, g):
        return g.reshape(N_CHIPS, -1, g.shape[-1]) if n in ROW_SHARDED else g

    ahead = {"ff1": ("w_in",), "out": ("w_up_dil", "w_up_sb", "w_out"), "ff2": ("w_ff1",)}
    cos, sin = _rope_tables(seq)
    first = ("w_in",)
    ready = placed(first, 0, dict(zip(first, _gather_layer([shards[n] for n in first], 0, name="gather_l0_w_in"))))
    pending = tuple(n for n in BIG if n not in first)
    act = x[0]
    saved = []
    for l in range(depth):
        def rest(landed, l=l, pending=pending):
            passed = _gather_pass(list(landed), name=f"gather_pass_l{l}_rest")
            return placed(pending, l, dict(zip(pending, passed)))

        sides = {"proj": ici_side(pending, l)}
        if l + 1 < depth:
            sides.update({k: ici_side(names, l + 1) for k, names in ahead.items()})
        w = dict(ready, norm1_g=norm1_g[l], norm2_g=norm2_g[l], q_norm_g=q_norm_g[l], k_norm_g=k_norm_g[l], gate_b=bias[l])
        act, sv, got = _layer_fwd(act, w, cos, sin, f"l{l}", sides, rest)
        saved.append(sv)
        if l + 1 < depth:
            names = [n for k in ahead for n in ahead[k]]
            passed = _gather_pass([t for k in ahead for t in got[k]], name=f"gather_pass_l{l + 1}")
            ready = placed(names, l + 1, dict(zip(names, passed)))
            pending = ("w_ff2",)
    loss, dx, dxb = _loss_head(act, loss_target[0])
    loss = lax.psum(loss, ("x", "y", "c"))

    below = {"d_a": ("w_in",), "dw_ff2": ("w_ff1", "w_out", "w_up_dil", "w_up_sb"), "d_h2": ("w_ff2",)}
    own = {"d_h": ("w_ff2", "w_ff1"), "dw_in": ("w_out", "w_up_dil", "w_up_sb")}
    known_early = tuple(n for names in own.values() for n in names)
    grads, reduced = [None] * depth, [None] * depth
    pair = None
    for l in reversed(range(depth)):
        sides = {k: _chip_side([pair[n] for n in names]) for k, names in below.items()} if pair else None
        early = {}

        def late(g, early=early):
            for k, names in own.items():
                early.update(_reduce_begin({n: by_chip(n, g[n]) for n in names}, which, f"l0_{k}"))
            return {k: _chip_side([early[n] for n in names]) for k, names in own.items()}

        def late_swap(g):
            return {"dw_in": _pair_side([by_chip(n, g[n]) for n in known_early])}

        dx, dxb, grads[l], got = _layer_bwd(dx, dxb, saved[l], cos, sin, f"l{l}", sides, late if l == 0 else late_swap)
        if pair:
            parts = {n: t for k, names in below.items() for n, t in zip(names, got[k])}
            reduced[l + 1] = _reduce_end(pair, parts, which, chip_index, f"l{l + 1}")
        if l > 0:
            pair = _reduce_begin({n: by_chip(n, grads[l][n]) for n in BIG}, which, f"l{l}",
                                 swapped=dict(zip(known_early, got["dw_in"])))
    last = _reduce_begin({"w_in": grads[0]["w_in"]}, which, "l0_w_in")
    parts = {n: t for k, names in own.items() for n, t in zip(names, got[k])}
    parts["w_in"] = _chip_exchange([last["w_in"]], name="rs_chips_l0_w_in")[0]
    pair = dict(early, **last)
    reduced[0] = _reduce_end({n: pair[n] for n in BIG}, parts, which, chip_index, "l0")
    final, deltas, new_m, new_v = {}, {}, {}, {}
    for n in BIG:
        final[n], deltas[n], new_m[n], new_v[n] = _adamw_layers(
            weights[n], [reduced[l][n] for l in range(depth)], moments_m[n], moments_v[n], name=f"adamw_{n}")

    small_names = ("norm1_g", "norm2_g", "q_norm_g", "k_norm_g", "gate_b")
    packed, spans = _pack_rows([grads[l][n] for l in range(depth) for n in small_names])
    total = _sum_slots(_exchange_small(packed, name="exchange_small_grads"), name="sum_small_grads")
    pieces = iter(total[at:at + r] for at, r in spans)
    small = {n: [] for n in small_names}
    for l in range(depth):
        for n in small_names:
            small[n].append(next(pieces))
    for n in ("norm1_g", "norm2_g"):
        final[n] = jnp.stack([p.reshape(d_model) for p in small[n]])
    for n in ("q_norm_g", "k_norm_g"):
        final[n] = jnp.stack([p.reshape(N_GROUPS, HEAD_DIM) for p in small[n]])
    shard_cols = d_model // N_CHIPS
    final["gate_b"] = jnp.stack([lax.dynamic_slice_in_dim(p.reshape(2, d_model), chip * shard_cols, shard_cols, axis=1)
                                 for p in small["gate_b"]])

    for n in small_names:
        deltas[n], new_m[n], new_v[n] = _adamw(weights[n], final[n], moments_m[n], moments_v[n], name=f"adamw_{n}")
    return (loss, dx[None], *[final[n] for n in WEIGHTS], *[deltas[n] for n in WEIGHTS],
            *[new_m[n] for n in WEIGHTS], *[new_v[n] for n in WEIGHTS])
```

```python
import functools
import math

import jax
import jax.numpy as jnp
from jax import lax
from jax.experimental import pallas as pl
from jax.experimental.pallas import tpu as pltpu

F32 = jnp.float32
BF16 = jnp.bfloat16

HEAD_DIM = 128
DIL_GROUPS = ((128, 1), (512, 4), (2048, 16))
N_GROUPS = 3
HEADS_PER_GROUP = 4
GROUP_WIDTH = HEADS_PER_GROUP * HEAD_DIM
DIL_WIDTH = N_GROUPS * GROUP_WIDTH
SB_HEADS = 8
SB_WIDTH = SB_HEADS * HEAD_DIM
QKV_WIDTH = 3 * DIL_WIDTH + 3 * SB_WIDTH
BLOCK = 128
ROPE_THETA = 10000.0
EPS = 1e-6
SCALE = 1.0 / math.sqrt(HEAD_DIM)
NEG_INF = float("-inf")

ADAM_LR = 0.001
ADAM_B1 = 0.9
ADAM_B2 = 0.999
ADAM_EPS = 1e-08
ADAM_WD = 0.01
ADAM_STEP = 10

N_CHIPS = 4
MESH = pl.DeviceIdType.MESH
MIB = 1024 * 1024
VMEM_LIMIT = 56 * MIB


def _params(semantics=None, vmem=VMEM_LIMIT):
    return pltpu.CompilerParams(dimension_semantics=semantics, vmem_limit_bytes=vmem)


def _pick(n, pref):
    if n <= pref:
        return n
    t = 1 << (pref.bit_length() - 1)
    while n % t:
        t //= 2
    return t


def _tile(n, pref):
    best = 0
    for t in range(256, min(n, pref) + 1, 256):
        if n % t == 0:
            best = t
    return best if best else n


def _matmul(a, b, *, mode, name, out_dtypes=(F32,), extras=(), epilogue=None, b_sharded=False, out_sharded=False,
            tm=1024, tn=1024, tk=2048, side=None):
    if mode == "nn":
        m, kdim = a.shape
        n = b.shape[-1] * (N_CHIPS if b_sharded else 1)
        assert (b.shape[-2] if b_sharded else b.shape[0]) == kdim
    elif mode == "nt":
        m, kdim = a.shape
        n = b.shape[-2]
        assert b.shape[-1] * (N_CHIPS if b_sharded else 1) == kdim
    else:
        kdim, m = a.shape
        n = b.shape[1]
        assert b.shape[0] == kdim and not b_sharded
    shard_n = n // N_CHIPS if (out_sharded or (b_sharded and mode == "nn")) else n
    shard_k = kdim // N_CHIPS if (b_sharded and mode == "nt") else kdim
    tm = _tile(m, tm)
    tn = _tile(shard_n, tn)
    tk = _tile(shard_k, tk)
    nk = kdim // tk
    nj_n = shard_n // tn
    nj_k = shard_k // tk
    j_outer = nk == 1 and (kdim * n + m * kdim * (n // tn)) < (m * kdim + kdim * n * (m // tm))
    grid = (n // tn, m // tm, nk) if j_outer else (m // tm, n // tn, nk)

    def at(index):
        return (lambda g0, g1, k: index(g1, g0, k)) if j_outer else index

    if mode == "nn":
        a_spec = pl.BlockSpec((tm, tk), at(lambda i, j, k: (i, k)))
        if b_sharded:
            b_spec = pl.BlockSpec((None, tk, tn), at(lambda i, j, k: (j // nj_n, k, j % nj_n)))
        else:
            b_spec = pl.BlockSpec((tk, tn), at(lambda i, j, k: (k, j)))
        dims = (((1,), (0,)), ((), ()))
    elif mode == "nt":
        a_spec = pl.BlockSpec((tm, tk), at(lambda i, j, k: (i, k)))
        if b_sharded:
            b_spec = pl.BlockSpec((None, tn, tk), at(lambda i, j, k: (k // nj_k, j, k % nj_k)))
        else:
            b_spec = pl.BlockSpec((tn, tk), at(lambda i, j, k: (j, k)))
        dims = (((1,), (1,)), ((), ()))
    else:
        a_spec = pl.BlockSpec((tk, tm), at(lambda i, j, k: (k, i)))
        b_spec = pl.BlockSpec((tk, tn), at(lambda i, j, k: (k, j)))
        dims = (((0,), (0,)), ((), ()))
    if out_sharded:
        o_spec = pl.BlockSpec((None, tm, tn), at(lambda i, j, k: (j // nj_n, i, j % nj_n)))
        o_shape = (N_CHIPS, m, shard_n)
    else:
        o_spec = pl.BlockSpec((tm, tn), at(lambda i, j, k: (i, j)))
        o_shape = (m, n)
    x_spec = pl.BlockSpec((tm, tn), at(lambda i, j, k: (i, j)))
    n_extra = len(extras)
    n_out = len(out_dtypes)

    side_ins, side_shapes, side_copies, n_copies = side if side else ((), (), None, 0)
    n_side_in, n_side_out = len(side_ins), len(side_shapes)

    def finish(acc, extra_refs, out_refs):
        res = (acc,) if epilogue is None else epilogue(acc, *[e[...] for e in extra_refs])
        for o, r in zip(out_refs, res):
            o[...] = r.astype(o.dtype)

    def body(a_ref, b_ref, *rest):
        extra_refs = rest[:n_extra]
        side_in_refs = rest[n_extra:n_extra + n_side_in]
        rest = rest[n_extra + n_side_in:]
        out_refs, side_out_refs, scratch = rest[:n_out], rest[n_out:n_out + n_side_out], rest[n_out + n_side_out:]
        if side:
            step = (pl.program_id(0) * grid[1] + pl.program_id(1)) * grid[2] + pl.program_id(2)
            send_sems, recv_sems = scratch[-2:]

            @pl.when(step == 0)
            def _():
                sends, _ = side_copies(side_in_refs, side_out_refs, send_sems, recv_sems)
                for cp in sends:
                    cp.start()

        if nk == 1:
            finish(lax.dot_general(a_ref[...], b_ref[...], dims, preferred_element_type=F32), extra_refs, out_refs)
        else:
            acc = scratch[0]
            k = pl.program_id(2)

            @pl.when(k == 0)
            def _():
                acc[...] = jnp.zeros_like(acc)

            acc[...] += lax.dot_general(a_ref[...], b_ref[...], dims, preferred_element_type=F32)

            @pl.when(k == nk - 1)
            def _():
                finish(acc[...], extra_refs, out_refs)

        if side:
            @pl.when(step == grid[0] * grid[1] * grid[2] - 1)
            def _():
                sends, recvs = side_copies(side_in_refs, side_out_refs, send_sems, recv_sems)
                for cp in recvs:
                    cp.wait_recv()
                for cp in sends:
                    cp.wait_send()

    scratch_shapes = [] if nk == 1 else [pltpu.VMEM((tm, tn), F32)]
    if side:
        scratch_shapes += [pltpu.SemaphoreType.DMA((n_copies,)), pltpu.SemaphoreType.DMA((n_copies,))]
    outs = pl.pallas_call(
        body,
        name=name,
        grid=grid,
        in_specs=[a_spec, b_spec] + [x_spec] * n_extra + [ANY] * n_side_in,
        out_specs=[o_spec] * n_out + [ANY] * n_side_out,
        out_shape=[jax.ShapeDtypeStruct(o_shape, dt) for dt in out_dtypes] + list(side_shapes),
        scratch_shapes=scratch_shapes,
        compiler_params=_params(("arbitrary",) * 3 if side else ("parallel", "parallel", "arbitrary")),
    )(a, b, *extras, *side_ins)
    if side:
        return (outs[0] if n_out == 1 else outs[:n_out]), outs[n_out:]
    return outs[0] if n_out == 1 else outs


def _rmsnorm_fwd(x, g, *, name):
    s, d = x.shape
    tr = _pick(s, 256)

    def body(x_ref, g_ref, o_ref):
        xv = x_ref[...]
        r = lax.rsqrt(jnp.mean(xv * xv, axis=-1, keepdims=True) + EPS)
        o_ref[...] = (xv * r * g_ref[...]).astype(o_ref.dtype)

    return pl.pallas_call(
        body, name=name, grid=(s // tr,),
        in_specs=[pl.BlockSpec((tr, d), lambda i: (i, 0)), pl.BlockSpec((1, d), lambda i: (0, 0))],
        out_specs=pl.BlockSpec((tr, d), lambda i: (i, 0)),
        out_shape=jax.ShapeDtypeStruct((s, d), BF16),
        compiler_params=_params(("parallel",)),
    )(x, g.reshape(1, d))


def _rmsnorm_bwd(x, g, dh, dres, *, name):
    s, d = x.shape
    tr = _pick(s, 256)

    def body(x_ref, g_ref, dh_ref, dres_ref, dx_ref, dxb_ref, dg_ref):
        xv = x_ref[...]
        r = lax.rsqrt(jnp.mean(xv * xv, axis=-1, keepdims=True) + EPS)
        y = xv * r
        dhv = dh_ref[...]
        dy = dhv * g_ref[...]
        dx = dres_ref[...] + r * (dy - y * jnp.mean(dy * y, axis=-1, keepdims=True))
        dx_ref[...] = dx
        dxb_ref[...] = dx.astype(BF16)

        @pl.when(pl.program_id(0) == 0)
        def _():
            dg_ref[...] = jnp.zeros_like(dg_ref)

        dg_ref[...] += jnp.sum(dhv * y, axis=0, keepdims=True)

    row = pl.BlockSpec((tr, d), lambda i: (i, 0))
    vec = pl.BlockSpec((1, d), lambda i: (0, 0))
    dx, dxb, dg = pl.pallas_call(
        body, name=name, grid=(s // tr,),
        in_specs=[row, vec, row, row],
        out_specs=[row, row, vec],
        out_shape=[jax.ShapeDtypeStruct((s, d), F32), jax.ShapeDtypeStruct((s, d), BF16),
                   jax.ShapeDtypeStruct((1, d), F32)],
        compiler_params=_params(("arbitrary",)),
    )(x, g.reshape(1, d), dh, dres)
    return dx, dxb, dg.reshape(d)


def _loss_head(y, target):
    s, d = y.shape
    tr = _pick(s, 256)

    def body(y_ref, t_ref, dy_ref, dyb_ref, part_ref):
        err = y_ref[...] - t_ref[...]
        dy = err * (1.0 / d)
        dy_ref[...] = dy
        dyb_ref[...] = dy.astype(BF16)

        @pl.when(pl.program_id(0) == 0)
        def _():
            part_ref[...] = jnp.zeros_like(part_ref)

        part_ref[...] += jnp.sum(err * err, axis=0, keepdims=True)

    row = pl.BlockSpec((tr, d), lambda i: (i, 0))
    vec = pl.BlockSpec((1, d), lambda i: (0, 0))
    dy, dyb, part = pl.pallas_call(
        body, name="loss_head", grid=(s // tr,),
        in_specs=[row, row], out_specs=[row, row, vec],
        out_shape=[jax.ShapeDtypeStruct((s, d), F32), jax.ShapeDtypeStruct((s, d), BF16),
                   jax.ShapeDtypeStruct((1, d), F32)],
        compiler_params=_params(("arbitrary",)),
    )(y, target)
    return 0.5 * jnp.sum(part) / d, dy, dyb


def _rope_tables(s):
    half = HEAD_DIM // 2
    inv_freq = ROPE_THETA ** (-jnp.arange(half, dtype=F32) / half)
    ang = jnp.arange(s, dtype=F32)[:, None] * inv_freq[None, :]
    cos, sin = jnp.cos(ang), jnp.sin(ang)
    return jnp.concatenate([cos, cos], axis=-1), jnp.concatenate([-sin, sin], axis=-1)


PREP_ROWS = 256


def _residue_spec(r, width):
    return pl.BlockSpec((r, PREP_ROWS // r, width), lambda i: (0, i, 0))


def _to_residues(scr, tile, r, dst_ref, cols, dtype):
    scr[...] = tile
    for c in range(r):
        dst_ref[c, :, cols] = scr[pl.ds(c, PREP_ROWS // r, stride=r), :].astype(dtype)


def _from_residues(scr, src_ref, cols, r):
    for c in range(r):
        scr[pl.ds(c, PREP_ROWS // r, stride=r), :] = src_ref[c, :, cols].astype(F32)
    return scr[...]


def _qkv_prep(proj, qg, kg, cos, sin, *, name):
    s = proj.shape[0]
    tr = PREP_ROWS
    assert s % tr == 0

    def body(p_ref, qg_ref, kg_ref, cos_ref, sin_ref, o_ref, res1_ref, res2_ref, scr):
        res_refs = (None, res1_ref, res2_ref)
        c = cos_ref[...]
        sn = sin_ref[...]
        for which, g_ref in ((0, qg_ref), (1, kg_ref)):
            for g in range(N_GROUPS):
                gain = g_ref[g:g + 1, :]
                for h in range(HEADS_PER_GROUP):
                    off = which * DIL_WIDTH + g * GROUP_WIDTH + h * HEAD_DIM
                    xv = p_ref[:, off:off + HEAD_DIM]
                    r = lax.rsqrt(jnp.mean(xv * xv, axis=-1, keepdims=True) + EPS)
                    y = xv * r * gain
                    rot = y * c + pltpu.roll(y, HEAD_DIM // 2, 1) * sn
                    if g == 0:
                        o_ref[:, off:off + HEAD_DIM] = rot.astype(BF16)
                    else:
                        at = which * GROUP_WIDTH + h * HEAD_DIM
                        _to_residues(scr, rot, DIL_GROUPS[g][1], res_refs[g], slice(at, at + HEAD_DIM), BF16)
        o_ref[:, 2 * DIL_WIDTH:2 * DIL_WIDTH + GROUP_WIDTH] = p_ref[:, 2 * DIL_WIDTH:2 * DIL_WIDTH + GROUP_WIDTH].astype(BF16)
        for g in range(1, N_GROUPS):
            for h in range(HEADS_PER_GROUP):
                off = 2 * DIL_WIDTH + g * GROUP_WIDTH + h * HEAD_DIM
                at = 2 * GROUP_WIDTH + h * HEAD_DIM
                _to_residues(scr, p_ref[:, off:off + HEAD_DIM], DIL_GROUPS[g][1], res_refs[g], slice(at, at + HEAD_DIM), BF16)
        o_ref[:, 3 * DIL_WIDTH:] = p_ref[:, 3 * DIL_WIDTH:].astype(BF16)

    r1, r2 = DIL_GROUPS[1][1], DIL_GROUPS[2][1]
    return pl.pallas_call(
        body, name=name, grid=(s // tr,),
        in_specs=[pl.BlockSpec((tr, QKV_WIDTH), lambda i: (i, 0)),
                  pl.BlockSpec((N_GROUPS, HEAD_DIM), lambda i: (0, 0)),
                  pl.BlockSpec((N_GROUPS, HEAD_DIM), lambda i: (0, 0)),
                  pl.BlockSpec((tr, HEAD_DIM), lambda i: (i, 0)),
                  pl.BlockSpec((tr, HEAD_DIM), lambda i: (i, 0))],
        out_specs=[pl.BlockSpec((tr, QKV_WIDTH), lambda i: (i, 0)),
                   _residue_spec(r1, 3 * GROUP_WIDTH), _residue_spec(r2, 3 * GROUP_WIDTH)],
        out_shape=[jax.ShapeDtypeStruct((s, QKV_WIDTH), BF16),
                   jax.ShapeDtypeStruct((r1, s // r1, 3 * GROUP_WIDTH), BF16),
                   jax.ShapeDtypeStruct((r2, s // r2, 3 * GROUP_WIDTH), BF16)],
        scratch_shapes=[pltpu.VMEM((tr, HEAD_DIM), F32)],
        compiler_params=_params(("parallel",)),
    )(proj, qg, kg, cos, sin)


def _qkv_prep_bwd(proj, qg, kg, cos, sin, dqkv_dil, dqkv_sb, dgate, *, name):
    s, n_in = proj.shape
    n_gate = dgate[0].shape[1]
    tr = PREP_ROWS

    def body(p_ref, qg_ref, kg_ref, cos_ref, sin_ref, *rest):
        dil_refs = rest[:9]
        sb_refs = rest[9:12]
        dgate_refs = rest[12:14]
        o_ref, dqg_ref, dkg_ref, scr = rest[14:18]
        c = cos_ref[...]
        sn = sin_ref[...]

        def incoming(g, which, h):
            cols = slice(h * HEAD_DIM, (h + 1) * HEAD_DIM)
            if g == 0:
                return dil_refs[which][:, cols]
            return _from_residues(scr, dil_refs[3 * g + which], cols, DIL_GROUPS[g][1])

        @pl.when(pl.program_id(0) == 0)
        def _():
            dqg_ref[...] = jnp.zeros_like(dqg_ref)
            dkg_ref[...] = jnp.zeros_like(dkg_ref)

        for which, g_ref, dg_ref in ((0, qg_ref, dqg_ref), (1, kg_ref, dkg_ref)):
            for g in range(N_GROUPS):
                gain = g_ref[g:g + 1, :]
                dgain = jnp.zeros((1, HEAD_DIM), F32)
                for h in range(HEADS_PER_GROUP):
                    off = which * DIL_WIDTH + g * GROUP_WIDTH + h * HEAD_DIM
                    xv = p_ref[:, off:off + HEAD_DIM]
                    r = lax.rsqrt(jnp.mean(xv * xv, axis=-1, keepdims=True) + EPS)
                    nx = xv * r
                    dout = incoming(g, which, h)
                    dy = dout * c + pltpu.roll(dout * sn, HEAD_DIM // 2, 1)
                    dgain = dgain + jnp.sum(dy * nx, axis=0, keepdims=True)
                    dn = dy * gain
                    dxv = r * (dn - nx * jnp.mean(dn * nx, axis=-1, keepdims=True))
                    o_ref[:, off:off + HEAD_DIM] = dxv.astype(BF16)
                dg_ref[g:g + 1, :] += dgain
        for g in range(N_GROUPS):
            for h in range(HEADS_PER_GROUP):
                off = 2 * DIL_WIDTH + g * GROUP_WIDTH + h * HEAD_DIM
                o_ref[:, off:off + HEAD_DIM] = incoming(g, 2, h).astype(BF16)
        for t in range(3):
            off = 3 * DIL_WIDTH + t * SB_WIDTH
            o_ref[:, off:off + SB_WIDTH] = sb_refs[t][...].astype(BF16)
        o_ref[:, QKV_WIDTH:QKV_WIDTH + n_gate] = dgate_refs[0][...]
        o_ref[:, QKV_WIDTH + n_gate:] = dgate_refs[1][...]

    grp = pl.BlockSpec((tr, GROUP_WIDTH), lambda i: (i, 0))
    sbs = pl.BlockSpec((tr, SB_WIDTH), lambda i: (i, 0))
    gain_spec = pl.BlockSpec((N_GROUPS, HEAD_DIM), lambda i: (0, 0))
    tab = pl.BlockSpec((tr, HEAD_DIM), lambda i: (i, 0))
    flat_dil = [t for grp3 in dqkv_dil for t in grp3]
    dil_specs = [grp] * 3 + [_residue_spec(DIL_GROUPS[g][1], GROUP_WIDTH) for g in (1, 2) for _ in range(3)]
    return pl.pallas_call(
        body, name=name, grid=(s // tr,),
        in_specs=[pl.BlockSpec((tr, 2 * DIL_WIDTH), lambda i: (i, 0)), gain_spec, gain_spec, tab, tab]
                 + dil_specs + [sbs] * 3 + [pl.BlockSpec((tr, n_gate), lambda i: (i, 0))] * 2,
        out_specs=[pl.BlockSpec((tr, n_in), lambda i: (i, 0)), gain_spec, gain_spec],
        out_shape=[jax.ShapeDtypeStruct((s, n_in), BF16),
                   jax.ShapeDtypeStruct((N_GROUPS, HEAD_DIM), F32),
                   jax.ShapeDtypeStruct((N_GROUPS, HEAD_DIM), F32)],
        scratch_shapes=[pltpu.VMEM((tr, HEAD_DIM), F32)],
        compiler_params=_params(("arbitrary",)),
    )(proj, qg, kg, cos, sin, *flat_dil, *dqkv_sb, *dgate)


def _dil_masks(i):
    row = lax.broadcasted_iota(jnp.int32, (BLOCK, BLOCK), 0)
    col = lax.broadcasted_iota(jnp.int32, (BLOCK, BLOCK), 1)
    return col <= row, (col >= row) & (i > 0)


def _dil_sub(nb):
    return 4 if nb % 4 == 0 else 2 if nb % 2 == 0 else 1


def _dot_nt(a, b):
    return lax.dot_general(a, b, (((1,), (1,)), ((), ())), preferred_element_type=F32)


def _dot_tn(a, b):
    return lax.dot_general(a, b, (((0,), (0,)), ((), ())), preferred_element_type=F32)


def _dot(a, b):
    return jnp.dot(a, b, preferred_element_type=F32)


def _dil_attn_fwd(src, g, cols, *, name):
    s = src.shape[0]
    r = DIL_GROUPS[g][1]
    nb = s // r // BLOCK
    sub = _dil_sub(nb)
    steps = nb // sub

    def body(q_ref, kp_ref, kc_ref, vp_ref, vc_ref, o_ref, ld_ref):
        i = pl.program_id(1)
        work = []
        for u in range(sub):
            rows = slice(u * BLOCK, (u + 1) * BLOCK)
            before = slice((u - 1) * BLOCK, u * BLOCK)
            mask_c, mask_p = _dil_masks(i if u == 0 else 1)
            for h in range(HEADS_PER_GROUP):
                sl = slice(h * HEAD_DIM, (h + 1) * HEAD_DIM)
                k_prev = kp_ref[:, sl] if u == 0 else kc_ref[before, sl]
                v_prev = vp_ref[:, sl] if u == 0 else vc_ref[before, sl]
                work.append((rows, sl, mask_c, mask_p, v_prev,
                             _dot_nt(q_ref[rows, sl], kc_ref[rows, sl]), _dot_nt(q_ref[rows, sl], k_prev)))
        probs = []
        for rows, sl, mask_c, mask_p, _, raw_c, raw_p in work:
            s_c = jnp.where(mask_c, raw_c * SCALE, NEG_INF)
            s_p = jnp.where(mask_p, raw_p * SCALE, NEG_INF)
            m = jnp.maximum(jnp.max(s_c, axis=-1, keepdims=True), jnp.max(s_p, axis=-1, keepdims=True))
            p_c = jnp.exp(s_c - m)
            p_p = jnp.exp(s_p - m)
            l = jnp.sum(p_c, axis=-1, keepdims=True) + jnp.sum(p_p, axis=-1, keepdims=True)
            inv = 1.0 / l
            probs.append(((p_c * inv).astype(BF16), (p_p * inv).astype(BF16)))
            ld_ref[rows, sl] = jnp.broadcast_to(m + jnp.log(l), (BLOCK, HEAD_DIM))
        for (rows, sl, _, _, v_prev, _, _), (pn_c, pn_p) in zip(work, probs):
            o_ref[rows, sl] = _dot(pn_c, vc_ref[rows, sl]) + _dot(pn_p, v_prev)

    def col(which):
        return lambda c, i: (c * steps + i, cols[which])

    def col_prev(which):
        return lambda c, i: (c * nb + jnp.maximum(sub * i - 1, 0), cols[which])

    own = (sub * BLOCK, GROUP_WIDTH)
    one = (BLOCK, GROUP_WIDTH)
    return pl.pallas_call(
        body, name=name, grid=(r, steps),
        in_specs=[pl.BlockSpec(own, col(0)), pl.BlockSpec(one, col_prev(1)), pl.BlockSpec(own, col(1)),
                  pl.BlockSpec(one, col_prev(2)), pl.BlockSpec(own, col(2))],
        out_specs=[pl.BlockSpec(own, lambda c, i: (c * steps + i, 0))] * 2,
        out_shape=[jax.ShapeDtypeStruct((s, GROUP_WIDTH), F32)] * 2,
        compiler_params=_params(("parallel", "parallel")),
    )(src, src, src, src, src)


def _dil_merge(outs, lds, *, name):
    s = outs[0].shape[0]
    tr = PREP_ROWS
    rs = [DIL_GROUPS[g][1] for g in range(N_GROUPS)]

    def body(o0, o1, o2, l0, l1, l2, yb_ref, y0_ref, y1_ref, y2_ref, e0_ref, e1_ref, e2_ref, scr_a, scr_b):
        o_refs, l_refs = (o0, o1, o2), (l0, l1, l2)
        y_refs, e_refs = (y0_ref, y1_ref, y2_ref), (e0_ref, e1_ref, e2_ref)
        for h in range(HEADS_PER_GROUP):
            cols = slice(h * HEAD_DIM, (h + 1) * HEAD_DIM)
            o = [o_refs[0][:, cols]] + [_from_residues(scr_a, o_refs[g], cols, rs[g]) for g in (1, 2)]
            a = [l_refs[0][:, cols]] + [_from_residues(scr_b, l_refs[g], cols, rs[g]) for g in (1, 2)]
            m = jnp.maximum(jnp.maximum(a[0], a[1]), a[2])
            e = [jnp.exp(t - m) for t in a]
            tot = e[0] + e[1] + e[2]
            inv = 1.0 / tot
            y = (e[0] * inv) * o[0] + (e[1] * inv) * o[1] + (e[2] * inv) * o[2]
            lse = m + jnp.log(tot)
            yb_ref[:, cols] = y.astype(BF16)
            y_refs[0][:, cols] = y
            e_refs[0][:, cols] = lse
            for g in (1, 2):
                _to_residues(scr_a, y, rs[g], y_refs[g], cols, F32)
                _to_residues(scr_b, lse, rs[g], e_refs[g], cols, F32)

    blk = pl.BlockSpec((tr, GROUP_WIDTH), lambda i: (i, 0))
    specs = [blk] + [_residue_spec(rs[g], GROUP_WIDTH) for g in (1, 2)]
    shapes = [jax.ShapeDtypeStruct((s, GROUP_WIDTH), F32)] + [
        jax.ShapeDtypeStruct((rs[g], s // rs[g], GROUP_WIDTH), F32) for g in (1, 2)]
    res = pl.pallas_call(
        body, name=name, grid=(s // tr,),
        in_specs=specs * 2, out_specs=[blk] + specs * 2,
        out_shape=[jax.ShapeDtypeStruct((s, GROUP_WIDTH), BF16)] + shapes * 2,
        scratch_shapes=[pltpu.VMEM((tr, HEAD_DIM), F32), pltpu.VMEM((tr, HEAD_DIM), F32)],
        compiler_params=_params(("parallel",)),
    )(*[t.reshape(sh.shape) for t, sh in zip(outs, shapes)], *[t.reshape(sh.shape) for t, sh in zip(lds, shapes)])
    yb, ys, lses = res[0], res[1:4], res[4:7]
    return yb, [t.reshape(s, GROUP_WIDTH) for t in ys], [t.reshape(s, GROUP_WIDTH) for t in lses]


def _dil_to_residues(x, *, name):
    s = x.shape[0]
    tr = PREP_ROWS
    rs = [DIL_GROUPS[g][1] for g in (1, 2)]

    def body(x_ref, a_ref, b_ref, scr):
        for h in range(HEADS_PER_GROUP):
            cols = slice(h * HEAD_DIM, (h + 1) * HEAD_DIM)
            tile = x_ref[:, cols]
            _to_residues(scr, tile, rs[0], a_ref, cols, F32)
            _to_residues(scr, tile, rs[1], b_ref, cols, F32)

    outs = pl.pallas_call(
        body, name=name, grid=(s // tr,),
        in_specs=[pl.BlockSpec((tr, GROUP_WIDTH), lambda i: (i, 0))],
        out_specs=[_residue_spec(r, GROUP_WIDTH) for r in rs],
        out_shape=[jax.ShapeDtypeStruct((r, s // r, GROUP_WIDTH), F32) for r in rs],
        scratch_shapes=[pltpu.VMEM((tr, HEAD_DIM), F32)],
        compiler_params=_params(("parallel",)),
    )(x)
    return [t.reshape(s, GROUP_WIDTH) for t in outs]


def _dil_attn_bwd(src, g, cols, y, lse, dy, *, name):
    s = src.shape[0]
    r = DIL_GROUPS[g][1]
    nb = s // r // BLOCK
    sub = _dil_sub(nb)
    steps = nb // sub

    def body(q_ref, qn_ref, kp_ref, kc_ref, vp_ref, vc_ref, y_ref, yn_ref, l_ref, ln_ref, d_ref, dn_ref,
             dq_ref, dk_ref, dv_ref):
        i = pl.program_id(1)
        work = []
        for u in range(sub):
            rows = slice(u * BLOCK, (u + 1) * BLOCK)
            before = slice((u - 1) * BLOCK, u * BLOCK)
            after = slice((u + 1) * BLOCK, (u + 2) * BLOCK)
            last = u == sub - 1
            mask_c, mask_p = _dil_masks(i if u == 0 else 1)
            _, mask_n = _dil_masks(jnp.where(sub * i + u + 1 < nb, 1, 0) if last else 1)
            for h in range(HEADS_PER_GROUP):
                sl = slice(h * HEAD_DIM, (h + 1) * HEAD_DIM)
                nxt = lambda own_ref, next_ref, sl=sl: next_ref[:, sl] if last else own_ref[after, sl]
                work.append(dict(
                    rows=rows, sl=sl, masks=(mask_c, mask_p, mask_n),
                    q=q_ref[rows, sl], qn=nxt(q_ref, qn_ref), kc=kc_ref[rows, sl], vc=vc_ref[rows, sl],
                    kp=kp_ref[:, sl] if u == 0 else kc_ref[before, sl],
                    vp=vp_ref[:, sl] if u == 0 else vc_ref[before, sl],
                    dy=d_ref[rows, sl], dyn=nxt(d_ref, dn_ref), y=y_ref[rows, sl], yn=nxt(y_ref, yn_ref),
                    lse=l_ref[rows, sl], lsen=nxt(l_ref, ln_ref)))
        for t in work:
            t["dyb"], t["dynb"] = t["dy"].astype(BF16), t["dyn"].astype(BF16)
            t["first"] = (_dot_nt(t["q"], t["kc"]), _dot_nt(t["q"], t["kp"]), _dot_nt(t["qn"], t["kc"]),
                          _dot_nt(t["dyb"], t["vc"]), _dot_nt(t["dyb"], t["vp"]), _dot_nt(t["dynb"], t["vc"]))
        for t in work:
            s_c, s_p, s_n, dp_c, dp_p, dp_n = t["first"]
            mask_c, mask_p, mask_n = t["masks"]
            delta = jnp.sum(t["dy"] * t["y"], axis=-1, keepdims=True)
            delta_n = jnp.sum(t["dyn"] * t["yn"], axis=-1, keepdims=True)
            p_c = jnp.where(mask_c, jnp.exp(s_c * SCALE - t["lse"]), 0.0)
            p_p = jnp.where(mask_p, jnp.exp(s_p * SCALE - t["lse"]), 0.0)
            p_n = jnp.where(mask_n, jnp.exp(s_n * SCALE - t["lsen"]), 0.0)
            t["second"] = ((p_c * (dp_c - delta) * SCALE).astype(BF16), (p_p * (dp_p - delta) * SCALE).astype(BF16),
                           (p_n * (dp_n - delta_n) * SCALE).astype(BF16), p_c.astype(BF16), p_n.astype(BF16))
        for t in work:
            ds_c, ds_p, ds_n, pb_c, pb_n = t["second"]
            rows, sl = t["rows"], t["sl"]
            dq_ref[rows, sl] = _dot(ds_c, t["kc"]) + _dot(ds_p, t["kp"])
            dk_ref[rows, sl] = _dot_tn(ds_c, t["q"]) + _dot_tn(ds_n, t["qn"])
            dv_ref[rows, sl] = _dot_tn(pb_c, t["dyb"]) + _dot_tn(pb_n, t["dynb"])

    def col(which):
        return lambda c, i: (c * steps + i, cols[which])

    def col_prev(which):
        return lambda c, i: (c * nb + jnp.maximum(sub * i - 1, 0), cols[which])

    def after_rows(c, i):
        return c * nb + jnp.minimum(sub * i + sub, nb - 1)

    rows_own = (sub * BLOCK, GROUP_WIDTH)
    one = (BLOCK, GROUP_WIDTH)
    own = pl.BlockSpec(rows_own, lambda c, i: (c * steps + i, 0))
    nxt = pl.BlockSpec(one, lambda c, i: (after_rows(c, i), 0))
    return pl.pallas_call(
        body, name=name, grid=(r, steps),
        in_specs=[pl.BlockSpec(rows_own, col(0)), pl.BlockSpec(one, lambda c, i: (after_rows(c, i), cols[0])),
                  pl.BlockSpec(one, col_prev(1)), pl.BlockSpec(rows_own, col(1)),
                  pl.BlockSpec(one, col_prev(2)), pl.BlockSpec(rows_own, col(2)),
                  own, nxt, own, nxt, own, nxt],
        out_specs=[own] * 3,
        out_shape=[jax.ShapeDtypeStruct((s, GROUP_WIDTH), F32)] * 3,
        compiler_params=_params(("parallel", "parallel")),
    )(src, src, src, src, src, src, y, y, lse, lse, dy, dy)


SB_TQ = 256
SB_GROUP = 2
SB_DEAD = -104.0
SB_NH_FWD = 4
SB_NH_BWD = 2


def _split_dot(x, w):
    hi = x.astype(BF16)
    lo = (x - hi.astype(F32)).astype(BF16)
    return _dot(hi, w) + _dot(lo, w)


def _sb_consts():
    j = lax.broadcasted_iota(jnp.int32, (BLOCK, BLOCK), 0)
    k = lax.broadcasted_iota(jnp.int32, (BLOCK, BLOCK), 1)
    ones = jnp.ones((BLOCK, BLOCK), F32)
    after = jnp.concatenate([(j > k).astype(F32), ones], axis=1).astype(BF16)
    before = jnp.concatenate([(j < k).astype(F32), ones], axis=1).astype(BF16)
    return after, before


def _sb_mask(row0, col0, tq):
    row = row0 + lax.broadcasted_iota(jnp.int32, (tq, BLOCK), 0)
    col = col0 + lax.broadcasted_iota(jnp.int32, (tq, BLOCK), 1)
    return col < row


def _softplus(z):
    return jnp.maximum(z, 0.0) + jnp.log1p(jnp.exp(-jnp.abs(z)))


def _sb_attn_fwd(qkv, *, name):
    s = qkv.shape[0]
    tq = _pick(s, SB_TQ)
    nq = s // tq
    per = tq // BLOCK
    base = 3 * DIL_WIDTH // HEAD_DIM
    after, _ = _sb_consts()

    nh = SB_NH_FWD
    wide = nh * HEAD_DIM

    def body(q_ref, k_ref, v_ref, w_ref, o_ref, rc_ref, trips_ref, acc_ref, run_ref):
        qi = pl.program_id(1)
        w = w_ref[...]
        acc_ref[...] = jnp.zeros_like(acc_ref)
        run_ref[...] = jnp.zeros_like(run_ref)
        rc_ref[...] = jnp.zeros_like(rc_ref)
        lane = lax.broadcasted_iota(jnp.int32, (tq, BLOCK), 1)
        n_chunks = (qi + 1) * per

        def chunks(js, masked):
            sls = [slice(h * HEAD_DIM, (h + 1) * HEAD_DIM) for h in range(nh)]
            work = [(sl, j, pl.multiple_of(j * BLOCK, BLOCK), _sb_mask(qi * tq, j * BLOCK, tq) if masked else None)
                    for j in js for sl in sls]
            zs = [_dot_nt(q_ref[:, sl], k_ref[pl.ds(start, BLOCK), sl]) * SCALE for sl, _, start, _ in work]
            logits, sums = [], []
            for z, (_, _, _, mask) in zip(zs, work):
                sp = _softplus(z)
                logits.append(z - sp)
                sums.append(_split_dot(jnp.where(mask, -sp, 0.0) if masked else -sp, w))
            for (sl, j, start, mask), lg, sm in zip(work, logits, sums):
                run = run_ref[:, sl]
                a = jnp.exp(lg + run + sm[:, :BLOCK])
                if masked:
                    a = jnp.where(mask, a, 0.0)
                acc_ref[:, sl] += _dot(a.astype(BF16), v_ref[pl.ds(start, BLOCK), sl])
                rc_ref[:, sl] = jnp.where(lane == j, run, rc_ref[:, sl])
                run_ref[:, sl] = run + sm[:, BLOCK:]

        def diagonal(t, carry):
            last = n_chunks - 1 - SB_GROUP * t
            chunks([last - g for g in range(SB_GROUP)], True)
            return carry

        def alive():
            return (jnp.max(run_ref[...]) > SB_DEAD).astype(jnp.int32)

        def more(carry):
            t, live = carry
            return jnp.logical_and(t < n_left, live > 0)

        def left(carry):
            t, _ = carry
            last = n_chunks - 1 - per - SB_GROUP * t
            chunks([last - g for g in range(SB_GROUP)], False)
            return t + 1, alive()

        lax.fori_loop(0, per // SB_GROUP, diagonal, 0)
        n_left = (n_chunks - per) // SB_GROUP
        trips, _ = lax.while_loop(more, left, (jnp.int32(0), alive()))
        trips_ref[pl.program_id(0), qi] = trips
        o_ref[...] = acc_ref[...].astype(o_ref.dtype)

    head = lambda off: (lambda h, i: (0, (base + off) // nh + h))
    once = pl.Buffered(1)
    return pl.pallas_call(
        body, name=name, grid=(SB_HEADS // nh, nq),
        in_specs=[pl.BlockSpec((tq, wide), lambda h, i: (i, base // nh + h)),
                  pl.BlockSpec((s, wide), head(SB_HEADS), pipeline_mode=once),
                  pl.BlockSpec((s, wide), head(2 * SB_HEADS), pipeline_mode=once),
                  pl.BlockSpec((BLOCK, 2 * BLOCK), lambda h, i: (0, 0))],
        out_specs=[pl.BlockSpec((tq, wide), lambda h, i: (i, h))] * 2 + [pl.BlockSpec(memory_space=pltpu.SMEM)],
        out_shape=[jax.ShapeDtypeStruct((s, SB_WIDTH), BF16), jax.ShapeDtypeStruct((s, SB_WIDTH), F32),
                   jax.ShapeDtypeStruct((SB_HEADS // nh, nq), jnp.int32)],
        scratch_shapes=[pltpu.VMEM((tq, wide), F32), pltpu.VMEM((tq, wide), F32)],
        compiler_params=_params(("arbitrary", "arbitrary")),
    )(qkv, qkv, qkv, after)


def _sb_attn_bwd(qkv, rc, trips, do, *, name):
    s = qkv.shape[0]
    tq = _pick(s, SB_TQ)
    nq = s // tq
    per = tq // BLOCK
    base = 3 * DIL_WIDTH // HEAD_DIM
    _, before = _sb_consts()

    nh = SB_NH_BWD
    wide = nh * HEAD_DIM

    def body(trips_ref, q_ref, k_ref, v_ref, w_ref, rc_ref, do_ref, dq_ref, dk_ref, dv_ref, acc_ref, pre_ref, dob_ref):
        qi = pl.program_id(1)

        @pl.when(qi == 0)
        def _():
            dk_ref[...] = jnp.zeros_like(dk_ref)
            dv_ref[...] = jnp.zeros_like(dv_ref)

        w = w_ref[...]
        wa = (lax.broadcasted_iota(jnp.int32, (BLOCK, BLOCK), 0)
              > lax.broadcasted_iota(jnp.int32, (BLOCK, BLOCK), 1)).astype(BF16)
        dob_ref[...] = do_ref[...].astype(BF16)
        acc_ref[...] = jnp.zeros_like(acc_ref)
        pre_ref[...] = jnp.zeros_like(pre_ref)
        lane = lax.broadcasted_iota(jnp.int32, (tq, BLOCK), 1)
        n_chunks = (qi + 1) * per

        def chunks(js, masked):
            sls = [slice(h * HEAD_DIM, (h + 1) * HEAD_DIM) for h in range(nh)]
            work = [(sl, j, pl.multiple_of(j * BLOCK, BLOCK), _sb_mask(qi * tq, j * BLOCK, tq) if masked else None)
                    for j in js for sl in sls]
            zs = [_dot_nt(q_ref[:, sl], k_ref[pl.ds(start, BLOCK), sl]) * SCALE for sl, _, start, _ in work]
            das = [_dot_nt(dob_ref[:, sl], v_ref[pl.ds(start, BLOCK), sl]) for sl, _, start, _ in work]
            sigs, betweens = [], []
            for z, (_, _, _, mask) in zip(zs, work):
                sp = _softplus(z)
                sigs.append(jnp.exp(z - sp))
                betweens.append(_split_dot(jnp.where(mask, -sp, 0.0) if masked else -sp, wa))
            es, avs, sums = [], [], []
            for (sl, j, _, mask), sig, bt, da in zip(work, sigs, betweens, das):
                run = jnp.sum(jnp.where(lane == j, rc_ref[:, sl], 0.0), axis=-1, keepdims=True)
                a = sig * jnp.exp(run + bt)
                if masked:
                    a = jnp.where(mask, a, 0.0)
                e = a * da
                es.append(e)
                avs.append(a.astype(BF16))
                sums.append(_split_dot(e, w))
            for (sl, _, start, mask), sig, e, ab, sm in zip(work, sigs, es, avs, sums):
                pre = pre_ref[:, sl]
                dz = (e * (1.0 - sig) - sig * (pre + sm[:, :BLOCK])) * SCALE
                if masked:
                    dz = jnp.where(mask, dz, 0.0)
                dz = dz.astype(BF16)
                pre_ref[:, sl] = pre + sm[:, BLOCK:]
                acc_ref[:, sl] += _dot(dz, k_ref[pl.ds(start, BLOCK), sl])
                dk_ref[pl.ds(start, BLOCK), sl] += _dot_tn(dz, q_ref[:, sl])
                dv_ref[pl.ds(start, BLOCK), sl] += _dot_tn(ab, dob_ref[:, sl])

        def left(t, carry):
            chunks([SB_GROUP * t + g for g in range(SB_GROUP)], False)
            return carry

        def diagonal(t, carry):
            chunks([n_chunks - per + SB_GROUP * t + g for g in range(SB_GROUP)], True)
            return carry

        n_left = (n_chunks - per) // SB_GROUP
        ran = trips_ref[pl.program_id(0) // (SB_NH_FWD // nh), qi]
        lax.fori_loop(n_left - ran, n_left, left, 0)
        lax.fori_loop(0, per // SB_GROUP, diagonal, 0)
        dq_ref[...] = acc_ref[...]

    head = lambda off: (lambda h, i, t: (0, (base + off) // nh + h))
    blk = pl.BlockSpec((tq, wide), lambda h, i, t: (i, h))
    whole = pl.BlockSpec((s, wide), lambda h, i, t: (0, h))
    once = pl.Buffered(1)
    grid_spec = pltpu.PrefetchScalarGridSpec(
        num_scalar_prefetch=1, grid=(SB_HEADS // nh, nq),
        in_specs=[pl.BlockSpec((tq, wide), lambda h, i, t: (i, base // nh + h)),
                  pl.BlockSpec((s, wide), head(SB_HEADS), pipeline_mode=once),
                  pl.BlockSpec((s, wide), head(2 * SB_HEADS), pipeline_mode=once),
                  pl.BlockSpec((BLOCK, 2 * BLOCK), lambda h, i, t: (0, 0)),
                  blk, blk],
        out_specs=[blk, whole, whole],
        scratch_shapes=[pltpu.VMEM((tq, wide), F32), pltpu.VMEM((tq, wide), F32), pltpu.VMEM((tq, wide), BF16)],
    )
    return pl.pallas_call(
        body, name=name, grid_spec=grid_spec,
        out_shape=[jax.ShapeDtypeStruct((s, SB_WIDTH), F32)] * 3,
        compiler_params=_params(("parallel", "arbitrary")),
    )(trips, qkv, qkv, qkv, before, rc, do)


GATE_TC = 512


def _gate_fwd(proj, gate_b, up_dil, up_sb, *, name):
    s, d = up_dil.shape
    tr = _pick(s, 512)
    g0 = QKV_WIDTH // GATE_TC
    nc = d // GATE_TC

    def body(ga_ref, gb_ref, b_ref, ud_ref, us_ref, o_ref):
        ga = jax.nn.sigmoid(ga_ref[...] + b_ref[0:1, :])
        gb = jax.nn.sigmoid(gb_ref[...] + b_ref[1:2, :])
        o_ref[...] = (ga * ud_ref[...] + gb * us_ref[...]).astype(o_ref.dtype)

    blk = pl.BlockSpec((tr, GATE_TC), lambda i, j: (i, j))
    return pl.pallas_call(
        body, name=name, grid=(s // tr, nc),
        in_specs=[pl.BlockSpec((tr, GATE_TC), lambda i, j: (i, g0 + j)),
                  pl.BlockSpec((tr, GATE_TC), lambda i, j: (i, g0 + nc + j)),
                  pl.BlockSpec((2, GATE_TC), lambda i, j: (0, j)), blk, blk],
        out_specs=blk,
        out_shape=jax.ShapeDtypeStruct((s, d), BF16),
        compiler_params=_params(("parallel", "parallel")),
    )(proj, proj, gate_b, up_dil, up_sb)


def _gate_bwd(proj, gate_b, up_dil, up_sb, dmixed, *, name):
    s, d = up_dil.shape
    tr = _pick(s, 512)
    g0 = QKV_WIDTH // GATE_TC
    nc = d // GATE_TC

    def body(ga_ref, gb_ref, b_ref, ud_ref, us_ref, dm_ref, dud_ref, dus_ref, dga_ref, dgb_ref, db_ref):
        ga = jax.nn.sigmoid(ga_ref[...] + b_ref[0:1, :])
        gb = jax.nn.sigmoid(gb_ref[...] + b_ref[1:2, :])
        dm = dm_ref[...]
        dud_ref[...] = (dm * ga).astype(BF16)
        dus_ref[...] = (dm * gb).astype(BF16)
        dla = dm * ud_ref[...] * ga * (1.0 - ga)
        dlb = dm * us_ref[...] * gb * (1.0 - gb)
        dga_ref[...] = dla.astype(BF16)
        dgb_ref[...] = dlb.astype(BF16)

        @pl.when(pl.program_id(1) == 0)
        def _():
            db_ref[...] = jnp.zeros_like(db_ref)

        db_ref[0:1, :] += jnp.sum(dla, axis=0, keepdims=True)
        db_ref[1:2, :] += jnp.sum(dlb, axis=0, keepdims=True)

    blk = pl.BlockSpec((tr, GATE_TC), lambda j, i: (i, j))
    dud, dus, dga, dgb, db = pl.pallas_call(
        body, name=name, grid=(nc, s // tr),
        in_specs=[pl.BlockSpec((tr, GATE_TC), lambda j, i: (i, g0 + j)),
                  pl.BlockSpec((tr, GATE_TC), lambda j, i: (i, g0 + nc + j)),
                  pl.BlockSpec((2, GATE_TC), lambda j, i: (0, j)), blk, blk, blk],
        out_specs=[blk, blk, blk, blk, pl.BlockSpec((2, GATE_TC), lambda j, i: (0, j))],
        out_shape=[jax.ShapeDtypeStruct((s, d), BF16)] * 4 + [jax.ShapeDtypeStruct((2, d), F32)],
        compiler_params=_params(("parallel", "arbitrary")),
    )(proj, proj, gate_b, up_dil, up_sb, dmixed)
    return dud, dus, (dga, dgb), db


def _adamw(w, g, m, v, *, name):
    shape = w.shape
    cols = shape[-1]
    rows = w.size // cols
    tr = _pick(rows, max(8, (512 * 1024) // cols))
    if rows % tr or (tr % 8 and tr != rows):
        tr = rows
    c1 = 1.0 - ADAM_B1 ** ADAM_STEP
    c2 = 1.0 - ADAM_B2 ** ADAM_STEP

    def body(w_ref, g_ref, m_ref, v_ref, d_ref, nm_ref, nv_ref):
        gv = g_ref[...]
        nm = ADAM_B1 * m_ref[...] + (1.0 - ADAM_B1) * gv
        nv = ADAM_B2 * v_ref[...] + (1.0 - ADAM_B2) * (gv * gv)
        d_ref[...] = -ADAM_LR * ((nm / c1) / (jnp.sqrt(nv / c2) + ADAM_EPS) + ADAM_WD * w_ref[...])
        nm_ref[...] = nm
        nv_ref[...] = nv

    blk = pl.BlockSpec((tr, cols), lambda i: (i, 0))
    outs = pl.pallas_call(
        body, name=name, grid=(rows // tr,),
        in_specs=[blk] * 4, out_specs=[blk] * 3,
        out_shape=[jax.ShapeDtypeStruct((rows, cols), F32)] * 3,
        compiler_params=_params(("parallel",)),
    )(*(t.reshape(rows, cols) for t in (w, g, m, v)))
    return tuple(t.reshape(shape) for t in outs)


def _adamw_layers(w, grads, m, v, *, name):
    depth, r, cols = w.shape
    tr = _pick(r, max(8, (512 * 1024) // cols))
    per = r // tr
    c1 = 1.0 - ADAM_B1 ** ADAM_STEP
    c2 = 1.0 - ADAM_B2 ** ADAM_STEP

    def body(w_ref, m_ref, v_ref, *rest):
        g_refs, (g_ref, d_ref, nm_ref, nv_ref) = rest[:depth], rest[depth:]
        layer = pl.program_id(0) // per
        gv = g_refs[0][...]
        for l in range(1, depth):
            gv = jnp.where(layer == l, g_refs[l][...], gv)
        nm = ADAM_B1 * m_ref[...] + (1.0 - ADAM_B1) * gv
        nv = ADAM_B2 * v_ref[...] + (1.0 - ADAM_B2) * (gv * gv)
        g_ref[...] = gv
        d_ref[...] = -ADAM_LR * ((nm / c1) / (jnp.sqrt(nv / c2) + ADAM_EPS) + ADAM_WD * w_ref[...])
        nm_ref[...] = nm
        nv_ref[...] = nv

    def of_layer(l):
        return pl.BlockSpec((tr, cols), lambda i: (jnp.clip(i - l * per, 0, per - 1), 0))

    blk = pl.BlockSpec((tr, cols), lambda i: (i, 0))
    outs = pl.pallas_call(
        body, name=name, grid=(depth * per,),
        in_specs=[blk] * 3 + [of_layer(l) for l in range(depth)], out_specs=[blk] * 4,
        out_shape=[jax.ShapeDtypeStruct((depth * r, cols), F32)] * 4,
        compiler_params=_params(("arbitrary",)),
    )(*(t.reshape(depth * r, cols) for t in (w, m, v)), *grads)
    return tuple(t.reshape(w.shape) for t in outs)


def _add_half(full, other, which, *, name):
    n, rh, c = other.shape
    tr = _pick(rh, max(8, (1024 * 1024) // c))
    nb = rh // tr

    def body(which_ref, a_ref, b_ref, o_ref):
        o_ref[...] = (a_ref[...] + b_ref[...]).astype(o_ref.dtype)

    grid_spec = pltpu.PrefetchScalarGridSpec(
        num_scalar_prefetch=1, grid=(n, nb),
        in_specs=[pl.BlockSpec((None, tr, c), lambda s, i, w: (s, w[0] * nb + i, 0)),
                  pl.BlockSpec((None, tr, c), lambda s, i, w: (s, i, 0))],
        out_specs=pl.BlockSpec((None, tr, c), lambda s, i, w: (s, i, 0)),
    )
    return pl.pallas_call(
        body, name=name, grid_spec=grid_spec,
        out_shape=jax.ShapeDtypeStruct((n, rh, c), BF16),
        compiler_params=_params(("parallel", "parallel")),
    )(which, full, other)


def _sum_chips(own, parts, chip, *, name):
    n, r, c = parts.shape
    tr = _pick(r, max(16, (1024 * 1024) // c))

    def body(chip_ref, own_ref, p1_ref, p2_ref, p3_ref, o_ref):
        acc = own_ref[...].astype(F32)
        for p in (p1_ref, p2_ref, p3_ref):
            acc = acc + p[...].astype(F32)
        o_ref[...] = acc

    def slot(k):
        return pl.BlockSpec((None, tr, c), lambda i, o: (o[0] ^ k, i, 0))

    grid_spec = pltpu.PrefetchScalarGridSpec(
        num_scalar_prefetch=1, grid=(r // tr,),
        in_specs=[slot(0), slot(1), slot(2), slot(3)],
        out_specs=pl.BlockSpec((tr, c), lambda i, o: (i, 0)),
    )
    return pl.pallas_call(
        body, name=name, grid_spec=grid_spec,
        out_shape=jax.ShapeDtypeStruct((r, c), F32),
        compiler_params=_params(("parallel",)),
    )(chip, own, parts, parts, parts)


def _place_piece(full, piece, index, *, name):
    a, n, r, c = full.shape
    tr = _pick(r, max(16, (1024 * 1024) // c))

    def body(index_ref, piece_ref, full_ref, o_ref):
        o_ref[...] = piece_ref[...]

    grid_spec = pltpu.PrefetchScalarGridSpec(
        num_scalar_prefetch=1, grid=(a, r // tr),
        in_specs=[pl.BlockSpec((None, tr, c), lambda l, i, k: (l, i, 0)), ANY],
        out_specs=pl.BlockSpec((None, None, tr, c), lambda l, i, k: (l, k[0], i, 0)),
    )
    return pl.pallas_call(
        body, name=name, grid_spec=grid_spec,
        out_shape=jax.ShapeDtypeStruct(full.shape, full.dtype),
        input_output_aliases={2: 0},
        compiler_params=_params(("parallel", "parallel")),
    )(index, piece, full)


ANY = pl.BlockSpec(memory_space=pl.ANY)


def _place():
    x, y, c = lax.axis_index("x"), lax.axis_index("y"), lax.axis_index("c")
    chips = [(1 - x, y), (x, 1 - y), (1 - x, 1 - y)]
    return x, y, c, chips


def _remote(src, dst, send_sem, recv_sem, to):
    return pltpu.make_async_remote_copy(src_ref=src, dst_ref=dst, send_sem=send_sem, recv_sem=recv_sem,
                                        device_id=to, device_id_type=MESH)


def _half(ref, chip, which):
    rh = ref.shape[1] // 2
    return ref.at[chip, pl.ds(which * rh, rh), :]


def _gather_ici_copies(layer):
    def copies(ins, outs, send_sems, recv_sems):
        x, y, c, chips = _place()
        me = 2 * x + y
        sends, recvs = [], []
        for t in range(len(ins)):
            rh = ins[t].shape[1] // 2
            for k, (px, py) in enumerate(chips):
                sems = (send_sems.at[3 * t + k], recv_sems.at[3 * t + k])
                sends.append(_remote(ins[t].at[layer, pl.ds(c * rh, rh), :], _half(outs[t], me, c), *sems, (px, py, c)))
                landed = _half(outs[t], 2 * px + py, c)
                recvs.append(_remote(landed, landed, *sems, (px, py, c)))
        return sends, recvs
    return copies


def _gather_pass_copies(ins, outs, send_sems, recv_sems):
    x, y, c, chips = _place()
    sibling = (x, y, 1 - c)
    sends, recvs = [], []
    for t in range(len(ins)):
        for k, (px, py) in enumerate(chips):
            sems = (send_sems.at[3 * t + k], recv_sems.at[3 * t + k])
            sends.append(_remote(_half(ins[t], 2 * px + py, c), _half(outs[t], 2 * px + py, c), *sems, sibling))
            other = _half(outs[t], 2 * px + py, 1 - c)
            recvs.append(_remote(other, other, *sems, sibling))
    return sends, recvs


def _gathered_shapes(shards):
    return [jax.ShapeDtypeStruct((N_CHIPS,) + w.shape[1:], w.dtype) for w in shards]


def _gather_ici_side(shards, layer):
    return (tuple(shards), _gathered_shapes(shards), _gather_ici_copies(layer), 3 * len(shards))


def _gather_layer(shards, layer, *, name):
    n = len(shards)
    ici = _gather_ici_copies(layer)

    def body(*refs):
        ins, outs = refs[:n], refs[n:2 * n]
        send_sems, recv_sems, pass_send, pass_recv = refs[2 * n:]
        sends, recvs = ici(ins, outs, send_sems, recv_sems)
        passes, arrivals = _gather_pass_copies(outs, outs, pass_send, pass_recv)
        for cp in sends:
            cp.start()
        for landed, onward in zip(recvs, passes):
            landed.wait_recv()
            onward.start()
        for cp in arrivals:
            cp.wait_recv()
        for cp in sends + passes:
            cp.wait_send()

    return pl.pallas_call(
        body, name=name,
        in_specs=[ANY] * n, out_specs=[ANY] * n,
        out_shape=_gathered_shapes(shards),
        scratch_shapes=[pltpu.SemaphoreType.DMA((3 * n,)) for _ in range(4)],
        compiler_params=pltpu.CompilerParams(has_side_effects=True),
    )(*shards)


def _gather_pass(landed, *, name):
    n = len(landed)

    def body(*refs):
        ins, outs = refs[:n], refs[n:2 * n]
        send_sems, recv_sems = refs[2 * n:]
        passes, arrivals = _gather_pass_copies(ins, outs, send_sems, recv_sems)
        for cp in passes:
            cp.start()
        for cp in arrivals:
            cp.wait_recv()
        for cp in passes:
            cp.wait_send()

    return pl.pallas_call(
        body, name=name,
        in_specs=[ANY] * n, out_specs=[ANY] * n,
        out_shape=[jax.ShapeDtypeStruct(w.shape, w.dtype) for w in landed],
        input_output_aliases={t: t for t in range(n)},
        scratch_shapes=[pltpu.SemaphoreType.DMA((3 * n,)), pltpu.SemaphoreType.DMA((3 * n,))],
        compiler_params=pltpu.CompilerParams(has_side_effects=True),
    )(*landed)


def _pair_copies(ins, outs, send_sems, recv_sems):
    x, y, c, _ = _place()
    sibling = (x, y, 1 - c)
    sends, recvs = [], []
    for t in range(len(ins)):
        rh = ins[t].shape[1] // 2
        sems = (send_sems.at[t], recv_sems.at[t])
        sends.append(_remote(ins[t].at[:, pl.ds((1 - c) * rh, rh), :], outs[t], *sems, sibling))
        recvs.append(_remote(outs[t], outs[t], *sems, sibling))
    return sends, recvs


def _pair_shapes(grads):
    return [jax.ShapeDtypeStruct((g.shape[0], g.shape[1] // 2, g.shape[2]), g.dtype) for g in grads]


def _pair_side(grads):
    return (tuple(grads), _pair_shapes(grads), _pair_copies, len(grads))


def _pair_swap_halves(grads, *, name):
    n = len(grads)

    def body(*refs):
        sends, recvs = _pair_copies(refs[:n], refs[n:2 * n], *refs[2 * n:])
        for cp in sends:
            cp.start()
        for cp in recvs:
            cp.wait_recv()
        for cp in sends:
            cp.wait_send()

    return pl.pallas_call(
        body, name=name,
        in_specs=[ANY] * n, out_specs=[ANY] * n,
        out_shape=_pair_shapes(grads),
        scratch_shapes=[pltpu.SemaphoreType.DMA((n,)), pltpu.SemaphoreType.DMA((n,))],
        compiler_params=pltpu.CompilerParams(has_side_effects=True),
    )(*grads)


def _chip_copies(ins, outs, send_sems, recv_sems):
    x, y, c, chips = _place()
    me = 2 * x + y
    sends, recvs = [], []
    for t in range(len(ins)):
        for k, (px, py) in enumerate(chips):
            sems = (send_sems.at[3 * t + k], recv_sems.at[3 * t + k])
            sends.append(_remote(ins[t].at[2 * px + py], outs[t].at[me], *sems, (px, py, c)))
            theirs = outs[t].at[2 * px + py]
            recvs.append(_remote(theirs, theirs, *sems, (px, py, c)))
    return sends, recvs


def _chip_side(parts):
    return (tuple(parts), [jax.ShapeDtypeStruct(p.shape, p.dtype) for p in parts], _chip_copies, 3 * len(parts))


def _chip_exchange(parts, *, name):
    n = len(parts)

    def body(*refs):
        sends, recvs = _chip_copies(refs[:n], refs[n:2 * n], *refs[2 * n:])
        for cp in sends:
            cp.start()
        for cp in recvs:
            cp.wait_recv()
        for cp in sends:
            cp.wait_send()

    return pl.pallas_call(
        body, name=name,
        in_specs=[ANY] * n, out_specs=[ANY] * n,
        out_shape=[jax.ShapeDtypeStruct(p.shape, p.dtype) for p in parts],
        scratch_shapes=[pltpu.SemaphoreType.DMA((3 * n,)), pltpu.SemaphoreType.DMA((3 * n,))],
        compiler_params=pltpu.CompilerParams(has_side_effects=True),
    )(*parts)


def _pair_join_halves(halves, *, name):
    n = len(halves)

    def body(*refs):
        ins, outs = refs[:n], refs[n:2 * n]
        send_sems, recv_sems = refs[2 * n:]
        x, y, c, _ = _place()
        sibling = (x, y, 1 - c)
        cps = []
        for t in range(n):
            rh = ins[t].shape[0]
            cp = _remote(ins[t], outs[t].at[pl.ds(c * rh, rh), :], send_sems.at[t], recv_sems.at[t], sibling)
            cp.start()
            cps.append(cp)
        for t, cp in enumerate(cps):
            rh = ins[t].shape[0]
            theirs = outs[t].at[pl.ds((1 - c) * rh, rh), :]
            _remote(theirs, theirs, send_sems.at[t], recv_sems.at[t], sibling).wait_recv()
            cp.wait_send()

    return pl.pallas_call(
        body, name=name,
        in_specs=[ANY] * n, out_specs=[ANY] * n,
        out_shape=[jax.ShapeDtypeStruct((2 * h.shape[0], h.shape[1]), h.dtype) for h in halves],
        scratch_shapes=[pltpu.SemaphoreType.DMA((n,)), pltpu.SemaphoreType.DMA((n,))],
        compiler_params=pltpu.CompilerParams(has_side_effects=True),
    )(*halves)


def _exchange_small(buf, *, name):
    rows, lanes = buf.shape

    def body(in_ref, out_ref, send_sems, recv_sems, local_sem):
        x, y, c = lax.axis_index("x"), lax.axis_index("y"), lax.axis_index("c")
        me = 4 * x + 2 * y + c
        lc = pltpu.make_async_copy(in_ref, out_ref.at[me], local_sem)
        lc.start()
        cps = []
        for k in range(1, 8):
            fx, fy, fc = (k >> 2) & 1, (k >> 1) & 1, k & 1
            to = (x ^ fx, y ^ fy, c ^ fc)
            cp = _remote(in_ref, out_ref.at[me], send_sems.at[k - 1], recv_sems.at[k - 1], to)
            cp.start()
            cps.append(cp)
        for k in range(1, 8):
            fx, fy, fc = (k >> 2) & 1, (k >> 1) & 1, k & 1
            theirs = out_ref.at[4 * (x ^ fx) + 2 * (y ^ fy) + (c ^ fc)]
            _remote(theirs, theirs, send_sems.at[k - 1], recv_sems.at[k - 1], (x ^ fx, y ^ fy, c ^ fc)).wait_recv()
        for cp in cps:
            cp.wait_send()
        lc.wait()

    return pl.pallas_call(
        body, name=name,
        in_specs=[ANY], out_specs=ANY,
        out_shape=jax.ShapeDtypeStruct((8, rows, lanes), buf.dtype),
        scratch_shapes=[pltpu.SemaphoreType.DMA((7,)), pltpu.SemaphoreType.DMA((7,)), pltpu.SemaphoreType.DMA],
        compiler_params=pltpu.CompilerParams(has_side_effects=True),
    )(buf)


def _sum_slots(slots, *, name):
    n, rows, lanes = slots.shape

    def body(s_ref, o_ref):
        acc = s_ref[0]
        for t in range(1, n):
            acc = acc + s_ref[t]
        o_ref[...] = acc

    return pl.pallas_call(
        body, name=name,
        out_shape=jax.ShapeDtypeStruct((rows, lanes), slots.dtype),
    )(slots)


def _relu2(acc):
    r = jnp.maximum(acc, 0.0)
    return acc, r * r


def _with_side(result, side):
    return result if side else (result, ())


def _layer_fwd(x, w, cos, sin, tag, sides=None, rest=None):
    sides = sides or {}
    got = {}
    h = _rmsnorm_fwd(x, w["norm1_g"], name=f"norm1_{tag}")
    proj, got["proj"] = _with_side(_matmul(h, w["w_in"], mode="nn", b_sharded=True, tm=512, side=sides.get("proj"),
                                           name=f"proj_{tag}"), sides.get("proj"))
    if rest:
        w = {**w, **rest(got["proj"])}
    qkv, res1, res2 = _qkv_prep(proj, w["q_norm_g"], w["k_norm_g"], cos, sin, name=f"qkv_prep_{tag}")
    dil_src = [(qkv, (0, 3, 6))] + [(t.reshape(-1, 3 * GROUP_WIDTH), (0, 1, 2)) for t in (res1, res2)]
    outs, lds = [], []
    for g, (src, cols) in enumerate(dil_src):
        o_g, ld_g = _dil_attn_fwd(src, g, cols, name=f"dil{g}_fwd_{tag}")
        outs.append(o_g)
        lds.append(ld_g)
    yb, y, lse = _dil_merge(outs, lds, name=f"dil_merge_{tag}")
    o_sb, rc, trips = _sb_attn_fwd(qkv, name=f"sb_fwd_{tag}")
    up_dil = _matmul(yb, w["w_up_dil"], mode="nn", b_sharded=True, name=f"up_dil_{tag}")
    up_sb = _matmul(o_sb, w["w_up_sb"], mode="nn", b_sharded=True, name=f"up_sb_{tag}")
    mixed = _gate_fwd(proj, w["gate_b"], up_dil, up_sb, name=f"gate_{tag}")
    x1, got["out"] = _with_side(_matmul(mixed, w["w_out"], mode="nn", extras=(x,), epilogue=lambda acc, res: (acc + res,),
                                        side=sides.get("out"), name=f"out_{tag}"), sides.get("out"))
    h2 = _rmsnorm_fwd(x1, w["norm2_g"], name=f"norm2_{tag}")
    (u, a), got["ff1"] = _with_side(_matmul(h2, w["w_ff1"], mode="nn", b_sharded=True, out_dtypes=(F32, BF16),
                                            epilogue=_relu2, side=sides.get("ff1"), name=f"ff1_{tag}"), sides.get("ff1"))
    x2, got["ff2"] = _with_side(_matmul(a, w["w_ff2"], mode="nn", extras=(x1,), epilogue=lambda acc, res: (acc + res,),
                                        side=sides.get("ff2"), name=f"ff2_{tag}"), sides.get("ff2"))
    saved = dict(x=x, h=h, proj=proj, qkv=qkv, dil_src=dil_src, y=y, yb=yb, lse=lse, o_sb=o_sb, rc=rc, trips=trips, up_dil=up_dil, up_sb=up_sb,
                 mixed=mixed, x1=x1, h2=h2, u=u, a=a, w=w)
    return x2, saved, got


def _layer_bwd(dx, dxb, sv, cos, sin, tag, sides=None, late=None):
    sides = dict(sides or {})
    w = sv["w"]
    grads, got = {}, {}
    du, got["d_a"] = _with_side(_matmul(dxb, w["w_ff2"], mode="nt", extras=(sv["u"],), out_dtypes=(BF16,),
                                        epilogue=lambda acc, u: (acc * (2.0 * jnp.maximum(u, 0.0)),),
                                        side=sides.get("d_a"), name=f"d_a_{tag}"), sides.get("d_a"))
    grads["w_ff2"], got["dw_ff2"] = _with_side(_matmul(sv["a"], dxb, mode="tn", side=sides.get("dw_ff2"),
                                                       name=f"dw_ff2_{tag}"), sides.get("dw_ff2"))
    dh2, got["d_h2"] = _with_side(_matmul(du, w["w_ff1"], mode="nt", b_sharded=True, side=sides.get("d_h2"),
                                          name=f"d_h2_{tag}"), sides.get("d_h2"))
    grads["w_ff1"] = _matmul(sv["h2"], du, mode="tn", out_sharded=True, name=f"dw_ff1_{tag}")
    dx1, dx1b, grads["norm2_g"] = _rmsnorm_bwd(sv["x1"], w["norm2_g"], dh2, dx, name=f"norm2_bwd_{tag}")
    dmixed = _matmul(dx1b, w["w_out"], mode="nt", name=f"d_mixed_{tag}")
    grads["w_out"] = _matmul(sv["mixed"], dx1b, mode="tn", name=f"dw_out_{tag}")
    dud, dus, dgate, grads["gate_b"] = _gate_bwd(sv["proj"], w["gate_b"], sv["up_dil"], sv["up_sb"], dmixed,
                                                 name=f"gate_bwd_{tag}")
    dy_dil = _matmul(dud, w["w_up_dil"], mode="nt", b_sharded=True, name=f"d_ydil_{tag}")
    grads["w_up_dil"] = _matmul(sv["yb"], dud, mode="tn", out_sharded=True, name=f"dw_up_dil_{tag}")
    dy_sb = _matmul(dus, w["w_up_sb"], mode="nt", b_sharded=True, name=f"d_ysb_{tag}")
    grads["w_up_sb"] = _matmul(sv["o_sb"], dus, mode="tn", out_sharded=True, name=f"dw_up_sb_{tag}")
    if late:
        sides.update(late(grads))
    dys = [dy_dil] + _dil_to_residues(dy_dil, name=f"dil_dy_{tag}")
    d_dil = [_dil_attn_bwd(src, g, cols, sv["y"][g], sv["lse"][g], dys[g], name=f"dil{g}_bwd_{tag}")
             for g, (src, cols) in enumerate(sv["dil_src"])]
    for g in (1, 2):
        d_dil[g] = [t.reshape(DIL_GROUPS[g][1], -1, GROUP_WIDTH) for t in d_dil[g]]
    d_sb = _sb_attn_bwd(sv["qkv"], sv["rc"], sv["trips"], dy_sb, name=f"sb_bwd_{tag}")
    dproj, grads["q_norm_g"], grads["k_norm_g"] = _qkv_prep_bwd(
        sv["proj"], w["q_norm_g"], w["k_norm_g"], cos, sin, d_dil, d_sb, dgate, name=f"qkv_prep_bwd_{tag}")
    dh, got["d_h"] = _with_side(_matmul(dproj, w["w_in"], mode="nt", b_sharded=True, tm=512, tn=2048, side=sides.get("d_h"),
                                        name=f"d_h_{tag}"), sides.get("d_h"))
    grads["w_in"], got["dw_in"] = _with_side(_matmul(sv["h"], dproj, mode="tn", out_sharded=True, tm=512, tk=1024,
                                                     side=sides.get("dw_in"), name=f"dw_in_{tag}"), sides.get("dw_in"))
    dx0, dx0b, grads["norm1_g"] = _rmsnorm_bwd(sv["x"], w["norm1_g"], dh, dx1, name=f"norm1_bwd_{tag}")
    return dx0, dx0b, grads, got


BIG = ("w_in", "w_up_dil", "w_up_sb", "w_out", "w_ff1", "w_ff2")
ROW_SHARDED = ("w_out", "w_ff2")
WEIGHTS = ("norm1_g", "w_in", "q_norm_g", "k_norm_g", "w_up_dil", "w_up_sb", "gate_b", "w_out", "norm2_g", "w_ff1", "w_ff2")


def _reduce_begin(grads, which, tag, swapped=None):
    theirs = dict(swapped or {})
    todo = [n for n in grads if n not in theirs]
    if todo:
        theirs.update(zip(todo, _pair_swap_halves([grads[n] for n in todo], name=f"rs_pair_{tag}")))
    return {n: _add_half(grads[n], theirs[n], which, name=f"rs_pair_sum_{n}_{tag}") for n in grads}


def _reduce_end(pair, parts, which, chip, tag):
    names = list(pair)
    halves = [_sum_chips(pair[n], parts[n], chip, name=f"rs_chip_sum_{n}_{tag}") for n in names]
    joined = _pair_join_halves(halves, name=f"rs_join_{tag}")
    out = {}
    for n, j, h in zip(names, joined, halves):
        r, c = j.shape
        out[n] = _place_piece(j.reshape(1, 2, r // 2, c), h[None], which, name=f"rs_place_{n}_{tag}").reshape(r, c)
    return out


def _pack_rows(vecs):
    rows, spans, at = [], [], 0
    for v in vecs:
        r = v.size // 128
        padded = -(-r // 8) * 8
        rows.append(jnp.pad(v.reshape(r, 128), ((0, padded - r), (0, 0))))
        spans.append((at, r))
        at += padded
    return jnp.concatenate(rows, axis=0), spans


def kernel(x, norm1_g, w_in, q_norm_g, k_norm_g, w_up_dil, w_up_sb, gate_b, w_out, norm2_g, w_ff1, w_ff2, loss_target, m_norm1_g, m_w_in, m_q_norm_g, m_k_norm_g, m_w_up_dil, m_w_up_sb, m_gate_b, m_w_out, m_norm2_g, m_w_ff1, m_w_ff2, v_norm1_g, v_w_in, v_q_norm_g, v_k_norm_g, v_w_up_dil, v_w_up_sb, v_gate_b, v_w_out, v_norm2_g, v_w_ff1, v_w_ff2):
    weights = dict(norm1_g=norm1_g, w_in=w_in, q_norm_g=q_norm_g, k_norm_g=k_norm_g, w_up_dil=w_up_dil,
                   w_up_sb=w_up_sb, gate_b=gate_b, w_out=w_out, norm2_g=norm2_g, w_ff1=w_ff1, w_ff2=w_ff2)
    moments_m = dict(norm1_g=m_norm1_g, w_in=m_w_in, q_norm_g=m_q_norm_g, k_norm_g=m_k_norm_g, w_up_dil=m_w_up_dil,
                     w_up_sb=m_w_up_sb, gate_b=m_gate_b, w_out=m_w_out, norm2_g=m_norm2_g, w_ff1=m_w_ff1, w_ff2=m_w_ff2)
    moments_v = dict(norm1_g=v_norm1_g, w_in=v_w_in, q_norm_g=v_q_norm_g, k_norm_g=v_k_norm_g, w_up_dil=v_w_up_dil,
                     w_up_sb=v_w_up_sb, gate_b=v_gate_b, w_out=v_w_out, norm2_g=v_norm2_g, w_ff1=v_w_ff1, w_ff2=v_w_ff2)
    depth = w_in.shape[0]
    seq, d_model = x.shape[1], x.shape[2]
    chip = 2 * lax.axis_index("x") + lax.axis_index("y")
    core = lax.axis_index("c")
    which = jnp.reshape(core, (1,)).astype(jnp.int32)

    chip_index = jnp.reshape(chip, (1,)).astype(jnp.int32)
    shards = {n: weights[n].astype(BF16) for n in BIG}
    bias_buf, ((_, bias_rows),) = _pack_rows([gate_b])
    bias_slots = _exchange_small(bias_buf, name="gather_gate_b")
    bias = bias_slots[0::2, :bias_rows].reshape(N_CHIPS, depth, 2, d_model // N_CHIPS)
    bias = jnp.transpose(bias, (1, 2, 0, 3)).reshape(depth, 2, d_model)

    def placed(names, l, landed):
        w = {}
        for n in names:
            w[n] = _place_piece(landed[n][None], shards[n][l:l + 1], chip_index, name=f"gather_place_{n}_l{l}")[0]
            if n in ROW_SHARDED:
                w[n] = w[n].reshape(-1, w[n].shape[-1])
        return w

    def ici_side(names, l):
        return _gather_ici_side([shards[n] for n in names], l)

    def by_chip(n, g):
        return g.reshape(N_CHIPS, -1, g.shape[-1]) if n in ROW_SHARDED else g

    ahead = {"ff1": ("w_in",), "out": ("w_up_dil", "w_up_sb", "w_out"), "ff2": ("w_ff1",)}
    cos, sin = _rope_tables(seq)
    first = ("w_in",)
    ready = placed(first, 0, dict(zip(first, _gather_layer([shards[n] for n in first], 0, name="gather_l0_w_in"))))
    pending = tuple(n for n in BIG if n not in first)
    act = x[0]
    saved = []
    for l in range(depth):
        def rest(landed, l=l, pending=pending):
            passed = _gather_pass(list(landed), name=f"gather_pass_l{l}_rest")
            return placed(pending, l, dict(zip(pending, passed)))

        sides = {"proj": ici_side(pending, l)}
        if l + 1 < depth:
            sides.update({k: ici_side(names, l + 1) for k, names in ahead.items()})
        w = dict(ready, norm1_g=norm1_g[l], norm2_g=norm2_g[l], q_norm_g=q_norm_g[l], k_norm_g=k_norm_g[l], gate_b=bias[l])
        act, sv, got = _layer_fwd(act, w, cos, sin, f"l{l}", sides, rest)
        saved.append(sv)
        if l + 1 < depth:
            names = [n for k in ahead for n in ahead[k]]
            passed = _gather_pass([t for k in ahead for t in got[k]], name=f"gather_pass_l{l + 1}")
            ready = placed(names, l + 1, dict(zip(names, passed)))
            pending = ("w_ff2",)
    loss, dx, dxb = _loss_head(act, loss_target[0])
    loss = lax.psum(loss, ("x", "y", "c"))

    below = {"d_a": ("w_in",), "dw_ff2": ("w_ff1", "w_out", "w_up_dil", "w_up_sb"), "d_h2": ("w_ff2",)}
    own = {"d_h": ("w_ff2", "w_ff1"), "dw_in": ("w_out", "w_up_dil", "w_up_sb")}
    known_early = tuple(n for names in own.values() for n in names)
    grads, reduced = [None] * depth, [None] * depth
    pair = None
    for l in reversed(range(depth)):
        sides = {k: _chip_side([pair[n] for n in names]) for k, names in below.items()} if pair else None
        early = {}

        def late(g, early=early):
            for k, names in own.items():
                early.update(_reduce_begin({n: by_chip(n, g[n]) for n in names}, which, f"l0_{k}"))
            return {k: _chip_side([early[n] for n in names]) for k, names in own.items()}

        def late_swap(g):
            return {"dw_in": _pair_side([by_chip(n, g[n]) for n in known_early])}

        dx, dxb, grads[l], got = _layer_bwd(dx, dxb, saved[l], cos, sin, f"l{l}", sides, late if l == 0 else late_swap)
        if pair:
            parts = {n: t for k, names in below.items() for n, t in zip(names, got[k])}
            reduced[l + 1] = _reduce_end(pair, parts, which, chip_index, f"l{l + 1}")
        if l > 0:
            pair = _reduce_begin({n: by_chip(n, grads[l][n]) for n in BIG}, which, f"l{l}",
                                 swapped=dict(zip(known_early, got["dw_in"])))
    last = _reduce_begin({"w_in": grads[0]["w_in"]}, which, "l0_w_in")
    parts = {n: t for k, names in own.items() for n, t in zip(names, got[k])}
    parts["w_in"] = _chip_exchange([last["w_in"]], name="rs_chips_l0_w_in")[0]
    pair = dict(early, **last)
    reduced[0] = _reduce_end({n: pair[n] for n in BIG}, parts, which, chip_index, "l0")
    final, deltas, new_m, new_v = {}, {}, {}, {}
    for n in BIG:
        final[n], deltas[n], new_m[n], new_v[n] = _adamw_layers(
            weights[n], [reduced[l][n] for l in range(depth)], moments_m[n], moments_v[n], name=f"adamw_{n}")

    small_names = ("norm1_g", "norm2_g", "q_norm_g", "k_norm_g", "gate_b")
    packed, spans = _pack_rows([grads[l][n] for l in range(depth) for n in small_names])
    total = _sum_slots(_exchange_small(packed, name="exchange_small_grads"), name="sum_small_grads")
    pieces = iter(total[at:at + r] for at, r in spans)
    small = {n: [] for n in small_names}
    for l in range(depth):
        for n in small_names:
            small[n].append(next(pieces))
    for n in ("norm1_g", "norm2_g"):
        final[n] = jnp.stack([p.reshape(d_model) for p in small[n]])
    for n in ("q_norm_g", "k_norm_g"):
        final[n] = jnp.stack([p.reshape(N_GROUPS, HEAD_DIM) for p in small[n]])
    shard_cols = d_model // N_CHIPS
    final["gate_b"] = jnp.stack([lax.dynamic_slice_in_dim(p.reshape(2, d_model), chip * shard_cols, shard_cols, axis=1)
                                 for p in small["gate_b"]])

    for n in small_names:
        deltas[n], new_m[n], new_v[n] = _adamw(weights[n], final[n], moments_m[n], moments_v[n], name=f"adamw_{n}")
    return (loss, dx[None], *[final[n] for n in WEIGHTS], *[deltas[n] for n in WEIGHTS],
            *[new_m[n] for n in WEIGHTS], *[new_v[n] for n in WEIGHTS])
```

```python
import functools
import math

import jax
import jax.numpy as jnp
from jax import lax
from jax.experimental import pallas as pl
from jax.experimental.pallas import tpu as pltpu

F32 = jnp.float32
BF16 = jnp.bfloat16

HEAD_DIM = 128
DIL_GROUPS = ((128, 1), (512, 4), (2048, 16))
N_GROUPS = 3
HEADS_PER_GROUP = 4
GROUP_WIDTH = HEADS_PER_GROUP * HEAD_DIM
DIL_WIDTH = N_GROUPS * GROUP_WIDTH
SB_HEADS = 8
SB_WIDTH = SB_HEADS * HEAD_DIM
QKV_WIDTH = 3 * DIL_WIDTH + 3 * SB_WIDTH
BLOCK = 128
ROPE_THETA = 10000.0
EPS = 1e-6
SCALE = 1.0 / math.sqrt(HEAD_DIM)
NEG_INF = float("-inf")

ADAM_LR = 0.001
ADAM_B1 = 0.9
ADAM_B2 = 0.999
ADAM_EPS = 1e-08
ADAM_WD = 0.01
ADAM_STEP = 10

N_CHIPS = 4
MESH = pl.DeviceIdType.MESH
MIB = 1024 * 1024
VMEM_LIMIT = 56 * MIB


def _params(semantics=None, vmem=VMEM_LIMIT):
    return pltpu.CompilerParams(dimension_semantics=semantics, vmem_limit_bytes=vmem)


def _pick(n, pref):
    if n <= pref:
        return n
    t = 1 << (pref.bit_length() - 1)
    while n % t:
        t //= 2
    return t


def _tile(n, pref):
    best = 0
    for t in range(256, min(n, pref) + 1, 256):
        if n % t == 0:
            best = t
    return best if best else n


def _matmul(a, b, *, mode, name, out_dtypes=(F32,), extras=(), epilogue=None, b_sharded=False, out_sharded=False,
            tm=1024, tn=1024, tk=2048, side=None):
    if mode == "nn":
        m, kdim = a.shape
        n = b.shape[-1] * (N_CHIPS if b_sharded else 1)
        assert (b.shape[-2] if b_sharded else b.shape[0]) == kdim
    elif mode == "nt":
        m, kdim = a.shape
        n = b.shape[-2]
        assert b.shape[-1] * (N_CHIPS if b_sharded else 1) == kdim
    else:
        kdim, m = a.shape
        n = b.shape[1]
        assert b.shape[0] == kdim and not b_sharded
    shard_n = n // N_CHIPS if (out_sharded or (b_sharded and mode == "nn")) else n
    shard_k = kdim // N_CHIPS if (b_sharded and mode == "nt") else kdim
    tm = _tile(m, tm)
    tn = _tile(shard_n, tn)
    tk = _tile(shard_k, tk)
    nk = kdim // tk
    nj_n = shard_n // tn
    nj_k = shard_k // tk
    j_outer = nk == 1 and (kdim * n + m * kdim * (n // tn)) < (m * kdim + kdim * n * (m // tm))
    grid = (n // tn, m // tm, nk) if j_outer else (m // tm, n // tn, nk)

    def at(index):
        return (lambda g0, g1, k: index(g1, g0, k)) if j_outer else index

    if mode == "nn":
        a_spec = pl.BlockSpec((tm, tk), at(lambda i, j, k: (i, k)))
        if b_sharded:
            b_spec = pl.BlockSpec((None, tk, tn), at(lambda i, j, k: (j // nj_n, k, j % nj_n)))
        else:
            b_spec = pl.BlockSpec((tk, tn), at(lambda i, j, k: (k, j)))
        dims = (((1,), (0,)), ((), ()))
    elif mode == "nt":
        a_spec = pl.BlockSpec((tm, tk), at(lambda i, j, k: (i, k)))
        if b_sharded:
            b_spec = pl.BlockSpec((None, tn, tk), at(lambda i, j, k: (k // nj_k, j, k % nj_k)))
        else:
            b_spec = pl.BlockSpec((tn, tk), at(lambda i, j, k: (j, k)))
        dims = (((1,), (1,)), ((), ()))
    else:
        a_spec = pl.BlockSpec((tk, tm), at(lambda i, j, k: (k, i)))
        b_spec = pl.BlockSpec((tk, tn), at(lambda i, j, k: (k, j)))
        dims = (((0,), (0,)), ((), ()))
    if out_sharded:
        o_spec = pl.BlockSpec((None, tm, tn), at(lambda i, j, k: (j // nj_n, i, j % nj_n)))
        o_shape = (N_CHIPS, m, shard_n)
    else:
        o_spec = pl.BlockSpec((tm, tn), at(lambda i, j, k: (i, j)))
        o_shape = (m, n)
    x_spec = pl.BlockSpec((tm, tn), at(lambda i, j, k: (i, j)))
    n_extra = len(extras)
    n_out = len(out_dtypes)

    side_ins, side_shapes, side_copies, n_copies = side if side else ((), (), None, 0)
    n_side_in, n_side_out = len(side_ins), len(side_shapes)

    def finish(acc, extra_refs, out_refs):
        res = (acc,) if epilogue is None else epilogue(acc, *[e[...] for e in extra_refs])
        for o, r in zip(out_refs, res):
            o[...] = r.astype(o.dtype)

    def body(a_ref, b_ref, *rest):
        extra_refs = rest[:n_extra]
        side_in_refs = rest[n_extra:n_extra + n_side_in]
        rest = rest[n_extra + n_side_in:]
        out_refs, side_out_refs, scratch = rest[:n_out], rest[n_out:n_out + n_side_out], rest[n_out + n_side_out:]
        if side:
            step = (pl.program_id(0) * grid[1] + pl.program_id(1)) * grid[2] + pl.program_id(2)
            send_sems, recv_sems = scratch[-2:]

            @pl.when(step == 0)
            def _():
                sends, _ = side_copies(side_in_refs, side_out_refs, send_sems, recv_sems)
                for cp in sends:
                    cp.start()

        if nk == 1:
            finish(lax.dot_general(a_ref[...], b_ref[...], dims, preferred_element_type=F32), extra_refs, out_refs)
        else:
            acc = scratch[0]
            k = pl.program_id(2)

            @pl.when(k == 0)
            def _():
                acc[...] = jnp.zeros_like(acc)

            acc[...] += lax.dot_general(a_ref[...], b_ref[...], dims, preferred_element_type=F32)

            @pl.when(k == nk - 1)
            def _():
                finish(acc[...], extra_refs, out_refs)

        if side:
            @pl.when(step == grid[0] * grid[1] * grid[2] - 1)
            def _():
                sends, recvs = side_copies(side_in_refs, side_out_refs, send_sems, recv_sems)
                for cp in recvs:
                    cp.wait_recv()
                for cp in sends:
                    cp.wait_send()

    scratch_shapes = [] if nk == 1 else [pltpu.VMEM((tm, tn), F32)]
    if side:
        scratch_shapes += [pltpu.SemaphoreType.DMA((n_copies,)), pltpu.SemaphoreType.DMA((n_copies,))]
    outs = pl.pallas_call(
        body,
        name=name,
        grid=grid,
        in_specs=[a_spec, b_spec] + [x_spec] * n_extra + [ANY] * n_side_in,
        out_specs=[o_spec] * n_out + [ANY] * n_side_out,
        out_shape=[jax.ShapeDtypeStruct(o_shape, dt) for dt in out_dtypes] + list(side_shapes),
        scratch_shapes=scratch_shapes,
        compiler_params=_params(("arbitrary",) * 3 if side else ("parallel", "parallel", "arbitrary")),
    )(a, b, *extras, *side_ins)
    if side:
        return (outs[0] if n_out == 1 else outs[:n_out]), outs[n_out:]
    return outs[0] if n_out == 1 else outs


def _rmsnorm_fwd(x, g, *, name):
    s, d = x.shape
    tr = _pick(s, 256)

    def body(x_ref, g_ref, o_ref):
        xv = x_ref[...]
        r = lax.rsqrt(jnp.mean(xv * xv, axis=-1, keepdims=True) + EPS)
        o_ref[...] = (xv * r * g_ref[...]).astype(o_ref.dtype)

    return pl.pallas_call(
        body, name=name, grid=(s // tr,),
        in_specs=[pl.BlockSpec((tr, d), lambda i: (i, 0)), pl.BlockSpec((1, d), lambda i: (0, 0))],
        out_specs=pl.BlockSpec((tr, d), lambda i: (i, 0)),
        out_shape=jax.ShapeDtypeStruct((s, d), BF16),
        compiler_params=_params(("parallel",)),
    )(x, g.reshape(1, d))


def _rmsnorm_bwd(x, g, dh, dres, *, name):
    s, d = x.shape
    tr = _pick(s, 256)

    def body(x_ref, g_ref, dh_ref, dres_ref, dx_ref, dxb_ref, dg_ref):
        xv = x_ref[...]
        r = lax.rsqrt(jnp.mean(xv * xv, axis=-1, keepdims=True) + EPS)
        y = xv * r
        dhv = dh_ref[...]
        dy = dhv * g_ref[...]
        dx = dres_ref[...] + r * (dy - y * jnp.mean(dy * y, axis=-1, keepdims=True))
        dx_ref[...] = dx
        dxb_ref[...] = dx.astype(BF16)

        @pl.when(pl.program_id(0) == 0)
        def _():
            dg_ref[...] = jnp.zeros_like(dg_ref)

        dg_ref[...] += jnp.sum(dhv * y, axis=0, keepdims=True)

    row = pl.BlockSpec((tr, d), lambda i: (i, 0))
    vec = pl.BlockSpec((1, d), lambda i: (0, 0))
    dx, dxb, dg = pl.pallas_call(
        body, name=name, grid=(s // tr,),
        in_specs=[row, vec, row, row],
        out_specs=[row, row, vec],
        out_shape=[jax.ShapeDtypeStruct((s, d), F32), jax.ShapeDtypeStruct((s, d), BF16),
                   jax.ShapeDtypeStruct((1, d), F32)],
        compiler_params=_params(("arbitrary",)),
    )(x, g.reshape(1, d), dh, dres)
    return dx, dxb, dg.reshape(d)


def _loss_head(y, target):
    s, d = y.shape
    tr = _pick(s, 256)

    def body(y_ref, t_ref, dy_ref, dyb_ref, part_ref):
        err = y_ref[...] - t_ref[...]
        dy = err * (1.0 / d)
        dy_ref[...] = dy
        dyb_ref[...] = dy.astype(BF16)

        @pl.when(pl.program_id(0) == 0)
        def _():
            part_ref[...] = jnp.zeros_like(part_ref)

        part_ref[...] += jnp.sum(err * err, axis=0, keepdims=True)

    row = pl.BlockSpec((tr, d), lambda i: (i, 0))
    vec = pl.BlockSpec((1, d), lambda i: (0, 0))
    dy, dyb, part = pl.pallas_call(
        body, name="loss_head", grid=(s // tr,),
        in_specs=[row, row], out_specs=[row, row, vec],
        out_shape=[jax.ShapeDtypeStruct((s, d), F32), jax.ShapeDtypeStruct((s, d), BF16),
                   jax.ShapeDtypeStruct((1, d), F32)],
        compiler_params=_params(("arbitrary",)),
    )(y, target)
    return 0.5 * jnp.sum(part) / d, dy, dyb


def _rope_tables(s):
    half = HEAD_DIM // 2
    inv_freq = ROPE_THETA ** (-jnp.arange(half, dtype=F32) / half)
    ang = jnp.arange(s, dtype=F32)[:, None] * inv_freq[None, :]
    cos, sin = jnp.cos(ang), jnp.sin(ang)
    return jnp.concatenate([cos, cos], axis=-1), jnp.concatenate([-sin, sin], axis=-1)


PREP_ROWS = 256


def _residue_spec(r, width):
    return pl.BlockSpec((r, PREP_ROWS // r, width), lambda i: (0, i, 0))


def _to_residues(scr, tile, r, dst_ref, cols, dtype):
    scr[...] = tile
    for c in range(r):
        dst_ref[c, :, cols] = scr[pl.ds(c, PREP_ROWS // r, stride=r), :].astype(dtype)


def _from_residues(scr, src_ref, cols, r):
    for c in range(r):
        scr[pl.ds(c, PREP_ROWS // r, stride=r), :] = src_ref[c, :, cols].astype(F32)
    return scr[...]


def _qkv_prep(proj, qg, kg, cos, sin, *, name):
    s = proj.shape[0]
    tr = PREP_ROWS
    assert s % tr == 0

    def body(p_ref, qg_ref, kg_ref, cos_ref, sin_ref, o_ref, res1_ref, res2_ref, scr):
        res_refs = (None, res1_ref, res2_ref)
        c = cos_ref[...]
        sn = sin_ref[...]
        for which, g_ref in ((0, qg_ref), (1, kg_ref)):
            for g in range(N_GROUPS):
                gain = g_ref[g:g + 1, :]
                for h in range(HEADS_PER_GROUP):
                    off = which * DIL_WIDTH + g * GROUP_WIDTH + h * HEAD_DIM
                    xv = p_ref[:, off:off + HEAD_DIM]
                    r = lax.rsqrt(jnp.mean(xv * xv, axis=-1, keepdims=True) + EPS)
                    y = xv * r * gain
                    rot = y * c + pltpu.roll(y, HEAD_DIM // 2, 1) * sn
                    if g == 0:
                        o_ref[:, off:off + HEAD_DIM] = rot.astype(BF16)
                    else:
                        at = which * GROUP_WIDTH + h * HEAD_DIM
                        _to_residues(scr, rot, DIL_GROUPS[g][1], res_refs[g], slice(at, at + HEAD_DIM), BF16)
        o_ref[:, 2 * DIL_WIDTH:2 * DIL_WIDTH + GROUP_WIDTH] = p_ref[:, 2 * DIL_WIDTH:2 * DIL_WIDTH + GROUP_WIDTH].astype(BF16)
        for g in range(1, N_GROUPS):
            for h in range(HEADS_PER_GROUP):
                off = 2 * DIL_WIDTH + g * GROUP_WIDTH + h * HEAD_DIM
                at = 2 * GROUP_WIDTH + h * HEAD_DIM
                _to_residues(scr, p_ref[:, off:off + HEAD_DIM], DIL_GROUPS[g][1], res_refs[g], slice(at, at + HEAD_DIM), BF16)
        o_ref[:, 3 * DIL_WIDTH:] = p_ref[:, 3 * DIL_WIDTH:].astype(BF16)

    r1, r2 = DIL_GROUPS[1][1], DIL_GROUPS[2][1]
    return pl.pallas_call(
        body, name=name, grid=(s // tr,),
        in_specs=[pl.BlockSpec((tr, QKV_WIDTH), lambda i: (i, 0)),
                  pl.BlockSpec((N_GROUPS, HEAD_DIM), lambda i: (0, 0)),
                  pl.BlockSpec((N_GROUPS, HEAD_DIM), lambda i: (0, 0)),
                  pl.BlockSpec((tr, HEAD_DIM), lambda i: (i, 0)),
                  pl.BlockSpec((tr, HEAD_DIM), lambda i: (i, 0))],
        out_specs=[pl.BlockSpec((tr, QKV_WIDTH), lambda i: (i, 0)),
                   _residue_spec(r1, 3 * GROUP_WIDTH), _residue_spec(r2, 3 * GROUP_WIDTH)],
        out_shape=[jax.ShapeDtypeStruct((s, QKV_WIDTH), BF16),
                   jax.ShapeDtypeStruct((r1, s // r1, 3 * GROUP_WIDTH), BF16),
                   jax.ShapeDtypeStruct((r2, s // r2, 3 * GROUP_WIDTH), BF16)],
        scratch_shapes=[pltpu.VMEM((tr, HEAD_DIM), F32)],
        compiler_params=_params(("parallel",)),
    )(proj, qg, kg, cos, sin)


def _qkv_prep_bwd(proj, qg, kg, cos, sin, dqkv_dil, dqkv_sb, dgate, *, name):
    s, n_in = proj.shape
    n_gate = dgate[0].shape[1]
    tr = PREP_ROWS

    def body(p_ref, qg_ref, kg_ref, cos_ref, sin_ref, *rest):
        dil_refs = rest[:9]
        sb_refs = rest[9:12]
        dgate_refs = rest[12:14]
        o_ref, dqg_ref, dkg_ref, scr = rest[14:18]
        c = cos_ref[...]
        sn = sin_ref[...]

        def incoming(g, which, h):
            cols = slice(h * HEAD_DIM, (h + 1) * HEAD_DIM)
            if g == 0:
                return dil_refs[which][:, cols]
            return _from_residues(scr, dil_refs[3 * g + which], cols, DIL_GROUPS[g][1])

        @pl.when(pl.program_id(0) == 0)
        def _():
            dqg_ref[...] = jnp.zeros_like(dqg_ref)
            dkg_ref[...] = jnp.zeros_like(dkg_ref)

        for which, g_ref, dg_ref in ((0, qg_ref, dqg_ref), (1, kg_ref, dkg_ref)):
            for g in range(N_GROUPS):
                gain = g_ref[g:g + 1, :]
                dgain = jnp.zeros((1, HEAD_DIM), F32)
                for h in range(HEADS_PER_GROUP):
                    off = which * DIL_WIDTH + g * GROUP_WIDTH + h * HEAD_DIM
                    xv = p_ref[:, off:off + HEAD_DIM]
                    r = lax.rsqrt(jnp.mean(xv * xv, axis=-1, keepdims=True) + EPS)
                    nx = xv * r
                    dout = incoming(g, which, h)
                    dy = dout * c + pltpu.roll(dout * sn, HEAD_DIM // 2, 1)
                    dgain = dgain + jnp.sum(dy * nx, axis=0, keepdims=True)
                    dn = dy * gain
                    dxv = r * (dn - nx * jnp.mean(dn * nx, axis=-1, keepdims=True))
                    o_ref[:, off:off + HEAD_DIM] = dxv.astype(BF16)
                dg_ref[g:g + 1, :] += dgain
        for g in range(N_GROUPS):
            for h in range(HEADS_PER_GROUP):
                off = 2 * DIL_WIDTH + g * GROUP_WIDTH + h * HEAD_DIM
                o_ref[:, off:off + HEAD_DIM] = incoming(g, 2, h).astype(BF16)
        for t in range(3):
            off = 3 * DIL_WIDTH + t * SB_WIDTH
            o_ref[:, off:off + SB_WIDTH] = sb_refs[t][...].astype(BF16)
        o_ref[:, QKV_WIDTH:QKV_WIDTH + n_gate] = dgate_refs[0][...]
        o_ref[:, QKV_WIDTH + n_gate:] = dgate_refs[1][...]

    grp = pl.BlockSpec((tr, GROUP_WIDTH), lambda i: (i, 0))
    sbs = pl.BlockSpec((tr, SB_WIDTH), lambda i: (i, 0))
    gain_spec = pl.BlockSpec((N_GROUPS, HEAD_DIM), lambda i: (0, 0))
    tab = pl.BlockSpec((tr, HEAD_DIM), lambda i: (i, 0))
    flat_dil = [t for grp3 in dqkv_dil for t in grp3]
    dil_specs = [grp] * 3 + [_residue_spec(DIL_GROUPS[g][1], GROUP_WIDTH) for g in (1, 2) for _ in range(3)]
    return pl.pallas_call(
        body, name=name, grid=(s // tr,),
        in_specs=[pl.BlockSpec((tr, 2 * DIL_WIDTH), lambda i: (i, 0)), gain_spec, gain_spec, tab, tab]
                 + dil_specs + [sbs] * 3 + [pl.BlockSpec((tr, n_gate), lambda i: (i, 0))] * 2,
        out_specs=[pl.BlockSpec((tr, n_in), lambda i: (i, 0)), gain_spec, gain_spec],
        out_shape=[jax.ShapeDtypeStruct((s, n_in), BF16),
                   jax.ShapeDtypeStruct((N_GROUPS, HEAD_DIM), F32),
                   jax.ShapeDtypeStruct((N_GROUPS, HEAD_DIM), F32)],
        scratch_shapes=[pltpu.VMEM((tr, HEAD_DIM), F32)],
        compiler_params=_params(("arbitrary",)),
    )(proj, qg, kg, cos, sin, *flat_dil, *dqkv_sb, *dgate)


def _dil_masks(i):
    row = lax.broadcasted_iota(jnp.int32, (BLOCK, BLOCK), 0)
    col = lax.broadcasted_iota(jnp.int32, (BLOCK, BLOCK), 1)
    return col <= row, (col >= row) & (i > 0)


def _dil_sub(nb):
    return 8 if nb % 8 == 0 else 4 if nb % 4 == 0 else 2 if nb % 2 == 0 else 1


def _dot_nt(a, b):
    return lax.dot_general(a, b, (((1,), (1,)), ((), ())), preferred_element_type=F32)


def _dot_tn(a, b):
    return lax.dot_general(a, b, (((0,), (0,)), ((), ())), preferred_element_type=F32)


def _dot(a, b):
    return jnp.dot(a, b, preferred_element_type=F32)


def _dil_attn_fwd(src, g, cols, *, name):
    s = src.shape[0]
    r = DIL_GROUPS[g][1]
    nb = s // r // BLOCK
    sub = _dil_sub(nb)
    steps = nb // sub

    def body(q_ref, kp_ref, kc_ref, vp_ref, vc_ref, o_ref, ld_ref):
        i = pl.program_id(1)
        work = []
        for u in range(sub):
            rows = slice(u * BLOCK, (u + 1) * BLOCK)
            before = slice((u - 1) * BLOCK, u * BLOCK)
            mask_c, mask_p = _dil_masks(i if u == 0 else 1)
            for h in range(HEADS_PER_GROUP):
                sl = slice(h * HEAD_DIM, (h + 1) * HEAD_DIM)
                k_prev = kp_ref[:, sl] if u == 0 else kc_ref[before, sl]
                v_prev = vp_ref[:, sl] if u == 0 else vc_ref[before, sl]
                work.append((rows, sl, mask_c, mask_p, v_prev,
                             _dot_nt(q_ref[rows, sl], kc_ref[rows, sl]), _dot_nt(q_ref[rows, sl], k_prev)))
        probs = []
        for rows, sl, mask_c, mask_p, _, raw_c, raw_p in work:
            s_c = jnp.where(mask_c, raw_c * SCALE, NEG_INF)
            s_p = jnp.where(mask_p, raw_p * SCALE, NEG_INF)
            m = jnp.maximum(jnp.max(s_c, axis=-1, keepdims=True), jnp.max(s_p, axis=-1, keepdims=True))
            p_c = jnp.exp(s_c - m)
            p_p = jnp.exp(s_p - m)
            l = jnp.sum(p_c, axis=-1, keepdims=True) + jnp.sum(p_p, axis=-1, keepdims=True)
            inv = 1.0 / l
            probs.append(((p_c * inv).astype(BF16), (p_p * inv).astype(BF16)))
            ld_ref[rows, sl] = jnp.broadcast_to(m + jnp.log(l), (BLOCK, HEAD_DIM))
        for (rows, sl, _, _, v_prev, _, _), (pn_c, pn_p) in zip(work, probs):
            o_ref[rows, sl] = _dot(pn_c, vc_ref[rows, sl]) + _dot(pn_p, v_prev)

    def col(which):
        return lambda c, i: (c * steps + i, cols[which])

    def col_prev(which):
        return lambda c, i: (c * nb + jnp.maximum(sub * i - 1, 0), cols[which])

    own = (sub * BLOCK, GROUP_WIDTH)
    one = (BLOCK, GROUP_WIDTH)
    return pl.pallas_call(
        body, name=name, grid=(r, steps),
        in_specs=[pl.BlockSpec(own, col(0)), pl.BlockSpec(one, col_prev(1)), pl.BlockSpec(own, col(1)),
                  pl.BlockSpec(one, col_prev(2)), pl.BlockSpec(own, col(2))],
        out_specs=[pl.BlockSpec(own, lambda c, i: (c * steps + i, 0))] * 2,
        out_shape=[jax.ShapeDtypeStruct((s, GROUP_WIDTH), F32)] * 2,
        compiler_params=_params(("parallel", "parallel")),
    )(src, src, src, src, src)


def _dil_merge(outs, lds, *, name):
    s = outs[0].shape[0]
    tr = PREP_ROWS
    rs = [DIL_GROUPS[g][1] for g in range(N_GROUPS)]

    def body(o0, o1, o2, l0, l1, l2, yb_ref, y0_ref, y1_ref, y2_ref, e0_ref, e1_ref, e2_ref, scr_a, scr_b):
        o_refs, l_refs = (o0, o1, o2), (l0, l1, l2)
        y_refs, e_refs = (y0_ref, y1_ref, y2_ref), (e0_ref, e1_ref, e2_ref)
        for h in range(HEADS_PER_GROUP):
            cols = slice(h * HEAD_DIM, (h + 1) * HEAD_DIM)
            o = [o_refs[0][:, cols]] + [_from_residues(scr_a, o_refs[g], cols, rs[g]) for g in (1, 2)]
            a = [l_refs[0][:, cols]] + [_from_residues(scr_b, l_refs[g], cols, rs[g]) for g in (1, 2)]
            m = jnp.maximum(jnp.maximum(a[0], a[1]), a[2])
            e = [jnp.exp(t - m) for t in a]
            tot = e[0] + e[1] + e[2]
            inv = 1.0 / tot
            y = (e[0] * inv) * o[0] + (e[1] * inv) * o[1] + (e[2] * inv) * o[2]
            lse = m + jnp.log(tot)
            yb_ref[:, cols] = y.astype(BF16)
            y_refs[0][:, cols] = y
            e_refs[0][:, cols] = lse
            for g in (1, 2):
                _to_residues(scr_a, y, rs[g], y_refs[g], cols, F32)
                _to_residues(scr_b, lse, rs[g], e_refs[g], cols, F32)

    blk = pl.BlockSpec((tr, GROUP_WIDTH), lambda i: (i, 0))
    specs = [blk] + [_residue_spec(rs[g], GROUP_WIDTH) for g in (1, 2)]
    shapes = [jax.ShapeDtypeStruct((s, GROUP_WIDTH), F32)] + [
        jax.ShapeDtypeStruct((rs[g], s // rs[g], GROUP_WIDTH), F32) for g in (1, 2)]
    res = pl.pallas_call(
        body, name=name, grid=(s // tr,),
        in_specs=specs * 2, out_specs=[blk] + specs * 2,
        out_shape=[jax.ShapeDtypeStruct((s, GROUP_WIDTH), BF16)] + shapes * 2,
        scratch_shapes=[pltpu.VMEM((tr, HEAD_DIM), F32), pltpu.VMEM((tr, HEAD_DIM), F32)],
        compiler_params=_params(("parallel",)),
    )(*[t.reshape(sh.shape) for t, sh in zip(outs, shapes)], *[t.reshape(sh.shape) for t, sh in zip(lds, shapes)])
    yb, ys, lses = res[0], res[1:4], res[4:7]
    return yb, [t.reshape(s, GROUP_WIDTH) for t in ys], [t.reshape(s, GROUP_WIDTH) for t in lses]


def _dil_to_residues(x, *, name):
    s = x.shape[0]
    tr = PREP_ROWS
    rs = [DIL_GROUPS[g][1] for g in (1, 2)]

    def body(x_ref, a_ref, b_ref, scr):
        for h in range(HEADS_PER_GROUP):
            cols = slice(h * HEAD_DIM, (h + 1) * HEAD_DIM)
            tile = x_ref[:, cols]
            _to_residues(scr, tile, rs[0], a_ref, cols, F32)
            _to_residues(scr, tile, rs[1], b_ref, cols, F32)

    outs = pl.pallas_call(
        body, name=name, grid=(s // tr,),
        in_specs=[pl.BlockSpec((tr, GROUP_WIDTH), lambda i: (i, 0))],
        out_specs=[_residue_spec(r, GROUP_WIDTH) for r in rs],
        out_shape=[jax.ShapeDtypeStruct((r, s // r, GROUP_WIDTH), F32) for r in rs],
        scratch_shapes=[pltpu.VMEM((tr, HEAD_DIM), F32)],
        compiler_params=_params(("parallel",)),
    )(x)
    return [t.reshape(s, GROUP_WIDTH) for t in outs]


def _dil_attn_bwd(src, g, cols, y, lse, dy, *, name):
    s = src.shape[0]
    r = DIL_GROUPS[g][1]
    nb = s // r // BLOCK
    sub = _dil_sub(nb)
    steps = nb // sub

    def body(q_ref, qn_ref, kp_ref, kc_ref, vp_ref, vc_ref, y_ref, yn_ref, l_ref, ln_ref, d_ref, dn_ref,
             dq_ref, dk_ref, dv_ref):
        i = pl.program_id(1)
        work = []
        for u in range(sub):
            rows = slice(u * BLOCK, (u + 1) * BLOCK)
            before = slice((u - 1) * BLOCK, u * BLOCK)
            after = slice((u + 1) * BLOCK, (u + 2) * BLOCK)
            last = u == sub - 1
            mask_c, mask_p = _dil_masks(i if u == 0 else 1)
            _, mask_n = _dil_masks(jnp.where(sub * i + u + 1 < nb, 1, 0) if last else 1)
            for h in range(HEADS_PER_GROUP):
                sl = slice(h * HEAD_DIM, (h + 1) * HEAD_DIM)
                nxt = lambda own_ref, next_ref, sl=sl: next_ref[:, sl] if last else own_ref[after, sl]
                work.append(dict(
                    rows=rows, sl=sl, masks=(mask_c, mask_p, mask_n),
                    q=q_ref[rows, sl], qn=nxt(q_ref, qn_ref), kc=kc_ref[rows, sl], vc=vc_ref[rows, sl],
                    kp=kp_ref[:, sl] if u == 0 else kc_ref[before, sl],
                    vp=vp_ref[:, sl] if u == 0 else vc_ref[before, sl],
                    dy=d_ref[rows, sl], dyn=nxt(d_ref, dn_ref), y=y_ref[rows, sl], yn=nxt(y_ref, yn_ref),
                    lse=l_ref[rows, sl], lsen=nxt(l_ref, ln_ref)))
        for t in work:
            t["dyb"], t["dynb"] = t["dy"].astype(BF16), t["dyn"].astype(BF16)
            t["first"] = (_dot_nt(t["q"], t["kc"]), _dot_nt(t["q"], t["kp"]), _dot_nt(t["qn"], t["kc"]),
                          _dot_nt(t["dyb"], t["vc"]), _dot_nt(t["dyb"], t["vp"]), _dot_nt(t["dynb"], t["vc"]))
        for t in work:
            s_c, s_p, s_n, dp_c, dp_p, dp_n = t["first"]
            mask_c, mask_p, mask_n = t["masks"]
            delta = jnp.sum(t["dy"] * t["y"], axis=-1, keepdims=True)
            delta_n = jnp.sum(t["dyn"] * t["yn"], axis=-1, keepdims=True)
            p_c = jnp.where(mask_c, jnp.exp(s_c * SCALE - t["lse"]), 0.0)
            p_p = jnp.where(mask_p, jnp.exp(s_p * SCALE - t["lse"]), 0.0)
            p_n = jnp.where(mask_n, jnp.exp(s_n * SCALE - t["lsen"]), 0.0)
            t["second"] = ((p_c * (dp_c - delta) * SCALE).astype(BF16), (p_p * (dp_p - delta) * SCALE).astype(BF16),
                           (p_n * (dp_n - delta_n) * SCALE).astype(BF16), p_c.astype(BF16), p_n.astype(BF16))
        for t in work:
            ds_c, ds_p, ds_n, pb_c, pb_n = t["second"]
            rows, sl = t["rows"], t["sl"]
            dq_ref[rows, sl] = _dot(ds_c, t["kc"]) + _dot(ds_p, t["kp"])
            dk_ref[rows, sl] = _dot_tn(ds_c, t["q"]) + _dot_tn(ds_n, t["qn"])
            dv_ref[rows, sl] = _dot_tn(pb_c, t["dyb"]) + _dot_tn(pb_n, t["dynb"])

    def col(which):
        return lambda c, i: (c * steps + i, cols[which])

    def col_prev(which):
        return lambda c, i: (c * nb + jnp.maximum(sub * i - 1, 0), cols[which])

    def after_rows(c, i):
        return c * nb + jnp.minimum(sub * i + sub, nb - 1)

    rows_own = (sub * BLOCK, GROUP_WIDTH)
    one = (BLOCK, GROUP_WIDTH)
    own = pl.BlockSpec(rows_own, lambda c, i: (c * steps + i, 0))
    nxt = pl.BlockSpec(one, lambda c, i: (after_rows(c, i), 0))
    return pl.pallas_call(
        body, name=name, grid=(r, steps),
        in_specs=[pl.BlockSpec(rows_own, col(0)), pl.BlockSpec(one, lambda c, i: (after_rows(c, i), cols[0])),
                  pl.BlockSpec(one, col_prev(1)), pl.BlockSpec(rows_own, col(1)),
                  pl.BlockSpec(one, col_prev(2)), pl.BlockSpec(rows_own, col(2)),
                  own, nxt, own, nxt, own, nxt],
        out_specs=[own] * 3,
        out_shape=[jax.ShapeDtypeStruct((s, GROUP_WIDTH), F32)] * 3,
        compiler_params=_params(("parallel", "parallel")),
    )(src, src, src, src, src, src, y, y, lse, lse, dy, dy)


SB_TQ = 256
SB_GROUP = 2
SB_DEAD = -104.0
SB_NH_FWD = 4
SB_NH_BWD = 2


def _split_dot(x, w):
    hi = x.astype(BF16)
    lo = (x - hi.astype(F32)).astype(BF16)
    return _dot(hi, w) + _dot(lo, w)


def _sb_consts():
    j = lax.broadcasted_iota(jnp.int32, (BLOCK, BLOCK), 0)
    k = lax.broadcasted_iota(jnp.int32, (BLOCK, BLOCK), 1)
    ones = jnp.ones((BLOCK, BLOCK), F32)
    after = jnp.concatenate([(j > k).astype(F32), ones], axis=1).astype(BF16)
    before = jnp.concatenate([(j < k).astype(F32), ones], axis=1).astype(BF16)
    return after, before


def _sb_mask(row0, col0, tq):
    row = row0 + lax.broadcasted_iota(jnp.int32, (tq, BLOCK), 0)
    col = col0 + lax.broadcasted_iota(jnp.int32, (tq, BLOCK), 1)
    return col < row


def _softplus(z):
    return jnp.maximum(z, 0.0) + jnp.log1p(jnp.exp(-jnp.abs(z)))


def _sb_attn_fwd(qkv, *, name):
    s = qkv.shape[0]
    tq = _pick(s, SB_TQ)
    nq = s // tq
    per = tq // BLOCK
    base = 3 * DIL_WIDTH // HEAD_DIM
    after, _ = _sb_consts()

    nh = SB_NH_FWD
    wide = nh * HEAD_DIM

    def body(q_ref, k_ref, v_ref, w_ref, o_ref, rc_ref, trips_ref, acc_ref, run_ref):
        qi = pl.program_id(1)
        w = w_ref[...]
        acc_ref[...] = jnp.zeros_like(acc_ref)
        run_ref[...] = jnp.zeros_like(run_ref)
        rc_ref[...] = jnp.zeros_like(rc_ref)
        lane = lax.broadcasted_iota(jnp.int32, (tq, BLOCK), 1)
        n_chunks = (qi + 1) * per

        def chunks(js, masked):
            sls = [slice(h * HEAD_DIM, (h + 1) * HEAD_DIM) for h in range(nh)]
            work = [(sl, j, pl.multiple_of(j * BLOCK, BLOCK), _sb_mask(qi * tq, j * BLOCK, tq) if masked else None)
                    for j in js for sl in sls]
            zs = [_dot_nt(q_ref[:, sl], k_ref[pl.ds(start, BLOCK), sl]) * SCALE for sl, _, start, _ in work]
            logits, sums = [], []
            for z, (_, _, _, mask) in zip(zs, work):
                sp = _softplus(z)
                logits.append(z - sp)
                sums.append(_split_dot(jnp.where(mask, -sp, 0.0) if masked else -sp, w))
            for (sl, j, start, mask), lg, sm in zip(work, logits, sums):
                run = run_ref[:, sl]
                a = jnp.exp(lg + run + sm[:, :BLOCK])
                if masked:
                    a = jnp.where(mask, a, 0.0)
                acc_ref[:, sl] += _dot(a.astype(BF16), v_ref[pl.ds(start, BLOCK), sl])
                rc_ref[:, sl] = jnp.where(lane == j, run, rc_ref[:, sl])
                run_ref[:, sl] = run + sm[:, BLOCK:]

        def diagonal(t, carry):
            last = n_chunks - 1 - SB_GROUP * t
            chunks([last - g for g in range(SB_GROUP)], True)
            return carry

        def alive():
            return (jnp.max(run_ref[...]) > SB_DEAD).astype(jnp.int32)

        def more(carry):
            t, live = carry
            return jnp.logical_and(t < n_left, live > 0)

        def left(carry):
            t, _ = carry
            last = n_chunks - 1 - per - SB_GROUP * t
            chunks([last - g for g in range(SB_GROUP)], False)
            return t + 1, alive()

        lax.fori_loop(0, per // SB_GROUP, diagonal, 0)
        n_left = (n_chunks - per) // SB_GROUP
        trips, _ = lax.while_loop(more, left, (jnp.int32(0), alive()))
        trips_ref[pl.program_id(0), qi] = trips
        o_ref[...] = acc_ref[...].astype(o_ref.dtype)

    head = lambda off: (lambda h, i: (0, (base + off) // nh + h))
    once = pl.Buffered(1)
    return pl.pallas_call(
        body, name=name, grid=(SB_HEADS // nh, nq),
        in_specs=[pl.BlockSpec((tq, wide), lambda h, i: (i, base // nh + h)),
                  pl.BlockSpec((s, wide), head(SB_HEADS), pipeline_mode=once),
                  pl.BlockSpec((s, wide), head(2 * SB_HEADS), pipeline_mode=once),
                  pl.BlockSpec((BLOCK, 2 * BLOCK), lambda h, i: (0, 0))],
        out_specs=[pl.BlockSpec((tq, wide), lambda h, i: (i, h))] * 2 + [pl.BlockSpec(memory_space=pltpu.SMEM)],
        out_shape=[jax.ShapeDtypeStruct((s, SB_WIDTH), BF16), jax.ShapeDtypeStruct((s, SB_WIDTH), F32),
                   jax.ShapeDtypeStruct((SB_HEADS // nh, nq), jnp.int32)],
        scratch_shapes=[pltpu.VMEM((tq, wide), F32), pltpu.VMEM((tq, wide), F32)],
        compiler_params=_params(("arbitrary", "arbitrary")),
    )(qkv, qkv, qkv, after)


def _sb_attn_bwd(qkv, rc, trips, do, *, name):
    s = qkv.shape[0]
    tq = _pick(s, SB_TQ)
    nq = s // tq
    per = tq // BLOCK
    base = 3 * DIL_WIDTH // HEAD_DIM
    _, before = _sb_consts()

    nh = SB_NH_BWD
    wide = nh * HEAD_DIM

    def body(trips_ref, q_ref, k_ref, v_ref, w_ref, rc_ref, do_ref, dq_ref, dk_ref, dv_ref, acc_ref, pre_ref, dob_ref):
        qi = pl.program_id(1)

        @pl.when(qi == 0)
        def _():
            dk_ref[...] = jnp.zeros_like(dk_ref)
            dv_ref[...] = jnp.zeros_like(dv_ref)

        w = w_ref[...]
        wa = (lax.broadcasted_iota(jnp.int32, (BLOCK, BLOCK), 0)
              > lax.broadcasted_iota(jnp.int32, (BLOCK, BLOCK), 1)).astype(BF16)
        dob_ref[...] = do_ref[...].astype(BF16)
        acc_ref[...] = jnp.zeros_like(acc_ref)
        pre_ref[...] = jnp.zeros_like(pre_ref)
        lane = lax.broadcasted_iota(jnp.int32, (tq, BLOCK), 1)
        n_chunks = (qi + 1) * per

        def chunks(js, masked):
            sls = [slice(h * HEAD_DIM, (h + 1) * HEAD_DIM) for h in range(nh)]
            work = [(sl, j, pl.multiple_of(j * BLOCK, BLOCK), _sb_mask(qi * tq, j * BLOCK, tq) if masked else None)
                    for j in js for sl in sls]
            zs = [_dot_nt(q_ref[:, sl], k_ref[pl.ds(start, BLOCK), sl]) * SCALE for sl, _, start, _ in work]
            das = [_dot_nt(dob_ref[:, sl], v_ref[pl.ds(start, BLOCK), sl]) for sl, _, start, _ in work]
            sigs, betweens = [], []
            for z, (_, _, _, mask) in zip(zs, work):
                sp = _softplus(z)
                sigs.append(jnp.exp(z - sp))
                betweens.append(_split_dot(jnp.where(mask, -sp, 0.0) if masked else -sp, wa))
            es, avs, sums = [], [], []
            for (sl, j, _, mask), sig, bt, da in zip(work, sigs, betweens, das):
                run = jnp.sum(jnp.where(lane == j, rc_ref[:, sl], 0.0), axis=-1, keepdims=True)
                a = sig * jnp.exp(run + bt)
                if masked:
                    a = jnp.where(mask, a, 0.0)
                e = a * da
                es.append(e)
                avs.append(a.astype(BF16))
                sums.append(_split_dot(e, w))
            for (sl, _, start, mask), sig, e, ab, sm in zip(work, sigs, es, avs, sums):
                pre = pre_ref[:, sl]
                dz = (e * (1.0 - sig) - sig * (pre + sm[:, :BLOCK])) * SCALE
                if masked:
                    dz = jnp.where(mask, dz, 0.0)
                dz = dz.astype(BF16)
                pre_ref[:, sl] = pre + sm[:, BLOCK:]
                acc_ref[:, sl] += _dot(dz, k_ref[pl.ds(start, BLOCK), sl])
                dk_ref[pl.ds(start, BLOCK), sl] += _dot_tn(dz, q_ref[:, sl])
                dv_ref[pl.ds(start, BLOCK), sl] += _dot_tn(ab, dob_ref[:, sl])

        def left(t, carry):
            chunks([SB_GROUP * t + g for g in range(SB_GROUP)], False)
            return carry

        def diagonal(t, carry):
            chunks([n_chunks - per + SB_GROUP * t + g for g in range(SB_GROUP)], True)
            return carry

        n_left = (n_chunks - per) // SB_GROUP
        ran = trips_ref[pl.program_id(0) // (SB_NH_FWD // nh), qi]
        lax.fori_loop(n_left - ran, n_left, left, 0)
        lax.fori_loop(0, per // SB_GROUP, diagonal, 0)
        dq_ref[...] = acc_ref[...]

    head = lambda off: (lambda h, i, t: (0, (base + off) // nh + h))
    blk = pl.BlockSpec((tq, wide), lambda h, i, t: (i, h))
    whole = pl.BlockSpec((s, wide), lambda h, i, t: (0, h))
    once = pl.Buffered(1)
    grid_spec = pltpu.PrefetchScalarGridSpec(
        num_scalar_prefetch=1, grid=(SB_HEADS // nh, nq),
        in_specs=[pl.BlockSpec((tq, wide), lambda h, i, t: (i, base // nh + h)),
                  pl.BlockSpec((s, wide), head(SB_HEADS), pipeline_mode=once),
                  pl.BlockSpec((s, wide), head(2 * SB_HEADS), pipeline_mode=once),
                  pl.BlockSpec((BLOCK, 2 * BLOCK), lambda h, i, t: (0, 0)),
                  blk, blk],
        out_specs=[blk, whole, whole],
        scratch_shapes=[pltpu.VMEM((tq, wide), F32), pltpu.VMEM((tq, wide), F32), pltpu.VMEM((tq, wide), BF16)],
    )
    return pl.pallas_call(
        body, name=name, grid_spec=grid_spec,
        out_shape=[jax.ShapeDtypeStruct((s, SB_WIDTH), F32)] * 3,
        compiler_params=_params(("parallel", "arbitrary")),
    )(trips, qkv, qkv, qkv, before, rc, do)


GATE_TC = 512


def _gate_fwd(proj, gate_b, up_dil, up_sb, *, name):
    s, d = up_dil.shape
    tr = _pick(s, 512)
    g0 = QKV_WIDTH // GATE_TC
    nc = d // GATE_TC

    def body(ga_ref, gb_ref, b_ref, ud_ref, us_ref, o_ref):
        ga = jax.nn.sigmoid(ga_ref[...] + b_ref[0:1, :])
        gb = jax.nn.sigmoid(gb_ref[...] + b_ref[1:2, :])
        o_ref[...] = (ga * ud_ref[...] + gb * us_ref[...]).astype(o_ref.dtype)

    blk = pl.BlockSpec((tr, GATE_TC), lambda i, j: (i, j))
    return pl.pallas_call(
        body, name=name, grid=(s // tr, nc),
        in_specs=[pl.BlockSpec((tr, GATE_TC), lambda i, j: (i, g0 + j)),
                  pl.BlockSpec((tr, GATE_TC), lambda i, j: (i, g0 + nc + j)),
                  pl.BlockSpec((2, GATE_TC), lambda i, j: (0, j)), blk, blk],
        out_specs=blk,
        out_shape=jax.ShapeDtypeStruct((s, d), BF16),
        compiler_params=_params(("parallel", "parallel")),
    )(proj, proj, gate_b, up_dil, up_sb)


def _gate_bwd(proj, gate_b, up_dil, up_sb, dmixed, *, name):
    s, d = up_dil.shape
    tr = _pick(s, 512)
    g0 = QKV_WIDTH // GATE_TC
    nc = d // GATE_TC

    def body(ga_ref, gb_ref, b_ref, ud_ref, us_ref, dm_ref, dud_ref, dus_ref, dga_ref, dgb_ref, db_ref):
        ga = jax.nn.sigmoid(ga_ref[...] + b_ref[0:1, :])
        gb = jax.nn.sigmoid(gb_ref[...] + b_ref[1:2, :])
        dm = dm_ref[...]
        dud_ref[...] = (dm * ga).astype(BF16)
        dus_ref[...] = (dm * gb).astype(BF16)
        dla = dm * ud_ref[...] * ga * (1.0 - ga)
        dlb = dm * us_ref[...] * gb * (1.0 - gb)
        dga_ref[...] = dla.astype(BF16)
        dgb_ref[...] = dlb.astype(BF16)

        @pl.when(pl.program_id(1) == 0)
        def _():
            db_ref[...] = jnp.zeros_like(db_ref)

        db_ref[0:1, :] += jnp.sum(dla, axis=0, keepdims=True)
        db_ref[1:2, :] += jnp.sum(dlb, axis=0, keepdims=True)

    blk = pl.BlockSpec((tr, GATE_TC), lambda j, i: (i, j))
    dud, dus, dga, dgb, db = pl.pallas_call(
        body, name=name, grid=(nc, s // tr),
        in_specs=[pl.BlockSpec((tr, GATE_TC), lambda j, i: (i, g0 + j)),
                  pl.BlockSpec((tr, GATE_TC), lambda j, i: (i, g0 + nc + j)),
                  pl.BlockSpec((2, GATE_TC), lambda j, i: (0, j)), blk, blk, blk],
        out_specs=[blk, blk, blk, blk, pl.BlockSpec((2, GATE_TC), lambda j, i: (0, j))],
        out_shape=[jax.ShapeDtypeStruct((s, d), BF16)] * 4 + [jax.ShapeDtypeStruct((2, d), F32)],
        compiler_params=_params(("parallel", "arbitrary")),
    )(proj, proj, gate_b, up_dil, up_sb, dmixed)
    return dud, dus, (dga, dgb), db


def _adamw(w, g, m, v, *, name):
    shape = w.shape
    cols = shape[-1]
    rows = w.size // cols
    tr = _pick(rows, max(8, (512 * 1024) // cols))
    if rows % tr or (tr % 8 and tr != rows):
        tr = rows
    c1 = 1.0 - ADAM_B1 ** ADAM_STEP
    c2 = 1.0 - ADAM_B2 ** ADAM_STEP

    def body(w_ref, g_ref, m_ref, v_ref, d_ref, nm_ref, nv_ref):
        gv = g_ref[...]
        nm = ADAM_B1 * m_ref[...] + (1.0 - ADAM_B1) * gv
        nv = ADAM_B2 * v_ref[...] + (1.0 - ADAM_B2) * (gv * gv)
        d_ref[...] = -ADAM_LR * ((nm / c1) / (jnp.sqrt(nv / c2) + ADAM_EPS) + ADAM_WD * w_ref[...])
        nm_ref[...] = nm
        nv_ref[...] = nv

    blk = pl.BlockSpec((tr, cols), lambda i: (i, 0))
    outs = pl.pallas_call(
        body, name=name, grid=(rows // tr,),
        in_specs=[blk] * 4, out_specs=[blk] * 3,
        out_shape=[jax.ShapeDtypeStruct((rows, cols), F32)] * 3,
        compiler_params=_params(("parallel",)),
    )(*(t.reshape(rows, cols) for t in (w, g, m, v)))
    return tuple(t.reshape(shape) for t in outs)


def _adamw_layers(w, grads, m, v, *, name):
    depth, r, cols = w.shape
    tr = _pick(r, max(8, (512 * 1024) // cols))
    per = r // tr
    c1 = 1.0 - ADAM_B1 ** ADAM_STEP
    c2 = 1.0 - ADAM_B2 ** ADAM_STEP

    def body(w_ref, m_ref, v_ref, *rest):
        g_refs, (g_ref, d_ref, nm_ref, nv_ref) = rest[:depth], rest[depth:]
        layer = pl.program_id(0) // per
        gv = g_refs[0][...]
        for l in range(1, depth):
            gv = jnp.where(layer == l, g_refs[l][...], gv)
        nm = ADAM_B1 * m_ref[...] + (1.0 - ADAM_B1) * gv
        nv = ADAM_B2 * v_ref[...] + (1.0 - ADAM_B2) * (gv * gv)
        g_ref[...] = gv
        d_ref[...] = -ADAM_LR * ((nm / c1) / (jnp.sqrt(nv / c2) + ADAM_EPS) + ADAM_WD * w_ref[...])
        nm_ref[...] = nm
        nv_ref[...] = nv

    def of_layer(l):
        return pl.BlockSpec((tr, cols), lambda i: (jnp.clip(i - l * per, 0, per - 1), 0))

    blk = pl.BlockSpec((tr, cols), lambda i: (i, 0))
    outs = pl.pallas_call(
        body, name=name, grid=(depth * per,),
        in_specs=[blk] * 3 + [of_layer(l) for l in range(depth)], out_specs=[blk] * 4,
        out_shape=[jax.ShapeDtypeStruct((depth * r, cols), F32)] * 4,
        compiler_params=_params(("arbitrary",)),
    )(*(t.reshape(depth * r, cols) for t in (w, m, v)), *grads)
    return tuple(t.reshape(w.shape) for t in outs)


def _add_half(full, other, which, *, name):
    n, rh, c = other.shape
    tr = _pick(rh, max(8, (1024 * 1024) // c))
    nb = rh // tr

    def body(which_ref, a_ref, b_ref, o_ref):
        o_ref[...] = (a_ref[...] + b_ref[...]).astype(o_ref.dtype)

    grid_spec = pltpu.PrefetchScalarGridSpec(
        num_scalar_prefetch=1, grid=(n, nb),
        in_specs=[pl.BlockSpec((None, tr, c), lambda s, i, w: (s, w[0] * nb + i, 0)),
                  pl.BlockSpec((None, tr, c), lambda s, i, w: (s, i, 0))],
        out_specs=pl.BlockSpec((None, tr, c), lambda s, i, w: (s, i, 0)),
    )
    return pl.pallas_call(
        body, name=name, grid_spec=grid_spec,
        out_shape=jax.ShapeDtypeStruct((n, rh, c), BF16),
        compiler_params=_params(("parallel", "parallel")),
    )(which, full, other)


def _sum_chips(own, parts, chip, *, name):
    n, r, c = parts.shape
    tr = _pick(r, max(16, (1024 * 1024) // c))

    def body(chip_ref, own_ref, p1_ref, p2_ref, p3_ref, o_ref):
        acc = own_ref[...].astype(F32)
        for p in (p1_ref, p2_ref, p3_ref):
            acc = acc + p[...].astype(F32)
        o_ref[...] = acc

    def slot(k):
        return pl.BlockSpec((None, tr, c), lambda i, o: (o[0] ^ k, i, 0))

    grid_spec = pltpu.PrefetchScalarGridSpec(
        num_scalar_prefetch=1, grid=(r // tr,),
        in_specs=[slot(0), slot(1), slot(2), slot(3)],
        out_specs=pl.BlockSpec((tr, c), lambda i, o: (i, 0)),
    )
    return pl.pallas_call(
        body, name=name, grid_spec=grid_spec,
        out_shape=jax.ShapeDtypeStruct((r, c), F32),
        compiler_params=_params(("parallel",)),
    )(chip, own, parts, parts, parts)


def _place_piece(full, piece, index, *, name):
    a, n, r, c = full.shape
    tr = _pick(r, max(16, (1024 * 1024) // c))

    def body(index_ref, piece_ref, full_ref, o_ref):
        o_ref[...] = piece_ref[...]

    grid_spec = pltpu.PrefetchScalarGridSpec(
        num_scalar_prefetch=1, grid=(a, r // tr),
        in_specs=[pl.BlockSpec((None, tr, c), lambda l, i, k: (l, i, 0)), ANY],
        out_specs=pl.BlockSpec((None, None, tr, c), lambda l, i, k: (l, k[0], i, 0)),
    )
    return pl.pallas_call(
        body, name=name, grid_spec=grid_spec,
        out_shape=jax.ShapeDtypeStruct(full.shape, full.dtype),
        input_output_aliases={2: 0},
        compiler_params=_params(("parallel", "parallel")),
    )(index, piece, full)


ANY = pl.BlockSpec(memory_space=pl.ANY)


def _place():
    x, y, c = lax.axis_index("x"), lax.axis_index("y"), lax.axis_index("c")
    chips = [(1 - x, y), (x, 1 - y), (1 - x, 1 - y)]
    return x, y, c, chips


def _remote(src, dst, send_sem, recv_sem, to):
    return pltpu.make_async_remote_copy(src_ref=src, dst_ref=dst, send_sem=send_sem, recv_sem=recv_sem,
                                        device_id=to, device_id_type=MESH)


def _half(ref, chip, which):
    rh = ref.shape[1] // 2
    return ref.at[chip, pl.ds(which * rh, rh), :]


def _gather_ici_copies(layer):
    def copies(ins, outs, send_sems, recv_sems):
        x, y, c, chips = _place()
        me = 2 * x + y
        sends, recvs = [], []
        for t in range(len(ins)):
            rh = ins[t].shape[1] // 2
            for k, (px, py) in enumerate(chips):
                sems = (send_sems.at[3 * t + k], recv_sems.at[3 * t + k])
                sends.append(_remote(ins[t].at[layer, pl.ds(c * rh, rh), :], _half(outs[t], me, c), *sems, (px, py, c)))
                landed = _half(outs[t], 2 * px + py, c)
                recvs.append(_remote(landed, landed, *sems, (px, py, c)))
        return sends, recvs
    return copies


def _gather_pass_copies(ins, outs, send_sems, recv_sems):
    x, y, c, chips = _place()
    sibling = (x, y, 1 - c)
    sends, recvs = [], []
    for t in range(len(ins)):
        for k, (px, py) in enumerate(chips):
            sems = (send_sems.at[3 * t + k], recv_sems.at[3 * t + k])
            sends.append(_remote(_half(ins[t], 2 * px + py, c), _half(outs[t], 2 * px + py, c), *sems, sibling))
            other = _half(outs[t], 2 * px + py, 1 - c)
            recvs.append(_remote(other, other, *sems, sibling))
    return sends, recvs


def _gathered_shapes(shards):
    return [jax.ShapeDtypeStruct((N_CHIPS,) + w.shape[1:], w.dtype) for w in shards]


def _gather_ici_side(shards, layer):
    return (tuple(shards), _gathered_shapes(shards), _gather_ici_copies(layer), 3 * len(shards))


def _gather_layer(shards, layer, *, name):
    n = len(shards)
    ici = _gather_ici_copies(layer)

    def body(*refs):
        ins, outs = refs[:n], refs[n:2 * n]
        send_sems, recv_sems, pass_send, pass_recv = refs[2 * n:]
        sends, recvs = ici(ins, outs, send_sems, recv_sems)
        passes, arrivals = _gather_pass_copies(outs, outs, pass_send, pass_recv)
        for cp in sends:
            cp.start()
        for landed, onward in zip(recvs, passes):
            landed.wait_recv()
            onward.start()
        for cp in arrivals:
            cp.wait_recv()
        for cp in sends + passes:
            cp.wait_send()

    return pl.pallas_call(
        body, name=name,
        in_specs=[ANY] * n, out_specs=[ANY] * n,
        out_shape=_gathered_shapes(shards),
        scratch_shapes=[pltpu.SemaphoreType.DMA((3 * n,)) for _ in range(4)],
        compiler_params=pltpu.CompilerParams(has_side_effects=True),
    )(*shards)


def _gather_pass(landed, *, name):
    n = len(landed)

    def body(*refs):
        ins, outs = refs[:n], refs[n:2 * n]
        send_sems, recv_sems = refs[2 * n:]
        passes, arrivals = _gather_pass_copies(ins, outs, send_sems, recv_sems)
        for cp in passes:
            cp.start()
        for cp in arrivals:
            cp.wait_recv()
        for cp in passes:
            cp.wait_send()

    return pl.pallas_call(
        body, name=name,
        in_specs=[ANY] * n, out_specs=[ANY] * n,
        out_shape=[jax.ShapeDtypeStruct(w.shape, w.dtype) for w in landed],
        input_output_aliases={t: t for t in range(n)},
        scratch_shapes=[pltpu.SemaphoreType.DMA((3 * n,)), pltpu.SemaphoreType.DMA((3 * n,))],
        compiler_params=pltpu.CompilerParams(has_side_effects=True),
    )(*landed)


def _pair_copies(ins, outs, send_sems, recv_sems):
    x, y, c, _ = _place()
    sibling = (x, y, 1 - c)
    sends, recvs = [], []
    for t in range(len(ins)):
        rh = ins[t].shape[1] // 2
        sems = (send_sems.at[t], recv_sems.at[t])
        sends.append(_remote(ins[t].at[:, pl.ds((1 - c) * rh, rh), :], outs[t], *sems, sibling))
        recvs.append(_remote(outs[t], outs[t], *sems, sibling))
    return sends, recvs


def _pair_shapes(grads):
    return [jax.ShapeDtypeStruct((g.shape[0], g.shape[1] // 2, g.shape[2]), g.dtype) for g in grads]


def _pair_side(grads):
    return (tuple(grads), _pair_shapes(grads), _pair_copies, len(grads))


def _pair_swap_halves(grads, *, name):
    n = len(grads)

    def body(*refs):
        sends, recvs = _pair_copies(refs[:n], refs[n:2 * n], *refs[2 * n:])
        for cp in sends:
            cp.start()
        for cp in recvs:
            cp.wait_recv()
        for cp in sends:
            cp.wait_send()

    return pl.pallas_call(
        body, name=name,
        in_specs=[ANY] * n, out_specs=[ANY] * n,
        out_shape=_pair_shapes(grads),
        scratch_shapes=[pltpu.SemaphoreType.DMA((n,)), pltpu.SemaphoreType.DMA((n,))],
        compiler_params=pltpu.CompilerParams(has_side_effects=True),
    )(*grads)


def _chip_copies(ins, outs, send_sems, recv_sems):
    x, y, c, chips = _place()
    me = 2 * x + y
    sends, recvs = [], []
    for t in range(len(ins)):
        for k, (px, py) in enumerate(chips):
            sems = (send_sems.at[3 * t + k], recv_sems.at[3 * t + k])
            sends.append(_remote(ins[t].at[2 * px + py], outs[t].at[me], *sems, (px, py, c)))
            theirs = outs[t].at[2 * px + py]
            recvs.append(_remote(theirs, theirs, *sems, (px, py, c)))
    return sends, recvs


def _chip_side(parts):
    return (tuple(parts), [jax.ShapeDtypeStruct(p.shape, p.dtype) for p in parts], _chip_copies, 3 * len(parts))


def _chip_exchange(parts, *, name):
    n = len(parts)

    def body(*refs):
        sends, recvs = _chip_copies(refs[:n], refs[n:2 * n], *refs[2 * n:])
        for cp in sends:
            cp.start()
        for cp in recvs:
            cp.wait_recv()
        for cp in sends:
            cp.wait_send()

    return pl.pallas_call(
        body, name=name,
        in_specs=[ANY] * n, out_specs=[ANY] * n,
        out_shape=[jax.ShapeDtypeStruct(p.shape, p.dtype) for p in parts],
        scratch_shapes=[pltpu.SemaphoreType.DMA((3 * n,)), pltpu.SemaphoreType.DMA((3 * n,))],
        compiler_params=pltpu.CompilerParams(has_side_effects=True),
    )(*parts)


def _pair_join_halves(halves, *, name):
    n = len(halves)

    def body(*refs):
        ins, outs = refs[:n], refs[n:2 * n]
        send_sems, recv_sems = refs[2 * n:]
        x, y, c, _ = _place()
        sibling = (x, y, 1 - c)
        cps = []
        for t in range(n):
            rh = ins[t].shape[0]
            cp = _remote(ins[t], outs[t].at[pl.ds(c * rh, rh), :], send_sems.at[t], recv_sems.at[t], sibling)
            cp.start()
            cps.append(cp)
        for t, cp in enumerate(cps):
            rh = ins[t].shape[0]
            theirs = outs[t].at[pl.ds((1 - c) * rh, rh), :]
            _remote(theirs, theirs, send_sems.at[t], recv_sems.at[t], sibling).wait_recv()
            cp.wait_send()

    return pl.pallas_call(
        body, name=name,
        in_specs=[ANY] * n, out_specs=[ANY] * n,
        out_shape=[jax.ShapeDtypeStruct((2 * h.shape[0], h.shape[1]), h.dtype) for h in halves],
        scratch_shapes=[pltpu.SemaphoreType.DMA((n,)), pltpu.SemaphoreType.DMA((n,))],
        compiler_params=pltpu.CompilerParams(has_side_effects=True),
    )(*halves)


def _exchange_small(buf, *, name):
    rows, lanes = buf.shape

    def body(in_ref, out_ref, send_sems, recv_sems, local_sem):
        x, y, c = lax.axis_index("x"), lax.axis_index("y"), lax.axis_index("c")
        me = 4 * x + 2 * y + c
        lc = pltpu.make_async_copy(in_ref, out_ref.at[me], local_sem)
        lc.start()
        cps = []
        for k in range(1, 8):
            fx, fy, fc = (k >> 2) & 1, (k >> 1) & 1, k & 1
            to = (x ^ fx, y ^ fy, c ^ fc)
            cp = _remote(in_ref, out_ref.at[me], send_sems.at[k - 1], recv_sems.at[k - 1], to)
            cp.start()
            cps.append(cp)
        for k in range(1, 8):
            fx, fy, fc = (k >> 2) & 1, (k >> 1) & 1, k & 1
            theirs = out_ref.at[4 * (x ^ fx) + 2 * (y ^ fy) + (c ^ fc)]
            _remote(theirs, theirs, send_sems.at[k - 1], recv_sems.at[k - 1], (x ^ fx, y ^ fy, c ^ fc)).wait_recv()
        for cp in cps:
            cp.wait_send()
        lc.wait()

    return pl.pallas_call(
        body, name=name,
        in_specs=[ANY], out_specs=ANY,
        out_shape=jax.ShapeDtypeStruct((8, rows, lanes), buf.dtype),
        scratch_shapes=[pltpu.SemaphoreType.DMA((7,)), pltpu.SemaphoreType.DMA((7,)), pltpu.SemaphoreType.DMA],
        compiler_params=pltpu.CompilerParams(has_side_effects=True),
    )(buf)


def _sum_slots(slots, *, name):
    n, rows, lanes = slots.shape

    def body(s_ref, o_ref):
        acc = s_ref[0]
        for t in range(1, n):
            acc = acc + s_ref[t]
        o_ref[...] = acc

    return pl.pallas_call(
        body, name=name,
        out_shape=jax.ShapeDtypeStruct((rows, lanes), slots.dtype),
    )(slots)


def _relu2(acc):
    r = jnp.maximum(acc, 0.0)
    return acc, r * r


def _with_side(result, side):
    return result if side else (result, ())


def _layer_fwd(x, w, cos, sin, tag, sides=None, rest=None):
    sides = sides or {}
    got = {}
    h = _rmsnorm_fwd(x, w["norm1_g"], name=f"norm1_{tag}")
    proj, got["proj"] = _with_side(_matmul(h, w["w_in"], mode="nn", b_sharded=True, tm=512, side=sides.get("proj"),
                                           name=f"proj_{tag}"), sides.get("proj"))
    if rest:
        w = {**w, **rest(got["proj"])}
    qkv, res1, res2 = _qkv_prep(proj, w["q_norm_g"], w["k_norm_g"], cos, sin, name=f"qkv_prep_{tag}")
    dil_src = [(qkv, (0, 3, 6))] + [(t.reshape(-1, 3 * GROUP_WIDTH), (0, 1, 2)) for t in (res1, res2)]
    outs, lds = [], []
    for g, (src, cols) in enumerate(dil_src):
        o_g, ld_g = _dil_attn_fwd(src, g, cols, name=f"dil{g}_fwd_{tag}")
        outs.append(o_g)
        lds.append(ld_g)
    yb, y, lse = _dil_merge(outs, lds, name=f"dil_merge_{tag}")
    o_sb, rc, trips = _sb_attn_fwd(qkv, name=f"sb_fwd_{tag}")
    up_dil = _matmul(yb, w["w_up_dil"], mode="nn", b_sharded=True, name=f"up_dil_{tag}")
    up_sb = _matmul(o_sb, w["w_up_sb"], mode="nn", b_sharded=True, name=f"up_sb_{tag}")
    mixed = _gate_fwd(proj, w["gate_b"], up_dil, up_sb, name=f"gate_{tag}")
    x1, got["out"] = _with_side(_matmul(mixed, w["w_out"], mode="nn", extras=(x,), epilogue=lambda acc, res: (acc + res,),
                                        side=sides.get("out"), name=f"out_{tag}"), sides.get("out"))
    h2 = _rmsnorm_fwd(x1, w["norm2_g"], name=f"norm2_{tag}")
    (u, a), got["ff1"] = _with_side(_matmul(h2, w["w_ff1"], mode="nn", b_sharded=True, out_dtypes=(F32, BF16),
                                            epilogue=_relu2, side=sides.get("ff1"), name=f"ff1_{tag}"), sides.get("ff1"))
    x2, got["ff2"] = _with_side(_matmul(a, w["w_ff2"], mode="nn", extras=(x1,), epilogue=lambda acc, res: (acc + res,),
                                        side=sides.get("ff2"), name=f"ff2_{tag}"), sides.get("ff2"))
    saved = dict(x=x, h=h, proj=proj, qkv=qkv, dil_src=dil_src, y=y, yb=yb, lse=lse, o_sb=o_sb, rc=rc, trips=trips, up_dil=up_dil, up_sb=up_sb,
                 mixed=mixed, x1=x1, h2=h2, u=u, a=a, w=w)
    return x2, saved, got


def _layer_bwd(dx, dxb, sv, cos, sin, tag, sides=None, late=None):
    sides = dict(sides or {})
    w = sv["w"]
    grads, got = {}, {}
    du, got["d_a"] = _with_side(_matmul(dxb, w["w_ff2"], mode="nt", extras=(sv["u"],), out_dtypes=(BF16,),
                                        epilogue=lambda acc, u: (acc * (2.0 * jnp.maximum(u, 0.0)),),
                                        side=sides.get("d_a"), name=f"d_a_{tag}"), sides.get("d_a"))
    grads["w_ff2"], got["dw_ff2"] = _with_side(_matmul(sv["a"], dxb, mode="tn", side=sides.get("dw_ff2"),
                                                       name=f"dw_ff2_{tag}"), sides.get("dw_ff2"))
    dh2, got["d_h2"] = _with_side(_matmul(du, w["w_ff1"], mode="nt", b_sharded=True, side=sides.get("d_h2"),
                                          name=f"d_h2_{tag}"), sides.get("d_h2"))
    grads["w_ff1"] = _matmul(sv["h2"], du, mode="tn", out_sharded=True, name=f"dw_ff1_{tag}")
    dx1, dx1b, grads["norm2_g"] = _rmsnorm_bwd(sv["x1"], w["norm2_g"], dh2, dx, name=f"norm2_bwd_{tag}")
    dmixed = _matmul(dx1b, w["w_out"], mode="nt", name=f"d_mixed_{tag}")
    grads["w_out"] = _matmul(sv["mixed"], dx1b, mode="tn", name=f"dw_out_{tag}")
    dud, dus, dgate, grads["gate_b"] = _gate_bwd(sv["proj"], w["gate_b"], sv["up_dil"], sv["up_sb"], dmixed,
                                                 name=f"gate_bwd_{tag}")
    dy_dil = _matmul(dud, w["w_up_dil"], mode="nt", b_sharded=True, name=f"d_ydil_{tag}")
    grads["w_up_dil"] = _matmul(sv["yb"], dud, mode="tn", out_sharded=True, name=f"dw_up_dil_{tag}")
    dy_sb = _matmul(dus, w["w_up_sb"], mode="nt", b_sharded=True, name=f"d_ysb_{tag}")
    grads["w_up_sb"] = _matmul(sv["o_sb"], dus, mode="tn", out_sharded=True, name=f"dw_up_sb_{tag}")
    if late:
        sides.update(late(grads))
    dys = [dy_dil] + _dil_to_residues(dy_dil, name=f"dil_dy_{tag}")
    d_dil = [_dil_attn_bwd(src, g, cols, sv["y"][g], sv["lse"][g], dys[g], name=f"dil{g}_bwd_{tag}")
             for g, (src, cols) in enumerate(sv["dil_src"])]
    for g in (1, 2):
        d_dil[g] = [t.reshape(DIL_GROUPS[g][1], -1, GROUP_WIDTH) for t in d_dil[g]]
    d_sb = _sb_attn_bwd(sv["qkv"], sv["rc"], sv["trips"], dy_sb, name=f"sb_bwd_{tag}")
    dproj, grads["q_norm_g"], grads["k_norm_g"] = _qkv_prep_bwd(
        sv["proj"], w["q_norm_g"], w["k_norm_g"], cos, sin, d_dil, d_sb, dgate, name=f"qkv_prep_bwd_{tag}")
    dh, got["d_h"] = _with_side(_matmul(dproj, w["w_in"], mode="nt", b_sharded=True, tm=512, tn=2048, side=sides.get("d_h"),
                                        name=f"d_h_{tag}"), sides.get("d_h"))
    grads["w_in"], got["dw_in"] = _with_side(_matmul(sv["h"], dproj, mode="tn", out_sharded=True, tm=512, tk=1024,
                                                     side=sides.get("dw_in"), name=f"dw_in_{tag}"), sides.get("dw_in"))
    dx0, dx0b, grads["norm1_g"] = _rmsnorm_bwd(sv["x"], w["norm1_g"], dh, dx1, name=f"norm1_bwd_{tag}")
    return dx0, dx0b, grads, got


BIG = ("w_in", "w_up_dil", "w_up_sb", "w_out", "w_ff1", "w_ff2")
ROW_SHARDED = ("w_out", "w_ff2")
WEIGHTS = ("norm1_g", "w_in", "q_norm_g", "k_norm_g", "w_up_dil", "w_up_sb", "gate_b", "w_out", "norm2_g", "w_ff1", "w_ff2")


def _reduce_begin(grads, which, tag, swapped=None):
    theirs = dict(swapped or {})
    todo = [n for n in grads if n not in theirs]
    if todo:
        theirs.update(zip(todo, _pair_swap_halves([grads[n] for n in todo], name=f"rs_pair_{tag}")))
    return {n: _add_half(grads[n], theirs[n], which, name=f"rs_pair_sum_{n}_{tag}") for n in grads}


def _reduce_end(pair, parts, which, chip, tag):
    names = list(pair)
    halves = [_sum_chips(pair[n], parts[n], chip, name=f"rs_chip_sum_{n}_{tag}") for n in names]
    joined = _pair_join_halves(halves, name=f"rs_join_{tag}")
    out = {}
    for n, j, h in zip(names, joined, halves):
        r, c = j.shape
        out[n] = _place_piece(j.reshape(1, 2, r // 2, c), h[None], which, name=f"rs_place_{n}_{tag}").reshape(r, c)
    return out


def _pack_rows(vecs):
    rows, spans, at = [], [], 0
    for v in vecs:
        r = v.size // 128
        padded = -(-r // 8) * 8
        rows.append(jnp.pad(v.reshape(r, 128), ((0, padded - r), (0, 0))))
        spans.append((at, r))
        at += padded
    return jnp.concatenate(rows, axis=0), spans


def kernel(x, norm1_g, w_in, q_norm_g, k_norm_g, w_up_dil, w_up_sb, gate_b, w_out, norm2_g, w_ff1, w_ff2, loss_target, m_norm1_g, m_w_in, m_q_norm_g, m_k_norm_g, m_w_up_dil, m_w_up_sb, m_gate_b, m_w_out, m_norm2_g, m_w_ff1, m_w_ff2, v_norm1_g, v_w_in, v_q_norm_g, v_k_norm_g, v_w_up_dil, v_w_up_sb, v_gate_b, v_w_out, v_norm2_g, v_w_ff1, v_w_ff2):
    weights = dict(norm1_g=norm1_g, w_in=w_in, q_norm_g=q_norm_g, k_norm_g=k_norm_g, w_up_dil=w_up_dil,
                   w_up_sb=w_up_sb, gate_b=gate_b, w_out=w_out, norm2_g=norm2_g, w_ff1=w_ff1, w_ff2=w_ff2)
    moments_m = dict(norm1_g=m_norm1_g, w_in=m_w_in, q_norm_g=m_q_norm_g, k_norm_g=m_k_norm_g, w_up_dil=m_w_up_dil,
                     w_up_sb=m_w_up_sb, gate_b=m_gate_b, w_out=m_w_out, norm2_g=m_norm2_g, w_ff1=m_w_ff1, w_ff2=m_w_ff2)
    moments_v = dict(norm1_g=v_norm1_g, w_in=v_w_in, q_norm_g=v_q_norm_g, k_norm_g=v_k_norm_g, w_up_dil=v_w_up_dil,
                     w_up_sb=v_w_up_sb, gate_b=v_gate_b, w_out=v_w_out, norm2_g=v_norm2_g, w_ff1=v_w_ff1, w_ff2=v_w_ff2)
    depth = w_in.shape[0]
    seq, d_model = x.shape[1], x.shape[2]
    chip = 2 * lax.axis_index("x") + lax.axis_index("y")
    core = lax.axis_index("c")
    which = jnp.reshape(core, (1,)).astype(jnp.int32)

    chip_index = jnp.reshape(chip, (1,)).astype(jnp.int32)
    shards = {n: weights[n].astype(BF16) for n in BIG}
    bias_buf, ((_, bias_rows),) = _pack_rows([gate_b])
    bias_slots = _exchange_small(bias_buf, name="gather_gate_b")
    bias = bias_slots[0::2, :bias_rows].reshape(N_CHIPS, depth, 2, d_model // N_CHIPS)
    bias = jnp.transpose(bias, (1, 2, 0, 3)).reshape(depth, 2, d_model)

    def placed(names, l, landed):
        w = {}
        for n in names:
            w[n] = _place_piece(landed[n][None], shards[n][l:l + 1], chip_index, name=f"gather_place_{n}_l{l}")[0]
            if n in ROW_SHARDED:
                w[n] = w[n].reshape(-1, w[n].shape[-1])
        return w

    def ici_side(names, l):
        return _gather_ici_side([shards[n] for n in names], l)

    def by_chip(n, g):
        return g.reshape(N_CHIPS, -1, g.shape[-1]) if n in ROW_SHARDED else g

    ahead = {"ff1": ("w_in",), "out": ("w_up_dil", "w_up_sb", "w_out"), "ff2": ("w_ff1",)}
    cos, sin = _rope_tables(seq)
    first = ("w_in",)
    ready = placed(first, 0, dict(zip(first, _gather_layer([shards[n] for n in first], 0, name="gather_l0_w_in"))))
    pending = tuple(n for n in BIG if n not in first)
    act = x[0]
    saved = []
    for l in range(depth):
        def rest(landed, l=l, pending=pending):
            passed = _gather_pass(list(landed), name=f"gather_pass_l{l}_rest")
            return placed(pending, l, dict(zip(pending, passed)))

        sides = {"proj": ici_side(pending, l)}
        if l + 1 < depth:
            sides.update({k: ici_side(names, l + 1) for k, names in ahead.items()})
        w = dict(ready, norm1_g=norm1_g[l], norm2_g=norm2_g[l], q_norm_g=q_norm_g[l], k_norm_g=k_norm_g[l], gate_b=bias[l])
        act, sv, got = _layer_fwd(act, w, cos, sin, f"l{l}", sides, rest)
        saved.append(sv)
        if l + 1 < depth:
            names = [n for k in ahead for n in ahead[k]]
            passed = _gather_pass([t for k in ahead for t in got[k]], name=f"gather_pass_l{l + 1}")
            ready = placed(names, l + 1, dict(zip(names, passed)))
            pending = ("w_ff2",)
    loss, dx, dxb = _loss_head(act, loss_target[0])
    loss = lax.psum(loss, ("x", "y", "c"))

    below = {"d_a": ("w_in",), "dw_ff2": ("w_ff1", "w_out", "w_up_dil", "w_up_sb"), "d_h2": ("w_ff2",)}
    own = {"d_h": ("w_ff2", "w_ff1"), "dw_in": ("w_out", "w_up_dil", "w_up_sb")}
    known_early = tuple(n for names in own.values() for n in names)
    grads, reduced = [None] * depth, [None] * depth
    pair = None
    for l in reversed(range(depth)):
        sides = {k: _chip_side([pair[n] for n in names]) for k, names in below.items()} if pair else None
        early = {}

        def late(g, early=early):
            for k, names in own.items():
                early.update(_reduce_begin({n: by_chip(n, g[n]) for n in names}, which, f"l0_{k}"))
            return {k: _chip_side([early[n] for n in names]) for k, names in own.items()}

        def late_swap(g):
            return {"dw_in": _pair_side([by_chip(n, g[n]) for n in known_early])}

        dx, dxb, grads[l], got = _layer_bwd(dx, dxb, saved[l], cos, sin, f"l{l}", sides, late if l == 0 else late_swap)
        if pair:
            parts = {n: t for k, names in below.items() for n, t in zip(names, got[k])}
            reduced[l + 1] = _reduce_end(pair, parts, which, chip_index, f"l{l + 1}")
        if l > 0:
            pair = _reduce_begin({n: by_chip(n, grads[l][n]) for n in BIG}, which, f"l{l}",
                                 swapped=dict(zip(known_early, got["dw_in"])))
    last = _reduce_begin({"w_in": grads[0]["w_in"]}, which, "l0_w_in")
    parts = {n: t for k, names in own.items() for n, t in zip(names, got[k])}
    parts["w_in"] = _chip_exchange([last["w_in"]], name="rs_chips_l0_w_in")[0]
    pair = dict(early, **last)
    reduced[0] = _reduce_end({n: pair[n] for n in BIG}, parts, which, chip_index, "l0")
    final, deltas, new_m, new_v = {}, {}, {}, {}
    for n in BIG:
        final[n], deltas[n], new_m[n], new_v[n] = _adamw_layers(
            weights[n], [reduced[l][n] for l in range(depth)], moments_m[n], moments_v[n], name=f"adamw_{n}")

    small_names = ("norm1_g", "norm2_g", "q_norm_g", "k_norm_g", "gate_b")
    packed, spans = _pack_rows([grads[l][n] for l in range(depth) for n in small_names])
    total = _sum_slots(_exchange_small(packed, name="exchange_small_grads"), name="sum_small_grads")
    pieces = iter(total[at:at + r] for at, r in spans)
    small = {n: [] for n in small_names}
    for l in range(depth):
        for n in small_names:
            small[n].append(next(pieces))
    for n in ("norm1_g", "norm2_g"):
        final[n] = jnp.stack([p.reshape(d_model) for p in small[n]])
    for n in ("q_norm_g", "k_norm_g"):
        final[n] = jnp.stack([p.reshape(N_GROUPS, HEAD_DIM) for p in small[n]])
    shard_cols = d_model // N_CHIPS
    final["gate_b"] = jnp.stack([lax.dynamic_slice_in_dim(p.reshape(2, d_model), chip * shard_cols, shard_cols, axis=1)
                                 for p in small["gate_b"]])

    for n in small_names:
        deltas[n], new_m[n], new_v[n] = _adamw(weights[n], final[n], moments_m[n], moments_v[n], name=f"adamw_{n}")
    return (loss, dx[None], *[final[n] for n in WEIGHTS], *[deltas[n] for n in WEIGHTS],
            *[new_m[n] for n in WEIGHTS], *[new_v[n] for n in WEIGHTS])
```
